```python
import math
import jax, jax.numpy as jnp
from jax import lax
import numpy as np

D_MODEL = 4096
BATCH = 4
SEQ = 2048
DEPTH = 2
DEC_BATCH = 8
DEC_SEQ = 8
PAST_LEN = 16384
PAGE_SIZE = 128

D_MIX = D_MODEL
D_SSD = D_MIX // 2
SSD_HEAD_DIM = 64
SSD_HEADS = D_SSD // SSD_HEAD_DIM
SSD_GROUPS = 8
SSD_STATE = 128
SSD_CONV = 4
SSD_CHUNK = 128
SSD_CONV_DIM = D_SSD + 2 * SSD_GROUPS * SSD_STATE
D_NSA = D_MIX // 4
NSA_HEAD_DIM = 128
NSA_HEADS = D_NSA // NSA_HEAD_DIM
NSA_KV_HEADS = 2
NSA_REP = NSA_HEADS // NSA_KV_HEADS
CMP_LEN = 32
CMP_STRIDE = 16
SEL_LEN = 64
TOP_N = 16
WINDOW = 512
Q_BLOCK = 128
FORCE_SCORE = 1e4
D_GLA = D_MIX - D_SSD - D_NSA
GLA_HEADS = 4
GLA_DV = D_GLA // GLA_HEADS
GLA_DK = GLA_DV // 2
GLA_RANK = 16
GLA_TAU = 16.0
GLA_CHUNK = 16
N_MEM = 256
X_HEADS = 4
X_HEAD_DIM = D_MODEL // X_HEADS
D_FF = 4 * D_MODEL
N_BUCKETS = 32
MAX_DISTANCE = 128
LN_EPS = 1e-5
NORM_EPS = 1e-6
DN_ALPHA = (2 * DEPTH) ** 0.25
DN_BETA = (8 * DEPTH) ** -0.25
IN_SPLITS = (D_SSD, SSD_CONV_DIM, SSD_HEADS, NSA_HEADS * NSA_HEAD_DIM) + (NSA_KV_HEADS * NSA_HEAD_DIM,) * 6 + (3 * NSA_HEADS, GLA_HEADS * GLA_DK, GLA_HEADS * GLA_DK, D_GLA, D_GLA, GLA_RANK)
D_IN = sum(IN_SPLITS)

kernel_name = 'hymba_ssd_nsa_gla_deepnorm_step'

F32 = jnp.float32


def _split(x, sizes):
    return jnp.split(x, np.cumsum(sizes)[:-1].tolist(), axis=-1)


def _layernorm(x, g, b):
    xf = x.astype(F32)
    mu = jnp.mean(xf, -1, keepdims=True)
    var = jnp.mean(jnp.square(xf - mu), -1, keepdims=True)
    return ((xf - mu) * lax.rsqrt(var + LN_EPS) * g + b).astype(x.dtype)


def _rmsnorm(x, w):
    xf = x.astype(F32)
    return (xf * lax.rsqrt(jnp.mean(xf * xf, -1, keepdims=True) + NORM_EPS) * w).astype(x.dtype)


def _masked_softmax(s, mask):
    s = jnp.where(mask, s.astype(F32), -jnp.inf)
    m = jnp.max(s, -1, keepdims=True)
    m = jnp.where(jnp.isfinite(m), m, 0.0)
    e = jnp.exp(s - m)
    d = jnp.sum(e, -1, keepdims=True)
    return e / jnp.where(d > 0, d, 1.0)


def _t5_bucket(dist):
    n = jnp.maximum(dist, 0)
    max_exact = N_BUCKETS // 2
    nf = jnp.maximum(n, 1).astype(F32)
    large = max_exact + (jnp.log(nf / max_exact) / math.log(MAX_DISTANCE / max_exact) * (N_BUCKETS - max_exact)).astype(jnp.int32)
    large = jnp.minimum(large, N_BUCKETS - 1)
    return jnp.where(n < max_exact, n, large)


def _ssd_scan(x, dt, a, bm, cm, h0):
    b, l, h, p = x.shape
    g, n = bm.shape[2], bm.shape[3]
    r = h // g
    lc = min(SSD_CHUNK, l)
    pad = (-l) % lc
    x, dt, bm, cm = [jnp.pad(t.astype(F32), ((0, 0), (0, pad)) + ((0, 0),) * (t.ndim - 2)) for t in (x, dt, bm, cm)]
    c = (l + pad) // lc
    xdt = (x * dt[..., None]).reshape(b, c, lc, g, r, p)
    acs = jnp.cumsum((dt * a).reshape(b, c, lc, g, r), axis=2)
    bm = bm.reshape(b, c, lc, g, n)
    cm = cm.reshape(b, c, lc, g, n)
    causal = np.tril(np.ones((lc, lc), bool))[None, None, :, :, None, None]
    decay = jnp.exp(jnp.where(causal, acs[:, :, :, None] - acs[:, :, None], -jnp.inf))
    cb = jnp.einsum('bctgn,bcsgn->bctsg', cm, bm)
    y_diag = jnp.einsum('bctsg,bctsgr,bcsgrp->bctgrp', cb, decay, xdt)
    decay_st = jnp.exp(acs[:, :, -1:] - acs)
    states = jnp.einsum('bcsgn,bcsgr,bcsgrp->bcgrpn', bm, decay_st, xdt)
    chunk_decay = jnp.exp(acs[:, :, -1])

    def step(hc, inp):
        st, dec = inp
        return hc * dec[..., None, None] + st, hc

    h_last, h_prev = lax.scan(step, h0.astype(F32).reshape(b, g, r, p, n), (jnp.moveaxis(states, 1, 0), jnp.moveaxis(chunk_decay, 1, 0)))
    h_prev = jnp.moveaxis(h_prev, 0, 1)
    y_off = jnp.einsum('bctgn,bcgrpn,bctgr->bctgrp', cm, h_prev, jnp.exp(acs))
    y = (y_diag + y_off).reshape(b, c * lc, h, p)[:, :l]
    return y, h_last.reshape(b, h, p, n)


def _ssd_mixer(z, xbc, dt, conv0, h0, conv_w, conv_b, dt_bias, a_log, d_skip, norm_w):
    b, l, _ = xbc.shape
    xpad = jnp.concatenate([conv0.astype(xbc.dtype), xbc], axis=1)
    conv = sum((xpad[:, k:k + l] * conv_w[k] for k in range(SSD_CONV)), conv_b)
    conv_new = xpad[:, -(SSD_CONV - 1):]
    xbc = jax.nn.silu(conv)
    xs, bm, cm = _split(xbc, (D_SSD, SSD_GROUPS * SSD_STATE, SSD_GROUPS * SSD_STATE))
    xs = xs.reshape(b, l, SSD_HEADS, SSD_HEAD_DIM)
    bm = bm.reshape(b, l, SSD_GROUPS, SSD_STATE)
    cm = cm.reshape(b, l, SSD_GROUPS, SSD_STATE)
    dt = jax.nn.softplus((dt + dt_bias).astype(F32))
    a = -jnp.exp(a_log.astype(F32))
    y, h_new = _ssd_scan(xs, dt, a, bm, cm, h0)
    y = y + xs.astype(F32) * d_skip.astype(F32)[:, None]
    y = (y.reshape(b, l, D_SSD) * jax.nn.silu(z.astype(F32))).reshape(b, l, SSD_GROUPS, D_SSD // SSD_GROUPS)
    y = _rmsnorm(y, norm_w.reshape(SSD_GROUPS, D_SSD // SSD_GROUPS)).reshape(b, l, D_SSD)
    return y.astype(z.dtype), conv_new, h_new


def _gla_scan(q, k, v, logf, s0):
    b, l, h, dk = q.shape
    dv = v.shape[-1]
    lc = min(GLA_CHUNK, l)
    pad = (-l) % lc
    q, k, v, logf = [jnp.pad(t.astype(F32), ((0, 0), (0, pad), (0, 0), (0, 0))) for t in (q, k, v, logf)]
    c = (l + pad) // lc
    q, k, logf = [t.reshape(b, c, lc, h, dk) for t in (q, k, logf)]
    v = v.reshape(b, c, lc, h, dv)
    bc = jnp.cumsum(logf, axis=2)
    causal = np.tril(np.ones((lc, lc), bool))[None, None, :, :, None, None]
    rel = jnp.exp(jnp.where(causal, bc[:, :, :, None] - bc[:, :, None], -jnp.inf))
    att = jnp.einsum('bcthd,bcshd,bctshd->bchts', q, k, rel)
    o_intra = jnp.einsum('bchts,bcshv->bcthv', att, v)
    k_st = k * jnp.exp(bc[:, :, -1:] - bc)
    chunk_states = jnp.einsum('bcshd,bcshv->bchdv', k_st, v)
    chunk_decay = jnp.exp(bc[:, :, -1])

    def step(s, inp):
        st, dec = inp
        return s * dec[..., None] + st, s

    s_last, s_prev = lax.scan(step, s0.astype(F32), (jnp.moveaxis(chunk_states, 1, 0), jnp.moveaxis(chunk_decay, 1, 0)))
    s_prev = jnp.moveaxis(s_prev, 0, 1)
    o_inter = jnp.einsum('bcthd,bchdv->bcthv', q * jnp.exp(bc), s_prev)
    o = (o_intra + o_inter).reshape(b, c * lc, h, dv)[:, :l]
    return o, s_last


def _gla_mixer(q, k, v, g, a_lr, s0, gate_w2, gate_b, norm_w):
    b, l, _ = q.shape
    q = q.reshape(b, l, GLA_HEADS, GLA_DK) * (GLA_DK ** -0.5)
    k = k.reshape(b, l, GLA_HEADS, GLA_DK)
    v = v.reshape(b, l, GLA_HEADS, GLA_DV)
    logf = jax.nn.log_sigmoid((a_lr @ gate_w2 + gate_b).astype(F32)) / GLA_TAU
    o, s_new = _gla_scan(q, k, v, logf.reshape(b, l, GLA_HEADS, GLA_DK), s0)
    o = _rmsnorm(o, norm_w).reshape(b, l, D_GLA) * jax.nn.silu(g.astype(F32))
    return o.astype(g.dtype), s_new


def _nsa_attend(q, gates, kv_full, win_ctx, q_pos0, w_pos0, cmp_pe, cmp_w, rel_bias):
    b, lq = q.shape[:2]
    t_len = kv_full.shape[1]
    G, R, HD = NSA_KV_HEADS, NSA_REP, NSA_HEAD_DIM
    kc, vc, ks, vs = [kv_full[:, :, i] for i in range(4)]
    n_cmp = (t_len - CMP_LEN) // CMP_STRIDE + 1
    cidx = (np.arange(n_cmp)[:, None] * CMP_STRIDE + np.arange(CMP_LEN)[None]).astype(np.int32)
    k_cmp = jnp.einsum('bnlgd,lde->bnge', kc[:, cidx] + cmp_pe[0][:, None], cmp_w[0]).astype(F32)
    v_cmp = jnp.einsum('bnlgd,lde->bnge', vc[:, cidx] + cmp_pe[1][:, None], cmp_w[1]).astype(F32)
    cmp_end = (np.arange(n_cmp) * CMP_STRIDE + CMP_LEN - 1).astype(np.int32)
    n_sel = -(-t_len // SEL_LEN)
    sel_pad = n_sel * SEL_LEN - t_len

    def sel_blocks(t):
        t = jnp.pad(t, ((0, 0), (0, sel_pad), (0, 0), (0, 0)))
        return t.reshape(b, n_sel, SEL_LEN, G, HD).transpose(0, 3, 1, 2, 4)

    ks_b, vs_b = sel_blocks(ks), sel_blocks(vs)
    c_lo = np.arange(n_cmp) * CMP_STRIDE
    s_lo = np.arange(n_sel) * SEL_LEN
    cover = jnp.asarray(((c_lo[:, None] < s_lo[None] + SEL_LEN) & (c_lo[:, None] + CMP_LEN > s_lo[None])).astype(np.float32))
    top_n = min(TOP_N, n_sel)
    kw_pad = jnp.pad(win_ctx[:, :, 0], ((0, 0), (WINDOW, 0), (0, 0), (0, 0)))
    vw_pad = jnp.pad(win_ctx[:, :, 1], ((0, 0), (WINDOW, 0), (0, 0), (0, 0)))
    qb = Q_BLOCK if lq % Q_BLOCK == 0 else lq
    nqb = lq // qb
    scale = HD ** -0.5
    bias_gr = rel_bias.reshape(N_BUCKETS, G, R)
    bi = jnp.arange(b)[:, None, None, None]
    gi = jnp.arange(G)[None, None, :, None]
    gi5 = jnp.arange(G)[None, None, :, None, None]
    blk = jnp.arange(n_sel)

    def block(args):
        i, qblk, gblk = args
        q0 = q_pos0 + i * qb
        tpos = q0 + jnp.arange(qb)
        qg = qblk.reshape(b, qb, G, R, HD).astype(F32) * scale
        bias_c = rel_bias[_t5_bucket(tpos[:, None] - cmp_end[None])].astype(F32).reshape(qb, n_cmp, G, R).transpose(0, 2, 3, 1)
        s_c = jnp.einsum('bqgrd,bngd->bqgrn', qg, k_cmp) + bias_c
        p_c = _masked_softmax(s_c, (cmp_end[None] <= tpos[:, None])[None, :, None, None, :])
        o_c = jnp.einsum('bqgrn,bngd->bqgrd', p_c, v_cmp)
        imp = jnp.einsum('bqgrn,nj->bqgj', p_c, cover)
        cur = (tpos // SEL_LEN)[:, None]
        forced = (blk == 0) | (blk == cur) | (blk == cur - 1)
        valid = blk * SEL_LEN <= tpos[:, None]
        score = jnp.where(forced[None, :, None, :], FORCE_SCORE, jnp.where(valid[None, :, None, :], imp, -FORCE_SCORE))
        _, sel = lax.top_k(score, top_n)
        k_sel = ks_b[bi, gi, sel].astype(F32)
        v_sel = vs_b[bi, gi, sel].astype(F32)
        spos = sel[..., None] * SEL_LEN + jnp.arange(SEL_LEN)
        dist_s = tpos[None, :, None, None, None] - spos
        bias_s = jnp.moveaxis(bias_gr[_t5_bucket(dist_s), gi5].astype(F32), -1, 3)
        s_s = jnp.einsum('bqgrd,bqgksd->bqgrks', qg, k_sel) + bias_s
        p_s = _masked_softmax(s_s.reshape(b, qb, G, R, top_n * SEL_LEN), (dist_s >= 0).reshape(b, qb, G, 1, top_n * SEL_LEN)).reshape(s_s.shape)
        o_s = jnp.einsum('bqgrks,bqgksd->bqgrd', p_s, v_sel)
        start = q0 - w_pos0
        k_w = lax.dynamic_slice_in_dim(kw_pad, start, WINDOW + qb, axis=1).astype(F32)
        v_w = lax.dynamic_slice_in_dim(vw_pad, start, WINDOW + qb, axis=1).astype(F32)
        wpos = q0 - WINDOW + jnp.arange(WINDOW + qb)
        dist_w = tpos[:, None] - wpos[None]
        mask_w = (dist_w >= 0) & (dist_w < WINDOW) & (wpos[None] >= w_pos0)
        bias_w = rel_bias[_t5_bucket(dist_w)].astype(F32).reshape(qb, WINDOW + qb, G, R).transpose(0, 2, 3, 1)
        s_w = jnp.einsum('bqgrd,bkgd->bqgrk', qg, k_w) + bias_w
        p_w = _masked_softmax(s_w, mask_w[None, :, None, None, :])
        o_w = jnp.einsum('bqgrk,bkgd->bqgrd', p_w, v_w)
        gg = gblk.reshape(b, qb, G, R, 3).astype(F32)
        return gg[..., 0:1] * o_c + gg[..., 1:2] * o_s + gg[..., 2:3] * o_w

    q_blocks = jnp.swapaxes(q.reshape(b, nqb, qb, NSA_HEADS, HD), 0, 1)
    g_blocks = jnp.swapaxes(gates.reshape(b, nqb, qb, NSA_HEADS, 3), 0, 1)
    outs = lax.map(block, (jnp.arange(nqb), q_blocks, g_blocks))
    return jnp.moveaxis(outs, 0, 1).reshape(b, lq, D_NSA)


def _mixer(x, lp, rel_bias, conv0, ssd_h0, gla_s0, nsa_past, win_past, past_len):
    b, l, _ = x.shape
    parts = _split(x @ lp['w_in'], IN_SPLITS)
    z, xbc, dt, q_n = parts[0], parts[1], parts[2], parts[3]
    kv6 = parts[4:10]
    g_n = parts[10]
    q_g, k_g, v_g, g_g, a_g = parts[11], parts[12], parts[13], parts[14], parts[15]
    y_ssd, conv_new, h_new = _ssd_mixer(z, xbc, dt, conv0, ssd_h0, lp['ssd_conv_w'], lp['ssd_conv_b'], lp['ssd_dt_bias'], lp['ssd_a_log'], lp['ssd_d'], lp['ssd_norm_w'])
    kvr = [t.reshape(b, l, NSA_KV_HEADS, NSA_HEAD_DIM) for t in kv6]
    rows = jnp.stack(kvr[:4], axis=2)
    win_rows = jnp.stack(kvr[4:], axis=2)
    if nsa_past is None:
        kv_full, win_ctx, w_pos0 = rows, win_rows, 0
        buf_len = min(WINDOW, l)
    else:
        kv_full = jnp.concatenate([nsa_past.astype(rows.dtype), rows], axis=1)
        win_ctx = jnp.concatenate([win_past.astype(rows.dtype), win_rows], axis=1)
        w_pos0 = past_len - win_past.shape[1]
        buf_len = win_past.shape[1]
    gates = jax.nn.sigmoid(g_n.astype(F32)).reshape(b, l, NSA_HEADS, 3)
    y_nsa = _nsa_attend(q_n.reshape(b, l, NSA_HEADS, NSA_HEAD_DIM), gates, kv_full, win_ctx, past_len, w_pos0, lp['nsa_cmp_pe'], lp['nsa_cmp_w'], rel_bias)
    y_gla, s_new = _gla_mixer(q_g, k_g, v_g, g_g, a_g, gla_s0, lp['gla_gate_w2'], lp['gla_gate_b'], lp['gla_norm_w'])
    y = jnp.concatenate([y_ssd, y_nsa.astype(x.dtype), y_gla], axis=-1) @ lp['w_out']
    return y, rows, win_ctx[:, -buf_len:], conv_new, h_new, s_new


def _cross_attn(x, mem_kv, wq, wo):
    b, l, _ = x.shape
    q = (x @ wq).reshape(b, l, X_HEADS, X_HEAD_DIM).astype(F32) * (X_HEAD_DIM ** -0.5)
    s = jnp.einsum('blhd,bmhd->bhlm', q, mem_kv[:, :, 0].astype(F32))
    p = jax.nn.softmax(s, axis=-1)
    o = jnp.einsum('bhlm,bmhd->blhd', p, mem_kv[:, :, 1].astype(F32)).reshape(b, l, D_MODEL)
    return o.astype(x.dtype) @ wo


def _sqrelu_ffn(x, w1, w2):
    return jnp.square(jax.nn.relu(x @ w1)) @ w2


def _layer(x, lp, rel_bias, mem_kv, conv0, ssd_h0, gla_s0, nsa_past, win_past, past_len):
    h, rows, win, conv, hs, sg = _mixer(x, lp, rel_bias, conv0, ssd_h0, gla_s0, nsa_past, win_past, past_len)
    x = _layernorm(DN_ALPHA * x + h, lp['ln_g'][0], lp['ln_b'][0])
    x = _layernorm(DN_ALPHA * x + _cross_attn(x, mem_kv, lp['x_wq'], lp['x_wo']), lp['ln_g'][1], lp['ln_b'][1])
    x = _layernorm(DN_ALPHA * x + _sqrelu_ffn(x, lp['ffn_w1'], lp['ffn_w2']), lp['ln_g'][2], lp['ln_b'][2])
    return x, (rows, win, conv, hs, sg)


def setup_inputs(seed: int = 0) -> dict:
    key = jax.random.key(seed)
    ks = jax.random.split(key, 32)

    def nrm(k, shape, s):
        return jax.random.normal(k, shape, F32) * s

    n_pages = PAST_LEN // PAGE_SIZE
    n_used = DEC_BATCH * n_pages
    n_phys = n_used + n_used // 4
    win_len = min(WINDOW, PAST_LEN)
    x_prompt = nrm(ks[0], (BATCH, SEQ, D_MODEL), 1.0)
    x_sample = nrm(ks[1], (DEC_BATCH, DEC_SEQ, D_MODEL), 1.0)
    cache_nsa_kv = nrm(ks[2], (DEPTH, n_phys, PAGE_SIZE, 4, NSA_KV_HEADS, NSA_HEAD_DIM), 1.0)
    cache_nsa_win = nrm(ks[3], (DEPTH, DEC_BATCH, win_len, 2, NSA_KV_HEADS, NSA_HEAD_DIM), 1.0)
    state_ssd = nrm(ks[4], (DEPTH, DEC_BATCH, SSD_HEADS, SSD_HEAD_DIM, SSD_STATE), 0.2)
    state_ssd_conv = nrm(ks[5], (DEPTH, DEC_BATCH, SSD_CONV - 1, SSD_CONV_DIM), 1.0)
    state_gla = nrm(ks[6], (DEPTH, DEC_BATCH, GLA_HEADS, GLA_DK, GLA_DV), 0.2)
    cache_mem_kv = nrm(ks[7], (DEPTH, DEC_BATCH, N_MEM, 2, X_HEADS, X_HEAD_DIM), 1.0)
    page_table = jax.random.permutation(ks[8], n_phys)[:n_used].reshape(DEC_BATCH, n_pages).astype(jnp.int32)
    mem_prompt = nrm(ks[9], (BATCH, N_MEM, D_MODEL), 1.0)
    w_in = nrm(ks[10], (DEPTH, D_MODEL, D_IN), D_MODEL ** -0.5)
    ssd_conv_w = nrm(ks[11], (DEPTH, SSD_CONV, SSD_CONV_DIM), SSD_CONV ** -0.5)
    ssd_conv_b = nrm(ks[12], (DEPTH, SSD_CONV_DIM), 0.02)
    dt0 = jnp.exp(jax.random.uniform(ks[13], (DEPTH, SSD_HEADS), F32, math.log(1e-3), math.log(1e-1)))
    ssd_dt_bias = dt0 + jnp.log(-jnp.expm1(-dt0))
    ssd_a_log = jnp.log(jax.random.uniform(ks[14], (DEPTH, SSD_HEADS), F32, 1.0, 16.0))
    ssd_d = 1.0 + nrm(ks[15], (DEPTH, SSD_HEADS), 0.02)
    ssd_norm_w = 1.0 + nrm(ks[16], (DEPTH, D_SSD), 0.02)
    nsa_cmp_pe = nrm(ks[17], (DEPTH, 2, CMP_LEN, NSA_HEAD_DIM), 0.02)
    nsa_cmp_w = nrm(ks[18], (DEPTH, 2, CMP_LEN, NSA_HEAD_DIM, NSA_HEAD_DIM), (CMP_LEN * NSA_HEAD_DIM) ** -0.5)
    rel_bias = nrm(ks[19], (N_BUCKETS, NSA_HEADS), 0.2)
    gla_gate_w2 = nrm(ks[20], (DEPTH, GLA_RANK, GLA_HEADS * GLA_DK), GLA_RANK ** -0.5)
    gla_gate_b = nrm(ks[21], (DEPTH, GLA_HEADS * GLA_DK), 0.1)
    gla_norm_w = 1.0 + nrm(ks[22], (DEPTH, GLA_DV), 0.02)
    w_out = nrm(ks[23], (DEPTH, D_MIX, D_MODEL), D_MIX ** -0.5 * DN_BETA)
    x_wq = nrm(ks[24], (DEPTH, D_MODEL, D_MODEL), D_MODEL ** -0.5)
    x_wkv = nrm(ks[25], (DEPTH, D_MODEL, 2 * D_MODEL), D_MODEL ** -0.5)
    x_wo = nrm(ks[26], (DEPTH, D_MODEL, D_MODEL), D_MODEL ** -0.5 * DN_BETA)
    ffn_w1 = nrm(ks[27], (DEPTH, D_MODEL, D_FF), D_MODEL ** -0.5)
    ffn_w2 = nrm(ks[28], (DEPTH, D_FF, D_MODEL), D_FF ** -0.5 * DN_BETA)
    ln_g = 1.0 + nrm(ks[29], (DEPTH, 3, D_MODEL), 0.02)
    ln_b = nrm(ks[30], (DEPTH, 3, D_MODEL), 0.02)
    return {'x_prompt': x_prompt, 'x_sample': x_sample, 'cache_nsa_kv': cache_nsa_kv, 'cache_nsa_win': cache_nsa_win,
            'state_ssd': state_ssd, 'state_ssd_conv': state_ssd_conv, 'state_gla': state_gla, 'cache_mem_kv': cache_mem_kv,
            'page_table': page_table, 'mem_prompt': mem_prompt, 'w_in': w_in, 'ssd_conv_w': ssd_conv_w, 'ssd_conv_b': ssd_conv_b,
            'ssd_dt_bias': ssd_dt_bias, 'ssd_a_log': ssd_a_log, 'ssd_d': ssd_d, 'ssd_norm_w': ssd_norm_w,
            'nsa_cmp_pe': nsa_cmp_pe, 'nsa_cmp_w': nsa_cmp_w, 'rel_bias': rel_bias, 'gla_gate_w2': gla_gate_w2,
            'gla_gate_b': gla_gate_b, 'gla_norm_w': gla_norm_w, 'w_out': w_out, 'x_wq': x_wq, 'x_wkv': x_wkv, 'x_wo': x_wo,
            'ffn_w1': ffn_w1, 'ffn_w2': ffn_w2, 'ln_g': ln_g, 'ln_b': ln_b}


def reference(x_prompt, x_sample, cache_nsa_kv, cache_nsa_win, state_ssd, state_ssd_conv, state_gla, cache_mem_kv,
              page_table, mem_prompt, w_in, ssd_conv_w, ssd_conv_b, ssd_dt_bias, ssd_a_log, ssd_d, ssd_norm_w,
              nsa_cmp_pe, nsa_cmp_w, rel_bias, gla_gate_w2, gla_gate_b, gla_norm_w, w_out, x_wq, x_wkv, x_wo,
              ffn_w1, ffn_w2, ln_g, ln_b):
    bp = x_prompt.shape[0]
    bs = x_sample.shape[0]
    past_len = page_table.shape[1] * cache_nsa_kv.shape[2]
    xp, xs = x_prompt, x_sample
    st_p, st_s, mem_p = [], [], []
    for l in range(DEPTH):
        lp = dict(w_in=w_in[l], ssd_conv_w=ssd_conv_w[l], ssd_conv_b=ssd_conv_b[l], ssd_dt_bias=ssd_dt_bias[l],
                  ssd_a_log=ssd_a_log[l], ssd_d=ssd_d[l], ssd_norm_w=ssd_norm_w[l], nsa_cmp_pe=nsa_cmp_pe[l],
                  nsa_cmp_w=nsa_cmp_w[l], gla_gate_w2=gla_gate_w2[l], gla_gate_b=gla_gate_b[l], gla_norm_w=gla_norm_w[l],
                  w_out=w_out[l], x_wq=x_wq[l], x_wo=x_wo[l], ffn_w1=ffn_w1[l], ffn_w2=ffn_w2[l], ln_g=ln_g[l], ln_b=ln_b[l])
        mem_kv_p = (mem_prompt @ x_wkv[l]).reshape(bp, N_MEM, 2, X_HEADS, X_HEAD_DIM)
        conv0 = jnp.zeros((bp, SSD_CONV - 1, SSD_CONV_DIM), xp.dtype)
        h0 = jnp.zeros((bp, SSD_HEADS, SSD_HEAD_DIM, SSD_STATE), F32)
        s0 = jnp.zeros((bp, GLA_HEADS, GLA_DK, GLA_DV), F32)
        xp, stp = _layer(xp, lp, rel_bias, mem_kv_p, conv0, h0, s0, None, None, 0)
        st_p.append(stp)
        mem_p.append(mem_kv_p)
        past = cache_nsa_kv[l][page_table].reshape(bs, past_len, 4, NSA_KV_HEADS, NSA_HEAD_DIM)
        xs, sts = _layer(xs, lp, rel_bias, cache_mem_kv[l], state_ssd_conv[l], state_ssd[l], state_gla[l], past, cache_nsa_win[l], past_len)
        st_s.append(sts)
    p_rows, p_win, p_conv, p_ssd, p_gla = [jnp.stack(s) for s in zip(*st_p)]
    s_rows, s_win, s_conv, s_ssd, s_gla = [jnp.stack(s) for s in zip(*st_s)]
    p_mem = jnp.stack(mem_p)
    return (xp, xs, p_rows, s_rows, p_win, s_win, p_ssd, s_ssd, p_conv, s_conv, p_gla, s_gla, p_mem)
```

```python
import functools
import math

import jax
import jax.numpy as jnp
import numpy as np
from jax import lax
from jax.experimental import pallas as pl
from jax.experimental.pallas import tpu as pltpu

F32 = jnp.float32
BF16 = jnp.bfloat16

D_MODEL = 4096
DEPTH = 2
PAGE_SIZE = 128
D_MIX = D_MODEL
D_SSD = D_MIX // 2
SSD_HEAD_DIM = 64
SSD_HEADS = D_SSD // SSD_HEAD_DIM
SSD_GROUPS = 8
SSD_STATE = 128
SSD_CONV = 4
SSD_CHUNK = 128
SSD_CONV_DIM = D_SSD + 2 * SSD_GROUPS * SSD_STATE
D_NSA = D_MIX // 4
NSA_HEAD_DIM = 128
NSA_HEADS = D_NSA // NSA_HEAD_DIM
NSA_KV_HEADS = 2
NSA_REP = NSA_HEADS // NSA_KV_HEADS
CMP_LEN = 32
CMP_STRIDE = 16
SEL_LEN = 64
TOP_N = 16
WINDOW = 512
Q_BLOCK = 128
FORCE_SCORE = 1e4
D_GLA = D_MIX - D_SSD - D_NSA
GLA_HEADS = 4
GLA_DV = D_GLA // GLA_HEADS
GLA_DK = GLA_DV // 2
GLA_RANK = 16
GLA_TAU = 16.0
GLA_CHUNK = 16
N_MEM = 256
X_HEADS = 4
X_HEAD_DIM = D_MODEL // X_HEADS
D_FF = 4 * D_MODEL
N_BUCKETS = 32
MAX_DISTANCE = 128
LN_EPS = 1e-5
NORM_EPS = 1e-6
DN_ALPHA = (2 * DEPTH) ** 0.25
IN_SPLITS = (D_SSD, SSD_CONV_DIM, SSD_HEADS, NSA_HEADS * NSA_HEAD_DIM) + (NSA_KV_HEADS * NSA_HEAD_DIM,) * 6 + (
    3 * NSA_HEADS, GLA_HEADS * GLA_DK, GLA_HEADS * GLA_DK, D_GLA, D_GLA, GLA_RANK)
D_IN = sum(IN_SPLITS)

LANE = 128
VMEM_LIMIT_BYTES = 56 * 1024 * 1024


def _mm_kernel(x_ref, w_ref, o_ref, *scratch, nk, act):
    def finish(acc):
        if act == 'sqrelu':
            acc = jnp.square(jnp.maximum(acc, 0.0))
        o_ref[...] = acc.astype(o_ref.dtype)

    if nk == 1:
        finish(jnp.dot(x_ref[...], w_ref[...], preferred_element_type=F32))
        return
    acc_ref, = scratch
    k = pl.program_id(2)
    part = jnp.dot(x_ref[...], w_ref[...], preferred_element_type=F32)

    @pl.when(k == 0)
    def _():
        acc_ref[...] = part

    @pl.when(k > 0)
    def _():
        acc_ref[...] += part

    @pl.when(k == nk - 1)
    def _():
        finish(acc_ref[...])


def _pick(n, pref):
    for t in pref:
        if n % t == 0:
            return t
    return n


def _matmul(x, w, out_dtype=F32, act=None, name='matmul'):
    m, k = x.shape
    n = w.shape[1]
    tm = _pick(m, (1024, 512, 256, 128, 64))
    tn = _pick(n, (512, 256, 128))
    tk = _pick(k, (4096, 2048, 1024, 512))
    nk = k // tk
    scratch = [pltpu.VMEM((tm, tn), F32)] if nk > 1 else []
    return pl.pallas_call(
        functools.partial(_mm_kernel, nk=nk, act=act),
        grid=(m // tm, n // tn, nk),
        in_specs=[pl.BlockSpec((tm, tk), lambda i, j, kk: (i, kk)),
                  pl.BlockSpec((tk, tn), lambda i, j, kk: (kk, j))],
        out_specs=pl.BlockSpec((tm, tn), lambda i, j, kk: (i, j)),
        out_shape=jax.ShapeDtypeStruct((m, n), out_dtype),
        scratch_shapes=scratch,
        compiler_params=pltpu.CompilerParams(
            dimension_semantics=('parallel', 'parallel', 'arbitrary'),
            vmem_limit_bytes=VMEM_LIMIT_BYTES),
        name=name,
    )(x, w)


def _proj(x, w_bf16, **kw):
    lead = x.shape[:-1]
    y = _matmul(x.reshape(-1, x.shape[-1]).astype(BF16), w_bf16, **kw)
    return y.reshape(lead + (w_bf16.shape[1],))


def _split(x, sizes):
    return jnp.split(x, np.cumsum(sizes)[:-1].tolist(), axis=-1)


def _layernorm(x, g, b):
    mu = jnp.mean(x, -1, keepdims=True)
    var = jnp.mean(jnp.square(x - mu), -1, keepdims=True)
    return (x - mu) * lax.rsqrt(var + LN_EPS) * g + b


def _rmsnorm(x, w):
    return x * lax.rsqrt(jnp.mean(x * x, -1, keepdims=True) + NORM_EPS) * w


def _masked_softmax(s, mask):
    s = jnp.where(mask, s.astype(F32), -jnp.inf)
    m = jnp.max(s, -1, keepdims=True)
    m = jnp.where(jnp.isfinite(m), m, 0.0)
    e = jnp.exp(s - m)
    d = jnp.sum(e, -1, keepdims=True)
    return e / jnp.where(d > 0, d, 1.0)


def _t5_bucket(dist):
    n = jnp.maximum(dist, 0)
    max_exact = N_BUCKETS // 2
    nf = jnp.maximum(n, 1).astype(F32)
    large = max_exact + (jnp.log(nf / max_exact) / math.log(MAX_DISTANCE / max_exact) * (N_BUCKETS - max_exact)).astype(jnp.int32)
    large = jnp.minimum(large, N_BUCKETS - 1)
    return jnp.where(n < max_exact, n, large)


def _ssd_scan(x, dt, a, bm, cm, h0):
    b, l, h, p = x.shape
    g, n = bm.shape[2], bm.shape[3]
    r = h // g
    lc = min(SSD_CHUNK, l)
    pad = (-l) % lc
    x, dt, bm, cm = [jnp.pad(t.astype(F32), ((0, 0), (0, pad)) + ((0, 0),) * (t.ndim - 2)) for t in (x, dt, bm, cm)]
    c = (l + pad) // lc
    xdt = (x * dt[..., None]).reshape(b, c, lc, g, r, p)
    acs = jnp.cumsum((dt * a).reshape(b, c, lc, g, r), axis=2)
    bm = bm.reshape(b, c, lc, g, n)
    cm = cm.reshape(b, c, lc, g, n)
    causal = np.tril(np.ones((lc, lc), bool))[None, None, :, :, None, None]
    decay = jnp.exp(jnp.where(causal, acs[:, :, :, None] - acs[:, :, None], -jnp.inf))
    cb = jnp.einsum('bctgn,bcsgn->bctsg', cm, bm)
    y_diag = jnp.einsum('bctsg,bctsgr,bcsgrp->bctgrp', cb, decay, xdt)
    decay_st = jnp.exp(acs[:, :, -1:] - acs)
    states = jnp.einsum('bcsgn,bcsgr,bcsgrp->bcgrpn', bm, decay_st, xdt)
    chunk_decay = jnp.exp(acs[:, :, -1])

    def step(hc, inp):
        st, dec = inp
        return hc * dec[..., None, None] + st, hc

    h_last, h_prev = lax.scan(step, h0.astype(F32).reshape(b, g, r, p, n), (jnp.moveaxis(states, 1, 0), jnp.moveaxis(chunk_decay, 1, 0)))
    h_prev = jnp.moveaxis(h_prev, 0, 1)
    y_off = jnp.einsum('bctgn,bcgrpn,bctgr->bctgrp', cm, h_prev, jnp.exp(acs))
    y = (y_diag + y_off).reshape(b, c * lc, h, p)[:, :l]
    return y, h_last.reshape(b, h, p, n)


def _ssd_mixer(z, xbc, dt, conv0, h0, conv_w, conv_b, dt_bias, a_log, d_skip, norm_w):
    b, l, _ = xbc.shape
    xpad = jnp.concatenate([conv0.astype(xbc.dtype), xbc], axis=1)
    conv = sum((xpad[:, k:k + l] * conv_w[k] for k in range(SSD_CONV)), conv_b)
    conv_new = xpad[:, -(SSD_CONV - 1):]
    xbc = jax.nn.silu(conv)
    xs, bm, cm = _split(xbc, (D_SSD, SSD_GROUPS * SSD_STATE, SSD_GROUPS * SSD_STATE))
    xs = xs.reshape(b, l, SSD_HEADS, SSD_HEAD_DIM)
    bm = bm.reshape(b, l, SSD_GROUPS, SSD_STATE)
    cm = cm.reshape(b, l, SSD_GROUPS, SSD_STATE)
    dt = jax.nn.softplus((dt + dt_bias).astype(F32))
    a = -jnp.exp(a_log.astype(F32))
    y, h_new = _ssd_scan(xs, dt, a, bm, cm, h0)
    y = y + xs.astype(F32) * d_skip.astype(F32)[:, None]
    y = (y.reshape(b, l, D_SSD) * jax.nn.silu(z.astype(F32))).reshape(b, l, SSD_GROUPS, D_SSD // SSD_GROUPS)
    y = _rmsnorm(y, norm_w.reshape(SSD_GROUPS, D_SSD // SSD_GROUPS)).reshape(b, l, D_SSD)
    return y, conv_new, h_new


def _gla_scan(q, k, v, logf, s0):
    b, l, h, dk = q.shape
    dv = v.shape[-1]
    lc = min(GLA_CHUNK, l)
    pad = (-l) % lc
    q, k, v, logf = [jnp.pad(t.astype(F32), ((0, 0), (0, pad), (0, 0), (0, 0))) for t in (q, k, v, logf)]
    c = (l + pad) // lc
    q, k, logf = [t.reshape(b, c, lc, h, dk) for t in (q, k, logf)]
    v = v.reshape(b, c, lc, h, dv)
    bc = jnp.cumsum(logf, axis=2)
    causal = np.tril(np.ones((lc, lc), bool))[None, None, :, :, None, None]
    rel = jnp.exp(jnp.where(causal, bc[:, :, :, None] - bc[:, :, None], -jnp.inf))
    att = jnp.einsum('bcthd,bcshd,bctshd->bchts', q, k, rel)
    o_intra = jnp.einsum('bchts,bcshv->bcthv', att, v)
    k_st = k * jnp.exp(bc[:, :, -1:] - bc)
    chunk_states = jnp.einsum('bcshd,bcshv->bchdv', k_st, v)
    chunk_decay = jnp.exp(bc[:, :, -1])

    def step(s, inp):
        st, dec = inp
        return s * dec[..., None] + st, s

    s_last, s_prev = lax.scan(step, s0.astype(F32), (jnp.moveaxis(chunk_states, 1, 0), jnp.moveaxis(chunk_decay, 1, 0)))
    s_prev = jnp.moveaxis(s_prev, 0, 1)
    o_inter = jnp.einsum('bcthd,bchdv->bcthv', q * jnp.exp(bc), s_prev)
    o = (o_intra + o_inter).reshape(b, c * lc, h, dv)[:, :l]
    return o, s_last


def _gla_mixer(q, k, v, g, a_lr, s0, gate_w2, gate_b, norm_w):
    b, l, _ = q.shape
    q = q.reshape(b, l, GLA_HEADS, GLA_DK) * (GLA_DK ** -0.5)
    k = k.reshape(b, l, GLA_HEADS, GLA_DK)
    v = v.reshape(b, l, GLA_HEADS, GLA_DV)
    logf = jax.nn.log_sigmoid((a_lr @ gate_w2 + gate_b).astype(F32)) / GLA_TAU
    o, s_new = _gla_scan(q, k, v, logf.reshape(b, l, GLA_HEADS, GLA_DK), s0)
    o = _rmsnorm(o, norm_w).reshape(b, l, D_GLA) * jax.nn.silu(g.astype(F32))
    return o, s_new


def _nsa_attend(q, gates, kv_full, win_ctx, q_pos0, w_pos0, cmp_pe, cmp_w, rel_bias):
    b, lq = q.shape[:2]
    t_len = kv_full.shape[1]
    G, R, HD = NSA_KV_HEADS, NSA_REP, NSA_HEAD_DIM
    kc, vc, ks, vs = [kv_full[:, :, i] for i in range(4)]
    n_cmp = (t_len - CMP_LEN) // CMP_STRIDE + 1
    cidx = (np.arange(n_cmp)[:, None] * CMP_STRIDE + np.arange(CMP_LEN)[None]).astype(np.int32)
    k_cmp = jnp.einsum('bnlgd,lde->bnge', kc[:, cidx] + cmp_pe[0][:, None], cmp_w[0]).astype(F32)
    v_cmp = jnp.einsum('bnlgd,lde->bnge', vc[:, cidx] + cmp_pe[1][:, None], cmp_w[1]).astype(F32)
    cmp_end = (np.arange(n_cmp) * CMP_STRIDE + CMP_LEN - 1).astype(np.int32)
    n_sel = -(-t_len // SEL_LEN)
    sel_pad = n_sel * SEL_LEN - t_len

    def sel_blocks(t):
        t = jnp.pad(t, ((0, 0), (0, sel_pad), (0, 0), (0, 0)))
        return t.reshape(b, n_sel, SEL_LEN, G, HD).transpose(0, 3, 1, 2, 4)

    ks_b, vs_b = sel_blocks(ks), sel_blocks(vs)
    c_lo = np.arange(n_cmp) * CMP_STRIDE
    s_lo = np.arange(n_sel) * SEL_LEN
    cover = jnp.asarray(((c_lo[:, None] < s_lo[None] + SEL_LEN) & (c_lo[:, None] + CMP_LEN > s_lo[None])).astype(np.float32))
    top_n = min(TOP_N, n_sel)
    kw_pad = jnp.pad(win_ctx[:, :, 0], ((0, 0), (WINDOW, 0), (0, 0), (0, 0)))
    vw_pad = jnp.pad(win_ctx[:, :, 1], ((0, 0), (WINDOW, 0), (0, 0), (0, 0)))
    qb = Q_BLOCK if lq % Q_BLOCK == 0 else lq
    nqb = lq // qb
    scale = HD ** -0.5
    bias_gr = rel_bias.reshape(N_BUCKETS, G, R)
    bi = jnp.arange(b)[:, None, None, None]
    gi = jnp.arange(G)[None, None, :, None]
    gi5 = jnp.arange(G)[None, None, :, None, None]
    blk = jnp.arange(n_sel)

    def block(args):
        i, qblk, gblk = args
        q0 = q_pos0 + i * qb
        tpos = q0 + jnp.arange(qb)
        qg = qblk.reshape(b, qb, G, R, HD).astype(F32) * scale
        bias_c = rel_bias[_t5_bucket(tpos[:, None] - cmp_end[None])].astype(F32).reshape(qb, n_cmp, G, R).transpose(0, 2, 3, 1)
        s_c = jnp.einsum('bqgrd,bngd->bqgrn', qg, k_cmp) + bias_c
        p_c = _masked_softmax(s_c, (cmp_end[None] <= tpos[:, None])[None, :, None, None, :])
        o_c = jnp.einsum('bqgrn,bngd->bqgrd', p_c, v_cmp)
        imp = jnp.einsum('bqgrn,nj->bqgj', p_c, cover)
        cur = (tpos // SEL_LEN)[:, None]
        forced = (blk == 0) | (blk == cur) | (blk == cur - 1)
        valid = blk * SEL_LEN <= tpos[:, None]
        score = jnp.where(forced[None, :, None, :], FORCE_SCORE, jnp.where(valid[None, :, None, :], imp, -FORCE_SCORE))
        _, sel = lax.top_k(score, top_n)
        k_sel = ks_b[bi, gi, sel].astype(F32)
        v_sel = vs_b[bi, gi, sel].astype(F32)
        spos = sel[..., None] * SEL_LEN + jnp.arange(SEL_LEN)
        dist_s = tpos[None, :, None, None, None] - spos
        bias_s = jnp.moveaxis(bias_gr[_t5_bucket(dist_s), gi5].astype(F32), -1, 3)
        s_s = jnp.einsum('bqgrd,bqgksd->bqgrks', qg, k_sel) + bias_s
        p_s = _masked_softmax(s_s.reshape(b, qb, G, R, top_n * SEL_LEN), (dist_s >= 0).reshape(b, qb, G, 1, top_n * SEL_LEN)).reshape(s_s.shape)
        o_s = jnp.einsum('bqgrks,bqgksd->bqgrd', p_s, v_sel)
        start = q0 - w_pos0
        k_w = lax.dynamic_slice_in_dim(kw_pad, start, WINDOW + qb, axis=1).astype(F32)
        v_w = lax.dynamic_slice_in_dim(vw_pad, start, WINDOW + qb, axis=1).astype(F32)
        wpos = q0 - WINDOW + jnp.arange(WINDOW + qb)
        dist_w = tpos[:, None] - wpos[None]
        mask_w = (dist_w >= 0) & (dist_w < WINDOW) & (wpos[None] >= w_pos0)
        bias_w = rel_bias[_t5_bucket(dist_w)].astype(F32).reshape(qb, WINDOW + qb, G, R).transpose(0, 2, 3, 1)
        s_w = jnp.einsum('bqgrd,bkgd->bqgrk', qg, k_w) + bias_w
        p_w = _masked_softmax(s_w, mask_w[None, :, None, None, :])
        o_w = jnp.einsum('bqgrk,bkgd->bqgrd', p_w, v_w)
        gg = gblk.reshape(b, qb, G, R, 3).astype(F32)
        return gg[..., 0:1] * o_c + gg[..., 1:2] * o_s + gg[..., 2:3] * o_w

    q_blocks = jnp.swapaxes(q.reshape(b, nqb, qb, NSA_HEADS, HD), 0, 1)
    g_blocks = jnp.swapaxes(gates.reshape(b, nqb, qb, NSA_HEADS, 3), 0, 1)
    outs = lax.map(block, (jnp.arange(nqb), q_blocks, g_blocks))
    return jnp.moveaxis(outs, 0, 1).reshape(b, lq, D_NSA)


def _mixer(x, lp, rel_bias, conv0, ssd_h0, gla_s0, nsa_past, win_past, past_len):
    b, l, _ = x.shape
    parts = _split(_proj(x, lp['w_in'], name='in_proj')[..., :D_IN], IN_SPLITS)
    z, xbc, dt, q_n = parts[0], parts[1], parts[2], parts[3]
    kv6 = parts[4:10]
    g_n = parts[10]
    q_g, k_g, v_g, g_g, a_g = parts[11], parts[12], parts[13], parts[14], parts[15]
    y_ssd, conv_new, h_new = _ssd_mixer(z, xbc, dt, conv0, ssd_h0, lp['ssd_conv_w'], lp['ssd_conv_b'], lp['ssd_dt_bias'], lp['ssd_a_log'], lp['ssd_d'], lp['ssd_norm_w'])
    kvr = [t.reshape(b, l, NSA_KV_HEADS, NSA_HEAD_DIM) for t in kv6]
    rows = jnp.stack(kvr[:4], axis=2)
    win_rows = jnp.stack(kvr[4:], axis=2)
    if nsa_past is None:
        kv_full, win_ctx, w_pos0 = rows, win_rows, 0
        buf_len = min(WINDOW, l)
    else:
        kv_full = jnp.concatenate([nsa_past.astype(rows.dtype), rows], axis=1)
        win_ctx = jnp.concatenate([win_past.astype(rows.dtype), win_rows], axis=1)
        w_pos0 = past_len - win_past.shape[1]
        buf_len = win_past.shape[1]
    gates = jax.nn.sigmoid(g_n.astype(F32)).reshape(b, l, NSA_HEADS, 3)
    y_nsa = _nsa_attend(q_n.reshape(b, l, NSA_HEADS, NSA_HEAD_DIM), gates, kv_full, win_ctx, past_len, w_pos0, lp['nsa_cmp_pe'], lp['nsa_cmp_w'], rel_bias)
    y_gla, s_new = _gla_mixer(q_g, k_g, v_g, g_g, a_g, gla_s0, lp['gla_gate_w2'], lp['gla_gate_b'], lp['gla_norm_w'])
    y = _proj(jnp.concatenate([y_ssd, y_nsa, y_gla], axis=-1), lp['w_out'], name='out_proj')
    return y, rows, win_ctx[:, -buf_len:], conv_new, h_new, s_new


def _cross_attn(x, mem_kv, wq, wo):
    b, l, _ = x.shape
    q = _proj(x, wq, name='xattn_q').reshape(b, l, X_HEADS, X_HEAD_DIM) * (X_HEAD_DIM ** -0.5)
    s = jnp.einsum('blhd,bmhd->bhlm', q, mem_kv[:, :, 0].astype(F32))
    p = jax.nn.softmax(s, axis=-1)
    o = jnp.einsum('bhlm,bmhd->blhd', p, mem_kv[:, :, 1].astype(F32)).reshape(b, l, D_MODEL)
    return _proj(o, wo, name='xattn_o')


def _sqrelu_ffn(x, w1, w2):
    h = _proj(x, w1, out_dtype=BF16, act='sqrelu', name='ffn_up')
    return _proj(h, w2, name='ffn_down')


def _layer(x, lp, rel_bias, mem_kv, conv0, ssd_h0, gla_s0, nsa_past, win_past, past_len):
    h, rows, win, conv, hs, sg = _mixer(x, lp, rel_bias, conv0, ssd_h0, gla_s0, nsa_past, win_past, past_len)
    x = _layernorm(DN_ALPHA * x + h, lp['ln_g'][0], lp['ln_b'][0])
    x = _layernorm(DN_ALPHA * x + _cross_attn(x, mem_kv, lp['x_wq'], lp['x_wo']), lp['ln_g'][1], lp['ln_b'][1])
    x = _layernorm(DN_ALPHA * x + _sqrelu_ffn(x, lp['ffn_w1'], lp['ffn_w2']), lp['ln_g'][2], lp['ln_b'][2])
    return x, (rows, win, conv, hs, sg)


def _pad_cols(w, mult):
    pad = (-w.shape[-1]) % mult
    return jnp.pad(w, ((0, 0), (0, pad))) if pad else w


def kernel(x_prompt, x_sample, cache_nsa_kv, cache_nsa_win, state_ssd, state_ssd_conv, state_gla, cache_mem_kv,
           page_table, mem_prompt, w_in, ssd_conv_w, ssd_conv_b, ssd_dt_bias, ssd_a_log, ssd_d, ssd_norm_w,
           nsa_cmp_pe, nsa_cmp_w, rel_bias, gla_gate_w2, gla_gate_b, gla_norm_w, w_out, x_wq, x_wkv, x_wo,
           ffn_w1, ffn_w2, ln_g, ln_b):
    bp = x_prompt.shape[0]
    bs = x_sample.shape[0]
    past_len = page_table.shape[1] * cache_nsa_kv.shape[2]
    xp, xs = x_prompt, x_sample
    st_p, st_s, mem_p = [], [], []
    for l in range(DEPTH):
        lp = dict(w_in=_pad_cols(w_in[l].astype(BF16), 512), ssd_conv_w=ssd_conv_w[l], ssd_conv_b=ssd_conv_b[l],
                  ssd_dt_bias=ssd_dt_bias[l], ssd_a_log=ssd_a_log[l], ssd_d=ssd_d[l], ssd_norm_w=ssd_norm_w[l],
                  nsa_cmp_pe=nsa_cmp_pe[l], nsa_cmp_w=nsa_cmp_w[l], gla_gate_w2=gla_gate_w2[l], gla_gate_b=gla_gate_b[l],
                  gla_norm_w=gla_norm_w[l], w_out=w_out[l].astype(BF16), x_wq=x_wq[l].astype(BF16),
                  x_wo=x_wo[l].astype(BF16), ffn_w1=ffn_w1[l].astype(BF16), ffn_w2=ffn_w2[l].astype(BF16),
                  ln_g=ln_g[l], ln_b=ln_b[l])
        mem_kv_p = _proj(mem_prompt, x_wkv[l].astype(BF16), name='mem_kv').reshape(bp, N_MEM, 2, X_HEADS, X_HEAD_DIM)
        conv0 = jnp.zeros((bp, SSD_CONV - 1, SSD_CONV_DIM), xp.dtype)
        h0 = jnp.zeros((bp, SSD_HEADS, SSD_HEAD_DIM, SSD_STATE), F32)
        s0 = jnp.zeros((bp, GLA_HEADS, GLA_DK, GLA_DV), F32)
        xp, stp = _layer(xp, lp, rel_bias, mem_kv_p, conv0, h0, s0, None, None, 0)
        st_p.append(stp)
        mem_p.append(mem_kv_p)
        past = cache_nsa_kv[l][page_table].reshape(bs, past_len, 4, NSA_KV_HEADS, NSA_HEAD_DIM)
        xs, sts = _layer(xs, lp, rel_bias, cache_mem_kv[l], state_ssd_conv[l], state_ssd[l], state_gla[l], past,
                         cache_nsa_win[l], past_len)
        st_s.append(sts)
    p_rows, p_win, p_conv, p_ssd, p_gla = [jnp.stack(s) for s in zip(*st_p)]
    s_rows, s_win, s_conv, s_ssd, s_gla = [jnp.stack(s) for s in zip(*st_s)]
    p_mem = jnp.stack(mem_p)
    return (xp, xs, p_rows, s_rows, p_win, s_win, p_ssd, s_ssd, p_conv, s_conv, p_gla, s_gla, p_mem)
```

```python
import functools
import math

import jax
import jax.numpy as jnp
import numpy as np
from jax import lax
from jax.experimental import pallas as pl
from jax.experimental.pallas import tpu as pltpu

F32 = jnp.float32
BF16 = jnp.bfloat16

D_MODEL = 4096
DEPTH = 2
PAGE_SIZE = 128
D_MIX = D_MODEL
D_SSD = D_MIX // 2
SSD_HEAD_DIM = 64
SSD_HEADS = D_SSD // SSD_HEAD_DIM
SSD_GROUPS = 8
SSD_STATE = 128
SSD_CONV = 4
SSD_CHUNK = 128
SSD_CONV_DIM = D_SSD + 2 * SSD_GROUPS * SSD_STATE
D_NSA = D_MIX // 4
NSA_HEAD_DIM = 128
NSA_HEADS = D_NSA // NSA_HEAD_DIM
NSA_KV_HEADS = 2
NSA_REP = NSA_HEADS // NSA_KV_HEADS
CMP_LEN = 32
CMP_STRIDE = 16
SEL_LEN = 64
TOP_N = 16
WINDOW = 512
Q_BLOCK = 128
FORCE_SCORE = 1e4
D_GLA = D_MIX - D_SSD - D_NSA
GLA_HEADS = 4
GLA_DV = D_GLA // GLA_HEADS
GLA_DK = GLA_DV // 2
GLA_RANK = 16
GLA_TAU = 16.0
GLA_CHUNK = 16
N_MEM = 256
X_HEADS = 4
X_HEAD_DIM = D_MODEL // X_HEADS
D_FF = 4 * D_MODEL
N_BUCKETS = 32
MAX_DISTANCE = 128
LN_EPS = 1e-5
NORM_EPS = 1e-6
DN_ALPHA = (2 * DEPTH) ** 0.25
IN_SPLITS = (D_SSD, SSD_CONV_DIM, SSD_HEADS, NSA_HEADS * NSA_HEAD_DIM) + (NSA_KV_HEADS * NSA_HEAD_DIM,) * 6 + (
    3 * NSA_HEADS, GLA_HEADS * GLA_DK, GLA_HEADS * GLA_DK, D_GLA, D_GLA, GLA_RANK)
D_IN = sum(IN_SPLITS)

LANE = 128
VMEM_LIMIT_BYTES = 56 * 1024 * 1024

KVW = NSA_KV_HEADS * NSA_HEAD_DIM
Z_OFF = 0
XBC_OFF = Z_OFF + D_SSD
QN_OFF = XBC_OFF + SSD_CONV_DIM
KV_OFF = QN_OFF + NSA_HEADS * NSA_HEAD_DIM
QG_OFF = KV_OFF + 6 * KVW
KG_OFF = QG_OFF + GLA_HEADS * GLA_DK
VG_OFF = KG_OFF + GLA_HEADS * GLA_DK
GG_OFF = VG_OFF + D_GLA
SMALL_OFF = GG_OFF + D_GLA
DT_OFF = SMALL_OFF
ALR_OFF = SMALL_OFF + SSD_HEADS
GATE_OFF = SMALL_OFF + LANE
D_IN_PAD = 12288
NEG = -1e30


def _in_proj_column_map():
    off = np.cumsum((0,) + IN_SPLITS)
    z, xbc, dt, qn = off[0], off[1], off[2], off[3]
    kv0, gn, qg, kg, vg, gg, alr = off[4], off[10], off[11], off[12], off[13], off[14], off[15]
    cmap = np.full((D_IN_PAD,), -1, np.int64)
    cmap[Z_OFF:Z_OFF + D_SSD] = z + np.arange(D_SSD)
    cmap[XBC_OFF:XBC_OFF + SSD_CONV_DIM] = xbc + np.arange(SSD_CONV_DIM)
    cmap[QN_OFF:QN_OFF + D_NSA] = qn + np.arange(D_NSA)
    cmap[KV_OFF:KV_OFF + 6 * KVW] = kv0 + np.arange(6 * KVW)
    cmap[QG_OFF:QG_OFF + 512] = qg + np.arange(512)
    cmap[KG_OFF:KG_OFF + 512] = kg + np.arange(512)
    cmap[VG_OFF:VG_OFF + D_GLA] = vg + np.arange(D_GLA)
    cmap[GG_OFF:GG_OFF + D_GLA] = gg + np.arange(D_GLA)
    cmap[DT_OFF:DT_OFF + SSD_HEADS] = dt + np.arange(SSD_HEADS)
    cmap[ALR_OFF:ALR_OFF + GLA_RANK] = alr + np.arange(GLA_RANK)
    for g in range(NSA_KV_HEADS):
        cmap[GATE_OFF + g * LANE:GATE_OFF + g * LANE + 3 * NSA_REP] = gn + g * 3 * NSA_REP + np.arange(3 * NSA_REP)
    return cmap


def _reorder_w_in(w):
    cmap = _in_proj_column_map()
    wb = jnp.take(w.astype(BF16), jnp.asarray(np.maximum(cmap, 0), jnp.int32), axis=1)
    return jnp.where(jnp.asarray(cmap >= 0)[None, :], wb, jnp.zeros((), BF16))


def _dot_nt(a, b):
    return lax.dot_general(a, b, (((1,), (1,)), ((), ())), preferred_element_type=F32)


NSA_L = 2048
NSA_QB = Q_BLOCK
NSA_NQB = NSA_L // NSA_QB
NSA_NCP = NSA_L // CMP_STRIDE
NSA_NSEL = NSA_L // SEL_LEN
CMP_HALF = CMP_STRIDE * NSA_HEAD_DIM


def _t5_bucket_np(dist):
    n = np.maximum(dist, 0)
    max_exact = N_BUCKETS // 2
    nf = np.maximum(n, 1).astype(np.float32)
    large = max_exact + (np.log(nf / max_exact) / np.float32(math.log(MAX_DISTANCE / max_exact)) * (N_BUCKETS - max_exact)).astype(np.int32)
    return np.where(n < max_exact, n, np.minimum(large, N_BUCKETS - 1)).astype(np.int32)


def _nsa_prompt_bias_tables(rel_bias):
    tq = np.arange(NSA_QB)[:, None]
    ts = np.arange(NSA_QB)[None, :]

    def table(idx):
        t = rel_bias[jnp.asarray(idx)]
        t = jnp.moveaxis(t, -1, 0).reshape((NSA_KV_HEADS, NSA_REP) + idx.shape)
        t = jnp.moveaxis(t, 1, -3)
        return t.reshape(t.shape[:-3] + (NSA_REP * NSA_QB, idx.shape[-1]))

    b_diag = table(_t5_bucket_np(tq - ts))
    b_prev = table(_t5_bucket_np(NSA_QB + tq - ts))
    b_far = table(np.full((NSA_QB, NSA_QB), N_BUCKETS - 1, np.int32))
    i = np.arange(NSA_NQB)[:, None, None]
    n = np.arange(NSA_NCP)[None, None, :]
    b_cmp = table(_t5_bucket_np(i * NSA_QB + tq[None] - (n * CMP_STRIDE + CMP_LEN - 1)))
    return b_diag, b_prev, b_far, b_cmp


def _nsa_compress_kernel(a_ref, pe_ref, w_ref, o_ref):
    a = a_ref[0, 0, 0]
    lo = jnp.dot((a + pe_ref[0, 0]).astype(BF16), w_ref[0, 0], preferred_element_type=F32)
    hi = jnp.dot((a + pe_ref[0, 1]).astype(BF16), w_ref[0, 1], preferred_element_type=F32)
    o_ref[0, 0, 0] = lo + pltpu.roll(hi, NSA_NCP - 1, 0)


def _nsa_compress_prompt(y, cmp_pe, cmp_w, bsz):
    a = y[:, KV_OFF:KV_OFF + 2 * KVW].reshape(bsz, NSA_NCP, CMP_STRIDE, 2, NSA_KV_HEADS, NSA_HEAD_DIM)
    a = a.transpose(0, 3, 4, 1, 2, 5).reshape(bsz, 2, NSA_KV_HEADS, NSA_NCP, CMP_HALF)
    pe = cmp_pe.reshape(2, 2, 1, CMP_HALF)
    w = cmp_w.astype(BF16).reshape(2, 2, CMP_HALF, NSA_HEAD_DIM)
    return pl.pallas_call(
        _nsa_compress_kernel,
        grid=(bsz, 2, NSA_KV_HEADS),
        in_specs=[pl.BlockSpec((1, 1, 1, NSA_NCP, CMP_HALF), lambda b, c, g: (b, c, g, 0, 0)),
                  pl.BlockSpec((1, 2, 1, CMP_HALF), lambda b, c, g: (c, 0, 0, 0)),
                  pl.BlockSpec((1, 2, CMP_HALF, NSA_HEAD_DIM), lambda b, c, g: (c, 0, 0, 0))],
        out_specs=pl.BlockSpec((1, 1, 1, NSA_NCP, NSA_HEAD_DIM), lambda b, c, g: (b, c, g, 0, 0)),
        out_shape=jax.ShapeDtypeStruct((bsz, 2, NSA_KV_HEADS, NSA_NCP, NSA_HEAD_DIM), F32),
        compiler_params=pltpu.CompilerParams(dimension_semantics=('parallel', 'parallel', 'parallel')),
        name='nsa_compress',
    )(a, pe, w)


def _nsa_prompt_kernel(q_ref, gate_ref, cmp_ref, ks_ref, vs_ref, kw_ref, vw_ref, bdiag_ref, bprev_ref, bfar_ref,
                       bcmp_ref, covt_ref, expand_ref, o_ref, m_ref, l_ref, acc_ref):
    i = pl.program_id(2)
    rows = NSA_REP * NSA_QB
    qblk = q_ref[...] * (NSA_HEAD_DIM ** -0.5)
    q = jnp.concatenate([qblk[:, r * LANE:(r + 1) * LANE] for r in range(NSA_REP)], axis=0).astype(BF16)
    tq = lax.broadcasted_iota(jnp.int32, (rows, LANE), 0) % NSA_QB
    lane = lax.broadcasted_iota(jnp.int32, (rows, LANE), 1)
    t_abs = i * NSA_QB + tq

    k_c = cmp_ref[0, 0, 0].astype(BF16)
    v_c = cmp_ref[0, 1, 0].astype(BF16)
    mask_c = lane * CMP_STRIDE + (CMP_LEN - 1) <= t_abs
    s_c = jnp.where(mask_c, _dot_nt(q, k_c) + bcmp_ref[0, 0], NEG)
    e_c = jnp.where(mask_c, jnp.exp(s_c - jnp.max(s_c, -1, keepdims=True)), 0.0)
    d_c = jnp.sum(e_c, -1, keepdims=True)
    p_c = (e_c / jnp.where(d_c > 0, d_c, 1.0)).astype(BF16)
    o_c = jnp.dot(p_c, v_c, preferred_element_type=F32)

    imp4 = _dot_nt(covt_ref[...], p_c)
    imp = sum(imp4[:, r * NSA_QB:(r + 1) * NSA_QB] for r in range(NSA_REP))
    blk = lax.broadcasted_iota(jnp.int32, (LANE, NSA_QB), 0)
    t_row = i * NSA_QB + lax.broadcasted_iota(jnp.int32, (LANE, NSA_QB), 1)
    cur = t_row // SEL_LEN
    forced = (blk == 0) | (blk == cur) | (blk == cur - 1)
    score = jnp.where(forced, FORCE_SCORE, jnp.where(blk * SEL_LEN <= t_row, imp, -FORCE_SCORE))
    score = jnp.where(blk < NSA_NSEL, score, -3.0 * FORCE_SCORE)
    rank = jnp.zeros((LANE, NSA_QB), jnp.int32)
    for j in range(NSA_NSEL):
        row = score[j:j + 1, :]
        rank = rank + jnp.where((row > score) | ((row == score) & (blk > j)), 1, 0)
    sel_t = jnp.where((rank < TOP_N) & (blk < NSA_NSEL), 1.0, 0.0)
    sel = sel_t.T.astype(BF16)

    def init():
        m_ref[...] = jnp.full((rows, 1), NEG, F32)
        l_ref[...] = jnp.zeros((rows, 1), F32)
        acc_ref[...] = jnp.zeros((rows, NSA_HEAD_DIM), F32)

    def step(k_ref, v_ref, c, bias, mask):
        start = pl.multiple_of(c * NSA_QB, NSA_QB)
        k = k_ref[pl.ds(start, NSA_QB), :].astype(BF16)
        v = v_ref[pl.ds(start, NSA_QB), :].astype(BF16)
        s = _dot_nt(q, k) + bias
        if mask is not None:
            s = jnp.where(mask, s, NEG)
        m_old = m_ref[...]
        m_new = jnp.maximum(m_old, jnp.max(s, -1, keepdims=True))
        p = jnp.exp(s - m_new)
        if mask is not None:
            p = jnp.where(mask, p, 0.0)
        alpha = jnp.exp(m_old - m_new)
        l_ref[...] = alpha * l_ref[...] + jnp.sum(p, -1, keepdims=True)
        acc_ref[...] = alpha * acc_ref[...] + jnp.dot(p.astype(BF16), v, preferred_element_type=F32)
        m_ref[...] = m_new

    def finish():
        l = l_ref[...]
        return acc_ref[...] / jnp.where(l > 0, l, 1.0)

    def sel_mask(c):
        m1 = jnp.dot(sel, expand_ref[c], preferred_element_type=F32)
        return jnp.concatenate([m1] * NSA_REP, axis=0) > 0.5

    init()
    bfar = bfar_ref[0]

    def far_body(c, carry):
        step(ks_ref, vs_ref, c, bfar, sel_mask(c))
        return carry

    lax.fori_loop(0, jnp.maximum(i - 1, 0), far_body, 0)

    @pl.when(i >= 1)
    def _():
        step(ks_ref, vs_ref, i - 1, bprev_ref[0], sel_mask(i - 1))

    causal = lane <= tq
    step(ks_ref, vs_ref, i, bdiag_ref[0], sel_mask(i) & causal)
    o_s = finish()

    init()
    step(kw_ref, vw_ref, i, bdiag_ref[0], causal)

    @pl.when(i >= 1)
    def _():
        step(kw_ref, vw_ref, i - 1, bprev_ref[0], None)

    for d in range(2, WINDOW // NSA_QB):
        @pl.when(i >= d)
        def _(d=d):
            step(kw_ref, vw_ref, i - d, bfar, None)

    @pl.when(i >= WINDOW // NSA_QB)
    def _():
        step(kw_ref, vw_ref, i - WINDOW // NSA_QB, bfar, lane > tq)

    o_w = finish()

    gates = jax.nn.sigmoid(gate_ref[...])
    for r in range(NSA_REP):
        rs = slice(r * NSA_QB, (r + 1) * NSA_QB)
        o_ref[:, r * LANE:(r + 1) * LANE] = (gates[:, 3 * r:3 * r + 1] * o_c[rs] + gates[:, 3 * r + 1:3 * r + 2] * o_s[rs]
                                             + gates[:, 3 * r + 2:3 * r + 3] * o_w[rs])


def _nsa_prompt(y, k_v_cmp, tables, bsz):
    b_diag, b_prev, b_far, b_cmp = tables
    rows = NSA_REP * NSA_QB
    c_lo = np.arange(NSA_NCP) * CMP_STRIDE
    s_lo = np.arange(LANE) * SEL_LEN
    cov_t = ((c_lo[None] < s_lo[:, None] + SEL_LEN) & (c_lo[None] + CMP_LEN > s_lo[:, None])
             & (np.arange(NSA_NCP)[None] < NSA_NCP - 1) & (np.arange(LANE)[:, None] < NSA_NSEL))
    key_blk = (np.arange(NSA_L) // SEL_LEN).reshape(NSA_NQB, 1, NSA_QB)
    expand = (np.arange(LANE)[None, :, None] == key_blk)
    kv_blk = KV_OFF // LANE
    grp = NSA_KV_HEADS

    def col(slot):
        return pl.BlockSpec((NSA_L, NSA_HEAD_DIM), lambda b, g, i: (b, kv_blk + slot * grp + g))

    tab = pl.BlockSpec((1, rows, LANE), lambda b, g, i: (g, 0, 0))
    return pl.pallas_call(
        _nsa_prompt_kernel,
        grid=(bsz, NSA_KV_HEADS, NSA_NQB),
        in_specs=[pl.BlockSpec((NSA_QB, NSA_REP * LANE), lambda b, g, i: (b * NSA_NQB + i, QN_OFF // (NSA_REP * LANE) + g)),
                  pl.BlockSpec((NSA_QB, LANE), lambda b, g, i: (b * NSA_NQB + i, GATE_OFF // LANE + g)),
                  pl.BlockSpec((1, 2, 1, NSA_NCP, NSA_HEAD_DIM), lambda b, g, i: (b, 0, g, 0, 0)),
                  col(2), col(3), col(4), col(5), tab, tab, tab,
                  pl.BlockSpec((1, 1, rows, NSA_NCP), lambda b, g, i: (g, i, 0, 0)),
                  pl.BlockSpec((LANE, NSA_NCP), lambda b, g, i: (0, 0)),
                  pl.BlockSpec((NSA_NQB, LANE, NSA_QB), lambda b, g, i: (0, 0, 0))],
        out_specs=pl.BlockSpec((NSA_QB, NSA_REP * LANE), lambda b, g, i: (b * NSA_NQB + i, g)),
        out_shape=jax.ShapeDtypeStruct((bsz * NSA_L, D_NSA), F32),
        scratch_shapes=[pltpu.VMEM((rows, 1), F32), pltpu.VMEM((rows, 1), F32), pltpu.VMEM((rows, NSA_HEAD_DIM), F32)],
        compiler_params=pltpu.CompilerParams(dimension_semantics=('parallel', 'parallel', 'arbitrary'),
                                             vmem_limit_bytes=VMEM_LIMIT_BYTES),
        name='nsa_prompt',
    )(y, y, k_v_cmp, y, y, y, y, b_diag, b_prev, b_far, b_cmp,
      jnp.asarray(cov_t, BF16), jnp.asarray(expand, BF16))


def _mm_kernel(x_ref, w_ref, o_ref, *scratch, nk, act):
    def finish(acc):
        if act == 'sqrelu':
            acc = jnp.square(jnp.maximum(acc, 0.0))
        o_ref[...] = acc.astype(o_ref.dtype)

    if nk == 1:
        finish(jnp.dot(x_ref[...], w_ref[...], preferred_element_type=F32))
        return
    acc_ref, = scratch
    k = pl.program_id(2)
    part = jnp.dot(x_ref[...], w_ref[...], preferred_element_type=F32)

    @pl.when(k == 0)
    def _():
        acc_ref[...] = part

    @pl.when(k > 0)
    def _():
        acc_ref[...] += part

    @pl.when(k == nk - 1)
    def _():
        finish(acc_ref[...])


def _pick(n, pref):
    for t in pref:
        if n % t == 0:
            return t
    return n


def _matmul(x, w, out_dtype=F32, act=None, name='matmul'):
    m, k = x.shape
    n = w.shape[1]
    tm = _pick(m, (1024, 512, 256, 128, 64))
    tn = _pick(n, (512, 256, 128))
    tk = _pick(k, (4096, 2048, 1024, 512))
    nk = k // tk
    scratch = [pltpu.VMEM((tm, tn), F32)] if nk > 1 else []
    return pl.pallas_call(
        functools.partial(_mm_kernel, nk=nk, act=act),
        grid=(m // tm, n // tn, nk),
        in_specs=[pl.BlockSpec((tm, tk), lambda i, j, kk: (i, kk)),
                  pl.BlockSpec((tk, tn), lambda i, j, kk: (kk, j))],
        out_specs=pl.BlockSpec((tm, tn), lambda i, j, kk: (i, j)),
        out_shape=jax.ShapeDtypeStruct((m, n), out_dtype),
        scratch_shapes=scratch,
        compiler_params=pltpu.CompilerParams(
            dimension_semantics=('parallel', 'parallel', 'arbitrary'),
            vmem_limit_bytes=VMEM_LIMIT_BYTES),
        name=name,
    )(x, w)


def _proj(x, w_bf16, **kw):
    lead = x.shape[:-1]
    y = _matmul(x.reshape(-1, x.shape[-1]).astype(BF16), w_bf16, **kw)
    return y.reshape(lead + (w_bf16.shape[1],))


def _split(x, sizes):
    return jnp.split(x, np.cumsum(sizes)[:-1].tolist(), axis=-1)


def _layernorm(x, g, b):
    mu = jnp.mean(x, -1, keepdims=True)
    var = jnp.mean(jnp.square(x - mu), -1, keepdims=True)
    return (x - mu) * lax.rsqrt(var + LN_EPS) * g + b


def _rmsnorm(x, w):
    return x * lax.rsqrt(jnp.mean(x * x, -1, keepdims=True) + NORM_EPS) * w


def _masked_softmax(s, mask):
    s = jnp.where(mask, s.astype(F32), -jnp.inf)
    m = jnp.max(s, -1, keepdims=True)
    m = jnp.where(jnp.isfinite(m), m, 0.0)
    e = jnp.exp(s - m)
    d = jnp.sum(e, -1, keepdims=True)
    return e / jnp.where(d > 0, d, 1.0)


def _t5_bucket(dist):
    n = jnp.maximum(dist, 0)
    max_exact = N_BUCKETS // 2
    nf = jnp.maximum(n, 1).astype(F32)
    large = max_exact + (jnp.log(nf / max_exact) / math.log(MAX_DISTANCE / max_exact) * (N_BUCKETS - max_exact)).astype(jnp.int32)
    large = jnp.minimum(large, N_BUCKETS - 1)
    return jnp.where(n < max_exact, n, large)


def _ssd_scan(x, dt, a, bm, cm, h0):
    b, l, h, p = x.shape
    g, n = bm.shape[2], bm.shape[3]
    r = h // g
    lc = min(SSD_CHUNK, l)
    pad = (-l) % lc
    x, dt, bm, cm = [jnp.pad(t.astype(F32), ((0, 0), (0, pad)) + ((0, 0),) * (t.ndim - 2)) for t in (x, dt, bm, cm)]
    c = (l + pad) // lc
    xdt = (x * dt[..., None]).reshape(b, c, lc, g, r, p)
    acs = jnp.cumsum((dt * a).reshape(b, c, lc, g, r), axis=2)
    bm = bm.reshape(b, c, lc, g, n)
    cm = cm.reshape(b, c, lc, g, n)
    causal = np.tril(np.ones((lc, lc), bool))[None, None, :, :, None, None]
    decay = jnp.exp(jnp.where(causal, acs[:, :, :, None] - acs[:, :, None], -jnp.inf))
    cb = jnp.einsum('bctgn,bcsgn->bctsg', cm, bm)
    y_diag = jnp.einsum('bctsg,bctsgr,bcsgrp->bctgrp', cb, decay, xdt)
    decay_st = jnp.exp(acs[:, :, -1:] - acs)
    states = jnp.einsum('bcsgn,bcsgr,bcsgrp->bcgrpn', bm, decay_st, xdt)
    chunk_decay = jnp.exp(acs[:, :, -1])

    def step(hc, inp):
        st, dec = inp
        return hc * dec[..., None, None] + st, hc

    h_last, h_prev = lax.scan(step, h0.astype(F32).reshape(b, g, r, p, n), (jnp.moveaxis(states, 1, 0), jnp.moveaxis(chunk_decay, 1, 0)))
    h_prev = jnp.moveaxis(h_prev, 0, 1)
    y_off = jnp.einsum('bctgn,bcgrpn,bctgr->bctgrp', cm, h_prev, jnp.exp(acs))
    y = (y_diag + y_off).reshape(b, c * lc, h, p)[:, :l]
    return y, h_last.reshape(b, h, p, n)


def _ssd_mixer(z, xbc, dt, conv0, h0, conv_w, conv_b, dt_bias, a_log, d_skip, norm_w):
    b, l, _ = xbc.shape
    xpad = jnp.concatenate([conv0.astype(xbc.dtype), xbc], axis=1)
    conv = sum((xpad[:, k:k + l] * conv_w[k] for k in range(SSD_CONV)), conv_b)
    conv_new = xpad[:, -(SSD_CONV - 1):]
    xbc = jax.nn.silu(conv)
    xs, bm, cm = _split(xbc, (D_SSD, SSD_GROUPS * SSD_STATE, SSD_GROUPS * SSD_STATE))
    xs = xs.reshape(b, l, SSD_HEADS, SSD_HEAD_DIM)
    bm = bm.reshape(b, l, SSD_GROUPS, SSD_STATE)
    cm = cm.reshape(b, l, SSD_GROUPS, SSD_STATE)
    dt = jax.nn.softplus((dt + dt_bias).astype(F32))
    a = -jnp.exp(a_log.astype(F32))
    y, h_new = _ssd_scan(xs, dt, a, bm, cm, h0)
    y = y + xs.astype(F32) * d_skip.astype(F32)[:, None]
    y = (y.reshape(b, l, D_SSD) * jax.nn.silu(z.astype(F32))).reshape(b, l, SSD_GROUPS, D_SSD // SSD_GROUPS)
    y = _rmsnorm(y, norm_w.reshape(SSD_GROUPS, D_SSD // SSD_GROUPS)).reshape(b, l, D_SSD)
    return y, conv_new, h_new


def _gla_scan(q, k, v, logf, s0):
    b, l, h, dk = q.shape
    dv = v.shape[-1]
    lc = min(GLA_CHUNK, l)
    pad = (-l) % lc
    q, k, v, logf = [jnp.pad(t.astype(F32), ((0, 0), (0, pad), (0, 0), (0, 0))) for t in (q, k, v, logf)]
    c = (l + pad) // lc
    q, k, logf = [t.reshape(b, c, lc, h, dk) for t in (q, k, logf)]
    v = v.reshape(b, c, lc, h, dv)
    bc = jnp.cumsum(logf, axis=2)
    causal = np.tril(np.ones((lc, lc), bool))[None, None, :, :, None, None]
    rel = jnp.exp(jnp.where(causal, bc[:, :, :, None] - bc[:, :, None], -jnp.inf))
    att = jnp.einsum('bcthd,bcshd,bctshd->bchts', q, k, rel)
    o_intra = jnp.einsum('bchts,bcshv->bcthv', att, v)
    k_st = k * jnp.exp(bc[:, :, -1:] - bc)
    chunk_states = jnp.einsum('bcshd,bcshv->bchdv', k_st, v)
    chunk_decay = jnp.exp(bc[:, :, -1])

    def step(s, inp):
        st, dec = inp
        return s * dec[..., None] + st, s

    s_last, s_prev = lax.scan(step, s0.astype(F32), (jnp.moveaxis(chunk_states, 1, 0), jnp.moveaxis(chunk_decay, 1, 0)))
    s_prev = jnp.moveaxis(s_prev, 0, 1)
    o_inter = jnp.einsum('bcthd,bchdv->bcthv', q * jnp.exp(bc), s_prev)
    o = (o_intra + o_inter).reshape(b, c * lc, h, dv)[:, :l]
    return o, s_last


def _gla_mixer(q, k, v, g, a_lr, s0, gate_w2, gate_b, norm_w):
    b, l, _ = q.shape
    q = q.reshape(b, l, GLA_HEADS, GLA_DK) * (GLA_DK ** -0.5)
    k = k.reshape(b, l, GLA_HEADS, GLA_DK)
    v = v.reshape(b, l, GLA_HEADS, GLA_DV)
    logf = jax.nn.log_sigmoid((a_lr @ gate_w2 + gate_b).astype(F32)) / GLA_TAU
    o, s_new = _gla_scan(q, k, v, logf.reshape(b, l, GLA_HEADS, GLA_DK), s0)
    o = _rmsnorm(o, norm_w).reshape(b, l, D_GLA) * jax.nn.silu(g.astype(F32))
    return o, s_new


def _nsa_attend(q, gates, kv_full, win_ctx, q_pos0, w_pos0, cmp_pe, cmp_w, rel_bias):
    b, lq = q.shape[:2]
    t_len = kv_full.shape[1]
    G, R, HD = NSA_KV_HEADS, NSA_REP, NSA_HEAD_DIM
    kc, vc, ks, vs = [kv_full[:, :, i] for i in range(4)]
    n_cmp = (t_len - CMP_LEN) // CMP_STRIDE + 1
    cidx = (np.arange(n_cmp)[:, None] * CMP_STRIDE + np.arange(CMP_LEN)[None]).astype(np.int32)
    k_cmp = jnp.einsum('bnlgd,lde->bnge', kc[:, cidx] + cmp_pe[0][:, None], cmp_w[0]).astype(F32)
    v_cmp = jnp.einsum('bnlgd,lde->bnge', vc[:, cidx] + cmp_pe[1][:, None], cmp_w[1]).astype(F32)
    cmp_end = (np.arange(n_cmp) * CMP_STRIDE + CMP_LEN - 1).astype(np.int32)
    n_sel = -(-t_len // SEL_LEN)
    sel_pad = n_sel * SEL_LEN - t_len

    def sel_blocks(t):
        t = jnp.pad(t, ((0, 0), (0, sel_pad), (0, 0), (0, 0)))
        return t.reshape(b, n_sel, SEL_LEN, G, HD).transpose(0, 3, 1, 2, 4)

    ks_b, vs_b = sel_blocks(ks), sel_blocks(vs)
    c_lo = np.arange(n_cmp) * CMP_STRIDE
    s_lo = np.arange(n_sel) * SEL_LEN
    cover = jnp.asarray(((c_lo[:, None] < s_lo[None] + SEL_LEN) & (c_lo[:, None] + CMP_LEN > s_lo[None])).astype(np.float32))
    top_n = min(TOP_N, n_sel)
    kw_pad = jnp.pad(win_ctx[:, :, 0], ((0, 0), (WINDOW, 0), (0, 0), (0, 0)))
    vw_pad = jnp.pad(win_ctx[:, :, 1], ((0, 0), (WINDOW, 0), (0, 0), (0, 0)))
    qb = Q_BLOCK if lq % Q_BLOCK == 0 else lq
    nqb = lq // qb
    scale = HD ** -0.5
    bias_gr = rel_bias.reshape(N_BUCKETS, G, R)
    bi = jnp.arange(b)[:, None, None, None]
    gi = jnp.arange(G)[None, None, :, None]
    gi5 = jnp.arange(G)[None, None, :, None, None]
    blk = jnp.arange(n_sel)

    def block(args):
        i, qblk, gblk = args
        q0 = q_pos0 + i * qb
        tpos = q0 + jnp.arange(qb)
        qg = qblk.reshape(b, qb, G, R, HD).astype(F32) * scale
        bias_c = rel_bias[_t5_bucket(tpos[:, None] - cmp_end[None])].astype(F32).reshape(qb, n_cmp, G, R).transpose(0, 2, 3, 1)
        s_c = jnp.einsum('bqgrd,bngd->bqgrn', qg, k_cmp) + bias_c
        p_c = _masked_softmax(s_c, (cmp_end[None] <= tpos[:, None])[None, :, None, None, :])
        o_c = jnp.einsum('bqgrn,bngd->bqgrd', p_c, v_cmp)
        imp = jnp.einsum('bqgrn,nj->bqgj', p_c, cover)
        cur = (tpos // SEL_LEN)[:, None]
        forced = (blk == 0) | (blk == cur) | (blk == cur - 1)
        valid = blk * SEL_LEN <= tpos[:, None]
        score = jnp.where(forced[None, :, None, :], FORCE_SCORE, jnp.where(valid[None, :, None, :], imp, -FORCE_SCORE))
        _, sel = lax.top_k(score, top_n)
        k_sel = ks_b[bi, gi, sel].astype(F32)
        v_sel = vs_b[bi, gi, sel].astype(F32)
        spos = sel[..., None] * SEL_LEN + jnp.arange(SEL_LEN)
        dist_s = tpos[None, :, None, None, None] - spos
        bias_s = jnp.moveaxis(bias_gr[_t5_bucket(dist_s), gi5].astype(F32), -1, 3)
        s_s = jnp.einsum('bqgrd,bqgksd->bqgrks', qg, k_sel) + bias_s
        p_s = _masked_softmax(s_s.reshape(b, qb, G, R, top_n * SEL_LEN), (dist_s >= 0).reshape(b, qb, G, 1, top_n * SEL_LEN)).reshape(s_s.shape)
        o_s = jnp.einsum('bqgrks,bqgksd->bqgrd', p_s, v_sel)
        start = q0 - w_pos0
        k_w = lax.dynamic_slice_in_dim(kw_pad, start, WINDOW + qb, axis=1).astype(F32)
        v_w = lax.dynamic_slice_in_dim(vw_pad, start, WINDOW + qb, axis=1).astype(F32)
        wpos = q0 - WINDOW + jnp.arange(WINDOW + qb)
        dist_w = tpos[:, None] - wpos[None]
        mask_w = (dist_w >= 0) & (dist_w < WINDOW) & (wpos[None] >= w_pos0)
        bias_w = rel_bias[_t5_bucket(dist_w)].astype(F32).reshape(qb, WINDOW + qb, G, R).transpose(0, 2, 3, 1)
        s_w = jnp.einsum('bqgrd,bkgd->bqgrk', qg, k_w) + bias_w
        p_w = _masked_softmax(s_w, mask_w[None, :, None, None, :])
        o_w = jnp.einsum('bqgrk,bkgd->bqgrd', p_w, v_w)
        gg = gblk.reshape(b, qb, G, R, 3).astype(F32)
        return gg[..., 0:1] * o_c + gg[..., 1:2] * o_s + gg[..., 2:3] * o_w

    q_blocks = jnp.swapaxes(q.reshape(b, nqb, qb, NSA_HEADS, HD), 0, 1)
    g_blocks = jnp.swapaxes(gates.reshape(b, nqb, qb, NSA_HEADS, 3), 0, 1)
    outs = lax.map(block, (jnp.arange(nqb), q_blocks, g_blocks))
    return jnp.moveaxis(outs, 0, 1).reshape(b, lq, D_NSA)


def _mixer(x, lp, rel_bias, tables, conv0, ssd_h0, gla_s0, nsa_past, win_past, past_len):
    b, l, _ = x.shape
    y = _matmul(x.reshape(b * l, D_MODEL).astype(BF16), lp['w_in'], name='in_proj')
    y3 = y.reshape(b, l, D_IN_PAD)

    def part(off, width):
        return y3[..., off:off + width]

    z, xbc, dt = part(Z_OFF, D_SSD), part(XBC_OFF, SSD_CONV_DIM), part(DT_OFF, SSD_HEADS)
    q_g, k_g = part(QG_OFF, GLA_HEADS * GLA_DK), part(KG_OFF, GLA_HEADS * GLA_DK)
    v_g, g_g, a_g = part(VG_OFF, D_GLA), part(GG_OFF, D_GLA), part(ALR_OFF, GLA_RANK)
    y_ssd, conv_new, h_new = _ssd_mixer(z, xbc, dt, conv0, ssd_h0, lp['ssd_conv_w'], lp['ssd_conv_b'], lp['ssd_dt_bias'], lp['ssd_a_log'], lp['ssd_d'], lp['ssd_norm_w'])
    rows = part(KV_OFF, 4 * KVW).reshape(b, l, 4, NSA_KV_HEADS, NSA_HEAD_DIM)
    win_rows = part(KV_OFF + 4 * KVW, 2 * KVW).reshape(b, l, 2, NSA_KV_HEADS, NSA_HEAD_DIM)
    if nsa_past is None:
        win_new = win_rows[:, -min(WINDOW, l):]
        y_nsa = _nsa_prompt(y, _nsa_compress_prompt(y, lp['nsa_cmp_pe'], lp['nsa_cmp_w'], b), tables, b).reshape(b, l, D_NSA)
    else:
        kv_full = jnp.concatenate([nsa_past.astype(rows.dtype), rows], axis=1)
        win_ctx = jnp.concatenate([win_past.astype(rows.dtype), win_rows], axis=1)
        win_new = win_ctx[:, -win_past.shape[1]:]
        g_n = jnp.concatenate([part(GATE_OFF + g * LANE, 3 * NSA_REP) for g in range(NSA_KV_HEADS)], axis=-1)
        gates = jax.nn.sigmoid(g_n).reshape(b, l, NSA_HEADS, 3)
        q_n = part(QN_OFF, D_NSA).reshape(b, l, NSA_HEADS, NSA_HEAD_DIM)
        y_nsa = _nsa_attend(q_n, gates, kv_full, win_ctx, past_len, past_len - win_past.shape[1], lp['nsa_cmp_pe'], lp['nsa_cmp_w'], rel_bias)
    y_gla, s_new = _gla_mixer(q_g, k_g, v_g, g_g, a_g, gla_s0, lp['gla_gate_w2'], lp['gla_gate_b'], lp['gla_norm_w'])
    y_out = _proj(jnp.concatenate([y_ssd, y_nsa, y_gla], axis=-1), lp['w_out'], name='out_proj')
    return y_out, rows, win_new, conv_new, h_new, s_new


def _cross_attn(x, mem_kv, wq, wo):
    b, l, _ = x.shape
    q = _proj(x, wq, name='xattn_q').reshape(b, l, X_HEADS, X_HEAD_DIM) * (X_HEAD_DIM ** -0.5)
    s = jnp.einsum('blhd,bmhd->bhlm', q, mem_kv[:, :, 0].astype(F32))
    p = jax.nn.softmax(s, axis=-1)
    o = jnp.einsum('bhlm,bmhd->blhd', p, mem_kv[:, :, 1].astype(F32)).reshape(b, l, D_MODEL)
    return _proj(o, wo, name='xattn_o')


def _sqrelu_ffn(x, w1, w2):
    h = _proj(x, w1, out_dtype=BF16, act='sqrelu', name='ffn_up')
    return _proj(h, w2, name='ffn_down')


def _layer(x, lp, rel_bias, tables, mem_kv, conv0, ssd_h0, gla_s0, nsa_past, win_past, past_len):
    h, rows, win, conv, hs, sg = _mixer(x, lp, rel_bias, tables, conv0, ssd_h0, gla_s0, nsa_past, win_past, past_len)
    x = _layernorm(DN_ALPHA * x + h, lp['ln_g'][0], lp['ln_b'][0])
    x = _layernorm(DN_ALPHA * x + _cross_attn(x, mem_kv, lp['x_wq'], lp['x_wo']), lp['ln_g'][1], lp['ln_b'][1])
    x = _layernorm(DN_ALPHA * x + _sqrelu_ffn(x, lp['ffn_w1'], lp['ffn_w2']), lp['ln_g'][2], lp['ln_b'][2])
    return x, (rows, win, conv, hs, sg)


def kernel(x_prompt, x_sample, cache_nsa_kv, cache_nsa_win, state_ssd, state_ssd_conv, state_gla, cache_mem_kv,
           page_table, mem_prompt, w_in, ssd_conv_w, ssd_conv_b, ssd_dt_bias, ssd_a_log, ssd_d, ssd_norm_w,
           nsa_cmp_pe, nsa_cmp_w, rel_bias, gla_gate_w2, gla_gate_b, gla_norm_w, w_out, x_wq, x_wkv, x_wo,
           ffn_w1, ffn_w2, ln_g, ln_b):
    bp = x_prompt.shape[0]
    bs = x_sample.shape[0]
    past_len = page_table.shape[1] * cache_nsa_kv.shape[2]
    xp, xs = x_prompt, x_sample
    st_p, st_s, mem_p = [], [], []
    tables = _nsa_prompt_bias_tables(rel_bias)
    for l in range(DEPTH):
        lp = dict(w_in=_reorder_w_in(w_in[l]), ssd_conv_w=ssd_conv_w[l], ssd_conv_b=ssd_conv_b[l],
                  ssd_dt_bias=ssd_dt_bias[l], ssd_a_log=ssd_a_log[l], ssd_d=ssd_d[l], ssd_norm_w=ssd_norm_w[l],
                  nsa_cmp_pe=nsa_cmp_pe[l], nsa_cmp_w=nsa_cmp_w[l], gla_gate_w2=gla_gate_w2[l], gla_gate_b=gla_gate_b[l],
                  gla_norm_w=gla_norm_w[l], w_out=w_out[l].astype(BF16), x_wq=x_wq[l].astype(BF16),
                  x_wo=x_wo[l].astype(BF16), ffn_w1=ffn_w1[l].astype(BF16), ffn_w2=ffn_w2[l].astype(BF16),
                  ln_g=ln_g[l], ln_b=ln_b[l])
        mem_kv_p = _proj(mem_prompt, x_wkv[l].astype(BF16), name='mem_kv').reshape(bp, N_MEM, 2, X_HEADS, X_HEAD_DIM)
        conv0 = jnp.zeros((bp, SSD_CONV - 1, SSD_CONV_DIM), xp.dtype)
        h0 = jnp.zeros((bp, SSD_HEADS, SSD_HEAD_DIM, SSD_STATE), F32)
        s0 = jnp.zeros((bp, GLA_HEADS, GLA_DK, GLA_DV), F32)
        xp, stp = _layer(xp, lp, rel_bias, tables, mem_kv_p, conv0, h0, s0, None, None, 0)
        st_p.append(stp)
        mem_p.append(mem_kv_p)
        past = cache_nsa_kv[l][page_table].reshape(bs, past_len, 4, NSA_KV_HEADS, NSA_HEAD_DIM)
        xs, sts = _layer(xs, lp, rel_bias, tables, cache_mem_kv[l], state_ssd_conv[l], state_ssd[l], state_gla[l], past,
                         cache_nsa_win[l], past_len)
        st_s.append(sts)
    p_rows, p_win, p_conv, p_ssd, p_gla = [jnp.stack(s) for s in zip(*st_p)]
    s_rows, s_win, s_conv, s_ssd, s_gla = [jnp.stack(s) for s in zip(*st_s)]
    p_mem = jnp.stack(mem_p)
    return (xp, xs, p_rows, s_rows, p_win, s_win, p_ssd, s_ssd, p_conv, s_conv, p_gla, s_gla, p_mem)
```

```python
import functools
import math

import jax
import jax.numpy as jnp
import numpy as np
from jax import lax
from jax.experimental import pallas as pl
from jax.experimental.pallas import tpu as pltpu

F32 = jnp.float32
BF16 = jnp.bfloat16

D_MODEL = 4096
DEPTH = 2
PAGE_SIZE = 128
D_MIX = D_MODEL
D_SSD = D_MIX // 2
SSD_HEAD_DIM = 64
SSD_HEADS = D_SSD // SSD_HEAD_DIM
SSD_GROUPS = 8
SSD_STATE = 128
SSD_CONV = 4
SSD_CHUNK = 128
SSD_CONV_DIM = D_SSD + 2 * SSD_GROUPS * SSD_STATE
D_NSA = D_MIX // 4
NSA_HEAD_DIM = 128
NSA_HEADS = D_NSA // NSA_HEAD_DIM
NSA_KV_HEADS = 2
NSA_REP = NSA_HEADS // NSA_KV_HEADS
CMP_LEN = 32
CMP_STRIDE = 16
SEL_LEN = 64
TOP_N = 16
WINDOW = 512
Q_BLOCK = 128
FORCE_SCORE = 1e4
D_GLA = D_MIX - D_SSD - D_NSA
GLA_HEADS = 4
GLA_DV = D_GLA // GLA_HEADS
GLA_DK = GLA_DV // 2
GLA_RANK = 16
GLA_TAU = 16.0
GLA_CHUNK = 16
N_MEM = 256
X_HEADS = 4
X_HEAD_DIM = D_MODEL // X_HEADS
D_FF = 4 * D_MODEL
N_BUCKETS = 32
MAX_DISTANCE = 128
LN_EPS = 1e-5
NORM_EPS = 1e-6
DN_ALPHA = (2 * DEPTH) ** 0.25
IN_SPLITS = (D_SSD, SSD_CONV_DIM, SSD_HEADS, NSA_HEADS * NSA_HEAD_DIM) + (NSA_KV_HEADS * NSA_HEAD_DIM,) * 6 + (
    3 * NSA_HEADS, GLA_HEADS * GLA_DK, GLA_HEADS * GLA_DK, D_GLA, D_GLA, GLA_RANK)
D_IN = sum(IN_SPLITS)

LANE = 128
VMEM_LIMIT_BYTES = 56 * 1024 * 1024

KVW = NSA_KV_HEADS * NSA_HEAD_DIM
Z_OFF = 0
XBC_OFF = Z_OFF + D_SSD
QN_OFF = XBC_OFF + SSD_CONV_DIM
VG_OFF = QN_OFF + NSA_HEADS * NSA_HEAD_DIM
GG_OFF = VG_OFF + D_GLA
KV_OFF = GG_OFF + D_GLA
QG_OFF = KV_OFF + 6 * KVW
KG_OFF = QG_OFF + GLA_HEADS * GLA_DK
SMALL_OFF = KG_OFF + GLA_HEADS * GLA_DK
DT_OFF = SMALL_OFF
ALR_OFF = SMALL_OFF + SSD_HEADS
GATE_OFF = SMALL_OFF + LANE
D_IN_PAD = 12288
NEG = -1e30


def _in_proj_column_map():
    off = np.cumsum((0,) + IN_SPLITS)
    z, xbc, dt, qn = off[0], off[1], off[2], off[3]
    kv0, gn, qg, kg, vg, gg, alr = off[4], off[10], off[11], off[12], off[13], off[14], off[15]
    cmap = np.full((D_IN_PAD,), -1, np.int64)
    cmap[Z_OFF:Z_OFF + D_SSD] = z + np.arange(D_SSD)
    cmap[XBC_OFF:XBC_OFF + SSD_CONV_DIM] = xbc + np.arange(SSD_CONV_DIM)
    cmap[QN_OFF:QN_OFF + D_NSA] = qn + np.arange(D_NSA)
    cmap[KV_OFF:KV_OFF + 6 * KVW] = kv0 + np.arange(6 * KVW)
    cmap[QG_OFF:QG_OFF + 512] = qg + np.arange(512)
    cmap[KG_OFF:KG_OFF + 512] = kg + np.arange(512)
    cmap[VG_OFF:VG_OFF + D_GLA] = vg + np.arange(D_GLA)
    cmap[GG_OFF:GG_OFF + D_GLA] = gg + np.arange(D_GLA)
    cmap[DT_OFF:DT_OFF + SSD_HEADS] = dt + np.arange(SSD_HEADS)
    cmap[ALR_OFF:ALR_OFF + GLA_RANK] = alr + np.arange(GLA_RANK)
    for g in range(NSA_KV_HEADS):
        cmap[GATE_OFF + g * LANE:GATE_OFF + g * LANE + 3 * NSA_REP] = gn + g * 3 * NSA_REP + np.arange(3 * NSA_REP)
    return cmap


def _reorder_w_in(w):
    cmap = _in_proj_column_map()
    wb = jnp.take(w.astype(BF16), jnp.asarray(np.maximum(cmap, 0), jnp.int32), axis=1)
    return jnp.where(jnp.asarray(cmap >= 0)[None, :], wb, jnp.zeros((), BF16))


def _dot_nt(a, b):
    return lax.dot_general(a, b, (((1,), (1,)), ((), ())), preferred_element_type=F32)


NSA_L = 2048
NSA_QB = Q_BLOCK
NSA_NQB = NSA_L // NSA_QB
NSA_NCP = NSA_L // CMP_STRIDE
NSA_NSEL = NSA_L // SEL_LEN
CMP_HALF = CMP_STRIDE * NSA_HEAD_DIM


def _t5_bucket_np(dist):
    n = np.maximum(dist, 0)
    max_exact = N_BUCKETS // 2
    nf = np.maximum(n, 1).astype(np.float32)
    large = max_exact + (np.log(nf / max_exact) / np.float32(math.log(MAX_DISTANCE / max_exact)) * (N_BUCKETS - max_exact)).astype(np.int32)
    return np.where(n < max_exact, n, np.minimum(large, N_BUCKETS - 1)).astype(np.int32)


def _nsa_prompt_bias_tables(rel_bias):
    tq = np.arange(NSA_QB)[:, None]
    ts = np.arange(NSA_QB)[None, :]

    def table(idx):
        t = rel_bias[jnp.asarray(idx)]
        t = jnp.moveaxis(t, -1, 0).reshape((NSA_KV_HEADS, NSA_REP) + idx.shape)
        t = jnp.moveaxis(t, 1, -3)
        return t.reshape(t.shape[:-3] + (NSA_REP * NSA_QB, idx.shape[-1]))

    b_diag = table(_t5_bucket_np(tq - ts))
    b_prev = table(_t5_bucket_np(NSA_QB + tq - ts))
    b_far = table(np.full((NSA_QB, NSA_QB), N_BUCKETS - 1, np.int32))
    i = np.arange(NSA_NQB)[:, None, None]
    n = np.arange(NSA_NCP)[None, None, :]
    b_cmp = table(_t5_bucket_np(i * NSA_QB + tq[None] - (n * CMP_STRIDE + CMP_LEN - 1)))
    return b_diag, b_prev, b_far, b_cmp


def _nsa_compress_kernel(a_ref, pe_ref, w_ref, o_ref):
    a = a_ref[0, 0, 0]
    lo = jnp.dot((a + pe_ref[0, 0]).astype(BF16), w_ref[0, 0], preferred_element_type=F32)
    hi = jnp.dot((a + pe_ref[0, 1]).astype(BF16), w_ref[0, 1], preferred_element_type=F32)
    o_ref[0, 0, 0] = lo + pltpu.roll(hi, NSA_NCP - 1, 0)


def _nsa_compress_prompt(y, cmp_pe, cmp_w, bsz):
    a = y[:, KV_OFF:KV_OFF + 2 * KVW].reshape(bsz, NSA_NCP, CMP_STRIDE, 2, NSA_KV_HEADS, NSA_HEAD_DIM)
    a = a.transpose(0, 3, 4, 1, 2, 5).reshape(bsz, 2, NSA_KV_HEADS, NSA_NCP, CMP_HALF)
    pe = cmp_pe.reshape(2, 2, 1, CMP_HALF)
    w = cmp_w.astype(BF16).reshape(2, 2, CMP_HALF, NSA_HEAD_DIM)
    return pl.pallas_call(
        _nsa_compress_kernel,
        grid=(bsz, 2, NSA_KV_HEADS),
        in_specs=[pl.BlockSpec((1, 1, 1, NSA_NCP, CMP_HALF), lambda b, c, g: (b, c, g, 0, 0)),
                  pl.BlockSpec((1, 2, 1, CMP_HALF), lambda b, c, g: (c, 0, 0, 0)),
                  pl.BlockSpec((1, 2, CMP_HALF, NSA_HEAD_DIM), lambda b, c, g: (c, 0, 0, 0))],
        out_specs=pl.BlockSpec((1, 1, 1, NSA_NCP, NSA_HEAD_DIM), lambda b, c, g: (b, c, g, 0, 0)),
        out_shape=jax.ShapeDtypeStruct((bsz, 2, NSA_KV_HEADS, NSA_NCP, NSA_HEAD_DIM), F32),
        compiler_params=pltpu.CompilerParams(dimension_semantics=('parallel', 'parallel', 'parallel')),
        name='nsa_compress',
    )(a, pe, w)


def _nsa_prompt_kernel(q_ref, gate_ref, cmp_ref, ks_ref, vs_ref, kw_ref, vw_ref, bdiag_ref, bprev_ref, bfar_ref,
                       bcmp_ref, covt_ref, expand_ref, o_ref, m_ref, l_ref, acc_ref):
    i = pl.program_id(2)
    rows = NSA_REP * NSA_QB
    qblk = q_ref[...] * (NSA_HEAD_DIM ** -0.5)
    q = jnp.concatenate([qblk[:, r * LANE:(r + 1) * LANE] for r in range(NSA_REP)], axis=0).astype(BF16)
    tq = lax.broadcasted_iota(jnp.int32, (rows, LANE), 0) % NSA_QB
    lane = lax.broadcasted_iota(jnp.int32, (rows, LANE), 1)
    t_abs = i * NSA_QB + tq

    k_c = cmp_ref[0, 0, 0].astype(BF16)
    v_c = cmp_ref[0, 1, 0].astype(BF16)
    mask_c = lane * CMP_STRIDE + (CMP_LEN - 1) <= t_abs
    s_c = jnp.where(mask_c, _dot_nt(q, k_c) + bcmp_ref[0, 0], NEG)
    e_c = jnp.where(mask_c, jnp.exp(s_c - jnp.max(s_c, -1, keepdims=True)), 0.0)
    d_c = jnp.sum(e_c, -1, keepdims=True)
    p_c = (e_c / jnp.where(d_c > 0, d_c, 1.0)).astype(BF16)
    o_c = jnp.dot(p_c, v_c, preferred_element_type=F32)

    imp4 = _dot_nt(covt_ref[...], p_c)
    imp = sum(imp4[:, r * NSA_QB:(r + 1) * NSA_QB] for r in range(NSA_REP))
    blk = lax.broadcasted_iota(jnp.int32, (LANE, NSA_QB), 0)
    t_row = i * NSA_QB + lax.broadcasted_iota(jnp.int32, (LANE, NSA_QB), 1)
    cur = t_row // SEL_LEN
    forced = (blk == 0) | (blk == cur) | (blk == cur - 1)
    score = jnp.where(forced, FORCE_SCORE, jnp.where(blk * SEL_LEN <= t_row, imp, -FORCE_SCORE))
    score = jnp.where(blk < NSA_NSEL, score, -3.0 * FORCE_SCORE)
    rank = jnp.zeros((LANE, NSA_QB), jnp.int32)
    for j in range(NSA_NSEL):
        row = score[j:j + 1, :]
        rank = rank + jnp.where((row > score) | ((row == score) & (blk > j)), 1, 0)
    sel_t = jnp.where((rank < TOP_N) & (blk < NSA_NSEL), 1.0, 0.0)
    sel = sel_t.T.astype(BF16)

    def init():
        m_ref[...] = jnp.full((rows, 1), NEG, F32)
        l_ref[...] = jnp.zeros((rows, 1), F32)
        acc_ref[...] = jnp.zeros((rows, NSA_HEAD_DIM), F32)

    def step(k_ref, v_ref, c, bias, mask):
        start = pl.multiple_of(c * NSA_QB, NSA_QB)
        k = k_ref[pl.ds(start, NSA_QB), :].astype(BF16)
        v = v_ref[pl.ds(start, NSA_QB), :].astype(BF16)
        s = _dot_nt(q, k) + bias
        if mask is not None:
            s = jnp.where(mask, s, NEG)
        m_old = m_ref[...]
        m_new = jnp.maximum(m_old, jnp.max(s, -1, keepdims=True))
        p = jnp.exp(s - m_new)
        if mask is not None:
            p = jnp.where(mask, p, 0.0)
        alpha = jnp.exp(m_old - m_new)
        l_ref[...] = alpha * l_ref[...] + jnp.sum(p, -1, keepdims=True)
        acc_ref[...] = alpha * acc_ref[...] + jnp.dot(p.astype(BF16), v, preferred_element_type=F32)
        m_ref[...] = m_new

    def finish():
        l = l_ref[...]
        return acc_ref[...] / jnp.where(l > 0, l, 1.0)

    def sel_mask(c):
        m1 = jnp.dot(sel, expand_ref[c], preferred_element_type=F32)
        return jnp.concatenate([m1] * NSA_REP, axis=0) > 0.5

    init()
    bfar = bfar_ref[0]

    def far_body(c, carry):
        step(ks_ref, vs_ref, c, bfar, sel_mask(c))
        return carry

    lax.fori_loop(0, jnp.maximum(i - 1, 0), far_body, 0)

    @pl.when(i >= 1)
    def _():
        step(ks_ref, vs_ref, i - 1, bprev_ref[0], sel_mask(i - 1))

    causal = lane <= tq
    step(ks_ref, vs_ref, i, bdiag_ref[0], sel_mask(i) & causal)
    o_s = finish()

    init()
    step(kw_ref, vw_ref, i, bdiag_ref[0], causal)

    @pl.when(i >= 1)
    def _():
        step(kw_ref, vw_ref, i - 1, bprev_ref[0], None)

    for d in range(2, WINDOW // NSA_QB):
        @pl.when(i >= d)
        def _(d=d):
            step(kw_ref, vw_ref, i - d, bfar, None)

    @pl.when(i >= WINDOW // NSA_QB)
    def _():
        step(kw_ref, vw_ref, i - WINDOW // NSA_QB, bfar, lane > tq)

    o_w = finish()

    gates = jax.nn.sigmoid(gate_ref[...])
    for r in range(NSA_REP):
        rs = slice(r * NSA_QB, (r + 1) * NSA_QB)
        o_ref[:, r * LANE:(r + 1) * LANE] = (gates[:, 3 * r:3 * r + 1] * o_c[rs] + gates[:, 3 * r + 1:3 * r + 2] * o_s[rs]
                                             + gates[:, 3 * r + 2:3 * r + 3] * o_w[rs]).astype(o_ref.dtype)


def _nsa_prompt(y, k_v_cmp, tables, bsz):
    b_diag, b_prev, b_far, b_cmp = tables
    rows = NSA_REP * NSA_QB
    c_lo = np.arange(NSA_NCP) * CMP_STRIDE
    s_lo = np.arange(LANE) * SEL_LEN
    cov_t = ((c_lo[None] < s_lo[:, None] + SEL_LEN) & (c_lo[None] + CMP_LEN > s_lo[:, None])
             & (np.arange(NSA_NCP)[None] < NSA_NCP - 1) & (np.arange(LANE)[:, None] < NSA_NSEL))
    key_blk = (np.arange(NSA_L) // SEL_LEN).reshape(NSA_NQB, 1, NSA_QB)
    expand = (np.arange(LANE)[None, :, None] == key_blk)
    kv_blk = KV_OFF // LANE
    grp = NSA_KV_HEADS

    def col(slot):
        return pl.BlockSpec((NSA_L, NSA_HEAD_DIM), lambda b, g, i: (b, kv_blk + slot * grp + g))

    tab = pl.BlockSpec((1, rows, LANE), lambda b, g, i: (g, 0, 0))
    return pl.pallas_call(
        _nsa_prompt_kernel,
        grid=(bsz, NSA_KV_HEADS, NSA_NQB),
        in_specs=[pl.BlockSpec((NSA_QB, NSA_REP * LANE), lambda b, g, i: (b * NSA_NQB + i, QN_OFF // (NSA_REP * LANE) + g)),
                  pl.BlockSpec((NSA_QB, LANE), lambda b, g, i: (b * NSA_NQB + i, GATE_OFF // LANE + g)),
                  pl.BlockSpec((1, 2, 1, NSA_NCP, NSA_HEAD_DIM), lambda b, g, i: (b, 0, g, 0, 0)),
                  col(2), col(3), col(4), col(5), tab, tab, tab,
                  pl.BlockSpec((1, 1, rows, NSA_NCP), lambda b, g, i: (g, i, 0, 0)),
                  pl.BlockSpec((LANE, NSA_NCP), lambda b, g, i: (0, 0)),
                  pl.BlockSpec((NSA_NQB, LANE, NSA_QB), lambda b, g, i: (0, 0, 0))],
        out_specs=pl.BlockSpec((NSA_QB, NSA_REP * LANE), lambda b, g, i: (b * NSA_NQB + i, g)),
        out_shape=jax.ShapeDtypeStruct((bsz * NSA_L, D_NSA), BF16),
        scratch_shapes=[pltpu.VMEM((rows, 1), F32), pltpu.VMEM((rows, 1), F32), pltpu.VMEM((rows, NSA_HEAD_DIM), F32)],
        compiler_params=pltpu.CompilerParams(dimension_semantics=('parallel', 'parallel', 'arbitrary'),
                                             vmem_limit_bytes=VMEM_LIMIT_BYTES),
        name='nsa_prompt',
    )(y, y, k_v_cmp, y, y, y, y, b_diag, b_prev, b_far, b_cmp,
      jnp.asarray(cov_t, BF16), jnp.asarray(expand, BF16))


def _mm_kernel(x_ref, w_ref, o_ref, *scratch, nk, act):
    def finish(acc):
        if act == 'sqrelu':
            acc = jnp.square(jnp.maximum(acc, 0.0))
        o_ref[...] = acc.astype(o_ref.dtype)

    if nk == 1:
        finish(jnp.dot(x_ref[...], w_ref[...], preferred_element_type=F32))
        return
    acc_ref, = scratch
    k = pl.program_id(2)
    part = jnp.dot(x_ref[...], w_ref[...], preferred_element_type=F32)

    @pl.when(k == 0)
    def _():
        acc_ref[...] = part

    @pl.when(k > 0)
    def _():
        acc_ref[...] += part

    @pl.when(k == nk - 1)
    def _():
        finish(acc_ref[...])


def _pick(n, pref):
    for t in pref:
        if n % t == 0:
            return t
    return n


def _matmul(x, w, out_dtype=F32, act=None, name='matmul'):
    m, k = x.shape
    n = w.shape[1]
    tm = _pick(m, (1024, 512, 256, 128, 64))
    tn = _pick(n, (512, 256, 128))
    tk = _pick(k, (4096, 2048, 1024, 512))
    nk = k // tk
    scratch = [pltpu.VMEM((tm, tn), F32)] if nk > 1 else []
    return pl.pallas_call(
        functools.partial(_mm_kernel, nk=nk, act=act),
        grid=(m // tm, n // tn, nk),
        in_specs=[pl.BlockSpec((tm, tk), lambda i, j, kk: (i, kk)),
                  pl.BlockSpec((tk, tn), lambda i, j, kk: (kk, j))],
        out_specs=pl.BlockSpec((tm, tn), lambda i, j, kk: (i, j)),
        out_shape=jax.ShapeDtypeStruct((m, n), out_dtype),
        scratch_shapes=scratch,
        compiler_params=pltpu.CompilerParams(
            dimension_semantics=('parallel', 'parallel', 'arbitrary'),
            vmem_limit_bytes=VMEM_LIMIT_BYTES),
        name=name,
    )(x, w)


def _proj(x, w_bf16, **kw):
    lead = x.shape[:-1]
    y = _matmul(x.reshape(-1, x.shape[-1]).astype(BF16), w_bf16, **kw)
    return y.reshape(lead + (w_bf16.shape[1],))


def _ln_kernel(x_ref, h_ref, g_ref, b_ref, o_ref, ob_ref):
    v = DN_ALPHA * x_ref[...] + h_ref[...]
    d = v - jnp.mean(v, -1, keepdims=True)
    y = d * lax.rsqrt(jnp.mean(d * d, -1, keepdims=True) + LN_EPS) * g_ref[...] + b_ref[...]
    o_ref[...] = y
    ob_ref[...] = y.astype(BF16)


def _add_layernorm(x, h, g, b):
    m, d = x.shape
    tm = _pick(m, (256, 64))
    row = pl.BlockSpec((tm, d), lambda i: (i, 0))
    vec = pl.BlockSpec((1, d), lambda i: (0, 0))
    return pl.pallas_call(
        _ln_kernel, grid=(m // tm,), in_specs=[row, row, vec, vec], out_specs=[row, row],
        out_shape=[jax.ShapeDtypeStruct((m, d), F32), jax.ShapeDtypeStruct((m, d), BF16)],
        compiler_params=pltpu.CompilerParams(dimension_semantics=('parallel',), vmem_limit_bytes=VMEM_LIMIT_BYTES),
        name='add_layernorm',
    )(x, h, g.reshape(1, d), b.reshape(1, d))


def _xattn_kernel(q_ref, k_ref, v_ref, o_ref):
    q = (q_ref[...] * (X_HEAD_DIM ** -0.5)).astype(BF16)
    s = _dot_nt(q, k_ref[0].astype(BF16))
    e = jnp.exp(s - jnp.max(s, -1, keepdims=True))
    p = e / jnp.sum(e, -1, keepdims=True)
    o_ref[...] = jnp.dot(p.astype(BF16), v_ref[0].astype(BF16), preferred_element_type=F32).astype(o_ref.dtype)


def _cross_attention(q, mem_kv, bsz):
    m = q.shape[0]
    l = m // bsz
    tq = _pick(l, (512, 8))
    nq = l // tq
    return pl.pallas_call(
        _xattn_kernel, grid=(bsz, X_HEADS, nq),
        in_specs=[pl.BlockSpec((tq, X_HEAD_DIM), lambda b, h, i: (b * nq + i, h)),
                  pl.BlockSpec((1, N_MEM, X_HEAD_DIM), lambda b, h, i: (b, 0, h)),
                  pl.BlockSpec((1, N_MEM, X_HEAD_DIM), lambda b, h, i: (b, 0, X_HEADS + h))],
        out_specs=pl.BlockSpec((tq, X_HEAD_DIM), lambda b, h, i: (b * nq + i, h)),
        out_shape=jax.ShapeDtypeStruct((m, D_MODEL), BF16),
        compiler_params=pltpu.CompilerParams(dimension_semantics=('parallel', 'parallel', 'parallel'),
                                             vmem_limit_bytes=VMEM_LIMIT_BYTES),
        name='cross_attention',
    )(q, mem_kv, mem_kv)


def _softplus(x):
    return jnp.maximum(x, 0.0) + jnp.log1p(jnp.exp(-jnp.abs(x)))


def _silu(x):
    return x * jax.nn.sigmoid(x)


def _pad_rows(x, rows):
    if x.shape[0] == rows:
        return x
    return jnp.concatenate([x, jnp.zeros((rows - x.shape[0],) + x.shape[1:], x.dtype)], axis=0)


def _cumsum_rows(x, seg):
    r = lax.broadcasted_iota(jnp.int32, x.shape, 0) % seg
    k = 1
    while k < seg:
        x = x + jnp.where(r >= k, pltpu.roll(x, k, 0), 0.0)
        k *= 2
    return x


def _segment_last(x, seg):
    n = x.shape[0]
    r = lax.broadcasted_iota(jnp.int32, x.shape, 0) % seg
    k = 1
    while k < seg:
        x = jnp.where(r < seg - k, pltpu.roll(x, n - k, 0), x)
        k *= 2
    return x


GLA_BLOCK = 128


def _gla_kernel(q_ref, k_ref, v_ref, g_ref, a_ref, w2_ref, gb_ref, nw_ref, s0_ref, o_ref, s_out_ref,
                s_scr, q_scr, k_scr, v_scr, bc_scr, o_scr, *, rows_in, ch):
    c = pl.program_id(1)
    nsub = -(-rows_in // ch)
    hk = GLA_DK

    @pl.when(c == 0)
    def _():
        s_scr[...] = s0_ref[0]

    valid = lax.broadcasted_iota(jnp.int32, (GLA_BLOCK, 1), 0) < rows_in
    a = _pad_rows(a_ref[...], GLA_BLOCK).astype(BF16)
    logf = -_softplus(-(jnp.dot(a, w2_ref[...], preferred_element_type=F32) + gb_ref[...])) / GLA_TAU
    logf = jnp.where(valid, logf, 0.0)
    bc = _cumsum_rows(logf, ch)
    b_last = _segment_last(bc, ch)
    q = _pad_rows(q_ref[...], GLA_BLOCK) * (GLA_DK ** -0.5)
    k = _pad_rows(k_ref[...], GLA_BLOCK)
    v = _pad_rows(v_ref[...], GLA_BLOCK)
    q_scr[...] = q
    k_scr[...] = k
    v_scr[...] = v
    bc_scr[...] = bc

    tt = lax.broadcasted_iota(jnp.int32, (ch, 1), 0)

    def intra(j, carry):
        r0 = pl.multiple_of(j * ch, ch)
        qj = q_scr[pl.ds(r0, ch), :]
        bj = bc_scr[pl.ds(r0, ch), :]
        acc = [jnp.zeros((ch, GLA_DV), F32) for _ in range(GLA_HEADS)]
        for s in range(ch):
            ks = k_scr[pl.ds(r0 + s, 1), :]
            bs = bc_scr[pl.ds(r0 + s, 1), :]
            vs = v_scr[pl.ds(r0 + s, 1), :]
            w = jnp.where(tt >= s, qj * ks * jnp.exp(jnp.minimum(bj - bs, 0.0)), 0.0)
            for h in range(GLA_HEADS):
                att = jnp.sum(w[:, h * hk:(h + 1) * hk], -1, keepdims=True)
                acc[h] = acc[h] + att * vs[:, h * GLA_DV:(h + 1) * GLA_DV]
        o_scr[pl.ds(r0, ch), :] = jnp.concatenate(acc, axis=1)
        return carry

    lax.fori_loop(0, nsub, intra, 0)

    qe = q * jnp.exp(bc)
    kst = k * jnp.exp(b_last - bc)
    dec = jnp.exp(b_last)
    for h in range(GLA_HEADS):
        kst_t = kst[:, h * hk:(h + 1) * hk].T
        dec_t = dec[:, h * hk:(h + 1) * hk].T
        s_h = s_scr[h]
        for j in range(nsub):
            rs = slice(j * ch, (j + 1) * ch)
            o_scr[rs, h * GLA_DV:(h + 1) * GLA_DV] += jnp.dot(qe[rs, h * hk:(h + 1) * hk].astype(BF16), s_h.astype(BF16),
                                                              preferred_element_type=F32)
            s_h = s_h * dec_t[:, j * ch:j * ch + 1] + jnp.dot(kst_t[:, rs].astype(BF16),
                                                              v[rs, h * GLA_DV:(h + 1) * GLA_DV].astype(BF16),
                                                              preferred_element_type=F32)
        s_scr[h] = s_h

    g = _pad_rows(g_ref[...], GLA_BLOCK)
    outs = []
    for h in range(GLA_HEADS):
        o_h = o_scr[:, h * GLA_DV:(h + 1) * GLA_DV]
        o_h = o_h * lax.rsqrt(jnp.mean(o_h * o_h, -1, keepdims=True) + NORM_EPS) * nw_ref[...]
        outs.append(o_h)
    o = jnp.concatenate(outs, axis=1) * _silu(g)
    o_ref[...] = o[:rows_in].astype(o_ref.dtype)

    @pl.when(c == pl.num_programs(1) - 1)
    def _():
        s_out_ref[0] = s_scr[...]


def _gla(y, s0, gate_w2, gate_b, norm_w, bsz):
    m = y.shape[0]
    l = m // bsz
    rows_in = min(GLA_BLOCK, l)
    ch = GLA_CHUNK
    nblk = l // rows_in
    hdk = GLA_HEADS * GLA_DK
    w2 = jnp.zeros((LANE, hdk), BF16).at[ALR_OFF - SMALL_OFF:ALR_OFF - SMALL_OFF + GLA_RANK].set(gate_w2.astype(BF16))

    def rows(width, off):
        assert off % width == 0
        return pl.BlockSpec((rows_in, width), lambda b, c: (b * nblk + c, off // width))

    def const(shape):
        return pl.BlockSpec(shape, lambda b, c: (0,) * len(shape))

    state = pl.BlockSpec((1, GLA_HEADS, GLA_DK, GLA_DV), lambda b, c: (b, 0, 0, 0))
    return pl.pallas_call(
        functools.partial(_gla_kernel, rows_in=rows_in, ch=ch),
        grid=(bsz, nblk),
        in_specs=[rows(hdk, QG_OFF), rows(hdk, KG_OFF), rows(D_GLA, VG_OFF), rows(D_GLA, GG_OFF), rows(LANE, SMALL_OFF),
                  const((LANE, hdk)), const((1, hdk)), const((1, GLA_DV)), state],
        out_specs=[pl.BlockSpec((rows_in, D_GLA), lambda b, c: (b * nblk + c, 0)), state],
        out_shape=[jax.ShapeDtypeStruct((m, D_GLA), BF16), jax.ShapeDtypeStruct((bsz, GLA_HEADS, GLA_DK, GLA_DV), F32)],
        scratch_shapes=[pltpu.VMEM((GLA_HEADS, GLA_DK, GLA_DV), F32), pltpu.VMEM((GLA_BLOCK, hdk), F32),
                        pltpu.VMEM((GLA_BLOCK, hdk), F32), pltpu.VMEM((GLA_BLOCK, D_GLA), F32),
                        pltpu.VMEM((GLA_BLOCK, hdk), F32), pltpu.VMEM((GLA_BLOCK, D_GLA), F32)],
        compiler_params=pltpu.CompilerParams(dimension_semantics=('parallel', 'arbitrary'),
                                             vmem_limit_bytes=VMEM_LIMIT_BYTES),
        name='gla',
    )(y, y, y, y, y, w2, gate_b.reshape(1, hdk), norm_w.reshape(1, GLA_DV), s0)


SSD_R = SSD_HEADS // SSD_GROUPS
SSD_GW = SSD_R * SSD_HEAD_DIM
SSD_CW = SSD_GW + 2 * SSD_STATE
SSD_TAIL = 8


def _ssd_kernel(xs_ref, bm_ref, cm_ref, z_ref, dt_ref, conv0_ref, cw_ref, cb_ref, hp_ref, dskip_ref, nw_ref, h0_ref,
                o_ref, h_out_ref, xbuf, h_scr, *, rows_in):
    g = pl.program_id(1)
    c = pl.program_id(2)
    t = SSD_CHUNK

    @pl.when(c == 0)
    def _():
        xbuf[0:SSD_TAIL, :] = conv0_ref[0, 0]
        h_scr[...] = h0_ref[0, 0]

    xbuf[SSD_TAIL:SSD_TAIL + t, 0:SSD_GW] = _pad_rows(xs_ref[...], t)
    xbuf[SSD_TAIL:SSD_TAIL + t, SSD_GW:SSD_GW + SSD_STATE] = _pad_rows(bm_ref[...], t)
    xbuf[SSD_TAIL:SSD_TAIL + t, SSD_GW + SSD_STATE:SSD_CW] = _pad_rows(cm_ref[...], t)
    conv = cb_ref[0]
    for kk in range(SSD_CONV):
        conv = conv + cw_ref[0, kk:kk + 1, :] * xbuf[SSD_TAIL - (SSD_CONV - 1) + kk:SSD_TAIL - (SSD_CONV - 1) + kk + t, :]
    xbuf[0:SSD_TAIL, :] = xbuf[t:t + SSD_TAIL, :]
    xc = _silu(conv)
    xs = xc[:, 0:SSD_GW]
    bm = xc[:, SSD_GW:SSD_GW + SSD_STATE]
    cm = xc[:, SSD_GW + SSD_STATE:SSD_CW].astype(BF16)

    lane = lax.broadcasted_iota(jnp.int32, (t, LANE), 1)
    row = lax.broadcasted_iota(jnp.int32, (t, LANE), 0)
    dt = pltpu.roll(_pad_rows(dt_ref[...], t), (LANE - SSD_R * g) % LANE, 1)
    dt = jnp.where((lane < SSD_R) & (row < rows_in), _softplus(dt + hp_ref[0, 0:1, :]), 0.0)
    acs = _cumsum_rows(dt * -jnp.exp(hp_ref[0, 1:2, :]), t)
    acs_t = acs.T
    a_last = acs[t - 1:t, :]

    head_of_lane = lax.broadcasted_iota(jnp.int32, (1, SSD_GW), 1) // SSD_HEAD_DIM

    def spread(arr):
        return sum(jnp.where(head_of_lane == r, arr[:, r:r + 1], 0.0) for r in range(SSD_R))

    xdt = xs * spread(dt)
    cb = _dot_nt(cm, bm.astype(BF16))
    tri = lax.broadcasted_iota(jnp.int32, (t, t), 0) >= lax.broadcasted_iota(jnp.int32, (t, t), 1)
    y = jnp.zeros((t, SSD_GW), F32)
    for r in range(SSD_R):
        decay = jnp.where(tri, jnp.exp(jnp.minimum(acs[:, r:r + 1] - acs_t[r:r + 1, :], 0.0)), 0.0)
        x_r = jnp.where(head_of_lane == r, xdt, 0.0).astype(BF16)
        y = y + jnp.dot((cb * decay).astype(BF16), x_r, preferred_element_type=F32)
    h_prev = h_scr[...]
    y = y + jnp.dot(cm, h_prev.astype(BF16), preferred_element_type=F32) * jnp.exp(spread(acs))
    x_st = (xdt * jnp.exp(spread(a_last - acs))).astype(BF16)
    h_new = h_prev * jnp.exp(spread(a_last)) + jnp.dot(bm.T.astype(BF16), x_st, preferred_element_type=F32)
    h_scr[...] = h_new

    y = (y + xs * dskip_ref[0]) * _silu(_pad_rows(z_ref[...], t))
    y = y * lax.rsqrt(jnp.mean(y * y, -1, keepdims=True) + NORM_EPS) * nw_ref[0]
    o_ref[...] = y[:rows_in].astype(o_ref.dtype)

    @pl.when(c == pl.num_programs(2) - 1)
    def _():
        h_out_ref[0, 0] = h_new


def _ssd(y, conv0, h0, conv_w, conv_b, dt_bias, a_log, d_skip, norm_w, bsz):
    m = y.shape[0]
    l = m // bsz
    rows_in = min(SSD_CHUNK, l)
    nblk = l // rows_in
    ng = SSD_GROUPS

    def per_group(v):
        xs = v[..., :D_SSD].reshape(v.shape[:-1] + (ng, SSD_GW))
        bm = v[..., D_SSD:D_SSD + ng * SSD_STATE].reshape(v.shape[:-1] + (ng, SSD_STATE))
        cm = v[..., D_SSD + ng * SSD_STATE:].reshape(v.shape[:-1] + (ng, SSD_STATE))
        return jnp.moveaxis(jnp.concatenate([xs, bm, cm], axis=-1), -2, 0)

    conv0_g = jnp.moveaxis(per_group(jnp.pad(conv0, ((0, 0), (SSD_TAIL - (SSD_CONV - 1), 0), (0, 0)))), 0, 1)
    cw_g = per_group(conv_w)
    cb_g = per_group(conv_b[None])
    hp = jnp.zeros((ng, 8, LANE), F32)
    hp = hp.at[:, 0, :SSD_R].set(dt_bias.reshape(ng, SSD_R)).at[:, 1, :SSD_R].set(a_log.reshape(ng, SSD_R))
    dskip_g = jnp.repeat(d_skip, SSD_HEAD_DIM).reshape(ng, 1, SSD_GW)
    nw_g = norm_w.reshape(ng, 1, SSD_GW)
    h0_t = h0.reshape(bsz, ng, SSD_GW, SSD_STATE).transpose(0, 1, 3, 2)

    def rows(width, off):
        assert off % width == 0
        return pl.BlockSpec((rows_in, width), lambda b, g, c: (b * nblk + c, off // width + g))

    def grp(shape):
        return pl.BlockSpec((1,) + shape, lambda b, g, c: (g,) + (0,) * len(shape))

    state = pl.BlockSpec((1, 1, SSD_STATE, SSD_GW), lambda b, g, c: (b, g, 0, 0))
    out, h_t = pl.pallas_call(
        functools.partial(_ssd_kernel, rows_in=rows_in),
        grid=(bsz, ng, nblk),
        in_specs=[rows(SSD_GW, XBC_OFF), rows(SSD_STATE, XBC_OFF + D_SSD), rows(SSD_STATE, XBC_OFF + D_SSD + ng * SSD_STATE),
                  rows(SSD_GW, Z_OFF), pl.BlockSpec((rows_in, LANE), lambda b, g, c: (b * nblk + c, SMALL_OFF // LANE)),
                  pl.BlockSpec((1, 1, SSD_TAIL, SSD_CW), lambda b, g, c: (b, g, 0, 0)),
                  grp((SSD_CONV, SSD_CW)), grp((1, SSD_CW)), grp((8, LANE)), grp((1, SSD_GW)), grp((1, SSD_GW)), state],
        out_specs=[pl.BlockSpec((rows_in, SSD_GW), lambda b, g, c: (b * nblk + c, g)), state],
        out_shape=[jax.ShapeDtypeStruct((m, D_SSD), BF16), jax.ShapeDtypeStruct((bsz, ng, SSD_STATE, SSD_GW), F32)],
        scratch_shapes=[pltpu.VMEM((SSD_TAIL + SSD_CHUNK, SSD_CW), F32), pltpu.VMEM((SSD_STATE, SSD_GW), F32)],
        compiler_params=pltpu.CompilerParams(dimension_semantics=('parallel', 'parallel', 'arbitrary'),
                                             vmem_limit_bytes=VMEM_LIMIT_BYTES),
        name='ssd',
    )(y, y, y, y, y, conv0_g, cw_g, cb_g, hp, dskip_g, nw_g, h0_t)
    h_new = h_t.transpose(0, 1, 3, 2).reshape(bsz, SSD_HEADS, SSD_HEAD_DIM, SSD_STATE)
    return out, h_new


def _split(x, sizes):
    return jnp.split(x, np.cumsum(sizes)[:-1].tolist(), axis=-1)


def _layernorm(x, g, b):
    mu = jnp.mean(x, -1, keepdims=True)
    var = jnp.mean(jnp.square(x - mu), -1, keepdims=True)
    return (x - mu) * lax.rsqrt(var + LN_EPS) * g + b


def _rmsnorm(x, w):
    return x * lax.rsqrt(jnp.mean(x * x, -1, keepdims=True) + NORM_EPS) * w


def _masked_softmax(s, mask):
    s = jnp.where(mask, s.astype(F32), -jnp.inf)
    m = jnp.max(s, -1, keepdims=True)
    m = jnp.where(jnp.isfinite(m), m, 0.0)
    e = jnp.exp(s - m)
    d = jnp.sum(e, -1, keepdims=True)
    return e / jnp.where(d > 0, d, 1.0)


def _t5_bucket(dist):
    n = jnp.maximum(dist, 0)
    max_exact = N_BUCKETS // 2
    nf = jnp.maximum(n, 1).astype(F32)
    large = max_exact + (jnp.log(nf / max_exact) / math.log(MAX_DISTANCE / max_exact) * (N_BUCKETS - max_exact)).astype(jnp.int32)
    large = jnp.minimum(large, N_BUCKETS - 1)
    return jnp.where(n < max_exact, n, large)


def _ssd_scan(x, dt, a, bm, cm, h0):
    b, l, h, p = x.shape
    g, n = bm.shape[2], bm.shape[3]
    r = h // g
    lc = min(SSD_CHUNK, l)
    pad = (-l) % lc
    x, dt, bm, cm = [jnp.pad(t.astype(F32), ((0, 0), (0, pad)) + ((0, 0),) * (t.ndim - 2)) for t in (x, dt, bm, cm)]
    c = (l + pad) // lc
    xdt = (x * dt[..., None]).reshape(b, c, lc, g, r, p)
    acs = jnp.cumsum((dt * a).reshape(b, c, lc, g, r), axis=2)
    bm = bm.reshape(b, c, lc, g, n)
    cm = cm.reshape(b, c, lc, g, n)
    causal = np.tril(np.ones((lc, lc), bool))[None, None, :, :, None, None]
    decay = jnp.exp(jnp.where(causal, acs[:, :, :, None] - acs[:, :, None], -jnp.inf))
    cb = jnp.einsum('bctgn,bcsgn->bctsg', cm, bm)
    y_diag = jnp.einsum('bctsg,bctsgr,bcsgrp->bctgrp', cb, decay, xdt)
    decay_st = jnp.exp(acs[:, :, -1:] - acs)
    states = jnp.einsum('bcsgn,bcsgr,bcsgrp->bcgrpn', bm, decay_st, xdt)
    chunk_decay = jnp.exp(acs[:, :, -1])

    def step(hc, inp):
        st, dec = inp
        return hc * dec[..., None, None] + st, hc

    h_last, h_prev = lax.scan(step, h0.astype(F32).reshape(b, g, r, p, n), (jnp.moveaxis(states, 1, 0), jnp.moveaxis(chunk_decay, 1, 0)))
    h_prev = jnp.moveaxis(h_prev, 0, 1)
    y_off = jnp.einsum('bctgn,bcgrpn,bctgr->bctgrp', cm, h_prev, jnp.exp(acs))
    y = (y_diag + y_off).reshape(b, c * lc, h, p)[:, :l]
    return y, h_last.reshape(b, h, p, n)


def _ssd_mixer(z, xbc, dt, conv0, h0, conv_w, conv_b, dt_bias, a_log, d_skip, norm_w):
    b, l, _ = xbc.shape
    xpad = jnp.concatenate([conv0.astype(xbc.dtype), xbc], axis=1)
    conv = sum((xpad[:, k:k + l] * conv_w[k] for k in range(SSD_CONV)), conv_b)
    conv_new = xpad[:, -(SSD_CONV - 1):]
    xbc = jax.nn.silu(conv)
    xs, bm, cm = _split(xbc, (D_SSD, SSD_GROUPS * SSD_STATE, SSD_GROUPS * SSD_STATE))
    xs = xs.reshape(b, l, SSD_HEADS, SSD_HEAD_DIM)
    bm = bm.reshape(b, l, SSD_GROUPS, SSD_STATE)
    cm = cm.reshape(b, l, SSD_GROUPS, SSD_STATE)
    dt = jax.nn.softplus((dt + dt_bias).astype(F32))
    a = -jnp.exp(a_log.astype(F32))
    y, h_new = _ssd_scan(xs, dt, a, bm, cm, h0)
    y = y + xs.astype(F32) * d_skip.astype(F32)[:, None]
    y = (y.reshape(b, l, D_SSD) * jax.nn.silu(z.astype(F32))).reshape(b, l, SSD_GROUPS, D_SSD // SSD_GROUPS)
    y = _rmsnorm(y, norm_w.reshape(SSD_GROUPS, D_SSD // SSD_GROUPS)).reshape(b, l, D_SSD)
    return y, conv_new, h_new


def _gla_scan(q, k, v, logf, s0):
    b, l, h, dk = q.shape
    dv = v.shape[-1]
    lc = min(GLA_CHUNK, l)
    pad = (-l) % lc
    q, k, v, logf = [jnp.pad(t.astype(F32), ((0, 0), (0, pad), (0, 0), (0, 0))) for t in (q, k, v, logf)]
    c = (l + pad) // lc
    q, k, logf = [t.reshape(b, c, lc, h, dk) for t in (q, k, logf)]
    v = v.reshape(b, c, lc, h, dv)
    bc = jnp.cumsum(logf, axis=2)
    causal = np.tril(np.ones((lc, lc), bool))[None, None, :, :, None, None]
    rel = jnp.exp(jnp.where(causal, bc[:, :, :, None] - bc[:, :, None], -jnp.inf))
    att = jnp.einsum('bcthd,bcshd,bctshd->bchts', q, k, rel)
    o_intra = jnp.einsum('bchts,bcshv->bcthv', att, v)
    k_st = k * jnp.exp(bc[:, :, -1:] - bc)
    chunk_states = jnp.einsum('bcshd,bcshv->bchdv', k_st, v)
    chunk_decay = jnp.exp(bc[:, :, -1])

    def step(s, inp):
        st, dec = inp
        return s * dec[..., None] + st, s

    s_last, s_prev = lax.scan(step, s0.astype(F32), (jnp.moveaxis(chunk_states, 1, 0), jnp.moveaxis(chunk_decay, 1, 0)))
    s_prev = jnp.moveaxis(s_prev, 0, 1)
    o_inter = jnp.einsum('bcthd,bchdv->bcthv', q * jnp.exp(bc), s_prev)
    o = (o_intra + o_inter).reshape(b, c * lc, h, dv)[:, :l]
    return o, s_last


def _gla_mixer(q, k, v, g, a_lr, s0, gate_w2, gate_b, norm_w):
    b, l, _ = q.shape
    q = q.reshape(b, l, GLA_HEADS, GLA_DK) * (GLA_DK ** -0.5)
    k = k.reshape(b, l, GLA_HEADS, GLA_DK)
    v = v.reshape(b, l, GLA_HEADS, GLA_DV)
    logf = jax.nn.log_sigmoid((a_lr @ gate_w2 + gate_b).astype(F32)) / GLA_TAU
    o, s_new = _gla_scan(q, k, v, logf.reshape(b, l, GLA_HEADS, GLA_DK), s0)
    o = _rmsnorm(o, norm_w).reshape(b, l, D_GLA) * jax.nn.silu(g.astype(F32))
    return o, s_new


def _nsa_attend(q, gates, kv_full, win_ctx, q_pos0, w_pos0, cmp_pe, cmp_w, rel_bias):
    b, lq = q.shape[:2]
    t_len = kv_full.shape[1]
    G, R, HD = NSA_KV_HEADS, NSA_REP, NSA_HEAD_DIM
    kc, vc, ks, vs = [kv_full[:, :, i] for i in range(4)]
    n_cmp = (t_len - CMP_LEN) // CMP_STRIDE + 1
    cidx = (np.arange(n_cmp)[:, None] * CMP_STRIDE + np.arange(CMP_LEN)[None]).astype(np.int32)
    k_cmp = jnp.einsum('bnlgd,lde->bnge', kc[:, cidx] + cmp_pe[0][:, None], cmp_w[0]).astype(F32)
    v_cmp = jnp.einsum('bnlgd,lde->bnge', vc[:, cidx] + cmp_pe[1][:, None], cmp_w[1]).astype(F32)
    cmp_end = (np.arange(n_cmp) * CMP_STRIDE + CMP_LEN - 1).astype(np.int32)
    n_sel = -(-t_len // SEL_LEN)
    sel_pad = n_sel * SEL_LEN - t_len

    def sel_blocks(t):
        t = jnp.pad(t, ((0, 0), (0, sel_pad), (0, 0), (0, 0)))
        return t.reshape(b, n_sel, SEL_LEN, G, HD).transpose(0, 3, 1, 2, 4)

    ks_b, vs_b = sel_blocks(ks), sel_blocks(vs)
    c_lo = np.arange(n_cmp) * CMP_STRIDE
    s_lo = np.arange(n_sel) * SEL_LEN
    cover = jnp.asarray(((c_lo[:, None] < s_lo[None] + SEL_LEN) & (c_lo[:, None] + CMP_LEN > s_lo[None])).astype(np.float32))
    top_n = min(TOP_N, n_sel)
    kw_pad = jnp.pad(win_ctx[:, :, 0], ((0, 0), (WINDOW, 0), (0, 0), (0, 0)))
    vw_pad = jnp.pad(win_ctx[:, :, 1], ((0, 0), (WINDOW, 0), (0, 0), (0, 0)))
    qb = Q_BLOCK if lq % Q_BLOCK == 0 else lq
    nqb = lq // qb
    scale = HD ** -0.5
    bias_gr = rel_bias.reshape(N_BUCKETS, G, R)
    bi = jnp.arange(b)[:, None, None, None]
    gi = jnp.arange(G)[None, None, :, None]
    gi5 = jnp.arange(G)[None, None, :, None, None]
    blk = jnp.arange(n_sel)

    def block(args):
        i, qblk, gblk = args
        q0 = q_pos0 + i * qb
        tpos = q0 + jnp.arange(qb)
        qg = qblk.reshape(b, qb, G, R, HD).astype(F32) * scale
        bias_c = rel_bias[_t5_bucket(tpos[:, None] - cmp_end[None])].astype(F32).reshape(qb, n_cmp, G, R).transpose(0, 2, 3, 1)
        s_c = jnp.einsum('bqgrd,bngd->bqgrn', qg, k_cmp) + bias_c
        p_c = _masked_softmax(s_c, (cmp_end[None] <= tpos[:, None])[None, :, None, None, :])
        o_c = jnp.einsum('bqgrn,bngd->bqgrd', p_c, v_cmp)
        imp = jnp.einsum('bqgrn,nj->bqgj', p_c, cover)
        cur = (tpos // SEL_LEN)[:, None]
        forced = (blk == 0) | (blk == cur) | (blk == cur - 1)
        valid = blk * SEL_LEN <= tpos[:, None]
        score = jnp.where(forced[None, :, None, :], FORCE_SCORE, jnp.where(valid[None, :, None, :], imp, -FORCE_SCORE))
        _, sel = lax.top_k(score, top_n)
        k_sel = ks_b[bi, gi, sel].astype(F32)
        v_sel = vs_b[bi, gi, sel].astype(F32)
        spos = sel[..., None] * SEL_LEN + jnp.arange(SEL_LEN)
        dist_s = tpos[None, :, None, None, None] - spos
        bias_s = jnp.moveaxis(bias_gr[_t5_bucket(dist_s), gi5].astype(F32), -1, 3)
        s_s = jnp.einsum('bqgrd,bqgksd->bqgrks', qg, k_sel) + bias_s
        p_s = _masked_softmax(s_s.reshape(b, qb, G, R, top_n * SEL_LEN), (dist_s >= 0).reshape(b, qb, G, 1, top_n * SEL_LEN)).reshape(s_s.shape)
        o_s = jnp.einsum('bqgrks,bqgksd->bqgrd', p_s, v_sel)
        start = q0 - w_pos0
        k_w = lax.dynamic_slice_in_dim(kw_pad, start, WINDOW + qb, axis=1).astype(F32)
        v_w = lax.dynamic_slice_in_dim(vw_pad, start, WINDOW + qb, axis=1).astype(F32)
        wpos = q0 - WINDOW + jnp.arange(WINDOW + qb)
        dist_w = tpos[:, None] - wpos[None]
        mask_w = (dist_w >= 0) & (dist_w < WINDOW) & (wpos[None] >= w_pos0)
        bias_w = rel_bias[_t5_bucket(dist_w)].astype(F32).reshape(qb, WINDOW + qb, G, R).transpose(0, 2, 3, 1)
        s_w = jnp.einsum('bqgrd,bkgd->bqgrk', qg, k_w) + bias_w
        p_w = _masked_softmax(s_w, mask_w[None, :, None, None, :])
        o_w = jnp.einsum('bqgrk,bkgd->bqgrd', p_w, v_w)
        gg = gblk.reshape(b, qb, G, R, 3).astype(F32)
        return gg[..., 0:1] * o_c + gg[..., 1:2] * o_s + gg[..., 2:3] * o_w

    q_blocks = jnp.swapaxes(q.reshape(b, nqb, qb, NSA_HEADS, HD), 0, 1)
    g_blocks = jnp.swapaxes(gates.reshape(b, nqb, qb, NSA_HEADS, 3), 0, 1)
    outs = lax.map(block, (jnp.arange(nqb), q_blocks, g_blocks))
    return jnp.moveaxis(outs, 0, 1).reshape(b, lq, D_NSA)


def _mixer(x, lp, rel_bias, tables, conv0, ssd_h0, gla_s0, nsa_past, win_past, past_len):
    b, l, _ = x.shape
    y = _matmul(x.reshape(b * l, D_MODEL).astype(BF16), lp['w_in'], name='in_proj')
    y3 = y.reshape(b, l, D_IN_PAD)

    def part(off, width):
        return y3[..., off:off + width]

    z, xbc, dt = part(Z_OFF, D_SSD), part(XBC_OFF, SSD_CONV_DIM), part(DT_OFF, SSD_HEADS)
    q_g, k_g = part(QG_OFF, GLA_HEADS * GLA_DK), part(KG_OFF, GLA_HEADS * GLA_DK)
    v_g, g_g, a_g = part(VG_OFF, D_GLA), part(GG_OFF, D_GLA), part(ALR_OFF, GLA_RANK)
    y_ssd, conv_new, h_new = _ssd_mixer(z, xbc, dt, conv0, ssd_h0, lp['ssd_conv_w'], lp['ssd_conv_b'], lp['ssd_dt_bias'], lp['ssd_a_log'], lp['ssd_d'], lp['ssd_norm_w'])
    rows = part(KV_OFF, 4 * KVW).reshape(b, l, 4, NSA_KV_HEADS, NSA_HEAD_DIM)
    win_rows = part(KV_OFF + 4 * KVW, 2 * KVW).reshape(b, l, 2, NSA_KV_HEADS, NSA_HEAD_DIM)
    if nsa_past is None:
        win_new = win_rows[:, -min(WINDOW, l):]
        y_nsa = _nsa_prompt(y, _nsa_compress_prompt(y, lp['nsa_cmp_pe'], lp['nsa_cmp_w'], b), tables, b).reshape(b, l, D_NSA)
    else:
        kv_full = jnp.concatenate([nsa_past.astype(rows.dtype), rows], axis=1)
        win_ctx = jnp.concatenate([win_past.astype(rows.dtype), win_rows], axis=1)
        win_new = win_ctx[:, -win_past.shape[1]:]
        g_n = jnp.concatenate([part(GATE_OFF + g * LANE, 3 * NSA_REP) for g in range(NSA_KV_HEADS)], axis=-1)
        gates = jax.nn.sigmoid(g_n).reshape(b, l, NSA_HEADS, 3)
        q_n = part(QN_OFF, D_NSA).reshape(b, l, NSA_HEADS, NSA_HEAD_DIM)
        y_nsa = _nsa_attend(q_n, gates, kv_full, win_ctx, past_len, past_len - win_past.shape[1], lp['nsa_cmp_pe'], lp['nsa_cmp_w'], rel_bias)
    y_gla, s_new = _gla_mixer(q_g, k_g, v_g, g_g, a_g, gla_s0, lp['gla_gate_w2'], lp['gla_gate_b'], lp['gla_norm_w'])
    y_out = _proj(jnp.concatenate([y_ssd, y_nsa, y_gla], axis=-1), lp['w_out'], name='out_proj')
    return y_out, rows, win_new, conv_new, h_new, s_new


def _cross_attn(x, mem_kv, wq, wo):
    b, l, _ = x.shape
    q = _proj(x, wq, name='xattn_q').reshape(b, l, X_HEADS, X_HEAD_DIM) * (X_HEAD_DIM ** -0.5)
    s = jnp.einsum('blhd,bmhd->bhlm', q, mem_kv[:, :, 0].astype(F32))
    p = jax.nn.softmax(s, axis=-1)
    o = jnp.einsum('bhlm,bmhd->blhd', p, mem_kv[:, :, 1].astype(F32)).reshape(b, l, D_MODEL)
    return _proj(o, wo, name='xattn_o')


def _sqrelu_ffn(x, w1, w2):
    h = _proj(x, w1, out_dtype=BF16, act='sqrelu', name='ffn_up')
    return _proj(h, w2, name='ffn_down')


def _layer(x, lp, rel_bias, tables, mem_kv, conv0, ssd_h0, gla_s0, nsa_past, win_past, past_len):
    h, rows, win, conv, hs, sg = _mixer(x, lp, rel_bias, tables, conv0, ssd_h0, gla_s0, nsa_past, win_past, past_len)
    x = _layernorm(DN_ALPHA * x + h, lp['ln_g'][0], lp['ln_b'][0])
    x = _layernorm(DN_ALPHA * x + _cross_attn(x, mem_kv, lp['x_wq'], lp['x_wo']), lp['ln_g'][1], lp['ln_b'][1])
    x = _layernorm(DN_ALPHA * x + _sqrelu_ffn(x, lp['ffn_w1'], lp['ffn_w2']), lp['ln_g'][2], lp['ln_b'][2])
    return x, (rows, win, conv, hs, sg)


def _layer_flat(x, xb, bsz, lp, rel_bias, tables, mem_kv, conv0, ssd_h0, gla_s0, nsa_past, win_past, past_len):
    m = x.shape[0]
    l = m // bsz
    assert l >= SSD_CONV - 1
    y = _matmul(xb, lp['w_in'], name='in_proj')
    y3 = y.reshape(bsz, l, D_IN_PAD)
    y_ssd, h_new = _ssd(y, conv0, ssd_h0, lp['ssd_conv_w'], lp['ssd_conv_b'], lp['ssd_dt_bias'], lp['ssd_a_log'],
                        lp['ssd_d'], lp['ssd_norm_w'], bsz)
    conv_new = y3[:, l - (SSD_CONV - 1):, XBC_OFF:XBC_OFF + SSD_CONV_DIM]
    y_gla, s_new = _gla(y, gla_s0, lp['gla_gate_w2'], lp['gla_gate_b'], lp['gla_norm_w'], bsz)
    rows = y3[..., KV_OFF:KV_OFF + 4 * KVW].reshape(bsz, l, 4, NSA_KV_HEADS, NSA_HEAD_DIM)
    win_rows = y3[..., KV_OFF + 4 * KVW:KV_OFF + 6 * KVW].reshape(bsz, l, 2, NSA_KV_HEADS, NSA_HEAD_DIM)
    if nsa_past is None:
        win_new = win_rows[:, -min(WINDOW, l):]
        y_nsa = _nsa_prompt(y, _nsa_compress_prompt(y, lp['nsa_cmp_pe'], lp['nsa_cmp_w'], bsz), tables, bsz)
    else:
        kv_full = jnp.concatenate([nsa_past, rows], axis=1)
        win_ctx = jnp.concatenate([win_past, win_rows], axis=1)
        win_new = win_ctx[:, -win_past.shape[1]:]
        g_n = jnp.concatenate([y3[..., GATE_OFF + g * LANE:GATE_OFF + g * LANE + 3 * NSA_REP] for g in range(NSA_KV_HEADS)], axis=-1)
        gates = jax.nn.sigmoid(g_n).reshape(bsz, l, NSA_HEADS, 3)
        q_n = y3[..., QN_OFF:QN_OFF + D_NSA].reshape(bsz, l, NSA_HEADS, NSA_HEAD_DIM)
        y_nsa = _nsa_attend(q_n, gates, kv_full, win_ctx, past_len, past_len - win_past.shape[1], lp['nsa_cmp_pe'],
                            lp['nsa_cmp_w'], rel_bias).reshape(m, D_NSA).astype(BF16)
    h = _matmul(jnp.concatenate([y_ssd, y_nsa, y_gla], axis=-1), lp['w_out'], name='out_proj')
    x, xb = _add_layernorm(x, h, lp['ln_g'][0], lp['ln_b'][0])
    o = _cross_attention(_matmul(xb, lp['x_wq'], name='xattn_q'), mem_kv, bsz)
    x, xb = _add_layernorm(x, _matmul(o, lp['x_wo'], name='xattn_o'), lp['ln_g'][1], lp['ln_b'][1])
    hidden = _matmul(xb, lp['ffn_w1'], out_dtype=BF16, act='sqrelu', name='ffn_up')
    x, xb = _add_layernorm(x, _matmul(hidden, lp['ffn_w2'], name='ffn_down'), lp['ln_g'][2], lp['ln_b'][2])
    return x, xb, (rows, win_new, conv_new, h_new, s_new)


def kernel(x_prompt, x_sample, cache_nsa_kv, cache_nsa_win, state_ssd, state_ssd_conv, state_gla, cache_mem_kv,
           page_table, mem_prompt, w_in, ssd_conv_w, ssd_conv_b, ssd_dt_bias, ssd_a_log, ssd_d, ssd_norm_w,
           nsa_cmp_pe, nsa_cmp_w, rel_bias, gla_gate_w2, gla_gate_b, gla_norm_w, w_out, x_wq, x_wkv, x_wo,
           ffn_w1, ffn_w2, ln_g, ln_b):
    bp = x_prompt.shape[0]
    bs = x_sample.shape[0]
    past_len = page_table.shape[1] * cache_nsa_kv.shape[2]
    xp, xs = x_prompt.reshape(-1, D_MODEL), x_sample.reshape(-1, D_MODEL)
    xpb, xsb = xp.astype(BF16), xs.astype(BF16)
    st_p, st_s, mem_p = [], [], []
    tables = _nsa_prompt_bias_tables(rel_bias)
    for l in range(DEPTH):
        lp = dict(w_in=_reorder_w_in(w_in[l]), ssd_conv_w=ssd_conv_w[l], ssd_conv_b=ssd_conv_b[l],
                  ssd_dt_bias=ssd_dt_bias[l], ssd_a_log=ssd_a_log[l], ssd_d=ssd_d[l], ssd_norm_w=ssd_norm_w[l],
                  nsa_cmp_pe=nsa_cmp_pe[l], nsa_cmp_w=nsa_cmp_w[l], gla_gate_w2=gla_gate_w2[l], gla_gate_b=gla_gate_b[l],
                  gla_norm_w=gla_norm_w[l], w_out=w_out[l].astype(BF16), x_wq=x_wq[l].astype(BF16),
                  x_wo=x_wo[l].astype(BF16), ffn_w1=ffn_w1[l].astype(BF16), ffn_w2=ffn_w2[l].astype(BF16),
                  ln_g=ln_g[l], ln_b=ln_b[l])
        mem_kv_p = _proj(mem_prompt, x_wkv[l].astype(BF16), name='mem_kv')
        conv0 = jnp.zeros((bp, SSD_CONV - 1, SSD_CONV_DIM), F32)
        h0 = jnp.zeros((bp, SSD_HEADS, SSD_HEAD_DIM, SSD_STATE), F32)
        s0 = jnp.zeros((bp, GLA_HEADS, GLA_DK, GLA_DV), F32)
        xp, xpb, stp = _layer_flat(xp, xpb, bp, lp, rel_bias, tables, mem_kv_p, conv0, h0, s0, None, None, 0)
        st_p.append(stp)
        mem_p.append(mem_kv_p.reshape(bp, N_MEM, 2, X_HEADS, X_HEAD_DIM))
        past = cache_nsa_kv[l][page_table].reshape(bs, past_len, 4, NSA_KV_HEADS, NSA_HEAD_DIM)
        xs, xsb, sts = _layer_flat(xs, xsb, bs, lp, rel_bias, tables, cache_mem_kv[l].reshape(bs, N_MEM, 2 * D_MODEL),
                                   state_ssd_conv[l], state_ssd[l], state_gla[l], past, cache_nsa_win[l], past_len)
        st_s.append(sts)
    p_rows, p_win, p_conv, p_ssd, p_gla = [jnp.stack(s) for s in zip(*st_p)]
    s_rows, s_win, s_conv, s_ssd, s_gla = [jnp.stack(s) for s in zip(*st_s)]
    p_mem = jnp.stack(mem_p)
    return (xp.reshape(x_prompt.shape), xs.reshape(x_sample.shape), p_rows, s_rows, p_win, s_win, p_ssd, s_ssd,
            p_conv, s_conv, p_gla, s_gla, p_mem)
```

```python
import functools
import math

import jax
import jax.numpy as jnp
import numpy as np
from jax import lax
from jax.experimental import pallas as pl
from jax.experimental.pallas import tpu as pltpu

F32 = jnp.float32
BF16 = jnp.bfloat16

D_MODEL = 4096
DEPTH = 2
PAGE_SIZE = 128
D_MIX = D_MODEL
D_SSD = D_MIX // 2
SSD_HEAD_DIM = 64
SSD_HEADS = D_SSD // SSD_HEAD_DIM
SSD_GROUPS = 8
SSD_STATE = 128
SSD_CONV = 4
SSD_CHUNK = 128
SSD_CONV_DIM = D_SSD + 2 * SSD_GROUPS * SSD_STATE
D_NSA = D_MIX // 4
NSA_HEAD_DIM = 128
NSA_HEADS = D_NSA // NSA_HEAD_DIM
NSA_KV_HEADS = 2
NSA_REP = NSA_HEADS // NSA_KV_HEADS
CMP_LEN = 32
CMP_STRIDE = 16
SEL_LEN = 64
TOP_N = 16
WINDOW = 512
Q_BLOCK = 128
FORCE_SCORE = 1e4
D_GLA = D_MIX - D_SSD - D_NSA
GLA_HEADS = 4
GLA_DV = D_GLA // GLA_HEADS
GLA_DK = GLA_DV // 2
GLA_RANK = 16
GLA_TAU = 16.0
GLA_CHUNK = 16
N_MEM = 256
X_HEADS = 4
X_HEAD_DIM = D_MODEL // X_HEADS
D_FF = 4 * D_MODEL
N_BUCKETS = 32
MAX_DISTANCE = 128
LN_EPS = 1e-5
NORM_EPS = 1e-6
DN_ALPHA = (2 * DEPTH) ** 0.25
IN_SPLITS = (D_SSD, SSD_CONV_DIM, SSD_HEADS, NSA_HEADS * NSA_HEAD_DIM) + (NSA_KV_HEADS * NSA_HEAD_DIM,) * 6 + (
    3 * NSA_HEADS, GLA_HEADS * GLA_DK, GLA_HEADS * GLA_DK, D_GLA, D_GLA, GLA_RANK)
D_IN = sum(IN_SPLITS)

LANE = 128
VMEM_LIMIT_BYTES = 56 * 1024 * 1024

KVW = NSA_KV_HEADS * NSA_HEAD_DIM
Z_OFF = 0
XBC_OFF = Z_OFF + D_SSD
QN_OFF = XBC_OFF + SSD_CONV_DIM
VG_OFF = QN_OFF + NSA_HEADS * NSA_HEAD_DIM
GG_OFF = VG_OFF + D_GLA
KV_OFF = GG_OFF + D_GLA
QG_OFF = KV_OFF + 6 * KVW
KG_OFF = QG_OFF + GLA_HEADS * GLA_DK
SMALL_OFF = KG_OFF + GLA_HEADS * GLA_DK
DT_OFF = SMALL_OFF
ALR_OFF = SMALL_OFF + SSD_HEADS
GATE_OFF = SMALL_OFF + LANE
D_IN_PAD = 12288
NEG = -1e30


def _in_proj_column_map():
    off = np.cumsum((0,) + IN_SPLITS)
    z, xbc, dt, qn = off[0], off[1], off[2], off[3]
    kv0, gn, qg, kg, vg, gg, alr = off[4], off[10], off[11], off[12], off[13], off[14], off[15]
    cmap = np.full((D_IN_PAD,), -1, np.int64)
    cmap[Z_OFF:Z_OFF + D_SSD] = z + np.arange(D_SSD)
    cmap[XBC_OFF:XBC_OFF + SSD_CONV_DIM] = xbc + np.arange(SSD_CONV_DIM)
    cmap[QN_OFF:QN_OFF + D_NSA] = qn + np.arange(D_NSA)
    cmap[KV_OFF:KV_OFF + 6 * KVW] = kv0 + np.arange(6 * KVW)
    cmap[QG_OFF:QG_OFF + 512] = qg + np.arange(512)
    cmap[KG_OFF:KG_OFF + 512] = kg + np.arange(512)
    cmap[VG_OFF:VG_OFF + D_GLA] = vg + np.arange(D_GLA)
    cmap[GG_OFF:GG_OFF + D_GLA] = gg + np.arange(D_GLA)
    cmap[DT_OFF:DT_OFF + SSD_HEADS] = dt + np.arange(SSD_HEADS)
    cmap[ALR_OFF:ALR_OFF + GLA_RANK] = alr + np.arange(GLA_RANK)
    for g in range(NSA_KV_HEADS):
        cmap[GATE_OFF + g * LANE:GATE_OFF + g * LANE + 3 * NSA_REP] = gn + g * 3 * NSA_REP + np.arange(3 * NSA_REP)
    return cmap


def _reorder_w_in(w):
    cmap = _in_proj_column_map()
    wb = jnp.take(w.astype(BF16), jnp.asarray(np.maximum(cmap, 0), jnp.int32), axis=1)
    return jnp.where(jnp.asarray(cmap >= 0)[None, :], wb, jnp.zeros((), BF16))


def _dot_nt(a, b):
    return lax.dot_general(a, b, (((1,), (1,)), ((), ())), preferred_element_type=F32)


NSA_L = 2048
NSA_QB = Q_BLOCK
NSA_NQB = NSA_L // NSA_QB
NSA_NCP = NSA_L // CMP_STRIDE
NSA_NSEL = NSA_L // SEL_LEN
CMP_HALF = CMP_STRIDE * NSA_HEAD_DIM


def _t5_bucket_np(dist):
    n = np.maximum(dist, 0)
    max_exact = N_BUCKETS // 2
    nf = np.maximum(n, 1).astype(np.float32)
    large = max_exact + (np.log(nf / max_exact) / np.float32(math.log(MAX_DISTANCE / max_exact)) * (N_BUCKETS - max_exact)).astype(np.int32)
    return np.where(n < max_exact, n, np.minimum(large, N_BUCKETS - 1)).astype(np.int32)


def _nsa_prompt_bias_tables(rel_bias):
    tq = np.arange(NSA_QB)[:, None]
    ts = np.arange(NSA_QB)[None, :]

    def table(idx):
        t = rel_bias[jnp.asarray(idx)]
        t = jnp.moveaxis(t, -1, 0).reshape((NSA_KV_HEADS, NSA_REP) + idx.shape)
        t = jnp.moveaxis(t, 1, -3)
        return t.reshape(t.shape[:-3] + (NSA_REP * NSA_QB, idx.shape[-1]))

    b_diag = table(_t5_bucket_np(tq - ts))
    b_prev = table(_t5_bucket_np(NSA_QB + tq - ts))
    b_far = table(np.full((NSA_QB, NSA_QB), N_BUCKETS - 1, np.int32))
    i = np.arange(NSA_NQB)[:, None, None]
    n = np.arange(NSA_NCP)[None, None, :]
    b_cmp = table(_t5_bucket_np(i * NSA_QB + tq[None] - (n * CMP_STRIDE + CMP_LEN - 1)))
    return b_diag, b_prev, b_far, b_cmp


def _nsa_compress_kernel(a_ref, pe_ref, w_ref, o_ref):
    a = a_ref[0, 0, 0]
    lo = jnp.dot((a + pe_ref[0, 0]).astype(BF16), w_ref[0, 0], preferred_element_type=F32)
    hi = jnp.dot((a + pe_ref[0, 1]).astype(BF16), w_ref[0, 1], preferred_element_type=F32)
    o_ref[0, 0, 0] = lo + pltpu.roll(hi, NSA_NCP - 1, 0)


def _nsa_compress_prompt(y, cmp_pe, cmp_w, bsz):
    a = y[:, KV_OFF:KV_OFF + 2 * KVW].reshape(bsz, NSA_NCP, CMP_STRIDE, 2, NSA_KV_HEADS, NSA_HEAD_DIM)
    a = a.transpose(0, 3, 4, 1, 2, 5).reshape(bsz, 2, NSA_KV_HEADS, NSA_NCP, CMP_HALF)
    pe = cmp_pe.reshape(2, 2, 1, CMP_HALF)
    w = cmp_w.astype(BF16).reshape(2, 2, CMP_HALF, NSA_HEAD_DIM)
    return pl.pallas_call(
        _nsa_compress_kernel,
        grid=(bsz, 2, NSA_KV_HEADS),
        in_specs=[pl.BlockSpec((1, 1, 1, NSA_NCP, CMP_HALF), lambda b, c, g: (b, c, g, 0, 0)),
                  pl.BlockSpec((1, 2, 1, CMP_HALF), lambda b, c, g: (c, 0, 0, 0)),
                  pl.BlockSpec((1, 2, CMP_HALF, NSA_HEAD_DIM), lambda b, c, g: (c, 0, 0, 0))],
        out_specs=pl.BlockSpec((1, 1, 1, NSA_NCP, NSA_HEAD_DIM), lambda b, c, g: (b, c, g, 0, 0)),
        out_shape=jax.ShapeDtypeStruct((bsz, 2, NSA_KV_HEADS, NSA_NCP, NSA_HEAD_DIM), F32),
        compiler_params=pltpu.CompilerParams(dimension_semantics=('parallel', 'parallel', 'parallel')),
        name='nsa_compress',
    )(a, pe, w)


def _nsa_prompt_kernel(q_ref, gate_ref, cmp_ref, ks_ref, vs_ref, kw_ref, vw_ref, bdiag_ref, bprev_ref, bfar_ref,
                       bcmp_ref, covt_ref, expand_ref, o_ref, m_ref, l_ref, acc_ref):
    i = pl.program_id(2)
    rows = NSA_REP * NSA_QB
    qblk = q_ref[...] * (NSA_HEAD_DIM ** -0.5)
    q = jnp.concatenate([qblk[:, r * LANE:(r + 1) * LANE] for r in range(NSA_REP)], axis=0).astype(BF16)
    tq = lax.broadcasted_iota(jnp.int32, (rows, LANE), 0) % NSA_QB
    lane = lax.broadcasted_iota(jnp.int32, (rows, LANE), 1)
    t_abs = i * NSA_QB + tq

    k_c = cmp_ref[0, 0, 0].astype(BF16)
    v_c = cmp_ref[0, 1, 0].astype(BF16)
    mask_c = lane * CMP_STRIDE + (CMP_LEN - 1) <= t_abs
    s_c = jnp.where(mask_c, _dot_nt(q, k_c) + bcmp_ref[0, 0], NEG)
    e_c = jnp.where(mask_c, jnp.exp(s_c - jnp.max(s_c, -1, keepdims=True)), 0.0)
    d_c = jnp.sum(e_c, -1, keepdims=True)
    p_c = (e_c / jnp.where(d_c > 0, d_c, 1.0)).astype(BF16)
    o_c = jnp.dot(p_c, v_c, preferred_element_type=F32)

    imp4 = _dot_nt(covt_ref[...], p_c)
    imp = sum(imp4[:, r * NSA_QB:(r + 1) * NSA_QB] for r in range(NSA_REP))
    blk = lax.broadcasted_iota(jnp.int32, (LANE, NSA_QB), 0)
    t_row = i * NSA_QB + lax.broadcasted_iota(jnp.int32, (LANE, NSA_QB), 1)
    cur = t_row // SEL_LEN
    forced = (blk == 0) | (blk == cur) | (blk == cur - 1)
    score = jnp.where(forced, FORCE_SCORE, jnp.where(blk * SEL_LEN <= t_row, imp, -FORCE_SCORE))
    score = jnp.where(blk < NSA_NSEL, score, -3.0 * FORCE_SCORE)
    rank = jnp.zeros((LANE, NSA_QB), jnp.int32)
    for j in range(NSA_NSEL):
        row = score[j:j + 1, :]
        rank = rank + jnp.where((row > score) | ((row == score) & (blk > j)), 1, 0)
    sel_t = jnp.where((rank < TOP_N) & (blk < NSA_NSEL), 1.0, 0.0)
    sel = sel_t.T.astype(BF16)

    def init():
        m_ref[...] = jnp.full((rows, 1), NEG, F32)
        l_ref[...] = jnp.zeros((rows, 1), F32)
        acc_ref[...] = jnp.zeros((rows, NSA_HEAD_DIM), F32)

    def step(k_ref, v_ref, c, bias, mask):
        start = pl.multiple_of(c * NSA_QB, NSA_QB)
        k = k_ref[pl.ds(start, NSA_QB), :].astype(BF16)
        v = v_ref[pl.ds(start, NSA_QB), :].astype(BF16)
        s = _dot_nt(q, k) + bias
        if mask is not None:
            s = jnp.where(mask, s, NEG)
        m_old = m_ref[...]
        m_new = jnp.maximum(m_old, jnp.max(s, -1, keepdims=True))
        p = jnp.exp(s - m_new)
        if mask is not None:
            p = jnp.where(mask, p, 0.0)
        alpha = jnp.exp(m_old - m_new)
        l_ref[...] = alpha * l_ref[...] + jnp.sum(p, -1, keepdims=True)
        acc_ref[...] = alpha * acc_ref[...] + jnp.dot(p.astype(BF16), v, preferred_element_type=F32)
        m_ref[...] = m_new

    def finish():
        l = l_ref[...]
        return acc_ref[...] / jnp.where(l > 0, l, 1.0)

    def sel_mask(c):
        m1 = jnp.dot(sel, expand_ref[c], preferred_element_type=F32)
        return jnp.concatenate([m1] * NSA_REP, axis=0) > 0.5

    init()
    bfar = bfar_ref[0]

    def far_body(c, carry):
        step(ks_ref, vs_ref, c, bfar, sel_mask(c))
        return carry

    lax.fori_loop(0, jnp.maximum(i - 1, 0), far_body, 0)

    @pl.when(i >= 1)
    def _():
        step(ks_ref, vs_ref, i - 1, bprev_ref[0], sel_mask(i - 1))

    causal = lane <= tq
    step(ks_ref, vs_ref, i, bdiag_ref[0], sel_mask(i) & causal)
    o_s = finish()

    init()
    step(kw_ref, vw_ref, i, bdiag_ref[0], causal)

    @pl.when(i >= 1)
    def _():
        step(kw_ref, vw_ref, i - 1, bprev_ref[0], None)

    for d in range(2, WINDOW // NSA_QB):
        @pl.when(i >= d)
        def _(d=d):
            step(kw_ref, vw_ref, i - d, bfar, None)

    @pl.when(i >= WINDOW // NSA_QB)
    def _():
        step(kw_ref, vw_ref, i - WINDOW // NSA_QB, bfar, lane > tq)

    o_w = finish()

    gates = jax.nn.sigmoid(gate_ref[...])
    for r in range(NSA_REP):
        rs = slice(r * NSA_QB, (r + 1) * NSA_QB)
        o_ref[:, r * LANE:(r + 1) * LANE] = (gates[:, 3 * r:3 * r + 1] * o_c[rs] + gates[:, 3 * r + 1:3 * r + 2] * o_s[rs]
                                             + gates[:, 3 * r + 2:3 * r + 3] * o_w[rs]).astype(o_ref.dtype)


def _nsa_prompt(y, k_v_cmp, tables, bsz):
    b_diag, b_prev, b_far, b_cmp = tables
    rows = NSA_REP * NSA_QB
    c_lo = np.arange(NSA_NCP) * CMP_STRIDE
    s_lo = np.arange(LANE) * SEL_LEN
    cov_t = ((c_lo[None] < s_lo[:, None] + SEL_LEN) & (c_lo[None] + CMP_LEN > s_lo[:, None])
             & (np.arange(NSA_NCP)[None] < NSA_NCP - 1) & (np.arange(LANE)[:, None] < NSA_NSEL))
    key_blk = (np.arange(NSA_L) // SEL_LEN).reshape(NSA_NQB, 1, NSA_QB)
    expand = (np.arange(LANE)[None, :, None] == key_blk)
    kv_blk = KV_OFF // LANE
    grp = NSA_KV_HEADS

    def col(slot):
        return pl.BlockSpec((NSA_L, NSA_HEAD_DIM), lambda b, g, i: (b, kv_blk + slot * grp + g))

    tab = pl.BlockSpec((1, rows, LANE), lambda b, g, i: (g, 0, 0))
    return pl.pallas_call(
        _nsa_prompt_kernel,
        grid=(bsz, NSA_KV_HEADS, NSA_NQB),
        in_specs=[pl.BlockSpec((NSA_QB, NSA_REP * LANE), lambda b, g, i: (b * NSA_NQB + i, QN_OFF // (NSA_REP * LANE) + g)),
                  pl.BlockSpec((NSA_QB, LANE), lambda b, g, i: (b * NSA_NQB + i, GATE_OFF // LANE + g)),
                  pl.BlockSpec((1, 2, 1, NSA_NCP, NSA_HEAD_DIM), lambda b, g, i: (b, 0, g, 0, 0)),
                  col(2), col(3), col(4), col(5), tab, tab, tab,
                  pl.BlockSpec((1, 1, rows, NSA_NCP), lambda b, g, i: (g, i, 0, 0)),
                  pl.BlockSpec((LANE, NSA_NCP), lambda b, g, i: (0, 0)),
                  pl.BlockSpec((NSA_NQB, LANE, NSA_QB), lambda b, g, i: (0, 0, 0))],
        out_specs=pl.BlockSpec((NSA_QB, NSA_REP * LANE), lambda b, g, i: (b * NSA_NQB + i, g)),
        out_shape=jax.ShapeDtypeStruct((bsz * NSA_L, D_NSA), BF16),
        scratch_shapes=[pltpu.VMEM((rows, 1), F32), pltpu.VMEM((rows, 1), F32), pltpu.VMEM((rows, NSA_HEAD_DIM), F32)],
        compiler_params=pltpu.CompilerParams(dimension_semantics=('parallel', 'parallel', 'arbitrary'),
                                             vmem_limit_bytes=VMEM_LIMIT_BYTES),
        name='nsa_prompt',
    )(y, y, k_v_cmp, y, y, y, y, b_diag, b_prev, b_far, b_cmp,
      jnp.asarray(cov_t, BF16), jnp.asarray(expand, BF16))


NS_SELW = 3 * LANE
NS_CHUNK = 2048


def _nsa_sample_tables(rel_bias, past_len, lq, win_len):
    tq = (past_len + np.arange(lq))[:, None]

    def table(idx):
        t = jnp.moveaxis(rel_bias[jnp.asarray(idx)], -1, 0).reshape(NSA_KV_HEADS, NSA_REP, lq, idx.shape[-1])
        return t.reshape(NSA_KV_HEADS, NSA_REP * lq, idx.shape[-1])

    ncp = past_len // CMP_STRIDE
    b_cmp = table(_t5_bucket_np(tq - (np.arange(ncp)[None] * CMP_STRIDE + CMP_LEN - 1)))
    b_last = table(_t5_bucket_np(tq - (past_len - LANE + np.arange(LANE)[None])))
    b_new = table(_t5_bucket_np(tq - (past_len + np.arange(LANE)[None])))
    b_far = table(np.full((lq, LANE), N_BUCKETS - 1, np.int32))
    wpos = np.concatenate([past_len - win_len + np.arange(win_len), past_len + np.arange(LANE)])
    b_win = table(_t5_bucket_np(tq - wpos[None]))
    return b_cmp, b_last, b_new, b_far, b_win


def _nsa_sample_kernel(pt_ref, q_ref, kvn_ref, gate0_ref, gate1_ref, win_ref, cache_ref, w_ref, pe_ref,
                       bcmp_ref, blast_ref, bnew_ref, bfar_ref, bwin_ref, cov_ref, expand_ref, o_ref,
                       buf, cmp_scr, s_scr, sem, *, n_pages, lq, past_len, win_len):
    b = pl.program_id(0)
    rows = NSA_REP * lq
    ncp = past_len // CMP_STRIDE
    n_cmp = (past_len + lq - CMP_LEN) // CMP_STRIDE + 1
    n_sel = -(-(past_len + lq) // SEL_LEN)
    n_chunks = past_len // NS_CHUNK
    ncol = 2 * NSA_KV_HEADS

    def page_copy(j, pair, col):
        return pltpu.make_async_copy(cache_ref.at[pt_ref[b, j], :, pl.ds((pair * ncol + col) * NSA_HEAD_DIM, NSA_HEAD_DIM)],
                                     buf.at[col, pl.ds(j * PAGE_SIZE, PAGE_SIZE), :], sem)

    def gather_start(pair):
        def body(j, carry):
            for col in range(ncol):
                page_copy(j, pair, col).start()
            return carry
        lax.fori_loop(0, n_pages, body, 0)

    def gather_wait(pair):
        def body(j, carry):
            for col in range(ncol):
                page_copy(j, pair, col).wait()
            return carry
        lax.fori_loop(0, n_pages, body, 0)

    gather_start(0)
    gather_wait(0)
    for c in range(2):
        for g in range(NSA_KV_HEADS):
            lo = jnp.zeros((ncp, NSA_HEAD_DIM), F32)
            hi = jnp.zeros((ncp, NSA_HEAD_DIM), F32)
            for l in range(CMP_STRIDE):
                x = buf[c * NSA_KV_HEADS + g, pl.ds(l, ncp, stride=CMP_STRIDE), :]
                lo = lo + jnp.dot((x + pe_ref[c, l:l + 1, :]).astype(BF16), w_ref[c, l], preferred_element_type=F32)
                hi = hi + jnp.dot((x + pe_ref[c, CMP_STRIDE + l:CMP_STRIDE + l + 1, :]).astype(BF16),
                                  w_ref[c, CMP_STRIDE + l], preferred_element_type=F32)
            cmp_scr[c, g] = lo + pltpu.roll(hi, ncp - 1, 0)
    gather_start(1)

    qall = q_ref[...] * (NSA_HEAD_DIM ** -0.5)
    tq = lax.broadcasted_iota(jnp.int32, (rows, 1), 0) % lq
    t_abs = past_len + tq
    o_c, sel, q_g = [], [], []
    for g in range(NSA_KV_HEADS):
        q = jnp.concatenate([qall[:, (g * NSA_REP + r) * LANE:(g * NSA_REP + r + 1) * LANE] for r in range(NSA_REP)],
                            axis=0).astype(BF16)
        q_g.append(q)
        n_idx = lax.broadcasted_iota(jnp.int32, (rows, ncp), 1)
        mask_c = (n_idx < n_cmp) & (n_idx * CMP_STRIDE + (CMP_LEN - 1) <= t_abs)
        s_c = jnp.where(mask_c, _dot_nt(q, cmp_scr[0, g].astype(BF16)) + bcmp_ref[g], NEG)
        e_c = jnp.where(mask_c, jnp.exp(s_c - jnp.max(s_c, -1, keepdims=True)), 0.0)
        d_c = jnp.sum(e_c, -1, keepdims=True)
        p_c = (e_c / jnp.where(d_c > 0, d_c, 1.0)).astype(BF16)
        o_c.append(jnp.dot(p_c, cmp_scr[1, g].astype(BF16), preferred_element_type=F32))
        imp4 = jnp.dot(p_c, cov_ref[...], preferred_element_type=F32)
        imp = sum(imp4[r * lq:(r + 1) * lq] for r in range(NSA_REP))
        blk = lax.broadcasted_iota(jnp.int32, (lq, NS_SELW), 1)
        t_q = past_len + lax.broadcasted_iota(jnp.int32, (lq, NS_SELW), 0)
        cur = t_q // SEL_LEN
        forced = (blk == 0) | (blk == cur) | (blk == cur - 1)
        score = jnp.where(forced, FORCE_SCORE, jnp.where(blk * SEL_LEN <= t_q, imp, -FORCE_SCORE))
        score = jnp.where(blk < n_sel, score, -3.0 * FORCE_SCORE)
        rank = jnp.zeros((lq, NS_SELW), jnp.int32)
        for j in range(n_sel):
            col = score[:, j:j + 1]
            rank = rank + jnp.where((col > score) | ((col == score) & (blk > j)), 1, 0)
        sel_q = jnp.where((rank < min(TOP_N, n_sel)) & (blk < n_sel), 1.0, 0.0)
        sel.append(jnp.concatenate([sel_q] * NSA_REP, axis=0))

    gather_wait(1)
    kvn = _pad_rows(kvn_ref[...], LANE)
    lane = lax.broadcasted_iota(jnp.int32, (rows, LANE), 1)
    blocks_per_chunk = NS_CHUNK // SEL_LEN
    gates = [jax.nn.sigmoid(gate0_ref[...]), jax.nn.sigmoid(gate1_ref[...])]
    for g in range(NSA_KV_HEADS):
        q = q_g[g]
        bfar = bfar_ref[g][:, 0:1]

        def new_rows(slot):
            return kvn[:, (slot * NSA_KV_HEADS + g) * NSA_HEAD_DIM:(slot * NSA_KV_HEADS + g + 1) * NSA_HEAD_DIM].astype(BF16)

        sel_b = sel[g].astype(BF16)
        for c in range(n_chunks):
            k = buf[g, c * NS_CHUNK:(c + 1) * NS_CHUNK, :].astype(BF16)
            s = _dot_nt(q, k) + bfar
            if c == n_chunks - 1:
                fix = blast_ref[g] - bfar_ref[g]
                s = jnp.concatenate([s[:, :NS_CHUNK - LANE], s[:, NS_CHUNK - LANE:] + fix], axis=1)
            m_c = jnp.dot(sel_b[:, c * blocks_per_chunk:(c + 1) * blocks_per_chunk], expand_ref[...],
                          preferred_element_type=F32)
            s_scr[:, c * NS_CHUNK:(c + 1) * NS_CHUNK] = jnp.where(m_c > 0.5, s, NEG)
        s_new = _dot_nt(q, new_rows(2)) + bnew_ref[g]
        mask_new = (lane <= tq) & (lane < lq) & (sel[g][:, n_sel - 1:n_sel] > 0.5)
        s_scr[:, past_len:past_len + LANE] = jnp.where(mask_new, s_new, NEG)
        m = jnp.max(s_scr[...], -1, keepdims=True)
        acc = jnp.zeros((rows, NSA_HEAD_DIM), F32)
        den = jnp.zeros((rows, 1), F32)
        for c in range(n_chunks):
            sc = s_scr[:, c * NS_CHUNK:(c + 1) * NS_CHUNK]
            e = jnp.where(sc > 0.5 * NEG, jnp.exp(sc - m), 0.0)
            den = den + jnp.sum(e, -1, keepdims=True)
            v = buf[NSA_KV_HEADS + g, c * NS_CHUNK:(c + 1) * NS_CHUNK, :]
            acc = acc + jnp.dot(e.astype(BF16), v.astype(BF16), preferred_element_type=F32)
        sc = s_scr[:, past_len:past_len + LANE]
        e = jnp.where(sc > 0.5 * NEG, jnp.exp(sc - m), 0.0)
        den = den + jnp.sum(e, -1, keepdims=True)
        acc = acc + jnp.dot(e.astype(BF16), new_rows(3), preferred_element_type=F32)
        o_s = acc / jnp.where(den > 0, den, 1.0)

        kw = jnp.concatenate([win_ref[0, :, g * NSA_HEAD_DIM:(g + 1) * NSA_HEAD_DIM].astype(BF16), new_rows(4)], axis=0)
        vw = jnp.concatenate([win_ref[0, :, (NSA_KV_HEADS + g) * NSA_HEAD_DIM:(NSA_KV_HEADS + g + 1) * NSA_HEAD_DIM].astype(BF16),
                              new_rows(5)], axis=0)
        wl = lax.broadcasted_iota(jnp.int32, (rows, win_len + LANE), 1)
        dist = jnp.where(wl < win_len, win_len + tq - wl, tq - (wl - win_len))
        mask_w = (dist >= 0) & (dist < WINDOW) & (wl < win_len + lq)
        s_w = jnp.where(mask_w, _dot_nt(q, kw) + bwin_ref[g], NEG)
        e_w = jnp.where(mask_w, jnp.exp(s_w - jnp.max(s_w, -1, keepdims=True)), 0.0)
        d_w = jnp.sum(e_w, -1, keepdims=True)
        o_w = jnp.dot((e_w / jnp.where(d_w > 0, d_w, 1.0)).astype(BF16), vw, preferred_element_type=F32)

        for r in range(NSA_REP):
            rs = slice(r * lq, (r + 1) * lq)
            gt = gates[g]
            h = g * NSA_REP + r
            o_ref[:, h * LANE:(h + 1) * LANE] = (gt[:, 3 * r:3 * r + 1] * o_c[g][rs] + gt[:, 3 * r + 1:3 * r + 2] * o_s[rs]
                                                 + gt[:, 3 * r + 2:3 * r + 3] * o_w[rs]).astype(o_ref.dtype)


def _nsa_sample(y, cache_kv, page_table, cache_win, cmp_pe, cmp_w, rel_bias, bsz):
    lq = y.shape[0] // bsz
    n_pages = page_table.shape[1]
    past_len = n_pages * PAGE_SIZE
    win_len = cache_win.shape[1]
    rows = NSA_REP * lq
    ncp = past_len // CMP_STRIDE
    n_cmp = (past_len + lq - CMP_LEN) // CMP_STRIDE + 1
    n_sel = -(-(past_len + lq) // SEL_LEN)
    assert lq % 8 == 0 and lq <= CMP_STRIDE and n_cmp <= ncp - 1 + lq // CMP_STRIDE and past_len % NS_CHUNK == 0
    assert past_len % SEL_LEN == 0 and n_sel <= NS_SELW and lq % Q_BLOCK != 0 and win_len == WINDOW
    tables = _nsa_sample_tables(rel_bias, past_len, lq, win_len)
    c_lo = np.arange(ncp) * CMP_STRIDE
    s_lo = np.arange(NS_SELW) * SEL_LEN
    cover = ((c_lo[:, None] < s_lo[None] + SEL_LEN) & (c_lo[:, None] + CMP_LEN > s_lo[None])
             & (np.arange(ncp)[:, None] < n_cmp) & (np.arange(NS_SELW)[None] < n_sel))
    expand = np.arange(NS_CHUNK // SEL_LEN)[:, None] == (np.arange(NS_CHUNK) // SEL_LEN)[None]
    cache2 = cache_kv.reshape(cache_kv.shape[0], PAGE_SIZE, 4 * KVW)
    win2 = cache_win.reshape(bsz, win_len, 2 * KVW)

    def full(shape):
        return pl.BlockSpec(shape, lambda b, pt: (0,) * len(shape))

    grid_spec = pltpu.PrefetchScalarGridSpec(
        num_scalar_prefetch=1, grid=(bsz,),
        in_specs=[pl.BlockSpec((lq, D_NSA), lambda b, pt: (b, QN_OFF // D_NSA)),
                  pl.BlockSpec((lq, 6 * KVW), lambda b, pt: (b, KV_OFF // (6 * KVW))),
                  pl.BlockSpec((lq, LANE), lambda b, pt: (b, GATE_OFF // LANE)),
                  pl.BlockSpec((lq, LANE), lambda b, pt: (b, GATE_OFF // LANE + 1)),
                  pl.BlockSpec((1, win_len, 2 * KVW), lambda b, pt: (b, 0, 0)),
                  pl.BlockSpec(memory_space=pl.ANY),
                  full((2, CMP_LEN, NSA_HEAD_DIM, NSA_HEAD_DIM)), full((2, CMP_LEN, NSA_HEAD_DIM)),
                  full((NSA_KV_HEADS, rows, ncp)), full((NSA_KV_HEADS, rows, LANE)), full((NSA_KV_HEADS, rows, LANE)),
                  full((NSA_KV_HEADS, rows, LANE)), full((NSA_KV_HEADS, rows, win_len + LANE)),
                  full((ncp, NS_SELW)), full((NS_CHUNK // SEL_LEN, NS_CHUNK))],
        out_specs=pl.BlockSpec((lq, D_NSA), lambda b, pt: (b, 0)),
        scratch_shapes=[pltpu.VMEM((2 * NSA_KV_HEADS, past_len, NSA_HEAD_DIM), F32),
                        pltpu.VMEM((2, NSA_KV_HEADS, ncp, NSA_HEAD_DIM), F32),
                        pltpu.VMEM((rows, past_len + LANE), F32), pltpu.SemaphoreType.DMA(())])
    assert QN_OFF % D_NSA == 0 and KV_OFF % (6 * KVW) == 0
    return pl.pallas_call(
        functools.partial(_nsa_sample_kernel, n_pages=n_pages, lq=lq, past_len=past_len, win_len=win_len),
        grid_spec=grid_spec,
        out_shape=jax.ShapeDtypeStruct((bsz * lq, D_NSA), BF16),
        compiler_params=pltpu.CompilerParams(dimension_semantics=('arbitrary',), vmem_limit_bytes=VMEM_LIMIT_BYTES),
        name='nsa_sample',
    )(page_table, y, y, y, y, win2, cache2, cmp_w.astype(BF16), cmp_pe, *tables,
      jnp.asarray(cover, BF16), jnp.asarray(expand, BF16))


def _mm_kernel(x_ref, w_ref, o_ref, *scratch, nk, act):
    def finish(acc):
        if act == 'sqrelu':
            acc = jnp.square(jnp.maximum(acc, 0.0))
        o_ref[...] = acc.astype(o_ref.dtype)

    if nk == 1:
        finish(jnp.dot(x_ref[...], w_ref[...], preferred_element_type=F32))
        return
    acc_ref, = scratch
    k = pl.program_id(2)
    part = jnp.dot(x_ref[...], w_ref[...], preferred_element_type=F32)

    @pl.when(k == 0)
    def _():
        acc_ref[...] = part

    @pl.when(k > 0)
    def _():
        acc_ref[...] += part

    @pl.when(k == nk - 1)
    def _():
        finish(acc_ref[...])


def _pick(n, pref):
    for t in pref:
        if n % t == 0:
            return t
    return n


def _matmul(x, w, out_dtype=F32, act=None, name='matmul'):
    m, k = x.shape
    n = w.shape[1]
    tm = _pick(m, (1024, 512, 256, 128, 64))
    tn = _pick(n, (512, 256, 128))
    tk = _pick(k, (4096, 2048, 1024, 512))
    nk = k // tk
    scratch = [pltpu.VMEM((tm, tn), F32)] if nk > 1 else []
    return pl.pallas_call(
        functools.partial(_mm_kernel, nk=nk, act=act),
        grid=(m // tm, n // tn, nk),
        in_specs=[pl.BlockSpec((tm, tk), lambda i, j, kk: (i, kk)),
                  pl.BlockSpec((tk, tn), lambda i, j, kk: (kk, j))],
        out_specs=pl.BlockSpec((tm, tn), lambda i, j, kk: (i, j)),
        out_shape=jax.ShapeDtypeStruct((m, n), out_dtype),
        scratch_shapes=scratch,
        compiler_params=pltpu.CompilerParams(
            dimension_semantics=('parallel', 'parallel', 'arbitrary'),
            vmem_limit_bytes=VMEM_LIMIT_BYTES),
        name=name,
    )(x, w)


def _proj(x, w_bf16, **kw):
    lead = x.shape[:-1]
    y = _matmul(x.reshape(-1, x.shape[-1]).astype(BF16), w_bf16, **kw)
    return y.reshape(lead + (w_bf16.shape[1],))


def _ln_kernel(x_ref, h_ref, g_ref, b_ref, o_ref, ob_ref):
    v = DN_ALPHA * x_ref[...] + h_ref[...]
    d = v - jnp.mean(v, -1, keepdims=True)
    y = d * lax.rsqrt(jnp.mean(d * d, -1, keepdims=True) + LN_EPS) * g_ref[...] + b_ref[...]
    o_ref[...] = y
    ob_ref[...] = y.astype(BF16)


def _add_layernorm(x, h, g, b):
    m, d = x.shape
    tm = _pick(m, (256, 64))
    row = pl.BlockSpec((tm, d), lambda i: (i, 0))
    vec = pl.BlockSpec((1, d), lambda i: (0, 0))
    return pl.pallas_call(
        _ln_kernel, grid=(m // tm,), in_specs=[row, row, vec, vec], out_specs=[row, row],
        out_shape=[jax.ShapeDtypeStruct((m, d), F32), jax.ShapeDtypeStruct((m, d), BF16)],
        compiler_params=pltpu.CompilerParams(dimension_semantics=('parallel',), vmem_limit_bytes=VMEM_LIMIT_BYTES),
        name='add_layernorm',
    )(x, h, g.reshape(1, d), b.reshape(1, d))


def _xattn_kernel(q_ref, k_ref, v_ref, o_ref):
    q = (q_ref[...] * (X_HEAD_DIM ** -0.5)).astype(BF16)
    s = _dot_nt(q, k_ref[0].astype(BF16))
    e = jnp.exp(s - jnp.max(s, -1, keepdims=True))
    p = e / jnp.sum(e, -1, keepdims=True)
    o_ref[...] = jnp.dot(p.astype(BF16), v_ref[0].astype(BF16), preferred_element_type=F32).astype(o_ref.dtype)


def _cross_attention(q, mem_kv, bsz):
    m = q.shape[0]
    l = m // bsz
    tq = _pick(l, (512, 8))
    nq = l // tq
    return pl.pallas_call(
        _xattn_kernel, grid=(bsz, X_HEADS, nq),
        in_specs=[pl.BlockSpec((tq, X_HEAD_DIM), lambda b, h, i: (b * nq + i, h)),
                  pl.BlockSpec((1, N_MEM, X_HEAD_DIM), lambda b, h, i: (b, 0, h)),
                  pl.BlockSpec((1, N_MEM, X_HEAD_DIM), lambda b, h, i: (b, 0, X_HEADS + h))],
        out_specs=pl.BlockSpec((tq, X_HEAD_DIM), lambda b, h, i: (b * nq + i, h)),
        out_shape=jax.ShapeDtypeStruct((m, D_MODEL), BF16),
        compiler_params=pltpu.CompilerParams(dimension_semantics=('parallel', 'parallel', 'parallel'),
                                             vmem_limit_bytes=VMEM_LIMIT_BYTES),
        name='cross_attention',
    )(q, mem_kv, mem_kv)


def _softplus(x):
    return jnp.maximum(x, 0.0) + jnp.log1p(jnp.exp(-jnp.abs(x)))


def _silu(x):
    return x * jax.nn.sigmoid(x)


def _pad_rows(x, rows):
    if x.shape[0] == rows:
        return x
    return jnp.concatenate([x, jnp.zeros((rows - x.shape[0],) + x.shape[1:], x.dtype)], axis=0)


def _cumsum_rows(x, seg):
    r = lax.broadcasted_iota(jnp.int32, x.shape, 0) % seg
    k = 1
    while k < seg:
        x = x + jnp.where(r >= k, pltpu.roll(x, k, 0), 0.0)
        k *= 2
    return x


def _segment_last(x, seg):
    n = x.shape[0]
    r = lax.broadcasted_iota(jnp.int32, x.shape, 0) % seg
    k = 1
    while k < seg:
        x = jnp.where(r < seg - k, pltpu.roll(x, n - k, 0), x)
        k *= 2
    return x


GLA_BLOCK = 128


def _gla_kernel(q_ref, k_ref, v_ref, g_ref, a_ref, w2_ref, gb_ref, nw_ref, s0_ref, o_ref, s_out_ref,
                s_scr, q_scr, k_scr, v_scr, bc_scr, o_scr, *, rows_in, ch):
    c = pl.program_id(1)
    nsub = -(-rows_in // ch)
    hk = GLA_DK

    @pl.when(c == 0)
    def _():
        s_scr[...] = s0_ref[0]

    valid = lax.broadcasted_iota(jnp.int32, (GLA_BLOCK, 1), 0) < rows_in
    a = _pad_rows(a_ref[...], GLA_BLOCK).astype(BF16)
    logf = -_softplus(-(jnp.dot(a, w2_ref[...], preferred_element_type=F32) + gb_ref[...])) / GLA_TAU
    logf = jnp.where(valid, logf, 0.0)
    bc = _cumsum_rows(logf, ch)
    b_last = _segment_last(bc, ch)
    q = _pad_rows(q_ref[...], GLA_BLOCK) * (GLA_DK ** -0.5)
    k = _pad_rows(k_ref[...], GLA_BLOCK)
    v = _pad_rows(v_ref[...], GLA_BLOCK)
    q_scr[...] = q
    k_scr[...] = k
    v_scr[...] = v
    bc_scr[...] = bc

    tt = lax.broadcasted_iota(jnp.int32, (ch, 1), 0)

    def intra(j, carry):
        r0 = pl.multiple_of(j * ch, ch)
        qj = q_scr[pl.ds(r0, ch), :]
        bj = bc_scr[pl.ds(r0, ch), :]
        acc = [jnp.zeros((ch, GLA_DV), F32) for _ in range(GLA_HEADS)]
        for s in range(ch):
            ks = k_scr[pl.ds(r0 + s, 1), :]
            bs = bc_scr[pl.ds(r0 + s, 1), :]
            vs = v_scr[pl.ds(r0 + s, 1), :]
            w = jnp.where(tt >= s, qj * ks * jnp.exp(jnp.minimum(bj - bs, 0.0)), 0.0)
            for h in range(GLA_HEADS):
                att = jnp.sum(w[:, h * hk:(h + 1) * hk], -1, keepdims=True)
                acc[h] = acc[h] + att * vs[:, h * GLA_DV:(h + 1) * GLA_DV]
        o_scr[pl.ds(r0, ch), :] = jnp.concatenate(acc, axis=1)
        return carry

    lax.fori_loop(0, nsub, intra, 0)

    qe = q * jnp.exp(bc)
    kst = k * jnp.exp(b_last - bc)
    dec = jnp.exp(b_last)
    for h in range(GLA_HEADS):
        kst_t = kst[:, h * hk:(h + 1) * hk].T
        dec_t = dec[:, h * hk:(h + 1) * hk].T
        s_h = s_scr[h]
        for j in range(nsub):
            rs = slice(j * ch, (j + 1) * ch)
            o_scr[rs, h * GLA_DV:(h + 1) * GLA_DV] += jnp.dot(qe[rs, h * hk:(h + 1) * hk].astype(BF16), s_h.astype(BF16),
                                                              preferred_element_type=F32)
            s_h = s_h * dec_t[:, j * ch:j * ch + 1] + jnp.dot(kst_t[:, rs].astype(BF16),
                                                              v[rs, h * GLA_DV:(h + 1) * GLA_DV].astype(BF16),
                                                              preferred_element_type=F32)
        s_scr[h] = s_h

    g = _pad_rows(g_ref[...], GLA_BLOCK)
    outs = []
    for h in range(GLA_HEADS):
        o_h = o_scr[:, h * GLA_DV:(h + 1) * GLA_DV]
        o_h = o_h * lax.rsqrt(jnp.mean(o_h * o_h, -1, keepdims=True) + NORM_EPS) * nw_ref[...]
        outs.append(o_h)
    o = jnp.concatenate(outs, axis=1) * _silu(g)
    o_ref[...] = o[:rows_in].astype(o_ref.dtype)

    @pl.when(c == pl.num_programs(1) - 1)
    def _():
        s_out_ref[0] = s_scr[...]


def _gla(y, s0, gate_w2, gate_b, norm_w, bsz):
    m = y.shape[0]
    l = m // bsz
    rows_in = min(GLA_BLOCK, l)
    ch = GLA_CHUNK
    nblk = l // rows_in
    hdk = GLA_HEADS * GLA_DK
    w2 = jnp.zeros((LANE, hdk), BF16).at[ALR_OFF - SMALL_OFF:ALR_OFF - SMALL_OFF + GLA_RANK].set(gate_w2.astype(BF16))

    def rows(width, off):
        assert off % width == 0
        return pl.BlockSpec((rows_in, width), lambda b, c: (b * nblk + c, off // width))

    def const(shape):
        return pl.BlockSpec(shape, lambda b, c: (0,) * len(shape))

    state = pl.BlockSpec((1, GLA_HEADS, GLA_DK, GLA_DV), lambda b, c: (b, 0, 0, 0))
    return pl.pallas_call(
        functools.partial(_gla_kernel, rows_in=rows_in, ch=ch),
        grid=(bsz, nblk),
        in_specs=[rows(hdk, QG_OFF), rows(hdk, KG_OFF), rows(D_GLA, VG_OFF), rows(D_GLA, GG_OFF), rows(LANE, SMALL_OFF),
                  const((LANE, hdk)), const((1, hdk)), const((1, GLA_DV)), state],
        out_specs=[pl.BlockSpec((rows_in, D_GLA), lambda b, c: (b * nblk + c, 0)), state],
        out_shape=[jax.ShapeDtypeStruct((m, D_GLA), BF16), jax.ShapeDtypeStruct((bsz, GLA_HEADS, GLA_DK, GLA_DV), F32)],
        scratch_shapes=[pltpu.VMEM((GLA_HEADS, GLA_DK, GLA_DV), F32), pltpu.VMEM((GLA_BLOCK, hdk), F32),
                        pltpu.VMEM((GLA_BLOCK, hdk), F32), pltpu.VMEM((GLA_BLOCK, D_GLA), F32),
                        pltpu.VMEM((GLA_BLOCK, hdk), F32), pltpu.VMEM((GLA_BLOCK, D_GLA), F32)],
        compiler_params=pltpu.CompilerParams(dimension_semantics=('parallel', 'arbitrary'),
                                             vmem_limit_bytes=VMEM_LIMIT_BYTES),
        name='gla',
    )(y, y, y, y, y, w2, gate_b.reshape(1, hdk), norm_w.reshape(1, GLA_DV), s0)


SSD_R = SSD_HEADS // SSD_GROUPS
SSD_GW = SSD_R * SSD_HEAD_DIM
SSD_CW = SSD_GW + 2 * SSD_STATE
SSD_TAIL = 8


def _ssd_kernel(xs_ref, bm_ref, cm_ref, z_ref, dt_ref, conv0_ref, cw_ref, cb_ref, hp_ref, dskip_ref, nw_ref, h0_ref,
                o_ref, h_out_ref, xbuf, h_scr, *, rows_in):
    g = pl.program_id(1)
    c = pl.program_id(2)
    t = SSD_CHUNK

    @pl.when(c == 0)
    def _():
        xbuf[0:SSD_TAIL, :] = conv0_ref[0, 0]
        h_scr[...] = h0_ref[0, 0]

    xbuf[SSD_TAIL:SSD_TAIL + t, 0:SSD_GW] = _pad_rows(xs_ref[...], t)
    xbuf[SSD_TAIL:SSD_TAIL + t, SSD_GW:SSD_GW + SSD_STATE] = _pad_rows(bm_ref[...], t)
    xbuf[SSD_TAIL:SSD_TAIL + t, SSD_GW + SSD_STATE:SSD_CW] = _pad_rows(cm_ref[...], t)
    conv = cb_ref[0]
    for kk in range(SSD_CONV):
        conv = conv + cw_ref[0, kk:kk + 1, :] * xbuf[SSD_TAIL - (SSD_CONV - 1) + kk:SSD_TAIL - (SSD_CONV - 1) + kk + t, :]
    xbuf[0:SSD_TAIL, :] = xbuf[t:t + SSD_TAIL, :]
    xc = _silu(conv)
    xs = xc[:, 0:SSD_GW]
    bm = xc[:, SSD_GW:SSD_GW + SSD_STATE]
    cm = xc[:, SSD_GW + SSD_STATE:SSD_CW].astype(BF16)

    lane = lax.broadcasted_iota(jnp.int32, (t, LANE), 1)
    row = lax.broadcasted_iota(jnp.int32, (t, LANE), 0)
    dt = pltpu.roll(_pad_rows(dt_ref[...], t), (LANE - SSD_R * g) % LANE, 1)
    dt = jnp.where((lane < SSD_R) & (row < rows_in), _softplus(dt + hp_ref[0, 0:1, :]), 0.0)
    acs = _cumsum_rows(dt * -jnp.exp(hp_ref[0, 1:2, :]), t)
    acs_t = acs.T
    a_last = acs[t - 1:t, :]

    head_of_lane = lax.broadcasted_iota(jnp.int32, (1, SSD_GW), 1) // SSD_HEAD_DIM

    def spread(arr):
        return sum(jnp.where(head_of_lane == r, arr[:, r:r + 1], 0.0) for r in range(SSD_R))

    xdt = xs * spread(dt)
    cb = _dot_nt(cm, bm.astype(BF16))
    tri = lax.broadcasted_iota(jnp.int32, (t, t), 0) >= lax.broadcasted_iota(jnp.int32, (t, t), 1)
    y = jnp.zeros((t, SSD_GW), F32)
    for r in range(SSD_R):
        decay = jnp.where(tri, jnp.exp(jnp.minimum(acs[:, r:r + 1] - acs_t[r:r + 1, :], 0.0)), 0.0)
        x_r = jnp.where(head_of_lane == r, xdt, 0.0).astype(BF16)
        y = y + jnp.dot((cb * decay).astype(BF16), x_r, preferred_element_type=F32)
    h_prev = h_scr[...]
    y = y + jnp.dot(cm, h_prev.astype(BF16), preferred_element_type=F32) * jnp.exp(spread(acs))
    x_st = (xdt * jnp.exp(spread(a_last - acs))).astype(BF16)
    h_new = h_prev * jnp.exp(spread(a_last)) + jnp.dot(bm.T.astype(BF16), x_st, preferred_element_type=F32)
    h_scr[...] = h_new

    y = (y + xs * dskip_ref[0]) * _silu(_pad_rows(z_ref[...], t))
    y = y * lax.rsqrt(jnp.mean(y * y, -1, keepdims=True) + NORM_EPS) * nw_ref[0]
    o_ref[...] = y[:rows_in].astype(o_ref.dtype)

    @pl.when(c == pl.num_programs(2) - 1)
    def _():
        h_out_ref[0, 0] = h_new


def _ssd(y, conv0, h0, conv_w, conv_b, dt_bias, a_log, d_skip, norm_w, bsz):
    m = y.shape[0]
    l = m // bsz
    rows_in = min(SSD_CHUNK, l)
    nblk = l // rows_in
    ng = SSD_GROUPS

    def per_group(v):
        xs = v[..., :D_SSD].reshape(v.shape[:-1] + (ng, SSD_GW))
        bm = v[..., D_SSD:D_SSD + ng * SSD_STATE].reshape(v.shape[:-1] + (ng, SSD_STATE))
        cm = v[..., D_SSD + ng * SSD_STATE:].reshape(v.shape[:-1] + (ng, SSD_STATE))
        return jnp.moveaxis(jnp.concatenate([xs, bm, cm], axis=-1), -2, 0)

    conv0_g = jnp.moveaxis(per_group(jnp.pad(conv0, ((0, 0), (SSD_TAIL - (SSD_CONV - 1), 0), (0, 0)))), 0, 1)
    cw_g = per_group(conv_w)
    cb_g = per_group(conv_b[None])
    hp = jnp.zeros((ng, 8, LANE), F32)
    hp = hp.at[:, 0, :SSD_R].set(dt_bias.reshape(ng, SSD_R)).at[:, 1, :SSD_R].set(a_log.reshape(ng, SSD_R))
    dskip_g = jnp.repeat(d_skip, SSD_HEAD_DIM).reshape(ng, 1, SSD_GW)
    nw_g = norm_w.reshape(ng, 1, SSD_GW)
    h0_t = h0.reshape(bsz, ng, SSD_GW, SSD_STATE).transpose(0, 1, 3, 2)

    def rows(width, off):
        assert off % width == 0
        return pl.BlockSpec((rows_in, width), lambda b, g, c: (b * nblk + c, off // width + g))

    def grp(shape):
        return pl.BlockSpec((1,) + shape, lambda b, g, c: (g,) + (0,) * len(shape))

    state = pl.BlockSpec((1, 1, SSD_STATE, SSD_GW), lambda b, g, c: (b, g, 0, 0))
    out, h_t = pl.pallas_call(
        functools.partial(_ssd_kernel, rows_in=rows_in),
        grid=(bsz, ng, nblk),
        in_specs=[rows(SSD_GW, XBC_OFF), rows(SSD_STATE, XBC_OFF + D_SSD), rows(SSD_STATE, XBC_OFF + D_SSD + ng * SSD_STATE),
                  rows(SSD_GW, Z_OFF), pl.BlockSpec((rows_in, LANE), lambda b, g, c: (b * nblk + c, SMALL_OFF // LANE)),
                  pl.BlockSpec((1, 1, SSD_TAIL, SSD_CW), lambda b, g, c: (b, g, 0, 0)),
                  grp((SSD_CONV, SSD_CW)), grp((1, SSD_CW)), grp((8, LANE)), grp((1, SSD_GW)), grp((1, SSD_GW)), state],
        out_specs=[pl.BlockSpec((rows_in, SSD_GW), lambda b, g, c: (b * nblk + c, g)), state],
        out_shape=[jax.ShapeDtypeStruct((m, D_SSD), BF16), jax.ShapeDtypeStruct((bsz, ng, SSD_STATE, SSD_GW), F32)],
        scratch_shapes=[pltpu.VMEM((SSD_TAIL + SSD_CHUNK, SSD_CW), F32), pltpu.VMEM((SSD_STATE, SSD_GW), F32)],
        compiler_params=pltpu.CompilerParams(dimension_semantics=('parallel', 'parallel', 'arbitrary'),
                                             vmem_limit_bytes=VMEM_LIMIT_BYTES),
        name='ssd',
    )(y, y, y, y, y, conv0_g, cw_g, cb_g, hp, dskip_g, nw_g, h0_t)
    h_new = h_t.transpose(0, 1, 3, 2).reshape(bsz, SSD_HEADS, SSD_HEAD_DIM, SSD_STATE)
    return out, h_new


def _split(x, sizes):
    return jnp.split(x, np.cumsum(sizes)[:-1].tolist(), axis=-1)


def _layernorm(x, g, b):
    mu = jnp.mean(x, -1, keepdims=True)
    var = jnp.mean(jnp.square(x - mu), -1, keepdims=True)
    return (x - mu) * lax.rsqrt(var + LN_EPS) * g + b


def _rmsnorm(x, w):
    return x * lax.rsqrt(jnp.mean(x * x, -1, keepdims=True) + NORM_EPS) * w


def _masked_softmax(s, mask):
    s = jnp.where(mask, s.astype(F32), -jnp.inf)
    m = jnp.max(s, -1, keepdims=True)
    m = jnp.where(jnp.isfinite(m), m, 0.0)
    e = jnp.exp(s - m)
    d = jnp.sum(e, -1, keepdims=True)
    return e / jnp.where(d > 0, d, 1.0)


def _t5_bucket(dist):
    n = jnp.maximum(dist, 0)
    max_exact = N_BUCKETS // 2
    nf = jnp.maximum(n, 1).astype(F32)
    large = max_exact + (jnp.log(nf / max_exact) / math.log(MAX_DISTANCE / max_exact) * (N_BUCKETS - max_exact)).astype(jnp.int32)
    large = jnp.minimum(large, N_BUCKETS - 1)
    return jnp.where(n < max_exact, n, large)


def _ssd_scan(x, dt, a, bm, cm, h0):
    b, l, h, p = x.shape
    g, n = bm.shape[2], bm.shape[3]
    r = h // g
    lc = min(SSD_CHUNK, l)
    pad = (-l) % lc
    x, dt, bm, cm = [jnp.pad(t.astype(F32), ((0, 0), (0, pad)) + ((0, 0),) * (t.ndim - 2)) for t in (x, dt, bm, cm)]
    c = (l + pad) // lc
    xdt = (x * dt[..., None]).reshape(b, c, lc, g, r, p)
    acs = jnp.cumsum((dt * a).reshape(b, c, lc, g, r), axis=2)
    bm = bm.reshape(b, c, lc, g, n)
    cm = cm.reshape(b, c, lc, g, n)
    causal = np.tril(np.ones((lc, lc), bool))[None, None, :, :, None, None]
    decay = jnp.exp(jnp.where(causal, acs[:, :, :, None] - acs[:, :, None], -jnp.inf))
    cb = jnp.einsum('bctgn,bcsgn->bctsg', cm, bm)
    y_diag = jnp.einsum('bctsg,bctsgr,bcsgrp->bctgrp', cb, decay, xdt)
    decay_st = jnp.exp(acs[:, :, -1:] - acs)
    states = jnp.einsum('bcsgn,bcsgr,bcsgrp->bcgrpn', bm, decay_st, xdt)
    chunk_decay = jnp.exp(acs[:, :, -1])

    def step(hc, inp):
        st, dec = inp
        return hc * dec[..., None, None] + st, hc

    h_last, h_prev = lax.scan(step, h0.astype(F32).reshape(b, g, r, p, n), (jnp.moveaxis(states, 1, 0), jnp.moveaxis(chunk_decay, 1, 0)))
    h_prev = jnp.moveaxis(h_prev, 0, 1)
    y_off = jnp.einsum('bctgn,bcgrpn,bctgr->bctgrp', cm, h_prev, jnp.exp(acs))
    y = (y_diag + y_off).reshape(b, c * lc, h, p)[:, :l]
    return y, h_last.reshape(b, h, p, n)


def _ssd_mixer(z, xbc, dt, conv0, h0, conv_w, conv_b, dt_bias, a_log, d_skip, norm_w):
    b, l, _ = xbc.shape
    xpad = jnp.concatenate([conv0.astype(xbc.dtype), xbc], axis=1)
    conv = sum((xpad[:, k:k + l] * conv_w[k] for k in range(SSD_CONV)), conv_b)
    conv_new = xpad[:, -(SSD_CONV - 1):]
    xbc = jax.nn.silu(conv)
    xs, bm, cm = _split(xbc, (D_SSD, SSD_GROUPS * SSD_STATE, SSD_GROUPS * SSD_STATE))
    xs = xs.reshape(b, l, SSD_HEADS, SSD_HEAD_DIM)
    bm = bm.reshape(b, l, SSD_GROUPS, SSD_STATE)
    cm = cm.reshape(b, l, SSD_GROUPS, SSD_STATE)
    dt = jax.nn.softplus((dt + dt_bias).astype(F32))
    a = -jnp.exp(a_log.astype(F32))
    y, h_new = _ssd_scan(xs, dt, a, bm, cm, h0)
    y = y + xs.astype(F32) * d_skip.astype(F32)[:, None]
    y = (y.reshape(b, l, D_SSD) * jax.nn.silu(z.astype(F32))).reshape(b, l, SSD_GROUPS, D_SSD // SSD_GROUPS)
    y = _rmsnorm(y, norm_w.reshape(SSD_GROUPS, D_SSD // SSD_GROUPS)).reshape(b, l, D_SSD)
    return y, conv_new, h_new


def _gla_scan(q, k, v, logf, s0):
    b, l, h, dk = q.shape
    dv = v.shape[-1]
    lc = min(GLA_CHUNK, l)
    pad = (-l) % lc
    q, k, v, logf = [jnp.pad(t.astype(F32), ((0, 0), (0, pad), (0, 0), (0, 0))) for t in (q, k, v, logf)]
    c = (l + pad) // lc
    q, k, logf = [t.reshape(b, c, lc, h, dk) for t in (q, k, logf)]
    v = v.reshape(b, c, lc, h, dv)
    bc = jnp.cumsum(logf, axis=2)
    causal = np.tril(np.ones((lc, lc), bool))[None, None, :, :, None, None]
    rel = jnp.exp(jnp.where(causal, bc[:, :, :, None] - bc[:, :, None], -jnp.inf))
    att = jnp.einsum('bcthd,bcshd,bctshd->bchts', q, k, rel)
    o_intra = jnp.einsum('bchts,bcshv->bcthv', att, v)
    k_st = k * jnp.exp(bc[:, :, -1:] - bc)
    chunk_states = jnp.einsum('bcshd,bcshv->bchdv', k_st, v)
    chunk_decay = jnp.exp(bc[:, :, -1])

    def step(s, inp):
        st, dec = inp
        return s * dec[..., None] + st, s

    s_last, s_prev = lax.scan(step, s0.astype(F32), (jnp.moveaxis(chunk_states, 1, 0), jnp.moveaxis(chunk_decay, 1, 0)))
    s_prev = jnp.moveaxis(s_prev, 0, 1)
    o_inter = jnp.einsum('bcthd,bchdv->bcthv', q * jnp.exp(bc), s_prev)
    o = (o_intra + o_inter).reshape(b, c * lc, h, dv)[:, :l]
    return o, s_last


def _gla_mixer(q, k, v, g, a_lr, s0, gate_w2, gate_b, norm_w):
    b, l, _ = q.shape
    q = q.reshape(b, l, GLA_HEADS, GLA_DK) * (GLA_DK ** -0.5)
    k = k.reshape(b, l, GLA_HEADS, GLA_DK)
    v = v.reshape(b, l, GLA_HEADS, GLA_DV)
    logf = jax.nn.log_sigmoid((a_lr @ gate_w2 + gate_b).astype(F32)) / GLA_TAU
    o, s_new = _gla_scan(q, k, v, logf.reshape(b, l, GLA_HEADS, GLA_DK), s0)
    o = _rmsnorm(o, norm_w).reshape(b, l, D_GLA) * jax.nn.silu(g.astype(F32))
    return o, s_new


def _nsa_attend(q, gates, kv_full, win_ctx, q_pos0, w_pos0, cmp_pe, cmp_w, rel_bias):
    b, lq = q.shape[:2]
    t_len = kv_full.shape[1]
    G, R, HD = NSA_KV_HEADS, NSA_REP, NSA_HEAD_DIM
    kc, vc, ks, vs = [kv_full[:, :, i] for i in range(4)]
    n_cmp = (t_len - CMP_LEN) // CMP_STRIDE + 1
    cidx = (np.arange(n_cmp)[:, None] * CMP_STRIDE + np.arange(CMP_LEN)[None]).astype(np.int32)
    k_cmp = jnp.einsum('bnlgd,lde->bnge', kc[:, cidx] + cmp_pe[0][:, None], cmp_w[0]).astype(F32)
    v_cmp = jnp.einsum('bnlgd,lde->bnge', vc[:, cidx] + cmp_pe[1][:, None], cmp_w[1]).astype(F32)
    cmp_end = (np.arange(n_cmp) * CMP_STRIDE + CMP_LEN - 1).astype(np.int32)
    n_sel = -(-t_len // SEL_LEN)
    sel_pad = n_sel * SEL_LEN - t_len

    def sel_blocks(t):
        t = jnp.pad(t, ((0, 0), (0, sel_pad), (0, 0), (0, 0)))
        return t.reshape(b, n_sel, SEL_LEN, G, HD).transpose(0, 3, 1, 2, 4)

    ks_b, vs_b = sel_blocks(ks), sel_blocks(vs)
    c_lo = np.arange(n_cmp) * CMP_STRIDE
    s_lo = np.arange(n_sel) * SEL_LEN
    cover = jnp.asarray(((c_lo[:, None] < s_lo[None] + SEL_LEN) & (c_lo[:, None] + CMP_LEN > s_lo[None])).astype(np.float32))
    top_n = min(TOP_N, n_sel)
    kw_pad = jnp.pad(win_ctx[:, :, 0], ((0, 0), (WINDOW, 0), (0, 0), (0, 0)))
    vw_pad = jnp.pad(win_ctx[:, :, 1], ((0, 0), (WINDOW, 0), (0, 0), (0, 0)))
    qb = Q_BLOCK if lq % Q_BLOCK == 0 else lq
    nqb = lq // qb
    scale = HD ** -0.5
    bias_gr = rel_bias.reshape(N_BUCKETS, G, R)
    bi = jnp.arange(b)[:, None, None, None]
    gi = jnp.arange(G)[None, None, :, None]
    gi5 = jnp.arange(G)[None, None, :, None, None]
    blk = jnp.arange(n_sel)

    def block(args):
        i, qblk, gblk = args
        q0 = q_pos0 + i * qb
        tpos = q0 + jnp.arange(qb)
        qg = qblk.reshape(b, qb, G, R, HD).astype(F32) * scale
        bias_c = rel_bias[_t5_bucket(tpos[:, None] - cmp_end[None])].astype(F32).reshape(qb, n_cmp, G, R).transpose(0, 2, 3, 1)
        s_c = jnp.einsum('bqgrd,bngd->bqgrn', qg, k_cmp) + bias_c
        p_c = _masked_softmax(s_c, (cmp_end[None] <= tpos[:, None])[None, :, None, None, :])
        o_c = jnp.einsum('bqgrn,bngd->bqgrd', p_c, v_cmp)
        imp = jnp.einsum('bqgrn,nj->bqgj', p_c, cover)
        cur = (tpos // SEL_LEN)[:, None]
        forced = (blk == 0) | (blk == cur) | (blk == cur - 1)
        valid = blk * SEL_LEN <= tpos[:, None]
        score = jnp.where(forced[None, :, None, :], FORCE_SCORE, jnp.where(valid[None, :, None, :], imp, -FORCE_SCORE))
        _, sel = lax.top_k(score, top_n)
        k_sel = ks_b[bi, gi, sel].astype(F32)
        v_sel = vs_b[bi, gi, sel].astype(F32)
        spos = sel[..., None] * SEL_LEN + jnp.arange(SEL_LEN)
        dist_s = tpos[None, :, None, None, None] - spos
        bias_s = jnp.moveaxis(bias_gr[_t5_bucket(dist_s), gi5].astype(F32), -1, 3)
        s_s = jnp.einsum('bqgrd,bqgksd->bqgrks', qg, k_sel) + bias_s
        p_s = _masked_softmax(s_s.reshape(b, qb, G, R, top_n * SEL_LEN), (dist_s >= 0).reshape(b, qb, G, 1, top_n * SEL_LEN)).reshape(s_s.shape)
        o_s = jnp.einsum('bqgrks,bqgksd->bqgrd', p_s, v_sel)
        start = q0 - w_pos0
        k_w = lax.dynamic_slice_in_dim(kw_pad, start, WINDOW + qb, axis=1).astype(F32)
        v_w = lax.dynamic_slice_in_dim(vw_pad, start, WINDOW + qb, axis=1).astype(F32)
        wpos = q0 - WINDOW + jnp.arange(WINDOW + qb)
        dist_w = tpos[:, None] - wpos[None]
        mask_w = (dist_w >= 0) & (dist_w < WINDOW) & (wpos[None] >= w_pos0)
        bias_w = rel_bias[_t5_bucket(dist_w)].astype(F32).reshape(qb, WINDOW + qb, G, R).transpose(0, 2, 3, 1)
        s_w = jnp.einsum('bqgrd,bkgd->bqgrk', qg, k_w) + bias_w
        p_w = _masked_softmax(s_w, mask_w[None, :, None, None, :])
        o_w = jnp.einsum('bqgrk,bkgd->bqgrd', p_w, v_w)
        gg = gblk.reshape(b, qb, G, R, 3).astype(F32)
        return gg[..., 0:1] * o_c + gg[..., 1:2] * o_s + gg[..., 2:3] * o_w

    q_blocks = jnp.swapaxes(q.reshape(b, nqb, qb, NSA_HEADS, HD), 0, 1)
    g_blocks = jnp.swapaxes(gates.reshape(b, nqb, qb, NSA_HEADS, 3), 0, 1)
    outs = lax.map(block, (jnp.arange(nqb), q_blocks, g_blocks))
    return jnp.moveaxis(outs, 0, 1).reshape(b, lq, D_NSA)


def _mixer(x, lp, rel_bias, tables, conv0, ssd_h0, gla_s0, nsa_past, win_past, past_len):
    b, l, _ = x.shape
    y = _matmul(x.reshape(b * l, D_MODEL).astype(BF16), lp['w_in'], name='in_proj')
    y3 = y.reshape(b, l, D_IN_PAD)

    def part(off, width):
        return y3[..., off:off + width]

    z, xbc, dt = part(Z_OFF, D_SSD), part(XBC_OFF, SSD_CONV_DIM), part(DT_OFF, SSD_HEADS)
    q_g, k_g = part(QG_OFF, GLA_HEADS * GLA_DK), part(KG_OFF, GLA_HEADS * GLA_DK)
    v_g, g_g, a_g = part(VG_OFF, D_GLA), part(GG_OFF, D_GLA), part(ALR_OFF, GLA_RANK)
    y_ssd, conv_new, h_new = _ssd_mixer(z, xbc, dt, conv0, ssd_h0, lp['ssd_conv_w'], lp['ssd_conv_b'], lp['ssd_dt_bias'], lp['ssd_a_log'], lp['ssd_d'], lp['ssd_norm_w'])
    rows = part(KV_OFF, 4 * KVW).reshape(b, l, 4, NSA_KV_HEADS, NSA_HEAD_DIM)
    win_rows = part(KV_OFF + 4 * KVW, 2 * KVW).reshape(b, l, 2, NSA_KV_HEADS, NSA_HEAD_DIM)
    if nsa_past is None:
        win_new = win_rows[:, -min(WINDOW, l):]
        y_nsa = _nsa_prompt(y, _nsa_compress_prompt(y, lp['nsa_cmp_pe'], lp['nsa_cmp_w'], b), tables, b).reshape(b, l, D_NSA)
    else:
        kv_full = jnp.concatenate([nsa_past.astype(rows.dtype), rows], axis=1)
        win_ctx = jnp.concatenate([win_past.astype(rows.dtype), win_rows], axis=1)
        win_new = win_ctx[:, -win_past.shape[1]:]
        g_n = jnp.concatenate([part(GATE_OFF + g * LANE, 3 * NSA_REP) for g in range(NSA_KV_HEADS)], axis=-1)
        gates = jax.nn.sigmoid(g_n).reshape(b, l, NSA_HEADS, 3)
        q_n = part(QN_OFF, D_NSA).reshape(b, l, NSA_HEADS, NSA_HEAD_DIM)
        y_nsa = _nsa_attend(q_n, gates, kv_full, win_ctx, past_len, past_len - win_past.shape[1], lp['nsa_cmp_pe'], lp['nsa_cmp_w'], rel_bias)
    y_gla, s_new = _gla_mixer(q_g, k_g, v_g, g_g, a_g, gla_s0, lp['gla_gate_w2'], lp['gla_gate_b'], lp['gla_norm_w'])
    y_out = _proj(jnp.concatenate([y_ssd, y_nsa, y_gla], axis=-1), lp['w_out'], name='out_proj')
    return y_out, rows, win_new, conv_new, h_new, s_new


def _cross_attn(x, mem_kv, wq, wo):
    b, l, _ = x.shape
    q = _proj(x, wq, name='xattn_q').reshape(b, l, X_HEADS, X_HEAD_DIM) * (X_HEAD_DIM ** -0.5)
    s = jnp.einsum('blhd,bmhd->bhlm', q, mem_kv[:, :, 0].astype(F32))
    p = jax.nn.softmax(s, axis=-1)
    o = jnp.einsum('bhlm,bmhd->blhd', p, mem_kv[:, :, 1].astype(F32)).reshape(b, l, D_MODEL)
    return _proj(o, wo, name='xattn_o')


def _sqrelu_ffn(x, w1, w2):
    h = _proj(x, w1, out_dtype=BF16, act='sqrelu', name='ffn_up')
    return _proj(h, w2, name='ffn_down')


def _layer(x, lp, rel_bias, tables, mem_kv, conv0, ssd_h0, gla_s0, nsa_past, win_past, past_len):
    h, rows, win, conv, hs, sg = _mixer(x, lp, rel_bias, tables, conv0, ssd_h0, gla_s0, nsa_past, win_past, past_len)
    x = _layernorm(DN_ALPHA * x + h, lp['ln_g'][0], lp['ln_b'][0])
    x = _layernorm(DN_ALPHA * x + _cross_attn(x, mem_kv, lp['x_wq'], lp['x_wo']), lp['ln_g'][1], lp['ln_b'][1])
    x = _layernorm(DN_ALPHA * x + _sqrelu_ffn(x, lp['ffn_w1'], lp['ffn_w2']), lp['ln_g'][2], lp['ln_b'][2])
    return x, (rows, win, conv, hs, sg)


def _layer_flat(x, xb, bsz, lp, rel_bias, tables, mem_kv, conv0, ssd_h0, gla_s0, nsa_past, win_past, past_len):
    m = x.shape[0]
    l = m // bsz
    assert l >= SSD_CONV - 1
    y = _matmul(xb, lp['w_in'], name='in_proj')
    y3 = y.reshape(bsz, l, D_IN_PAD)
    y_ssd, h_new = _ssd(y, conv0, ssd_h0, lp['ssd_conv_w'], lp['ssd_conv_b'], lp['ssd_dt_bias'], lp['ssd_a_log'],
                        lp['ssd_d'], lp['ssd_norm_w'], bsz)
    conv_new = y3[:, l - (SSD_CONV - 1):, XBC_OFF:XBC_OFF + SSD_CONV_DIM]
    y_gla, s_new = _gla(y, gla_s0, lp['gla_gate_w2'], lp['gla_gate_b'], lp['gla_norm_w'], bsz)
    rows = y3[..., KV_OFF:KV_OFF + 4 * KVW].reshape(bsz, l, 4, NSA_KV_HEADS, NSA_HEAD_DIM)
    win_rows = y3[..., KV_OFF + 4 * KVW:KV_OFF + 6 * KVW].reshape(bsz, l, 2, NSA_KV_HEADS, NSA_HEAD_DIM)
    if nsa_past is None:
        win_new = win_rows[:, -min(WINDOW, l):]
        y_nsa = _nsa_prompt(y, _nsa_compress_prompt(y, lp['nsa_cmp_pe'], lp['nsa_cmp_w'], bsz), tables, bsz)
    else:
        cache_kv, page_table = nsa_past
        win_new = jnp.concatenate([win_past, win_rows], axis=1)[:, -win_past.shape[1]:]
        y_nsa = _nsa_sample(y, cache_kv, page_table, win_past, lp['nsa_cmp_pe'], lp['nsa_cmp_w'], rel_bias, bsz)
    h = _matmul(jnp.concatenate([y_ssd, y_nsa, y_gla], axis=-1), lp['w_out'], name='out_proj')
    x, xb = _add_layernorm(x, h, lp['ln_g'][0], lp['ln_b'][0])
    o = _cross_attention(_matmul(xb, lp['x_wq'], name='xattn_q'), mem_kv, bsz)
    x, xb = _add_layernorm(x, _matmul(o, lp['x_wo'], name='xattn_o'), lp['ln_g'][1], lp['ln_b'][1])
    hidden = _matmul(xb, lp['ffn_w1'], out_dtype=BF16, act='sqrelu', name='ffn_up')
    x, xb = _add_layernorm(x, _matmul(hidden, lp['ffn_w2'], name='ffn_down'), lp['ln_g'][2], lp['ln_b'][2])
    return x, xb, (rows, win_new, conv_new, h_new, s_new)


def kernel(x_prompt, x_sample, cache_nsa_kv, cache_nsa_win, state_ssd, state_ssd_conv, state_gla, cache_mem_kv,
           page_table, mem_prompt, w_in, ssd_conv_w, ssd_conv_b, ssd_dt_bias, ssd_a_log, ssd_d, ssd_norm_w,
           nsa_cmp_pe, nsa_cmp_w, rel_bias, gla_gate_w2, gla_gate_b, gla_norm_w, w_out, x_wq, x_wkv, x_wo,
           ffn_w1, ffn_w2, ln_g, ln_b):
    bp = x_prompt.shape[0]
    bs = x_sample.shape[0]
    past_len = page_table.shape[1] * cache_nsa_kv.shape[2]
    xp, xs = x_prompt.reshape(-1, D_MODEL), x_sample.reshape(-1, D_MODEL)
    xpb, xsb = xp.astype(BF16), xs.astype(BF16)
    st_p, st_s, mem_p = [], [], []
    tables = _nsa_prompt_bias_tables(rel_bias)
    for l in range(DEPTH):
        lp = dict(w_in=_reorder_w_in(w_in[l]), ssd_conv_w=ssd_conv_w[l], ssd_conv_b=ssd_conv_b[l],
                  ssd_dt_bias=ssd_dt_bias[l], ssd_a_log=ssd_a_log[l], ssd_d=ssd_d[l], ssd_norm_w=ssd_norm_w[l],
                  nsa_cmp_pe=nsa_cmp_pe[l], nsa_cmp_w=nsa_cmp_w[l], gla_gate_w2=gla_gate_w2[l], gla_gate_b=gla_gate_b[l],
                  gla_norm_w=gla_norm_w[l], w_out=w_out[l].astype(BF16), x_wq=x_wq[l].astype(BF16),
                  x_wo=x_wo[l].astype(BF16), ffn_w1=ffn_w1[l].astype(BF16), ffn_w2=ffn_w2[l].astype(BF16),
                  ln_g=ln_g[l], ln_b=ln_b[l])
        mem_kv_p = _proj(mem_prompt, x_wkv[l].astype(BF16), name='mem_kv')
        conv0 = jnp.zeros((bp, SSD_CONV - 1, SSD_CONV_DIM), F32)
        h0 = jnp.zeros((bp, SSD_HEADS, SSD_HEAD_DIM, SSD_STATE), F32)
        s0 = jnp.zeros((bp, GLA_HEADS, GLA_DK, GLA_DV), F32)
        xp, xpb, stp = _layer_flat(xp, xpb, bp, lp, rel_bias, tables, mem_kv_p, conv0, h0, s0, None, None, 0)
        st_p.append(stp)
        mem_p.append(mem_kv_p.reshape(bp, N_MEM, 2, X_HEADS, X_HEAD_DIM))
        xs, xsb, sts = _layer_flat(xs, xsb, bs, lp, rel_bias, tables, cache_mem_kv[l].reshape(bs, N_MEM, 2 * D_MODEL),
                                   state_ssd_conv[l], state_ssd[l], state_gla[l], (cache_nsa_kv[l], page_table),
                                   cache_nsa_win[l], past_len)
        st_s.append(sts)
    p_rows, p_win, p_conv, p_ssd, p_gla = [jnp.stack(s) for s in zip(*st_p)]
    s_rows, s_win, s_conv, s_ssd, s_gla = [jnp.stack(s) for s in zip(*st_s)]
    p_mem = jnp.stack(mem_p)
    return (xp.reshape(x_prompt.shape), xs.reshape(x_sample.shape), p_rows, s_rows, p_win, s_win, p_ssd, s_ssd,
            p_conv, s_conv, p_gla, s_gla, p_mem)
```

```python
import functools
import math

import jax
import jax.numpy as jnp
import numpy as np
from jax import lax
from jax.experimental import pallas as pl
from jax.experimental.pallas import tpu as pltpu

F32 = jnp.float32
BF16 = jnp.bfloat16

D_MODEL = 4096
DEPTH = 2
PAGE_SIZE = 128
D_MIX = D_MODEL
D_SSD = D_MIX // 2
SSD_HEAD_DIM = 64
SSD_HEADS = D_SSD // SSD_HEAD_DIM
SSD_GROUPS = 8
SSD_STATE = 128
SSD_CONV = 4
SSD_CHUNK = 128
SSD_CONV_DIM = D_SSD + 2 * SSD_GROUPS * SSD_STATE
D_NSA = D_MIX // 4
NSA_HEAD_DIM = 128
NSA_HEADS = D_NSA // NSA_HEAD_DIM
NSA_KV_HEADS = 2
NSA_REP = NSA_HEADS // NSA_KV_HEADS
CMP_LEN = 32
CMP_STRIDE = 16
SEL_LEN = 64
TOP_N = 16
WINDOW = 512
Q_BLOCK = 128
FORCE_SCORE = 1e4
D_GLA = D_MIX - D_SSD - D_NSA
GLA_HEADS = 4
GLA_DV = D_GLA // GLA_HEADS
GLA_DK = GLA_DV // 2
GLA_RANK = 16
GLA_TAU = 16.0
GLA_CHUNK = 16
N_MEM = 256
X_HEADS = 4
X_HEAD_DIM = D_MODEL // X_HEADS
D_FF = 4 * D_MODEL
N_BUCKETS = 32
MAX_DISTANCE = 128
LN_EPS = 1e-5
NORM_EPS = 1e-6
DN_ALPHA = (2 * DEPTH) ** 0.25
IN_SPLITS = (D_SSD, SSD_CONV_DIM, SSD_HEADS, NSA_HEADS * NSA_HEAD_DIM) + (NSA_KV_HEADS * NSA_HEAD_DIM,) * 6 + (
    3 * NSA_HEADS, GLA_HEADS * GLA_DK, GLA_HEADS * GLA_DK, D_GLA, D_GLA, GLA_RANK)
D_IN = sum(IN_SPLITS)

LANE = 128
VMEM_LIMIT_BYTES = 56 * 1024 * 1024

KVW = NSA_KV_HEADS * NSA_HEAD_DIM
Z_OFF = 0
XBC_OFF = Z_OFF + D_SSD
QN_OFF = XBC_OFF + SSD_CONV_DIM
VG_OFF = QN_OFF + NSA_HEADS * NSA_HEAD_DIM
GG_OFF = VG_OFF + D_GLA
KV_OFF = GG_OFF + D_GLA
QG_OFF = KV_OFF + 6 * KVW
KG_OFF = QG_OFF + GLA_HEADS * GLA_DK
SMALL_OFF = KG_OFF + GLA_HEADS * GLA_DK
DT_OFF = SMALL_OFF
ALR_OFF = SMALL_OFF + SSD_HEADS
GATE_OFF = SMALL_OFF + LANE
D_IN_PAD = 12288
NEG = -1e30


def _in_proj_column_map():
    off = np.cumsum((0,) + IN_SPLITS)
    z, xbc, dt, qn = off[0], off[1], off[2], off[3]
    kv0, gn, qg, kg, vg, gg, alr = off[4], off[10], off[11], off[12], off[13], off[14], off[15]
    cmap = np.full((D_IN_PAD,), -1, np.int64)
    cmap[Z_OFF:Z_OFF + D_SSD] = z + np.arange(D_SSD)
    cmap[XBC_OFF:XBC_OFF + SSD_CONV_DIM] = xbc + np.arange(SSD_CONV_DIM)
    cmap[QN_OFF:QN_OFF + D_NSA] = qn + np.arange(D_NSA)
    cmap[KV_OFF:KV_OFF + 6 * KVW] = kv0 + np.arange(6 * KVW)
    cmap[QG_OFF:QG_OFF + 512] = qg + np.arange(512)
    cmap[KG_OFF:KG_OFF + 512] = kg + np.arange(512)
    cmap[VG_OFF:VG_OFF + D_GLA] = vg + np.arange(D_GLA)
    cmap[GG_OFF:GG_OFF + D_GLA] = gg + np.arange(D_GLA)
    cmap[DT_OFF:DT_OFF + SSD_HEADS] = dt + np.arange(SSD_HEADS)
    cmap[ALR_OFF:ALR_OFF + GLA_RANK] = alr + np.arange(GLA_RANK)
    for g in range(NSA_KV_HEADS):
        cmap[GATE_OFF + g * LANE:GATE_OFF + g * LANE + 3 * NSA_REP] = gn + g * 3 * NSA_REP + np.arange(3 * NSA_REP)
    return cmap


def _reorder_w_in(w):
    cmap = _in_proj_column_map()
    cuts = [0] + [i for i in range(1, D_IN_PAD) if cmap[i] != cmap[i - 1] + 1 and not (cmap[i] == -1 and cmap[i - 1] == -1)] + [D_IN_PAD]
    pieces = []
    for a, b in zip(cuts[:-1], cuts[1:]):
        pieces.append(jnp.zeros((w.shape[0], b - a), BF16) if cmap[a] < 0 else w[:, cmap[a]:cmap[a] + b - a].astype(BF16))
    return jnp.concatenate(pieces, axis=1)


def _dot_nt(a, b):
    return lax.dot_general(a, b, (((1,), (1,)), ((), ())), preferred_element_type=F32)


NSA_L = 2048
NSA_QB = Q_BLOCK
NSA_NQB = NSA_L // NSA_QB
NSA_NCP = NSA_L // CMP_STRIDE
NSA_NSEL = NSA_L // SEL_LEN
CMP_HALF = CMP_STRIDE * NSA_HEAD_DIM
NSA_KSTEP = 512


def _t5_bucket_np(dist):
    n = np.maximum(dist, 0)
    max_exact = N_BUCKETS // 2
    nf = np.maximum(n, 1).astype(np.float32)
    large = max_exact + (np.log(nf / max_exact) / np.float32(math.log(MAX_DISTANCE / max_exact)) * (N_BUCKETS - max_exact)).astype(np.int32)
    return np.where(n < max_exact, n, np.minimum(large, N_BUCKETS - 1)).astype(np.int32)


def _nsa_prompt_bias_tables(rel_bias):
    tq = np.arange(NSA_QB)[:, None]
    ts = np.arange(NSA_QB)[None, :]

    def table(idx):
        t = rel_bias[jnp.asarray(idx)]
        t = jnp.moveaxis(t, -1, 0).reshape((NSA_KV_HEADS, NSA_REP) + idx.shape)
        t = jnp.moveaxis(t, 1, -3)
        return t.reshape(t.shape[:-3] + (NSA_REP * NSA_QB, idx.shape[-1]))

    b_diag = table(_t5_bucket_np(tq - ts))
    b_prev = table(_t5_bucket_np(NSA_QB + tq - ts))
    b_far = table(np.full((NSA_QB, NSA_QB), N_BUCKETS - 1, np.int32))
    i = np.arange(NSA_NQB)[:, None, None]
    n = np.arange(NSA_NCP)[None, None, :]
    b_cmp = table(_t5_bucket_np(i * NSA_QB + tq[None] - (n * CMP_STRIDE + CMP_LEN - 1)))
    return b_diag, b_prev, b_far, b_cmp


def _nsa_compress_kernel(a_ref, pe_ref, w_ref, o_ref):
    a = a_ref[0, 0, 0]
    lo = jnp.dot((a + pe_ref[0, 0]).astype(BF16), w_ref[0, 0], preferred_element_type=F32)
    hi = jnp.dot((a + pe_ref[0, 1]).astype(BF16), w_ref[0, 1], preferred_element_type=F32)
    o_ref[0, 0, 0] = lo + pltpu.roll(hi, NSA_NCP - 1, 0)


def _nsa_compress_prompt(y, cmp_pe, cmp_w, bsz):
    a = y[:, KV_OFF:KV_OFF + 2 * KVW].reshape(bsz, NSA_NCP, CMP_STRIDE, 2, NSA_KV_HEADS, NSA_HEAD_DIM)
    a = a.transpose(0, 3, 4, 1, 2, 5).reshape(bsz, 2, NSA_KV_HEADS, NSA_NCP, CMP_HALF)
    pe = cmp_pe.reshape(2, 2, 1, CMP_HALF)
    w = cmp_w.astype(BF16).reshape(2, 2, CMP_HALF, NSA_HEAD_DIM)
    return pl.pallas_call(
        _nsa_compress_kernel,
        grid=(bsz, 2, NSA_KV_HEADS),
        in_specs=[pl.BlockSpec((1, 1, 1, NSA_NCP, CMP_HALF), lambda b, c, g: (b, c, g, 0, 0)),
                  pl.BlockSpec((1, 2, 1, CMP_HALF), lambda b, c, g: (c, 0, 0, 0)),
                  pl.BlockSpec((1, 2, CMP_HALF, NSA_HEAD_DIM), lambda b, c, g: (c, 0, 0, 0))],
        out_specs=pl.BlockSpec((1, 1, 1, NSA_NCP, NSA_HEAD_DIM), lambda b, c, g: (b, c, g, 0, 0)),
        out_shape=jax.ShapeDtypeStruct((bsz, 2, NSA_KV_HEADS, NSA_NCP, NSA_HEAD_DIM), F32),
        compiler_params=pltpu.CompilerParams(dimension_semantics=('parallel', 'parallel', 'parallel')),
        name='nsa_compress',
    )(a, pe, w)


def _nsa_prompt_kernel(q_ref, gate_ref, cmp_ref, ks_ref, vs_ref, kw_ref, vw_ref, bdiag_ref, bprev_ref, bfar_ref,
                       bcmp_ref, covt_ref, expand_ref, o_ref, m_ref, l_ref, acc_ref):
    i = pl.program_id(2)
    rows = NSA_REP * NSA_QB
    qblk = q_ref[...] * (NSA_HEAD_DIM ** -0.5)
    q = jnp.concatenate([qblk[:, r * LANE:(r + 1) * LANE] for r in range(NSA_REP)], axis=0).astype(BF16)
    tq = lax.broadcasted_iota(jnp.int32, (rows, LANE), 0) % NSA_QB
    lane = lax.broadcasted_iota(jnp.int32, (rows, LANE), 1)
    t_abs = i * NSA_QB + tq

    k_c = cmp_ref[0, 0, 0].astype(BF16)
    v_c = cmp_ref[0, 1, 0].astype(BF16)
    mask_c = lane * CMP_STRIDE + (CMP_LEN - 1) <= t_abs
    s_c = jnp.where(mask_c, _dot_nt(q, k_c) + bcmp_ref[0, 0], NEG)
    e_c = jnp.where(mask_c, jnp.exp(s_c - jnp.max(s_c, -1, keepdims=True)), 0.0)
    d_c = jnp.sum(e_c, -1, keepdims=True)
    p_c = (e_c / jnp.where(d_c > 0, d_c, 1.0)).astype(BF16)
    o_c = jnp.dot(p_c, v_c, preferred_element_type=F32)

    imp4 = _dot_nt(covt_ref[...], p_c)
    imp = sum(imp4[:, r * NSA_QB:(r + 1) * NSA_QB] for r in range(NSA_REP))
    blk = lax.broadcasted_iota(jnp.int32, (LANE, NSA_QB), 0)
    t_row = i * NSA_QB + lax.broadcasted_iota(jnp.int32, (LANE, NSA_QB), 1)
    cur = t_row // SEL_LEN
    forced = (blk == 0) | (blk == cur) | (blk == cur - 1)
    score = jnp.where(forced, FORCE_SCORE, jnp.where(blk * SEL_LEN <= t_row, imp, -FORCE_SCORE))
    score = jnp.where(blk < NSA_NSEL, score, -3.0 * FORCE_SCORE)
    rank = jnp.zeros((LANE, NSA_QB), jnp.int32)
    for j in range(NSA_NSEL):
        row = score[j:j + 1, :]
        rank = rank + jnp.where((row > score) | ((row == score) & (blk > j)), 1, 0)
    sel_t = jnp.where((rank < TOP_N) & (blk < NSA_NSEL), 1.0, 0.0)
    sel = sel_t.T.astype(BF16)

    def init():
        m_ref[...] = jnp.full((rows, 1), NEG, F32)
        l_ref[...] = jnp.zeros((rows, 1), F32)
        acc_ref[...] = jnp.zeros((rows, NSA_HEAD_DIM), F32)

    def step(k_ref, v_ref, c, bias, mask):
        start = pl.multiple_of(c * NSA_QB, NSA_QB)
        k = k_ref[pl.ds(start, NSA_QB), :].astype(BF16)
        v = v_ref[pl.ds(start, NSA_QB), :].astype(BF16)
        s = _dot_nt(q, k) + bias
        if mask is not None:
            s = jnp.where(mask, s, NEG)
        m_old = m_ref[...]
        m_new = jnp.maximum(m_old, jnp.max(s, -1, keepdims=True))
        p = jnp.exp(s - m_new)
        if mask is not None:
            p = jnp.where(mask, p, 0.0)
        alpha = jnp.exp(m_old - m_new)
        l_ref[...] = alpha * l_ref[...] + jnp.sum(p, -1, keepdims=True)
        acc_ref[...] = alpha * acc_ref[...] + jnp.dot(p.astype(BF16), v, preferred_element_type=F32)
        m_ref[...] = m_new

    def finish():
        l = l_ref[...]
        return acc_ref[...] / jnp.where(l > 0, l, 1.0)

    def sel_mask(c):
        m1 = jnp.dot(sel, expand_ref[c], preferred_element_type=F32)
        return jnp.concatenate([m1] * NSA_REP, axis=0) > 0.5

    init()
    bfar = bfar_ref[0]

    def far_body(c, carry):
        step(ks_ref, vs_ref, c, bfar, sel_mask(c))
        return carry

    lax.fori_loop(0, jnp.maximum(i - 1, 0), far_body, 0)

    @pl.when(i >= 1)
    def _():
        step(ks_ref, vs_ref, i - 1, bprev_ref[0], sel_mask(i - 1))

    causal = lane <= tq
    step(ks_ref, vs_ref, i, bdiag_ref[0], sel_mask(i) & causal)
    o_s = finish()

    init()
    step(kw_ref, vw_ref, i, bdiag_ref[0], causal)

    @pl.when(i >= 1)
    def _():
        step(kw_ref, vw_ref, i - 1, bprev_ref[0], None)

    for d in range(2, WINDOW // NSA_QB):
        @pl.when(i >= d)
        def _(d=d):
            step(kw_ref, vw_ref, i - d, bfar, None)

    @pl.when(i >= WINDOW // NSA_QB)
    def _():
        step(kw_ref, vw_ref, i - WINDOW // NSA_QB, bfar, lane > tq)

    o_w = finish()

    gates = jax.nn.sigmoid(gate_ref[...])
    for r in range(NSA_REP):
        rs = slice(r * NSA_QB, (r + 1) * NSA_QB)
        o_ref[:, r * LANE:(r + 1) * LANE] = (gates[:, 3 * r:3 * r + 1] * o_c[rs] + gates[:, 3 * r + 1:3 * r + 2] * o_s[rs]
                                             + gates[:, 3 * r + 2:3 * r + 3] * o_w[rs]).astype(o_ref.dtype)


def _nsa_prompt(y, k_v_cmp, tables, bsz):
    b_diag, b_prev, b_far, b_cmp = tables
    rows = NSA_REP * NSA_QB
    c_lo = np.arange(NSA_NCP) * CMP_STRIDE
    s_lo = np.arange(LANE) * SEL_LEN
    cov_t = ((c_lo[None] < s_lo[:, None] + SEL_LEN) & (c_lo[None] + CMP_LEN > s_lo[:, None])
             & (np.arange(NSA_NCP)[None] < NSA_NCP - 1) & (np.arange(LANE)[:, None] < NSA_NSEL))
    key_blk = (np.arange(NSA_L) // SEL_LEN).reshape(NSA_NQB, 1, NSA_QB)
    expand = (np.arange(LANE)[None, :, None] == key_blk)
    kv_blk = KV_OFF // LANE
    grp = NSA_KV_HEADS

    def col(slot):
        return pl.BlockSpec((NSA_L, NSA_HEAD_DIM), lambda b, g, i: (b, kv_blk + slot * grp + g))

    tab = pl.BlockSpec((1, rows, LANE), lambda b, g, i: (g, 0, 0))
    return pl.pallas_call(
        _nsa_prompt_kernel,
        grid=(bsz, NSA_KV_HEADS, NSA_NQB),
        in_specs=[pl.BlockSpec((NSA_QB, NSA_REP * LANE), lambda b, g, i: (b * NSA_NQB + i, QN_OFF // (NSA_REP * LANE) + g)),
                  pl.BlockSpec((NSA_QB, LANE), lambda b, g, i: (b * NSA_NQB + i, GATE_OFF // LANE + g)),
                  pl.BlockSpec((1, 2, 1, NSA_NCP, NSA_HEAD_DIM), lambda b, g, i: (b, 0, g, 0, 0)),
                  col(2), col(3), col(4), col(5), tab, tab, tab,
                  pl.BlockSpec((1, 1, rows, NSA_NCP), lambda b, g, i: (g, i, 0, 0)),
                  pl.BlockSpec((LANE, NSA_NCP), lambda b, g, i: (0, 0)),
                  pl.BlockSpec((NSA_NQB, LANE, NSA_QB), lambda b, g, i: (0, 0, 0))],
        out_specs=pl.BlockSpec((NSA_QB, NSA_REP * LANE), lambda b, g, i: (b * NSA_NQB + i, g)),
        out_shape=jax.ShapeDtypeStruct((bsz * NSA_L, D_NSA), BF16),
        scratch_shapes=[pltpu.VMEM((rows, 1), F32), pltpu.VMEM((rows, 1), F32), pltpu.VMEM((rows, NSA_HEAD_DIM), F32)],
        compiler_params=pltpu.CompilerParams(dimension_semantics=('parallel', 'parallel', 'arbitrary'),
                                             vmem_limit_bytes=VMEM_LIMIT_BYTES),
        name='nsa_prompt',
    )(y, y, k_v_cmp, y, y, y, y, b_diag, b_prev, b_far, b_cmp,
      jnp.asarray(cov_t, BF16), jnp.asarray(expand, BF16))


def _nsa_prompt_bias_tables_t(rel_bias):
    tq = np.arange(NSA_QB)[None, :]
    ts = np.arange(NSA_QB)[:, None]

    def table(idx):
        t = rel_bias[jnp.asarray(idx)]
        t = jnp.moveaxis(t, -1, 0).reshape((NSA_KV_HEADS, NSA_REP) + idx.shape)
        t = jnp.moveaxis(t, 1, -2)
        return t.reshape(t.shape[:-2] + (NSA_REP * NSA_QB,))

    b_diag = table(_t5_bucket_np(tq - ts))
    b_prev = table(_t5_bucket_np(NSA_QB + tq - ts))
    b_far = table(np.full((8, NSA_QB), N_BUCKETS - 1, np.int32))
    i = np.arange(NSA_NQB)[:, None, None]
    n = np.arange(NSA_NCP)[None, :, None]
    b_cmp = table(_t5_bucket_np(i * NSA_QB + tq[None] - (n * CMP_STRIDE + CMP_LEN - 1)))
    return b_diag, b_prev, b_far, b_cmp


def _nsa_prompt_t_kernel(q_ref, gate_ref, cmp_ref, ks_ref, vs_ref, kw_ref, vw_ref, bdiag_ref, bprev_ref, bfar_ref,
                         bcmp_ref, covt_ref, expand_ref, o_ref, m_ref, l_ref, acc_ref, ksb, vst, kwb, vwt):
    i = pl.program_id(2)
    cols = NSA_REP * NSA_QB

    @pl.when(i == 0)
    def _():
        ksb[...] = ks_ref[...].astype(BF16)
        kwb[...] = kw_ref[...].astype(BF16)
        for c in range(NSA_L // NSA_KSTEP):
            vst[c] = vs_ref[c * NSA_KSTEP:(c + 1) * NSA_KSTEP, :].T.astype(BF16)
            vwt[c] = vw_ref[c * NSA_KSTEP:(c + 1) * NSA_KSTEP, :].T.astype(BF16)

    qblk = q_ref[...] * (NSA_HEAD_DIM ** -0.5)
    q_t = jnp.concatenate([qblk[:, r * LANE:(r + 1) * LANE].T for r in range(NSA_REP)], axis=1).astype(BF16)
    key = lax.broadcasted_iota(jnp.int32, (NSA_QB, cols), 0)
    tq = lax.broadcasted_iota(jnp.int32, (NSA_QB, cols), 1) % NSA_QB
    t_abs = i * NSA_QB + tq

    k_c = cmp_ref[0, 0, 0].astype(BF16)
    v_c_t = cmp_ref[0, 1, 0].T.astype(BF16)
    mask_c = key * CMP_STRIDE + (CMP_LEN - 1) <= t_abs
    s_c = jnp.where(mask_c, jnp.dot(k_c, q_t, preferred_element_type=F32) + bcmp_ref[0, 0], NEG)
    e_c = jnp.where(mask_c, jnp.exp(s_c - jnp.max(s_c, 0, keepdims=True)), 0.0)
    d_c = jnp.sum(e_c, 0, keepdims=True)
    p_c = (e_c / jnp.where(d_c > 0, d_c, 1.0)).astype(BF16)
    o_c = jnp.dot(v_c_t, p_c, preferred_element_type=F32)

    imp4 = jnp.dot(covt_ref[...], p_c, preferred_element_type=F32)
    imp = sum(imp4[:, r * NSA_QB:(r + 1) * NSA_QB] for r in range(NSA_REP))
    blk = lax.broadcasted_iota(jnp.int32, (LANE, NSA_QB), 0)
    t_row = i * NSA_QB + lax.broadcasted_iota(jnp.int32, (LANE, NSA_QB), 1)
    cur = t_row // SEL_LEN
    forced = (blk == 0) | (blk == cur) | (blk == cur - 1)
    score = jnp.where(forced, FORCE_SCORE, jnp.where(blk * SEL_LEN <= t_row, imp, -FORCE_SCORE))
    score = jnp.where(blk < NSA_NSEL, score, -3.0 * FORCE_SCORE)
    rank = jnp.zeros((LANE, NSA_QB), jnp.int32)
    for j in range(NSA_NSEL):
        row = score[j:j + 1, :]
        rank = rank + jnp.where((row > score) | ((row == score) & (blk > j)), 1, 0)
    sel_t = jnp.where((rank < TOP_N) & (blk < NSA_NSEL), 1.0, 0.0).astype(BF16)

    def init():
        m_ref[...] = jnp.full((1, cols), NEG, F32)
        l_ref[...] = jnp.zeros((1, cols), F32)
        acc_ref[...] = jnp.zeros((NSA_HEAD_DIM, cols), F32)

    dist0 = (lax.broadcasted_iota(jnp.int32, (NSA_KSTEP, cols), 1) % NSA_QB
             - lax.broadcasted_iota(jnp.int32, (NSA_KSTEP, cols), 0))
    per_step = NSA_KSTEP // NSA_QB
    bfar = bfar_ref[0, 0:1, :]

    def step(k_scr, v_scr, j, selected):
        start = pl.multiple_of(j * NSA_KSTEP, NSA_KSTEP)
        s = jnp.dot(k_scr[pl.ds(start, NSA_KSTEP), :], q_t, preferred_element_type=F32)
        bias = []
        for u in range(per_step):
            d = i - (j * per_step + u)
            bias.append(jnp.where(d == 0, bdiag_ref[0], jnp.where(d == 1, bprev_ref[0], bfar)))
        dist = dist0 + (i * NSA_QB - j * NSA_KSTEP)
        if selected:
            m1 = jnp.dot(expand_ref[j], sel_t, preferred_element_type=F32)
            mask = (jnp.concatenate([m1] * NSA_REP, axis=1) > 0.5) & (dist >= 0)
        else:
            mask = (dist >= 0) & (dist < WINDOW)
        s = jnp.where(mask, s + jnp.concatenate(bias, axis=0), NEG)
        m_old = m_ref[...]
        m_new = jnp.maximum(m_old, jnp.max(s, 0, keepdims=True))
        p = jnp.where(mask, jnp.exp(s - m_new), 0.0)
        alpha = jnp.exp(m_old - m_new)
        l_ref[...] = alpha * l_ref[...] + jnp.sum(p, 0, keepdims=True)
        acc_ref[...] = alpha * acc_ref[...] + jnp.dot(v_scr[j], p.astype(BF16), preferred_element_type=F32)
        m_ref[...] = m_new

    def finish():
        l = l_ref[...]
        return acc_ref[...] / jnp.where(l > 0, l, 1.0)

    j_diag = i // per_step

    init()

    def sel_body(j, carry):
        step(ksb, vst, j, True)
        return carry

    lax.fori_loop(0, j_diag + 1, sel_body, 0)
    o_s = finish()

    init()
    step(kwb, vwt, j_diag, False)

    @pl.when(j_diag >= 1)
    def _():
        step(kwb, vwt, j_diag - 1, False)

    o_w = finish()

    g_t = jax.nn.sigmoid(gate_ref[...]).T
    for r in range(NSA_REP):
        cs = slice(r * NSA_QB, (r + 1) * NSA_QB)
        o_r = g_t[3 * r:3 * r + 1, :] * o_c[:, cs] + g_t[3 * r + 1:3 * r + 2, :] * o_s[:, cs] + g_t[3 * r + 2:3 * r + 3, :] * o_w[:, cs]
        o_ref[:, r * LANE:(r + 1) * LANE] = o_r.T.astype(o_ref.dtype)


def _nsa_prompt_t(y, k_v_cmp, tables, bsz):
    b_diag, b_prev, b_far, b_cmp = tables
    cols = NSA_REP * NSA_QB
    nstep = NSA_L // NSA_KSTEP
    assert WINDOW <= NSA_KSTEP and NSA_KSTEP % NSA_QB == 0
    c_lo = np.arange(NSA_NCP) * CMP_STRIDE
    s_lo = np.arange(LANE) * SEL_LEN
    cov_t = ((c_lo[None] < s_lo[:, None] + SEL_LEN) & (c_lo[None] + CMP_LEN > s_lo[:, None])
             & (np.arange(NSA_NCP)[None] < NSA_NCP - 1) & (np.arange(LANE)[:, None] < NSA_NSEL))
    key_blk = (np.arange(NSA_L) // SEL_LEN).reshape(nstep, NSA_KSTEP, 1)
    expand = (np.arange(LANE)[None, None, :] == key_blk)
    kv_blk = KV_OFF // LANE
    grp = NSA_KV_HEADS

    def col(slot):
        return pl.BlockSpec((NSA_L, NSA_HEAD_DIM), lambda b, g, i: (b, kv_blk + slot * grp + g))

    tab = pl.BlockSpec((1, NSA_QB, cols), lambda b, g, i: (g, 0, 0))
    return pl.pallas_call(
        _nsa_prompt_t_kernel,
        grid=(bsz, NSA_KV_HEADS, NSA_NQB),
        in_specs=[pl.BlockSpec((NSA_QB, NSA_REP * LANE), lambda b, g, i: (b * NSA_NQB + i, QN_OFF // (NSA_REP * LANE) + g)),
                  pl.BlockSpec((NSA_QB, LANE), lambda b, g, i: (b * NSA_NQB + i, GATE_OFF // LANE + g)),
                  pl.BlockSpec((1, 2, 1, NSA_NCP, NSA_HEAD_DIM), lambda b, g, i: (b, 0, g, 0, 0)),
                  col(2), col(3), col(4), col(5), tab, tab,
                  pl.BlockSpec((1, 8, cols), lambda b, g, i: (g, 0, 0)),
                  pl.BlockSpec((1, 1, NSA_NCP, cols), lambda b, g, i: (g, i, 0, 0)),
                  pl.BlockSpec((LANE, NSA_NCP), lambda b, g, i: (0, 0)),
                  pl.BlockSpec((nstep, NSA_KSTEP, LANE), lambda b, g, i: (0, 0, 0))],
        out_specs=pl.BlockSpec((NSA_QB, NSA_REP * LANE), lambda b, g, i: (b * NSA_NQB + i, g)),
        out_shape=jax.ShapeDtypeStruct((bsz * NSA_L, D_NSA), BF16),
        scratch_shapes=[pltpu.VMEM((1, cols), F32), pltpu.VMEM((1, cols), F32), pltpu.VMEM((NSA_HEAD_DIM, cols), F32),
                        pltpu.VMEM((NSA_L, NSA_HEAD_DIM), BF16), pltpu.VMEM((nstep, NSA_HEAD_DIM, NSA_KSTEP), BF16),
                        pltpu.VMEM((NSA_L, NSA_HEAD_DIM), BF16), pltpu.VMEM((nstep, NSA_HEAD_DIM, NSA_KSTEP), BF16)],
        compiler_params=pltpu.CompilerParams(dimension_semantics=('parallel', 'parallel', 'arbitrary'),
                                             vmem_limit_bytes=VMEM_LIMIT_BYTES),
        name='nsa_prompt',
    )(y, y, k_v_cmp, y, y, y, y, b_diag, b_prev, b_far, b_cmp,
      jnp.asarray(cov_t, BF16), jnp.asarray(expand, BF16))


NS_SELW = 3 * LANE
NS_CHUNK = 2048


def _nsa_sample_tables(rel_bias, past_len, lq, win_len):
    tq = (past_len + np.arange(lq))[:, None]

    def table(idx):
        t = jnp.moveaxis(rel_bias[jnp.asarray(idx)], -1, 0).reshape(NSA_KV_HEADS, NSA_REP, lq, idx.shape[-1])
        return t.reshape(NSA_KV_HEADS, NSA_REP * lq, idx.shape[-1])

    ncp = past_len // CMP_STRIDE
    b_cmp = table(_t5_bucket_np(tq - (np.arange(ncp)[None] * CMP_STRIDE + CMP_LEN - 1)))
    b_last = table(_t5_bucket_np(tq - (past_len - LANE + np.arange(LANE)[None])))
    b_new = table(_t5_bucket_np(tq - (past_len + np.arange(LANE)[None])))
    b_far = table(np.full((lq, LANE), N_BUCKETS - 1, np.int32))
    wpos = np.concatenate([past_len - win_len + np.arange(win_len), past_len + np.arange(LANE)])
    b_win = table(_t5_bucket_np(tq - wpos[None]))
    return b_cmp, b_last, b_new, b_far, b_win


def _nsa_sample_kernel(pt_ref, q_ref, kvn_ref, gate0_ref, gate1_ref, win_ref, cache_ref, w_ref, pe_ref,
                       bcmp_ref, blast_ref, bnew_ref, bfar_ref, bwin_ref, cov_ref, expand_ref, o_ref,
                       buf, cmp_scr, s_scr, sem, *, n_pages, lq, past_len, win_len, page_base):
    b = pl.program_id(0)
    rows = NSA_REP * lq
    ncp = past_len // CMP_STRIDE
    n_cmp = (past_len + lq - CMP_LEN) // CMP_STRIDE + 1
    n_sel = -(-(past_len + lq) // SEL_LEN)
    n_chunks = past_len // NS_CHUNK
    ncol = 2 * NSA_KV_HEADS

    def page_copy(j, pair, col):
        return pltpu.make_async_copy(cache_ref.at[page_base + pt_ref[b, j], :, pl.ds((pair * ncol + col) * NSA_HEAD_DIM, NSA_HEAD_DIM)],
                                     buf.at[col, pl.ds(j * PAGE_SIZE, PAGE_SIZE), :], sem)

    def gather_start(pair):
        def body(j, carry):
            for col in range(ncol):
                page_copy(j, pair, col).start()
            return carry
        lax.fori_loop(0, n_pages, body, 0)

    def gather_wait(pair):
        def body(j, carry):
            for col in range(ncol):
                page_copy(j, pair, col).wait()
            return carry
        lax.fori_loop(0, n_pages, body, 0)

    gather_start(0)
    gather_wait(0)
    for c in range(2):
        for g in range(NSA_KV_HEADS):
            lo = jnp.zeros((ncp, NSA_HEAD_DIM), F32)
            hi = jnp.zeros((ncp, NSA_HEAD_DIM), F32)
            for l in range(CMP_STRIDE):
                x = buf[c * NSA_KV_HEADS + g, pl.ds(l, ncp, stride=CMP_STRIDE), :]
                lo = lo + jnp.dot((x + pe_ref[c, l:l + 1, :]).astype(BF16), w_ref[c, l], preferred_element_type=F32)
                hi = hi + jnp.dot((x + pe_ref[c, CMP_STRIDE + l:CMP_STRIDE + l + 1, :]).astype(BF16),
                                  w_ref[c, CMP_STRIDE + l], preferred_element_type=F32)
            cmp_scr[c, g] = lo + pltpu.roll(hi, ncp - 1, 0)
    gather_start(1)

    qall = q_ref[...] * (NSA_HEAD_DIM ** -0.5)
    tq = lax.broadcasted_iota(jnp.int32, (rows, 1), 0) % lq
    t_abs = past_len + tq
    o_c, sel, q_g = [], [], []
    for g in range(NSA_KV_HEADS):
        q = jnp.concatenate([qall[:, (g * NSA_REP + r) * LANE:(g * NSA_REP + r + 1) * LANE] for r in range(NSA_REP)],
                            axis=0).astype(BF16)
        q_g.append(q)
        n_idx = lax.broadcasted_iota(jnp.int32, (rows, ncp), 1)
        mask_c = (n_idx < n_cmp) & (n_idx * CMP_STRIDE + (CMP_LEN - 1) <= t_abs)
        s_c = jnp.where(mask_c, _dot_nt(q, cmp_scr[0, g].astype(BF16)) + bcmp_ref[g], NEG)
        e_c = jnp.where(mask_c, jnp.exp(s_c - jnp.max(s_c, -1, keepdims=True)), 0.0)
        d_c = jnp.sum(e_c, -1, keepdims=True)
        p_c = (e_c / jnp.where(d_c > 0, d_c, 1.0)).astype(BF16)
        o_c.append(jnp.dot(p_c, cmp_scr[1, g].astype(BF16), preferred_element_type=F32))
        imp4 = jnp.dot(p_c, cov_ref[...], preferred_element_type=F32)
        imp = sum(imp4[r * lq:(r + 1) * lq] for r in range(NSA_REP))
        blk = lax.broadcasted_iota(jnp.int32, (lq, NS_SELW), 1)
        t_q = past_len + lax.broadcasted_iota(jnp.int32, (lq, NS_SELW), 0)
        cur = t_q // SEL_LEN
        forced = (blk == 0) | (blk == cur) | (blk == cur - 1)
        score = jnp.where(forced, FORCE_SCORE, jnp.where(blk * SEL_LEN <= t_q, imp, -FORCE_SCORE))
        score = jnp.where(blk < n_sel, score, -3.0 * FORCE_SCORE)
        rank = jnp.zeros((lq, NS_SELW), jnp.int32)
        for j in range(n_sel):
            col = score[:, j:j + 1]
            rank = rank + jnp.where((col > score) | ((col == score) & (blk > j)), 1, 0)
        sel_q = jnp.where((rank < min(TOP_N, n_sel)) & (blk < n_sel), 1.0, 0.0)
        sel.append(jnp.concatenate([sel_q] * NSA_REP, axis=0))

    gather_wait(1)
    kvn = _pad_rows(kvn_ref[...], LANE)
    lane = lax.broadcasted_iota(jnp.int32, (rows, LANE), 1)
    blocks_per_chunk = NS_CHUNK // SEL_LEN
    gates = [jax.nn.sigmoid(gate0_ref[...]), jax.nn.sigmoid(gate1_ref[...])]
    for g in range(NSA_KV_HEADS):
        q = q_g[g]
        bfar = bfar_ref[g][:, 0:1]

        def new_rows(slot):
            return kvn[:, (slot * NSA_KV_HEADS + g) * NSA_HEAD_DIM:(slot * NSA_KV_HEADS + g + 1) * NSA_HEAD_DIM].astype(BF16)

        sel_b = sel[g].astype(BF16)
        for c in range(n_chunks):
            k = buf[g, c * NS_CHUNK:(c + 1) * NS_CHUNK, :].astype(BF16)
            s = _dot_nt(q, k) + bfar
            if c == n_chunks - 1:
                fix = blast_ref[g] - bfar_ref[g]
                s = jnp.concatenate([s[:, :NS_CHUNK - LANE], s[:, NS_CHUNK - LANE:] + fix], axis=1)
            m_c = jnp.dot(sel_b[:, c * blocks_per_chunk:(c + 1) * blocks_per_chunk], expand_ref[...],
                          preferred_element_type=F32)
            s_scr[:, c * NS_CHUNK:(c + 1) * NS_CHUNK] = jnp.where(m_c > 0.5, s, NEG)
        s_new = _dot_nt(q, new_rows(2)) + bnew_ref[g]
        mask_new = (lane <= tq) & (lane < lq) & (sel[g][:, n_sel - 1:n_sel] > 0.5)
        s_scr[:, past_len:past_len + LANE] = jnp.where(mask_new, s_new, NEG)
        m = jnp.max(s_scr[...], -1, keepdims=True)
        acc = jnp.zeros((rows, NSA_HEAD_DIM), F32)
        den = jnp.zeros((rows, 1), F32)
        for c in range(n_chunks):
            sc = s_scr[:, c * NS_CHUNK:(c + 1) * NS_CHUNK]
            e = jnp.where(sc > 0.5 * NEG, jnp.exp(sc - m), 0.0)
            den = den + jnp.sum(e, -1, keepdims=True)
            v = buf[NSA_KV_HEADS + g, c * NS_CHUNK:(c + 1) * NS_CHUNK, :]
            acc = acc + jnp.dot(e.astype(BF16), v.astype(BF16), preferred_element_type=F32)
        sc = s_scr[:, past_len:past_len + LANE]
        e = jnp.where(sc > 0.5 * NEG, jnp.exp(sc - m), 0.0)
        den = den + jnp.sum(e, -1, keepdims=True)
        acc = acc + jnp.dot(e.astype(BF16), new_rows(3), preferred_element_type=F32)
        o_s = acc / jnp.where(den > 0, den, 1.0)

        kw = jnp.concatenate([win_ref[0, :, g * NSA_HEAD_DIM:(g + 1) * NSA_HEAD_DIM].astype(BF16), new_rows(4)], axis=0)
        vw = jnp.concatenate([win_ref[0, :, (NSA_KV_HEADS + g) * NSA_HEAD_DIM:(NSA_KV_HEADS + g + 1) * NSA_HEAD_DIM].astype(BF16),
                              new_rows(5)], axis=0)
        wl = lax.broadcasted_iota(jnp.int32, (rows, win_len + LANE), 1)
        dist = jnp.where(wl < win_len, win_len + tq - wl, tq - (wl - win_len))
        mask_w = (dist >= 0) & (dist < WINDOW) & (wl < win_len + lq)
        s_w = jnp.where(mask_w, _dot_nt(q, kw) + bwin_ref[g], NEG)
        e_w = jnp.where(mask_w, jnp.exp(s_w - jnp.max(s_w, -1, keepdims=True)), 0.0)
        d_w = jnp.sum(e_w, -1, keepdims=True)
        o_w = jnp.dot((e_w / jnp.where(d_w > 0, d_w, 1.0)).astype(BF16), vw, preferred_element_type=F32)

        for r in range(NSA_REP):
            rs = slice(r * lq, (r + 1) * lq)
            gt = gates[g]
            h = g * NSA_REP + r
            o_ref[:, h * LANE:(h + 1) * LANE] = (gt[:, 3 * r:3 * r + 1] * o_c[g][rs] + gt[:, 3 * r + 1:3 * r + 2] * o_s[rs]
                                                 + gt[:, 3 * r + 2:3 * r + 3] * o_w[rs]).astype(o_ref.dtype)


def _nsa_sample(y, cache_kv, layer, page_table, cache_win, cmp_pe, cmp_w, rel_bias, bsz):
    lq = y.shape[0] // bsz
    n_pages = page_table.shape[1]
    past_len = n_pages * PAGE_SIZE
    win_len = cache_win.shape[1]
    rows = NSA_REP * lq
    ncp = past_len // CMP_STRIDE
    n_cmp = (past_len + lq - CMP_LEN) // CMP_STRIDE + 1
    n_sel = -(-(past_len + lq) // SEL_LEN)
    assert lq % 8 == 0 and lq <= CMP_STRIDE and n_cmp <= ncp - 1 + lq // CMP_STRIDE and past_len % NS_CHUNK == 0
    assert past_len % SEL_LEN == 0 and n_sel <= NS_SELW and lq % Q_BLOCK != 0 and win_len == WINDOW
    tables = _nsa_sample_tables(rel_bias, past_len, lq, win_len)
    c_lo = np.arange(ncp) * CMP_STRIDE
    s_lo = np.arange(NS_SELW) * SEL_LEN
    cover = ((c_lo[:, None] < s_lo[None] + SEL_LEN) & (c_lo[:, None] + CMP_LEN > s_lo[None])
             & (np.arange(ncp)[:, None] < n_cmp) & (np.arange(NS_SELW)[None] < n_sel))
    expand = np.arange(NS_CHUNK // SEL_LEN)[:, None] == (np.arange(NS_CHUNK) // SEL_LEN)[None]
    n_phys = cache_kv.shape[1]
    cache2 = cache_kv.reshape(cache_kv.shape[0] * n_phys, PAGE_SIZE, 4 * KVW)
    win2 = cache_win.reshape(bsz, win_len, 2 * KVW)

    def full(shape):
        return pl.BlockSpec(shape, lambda b, pt: (0,) * len(shape))

    grid_spec = pltpu.PrefetchScalarGridSpec(
        num_scalar_prefetch=1, grid=(bsz,),
        in_specs=[pl.BlockSpec((lq, D_NSA), lambda b, pt: (b, QN_OFF // D_NSA)),
                  pl.BlockSpec((lq, 6 * KVW), lambda b, pt: (b, KV_OFF // (6 * KVW))),
                  pl.BlockSpec((lq, LANE), lambda b, pt: (b, GATE_OFF // LANE)),
                  pl.BlockSpec((lq, LANE), lambda b, pt: (b, GATE_OFF // LANE + 1)),
                  pl.BlockSpec((1, win_len, 2 * KVW), lambda b, pt: (b, 0, 0)),
                  pl.BlockSpec(memory_space=pl.ANY),
                  full((2, CMP_LEN, NSA_HEAD_DIM, NSA_HEAD_DIM)), full((2, CMP_LEN, NSA_HEAD_DIM)),
                  full((NSA_KV_HEADS, rows, ncp)), full((NSA_KV_HEADS, rows, LANE)), full((NSA_KV_HEADS, rows, LANE)),
                  full((NSA_KV_HEADS, rows, LANE)), full((NSA_KV_HEADS, rows, win_len + LANE)),
                  full((ncp, NS_SELW)), full((NS_CHUNK // SEL_LEN, NS_CHUNK))],
        out_specs=pl.BlockSpec((lq, D_NSA), lambda b, pt: (b, 0)),
        scratch_shapes=[pltpu.VMEM((2 * NSA_KV_HEADS, past_len, NSA_HEAD_DIM), F32),
                        pltpu.VMEM((2, NSA_KV_HEADS, ncp, NSA_HEAD_DIM), F32),
                        pltpu.VMEM((rows, past_len + LANE), F32), pltpu.SemaphoreType.DMA(())])
    assert QN_OFF % D_NSA == 0 and KV_OFF % (6 * KVW) == 0
    return pl.pallas_call(
        functools.partial(_nsa_sample_kernel, n_pages=n_pages, lq=lq, past_len=past_len, win_len=win_len,
                          page_base=layer * n_phys),
        grid_spec=grid_spec,
        out_shape=jax.ShapeDtypeStruct((bsz * lq, D_NSA), BF16),
        compiler_params=pltpu.CompilerParams(dimension_semantics=('arbitrary',), vmem_limit_bytes=VMEM_LIMIT_BYTES),
        name='nsa_sample',
    )(page_table, y, y, y, y, win2, cache2, cmp_w.astype(BF16), cmp_pe, *tables,
      jnp.asarray(cover, BF16), jnp.asarray(expand, BF16))


def _mm_kernel(x_ref, w_ref, o_ref, *scratch, nk, act):
    def finish(acc):
        if act == 'sqrelu':
            acc = jnp.square(jnp.maximum(acc, 0.0))
        o_ref[...] = acc.astype(o_ref.dtype)

    if nk == 1:
        finish(jnp.dot(x_ref[...], w_ref[...], preferred_element_type=F32))
        return
    acc_ref, = scratch
    k = pl.program_id(2)
    part = jnp.dot(x_ref[...], w_ref[...], preferred_element_type=F32)

    @pl.when(k == 0)
    def _():
        acc_ref[...] = part

    @pl.when(k > 0)
    def _():
        acc_ref[...] += part

    @pl.when(k == nk - 1)
    def _():
        finish(acc_ref[...])


def _pick(n, pref):
    for t in pref:
        if n % t == 0:
            return t
    return n


def _matmul(x, w, out_dtype=F32, act=None, name='matmul'):
    m, k = x.shape
    n = w.shape[1]
    tm = _pick(m, (1024, 512, 256, 128, 64))
    tn = _pick(n, (512, 256, 128))
    tk = _pick(k, (4096, 2048, 1024, 512))
    nk = k // tk
    scratch = [pltpu.VMEM((tm, tn), F32)] if nk > 1 else []
    return pl.pallas_call(
        functools.partial(_mm_kernel, nk=nk, act=act),
        grid=(m // tm, n // tn, nk),
        in_specs=[pl.BlockSpec((tm, tk), lambda i, j, kk: (i, kk)),
                  pl.BlockSpec((tk, tn), lambda i, j, kk: (kk, j))],
        out_specs=pl.BlockSpec((tm, tn), lambda i, j, kk: (i, j)),
        out_shape=jax.ShapeDtypeStruct((m, n), out_dtype),
        scratch_shapes=scratch,
        compiler_params=pltpu.CompilerParams(
            dimension_semantics=('parallel', 'parallel', 'arbitrary'),
            vmem_limit_bytes=VMEM_LIMIT_BYTES),
        name=name,
    )(x, w)


def _proj(x, w_bf16, **kw):
    lead = x.shape[:-1]
    y = _matmul(x.reshape(-1, x.shape[-1]).astype(BF16), w_bf16, **kw)
    return y.reshape(lead + (w_bf16.shape[1],))


def _ln_kernel(x_ref, h_ref, g_ref, b_ref, o_ref, ob_ref):
    v = DN_ALPHA * x_ref[...] + h_ref[...]
    d = v - jnp.mean(v, -1, keepdims=True)
    y = d * lax.rsqrt(jnp.mean(d * d, -1, keepdims=True) + LN_EPS) * g_ref[...] + b_ref[...]
    o_ref[...] = y
    ob_ref[...] = y.astype(BF16)


def _add_layernorm(x, h, g, b):
    m, d = x.shape
    tm = _pick(m, (256, 64))
    row = pl.BlockSpec((tm, d), lambda i: (i, 0))
    vec = pl.BlockSpec((1, d), lambda i: (0, 0))
    return pl.pallas_call(
        _ln_kernel, grid=(m // tm,), in_specs=[row, row, vec, vec], out_specs=[row, row],
        out_shape=[jax.ShapeDtypeStruct((m, d), F32), jax.ShapeDtypeStruct((m, d), BF16)],
        compiler_params=pltpu.CompilerParams(dimension_semantics=('parallel',), vmem_limit_bytes=VMEM_LIMIT_BYTES),
        name='add_layernorm',
    )(x, h, g.reshape(1, d), b.reshape(1, d))


def _xattn_kernel(q_ref, k_ref, v_ref, o_ref):
    q = (q_ref[...] * (X_HEAD_DIM ** -0.5)).astype(BF16)
    s = _dot_nt(q, k_ref[0].astype(BF16))
    e = jnp.exp(s - jnp.max(s, -1, keepdims=True))
    p = e / jnp.sum(e, -1, keepdims=True)
    o_ref[...] = jnp.dot(p.astype(BF16), v_ref[0].astype(BF16), preferred_element_type=F32).astype(o_ref.dtype)


def _cross_attention(q, mem_kv, bsz):
    m = q.shape[0]
    l = m // bsz
    tq = _pick(l, (512, 8))
    nq = l // tq
    return pl.pallas_call(
        _xattn_kernel, grid=(bsz, X_HEADS, nq),
        in_specs=[pl.BlockSpec((tq, X_HEAD_DIM), lambda b, h, i: (b * nq + i, h)),
                  pl.BlockSpec((1, N_MEM, X_HEAD_DIM), lambda b, h, i: (b, 0, h)),
                  pl.BlockSpec((1, N_MEM, X_HEAD_DIM), lambda b, h, i: (b, 0, X_HEADS + h))],
        out_specs=pl.BlockSpec((tq, X_HEAD_DIM), lambda b, h, i: (b * nq + i, h)),
        out_shape=jax.ShapeDtypeStruct((m, D_MODEL), BF16),
        compiler_params=pltpu.CompilerParams(dimension_semantics=('parallel', 'parallel', 'parallel'),
                                             vmem_limit_bytes=VMEM_LIMIT_BYTES),
        name='cross_attention',
    )(q, mem_kv, mem_kv)


def _softplus(x):
    return jnp.maximum(x, 0.0) + jnp.log1p(jnp.exp(-jnp.abs(x)))


def _silu(x):
    return x * jax.nn.sigmoid(x)


def _pad_rows(x, rows):
    if x.shape[0] == rows:
        return x
    return jnp.concatenate([x, jnp.zeros((rows - x.shape[0],) + x.shape[1:], x.dtype)], axis=0)


def _cumsum_rows(x, seg):
    r = lax.broadcasted_iota(jnp.int32, x.shape, 0) % seg
    k = 1
    while k < seg:
        x = x + jnp.where(r >= k, pltpu.roll(x, k, 0), 0.0)
        k *= 2
    return x


def _segment_last(x, seg):
    n = x.shape[0]
    r = lax.broadcasted_iota(jnp.int32, x.shape, 0) % seg
    k = 1
    while k < seg:
        x = jnp.where(r < seg - k, pltpu.roll(x, n - k, 0), x)
        k *= 2
    return x


GLA_BLOCK = 128


def _gla_kernel(q_ref, k_ref, v_ref, g_ref, a_ref, w2_ref, gb_ref, nw_ref, s0_ref, o_ref, s_out_ref,
                s_scr, q_scr, k_scr, v_scr, bc_scr, o_scr, *, rows_in, ch):
    c = pl.program_id(1)
    nsub = -(-rows_in // ch)
    hk = GLA_DK

    @pl.when(c == 0)
    def _():
        s_scr[...] = s0_ref[0]

    valid = lax.broadcasted_iota(jnp.int32, (GLA_BLOCK, 1), 0) < rows_in
    a = _pad_rows(a_ref[...], GLA_BLOCK).astype(BF16)
    logf = -_softplus(-(jnp.dot(a, w2_ref[...], preferred_element_type=F32) + gb_ref[...])) / GLA_TAU
    logf = jnp.where(valid, logf, 0.0)
    bc = _cumsum_rows(logf, ch)
    b_last = _segment_last(bc, ch)
    q = _pad_rows(q_ref[...], GLA_BLOCK) * (GLA_DK ** -0.5)
    k = _pad_rows(k_ref[...], GLA_BLOCK)
    v = _pad_rows(v_ref[...], GLA_BLOCK)
    q_scr[...] = q
    k_scr[...] = k
    v_scr[...] = v
    bc_scr[...] = bc

    tt = lax.broadcasted_iota(jnp.int32, (ch, 1), 0)

    def intra(j, carry):
        r0 = pl.multiple_of(j * ch, ch)
        qj = q_scr[pl.ds(r0, ch), :]
        bj = bc_scr[pl.ds(r0, ch), :]
        acc = [jnp.zeros((ch, GLA_DV), F32) for _ in range(GLA_HEADS)]
        for s in range(ch):
            ks = k_scr[pl.ds(r0 + s, 1), :]
            bs = bc_scr[pl.ds(r0 + s, 1), :]
            vs = v_scr[pl.ds(r0 + s, 1), :]
            w = jnp.where(tt >= s, qj * ks * jnp.exp(jnp.minimum(bj - bs, 0.0)), 0.0)
            for h in range(GLA_HEADS):
                att = jnp.sum(w[:, h * hk:(h + 1) * hk], -1, keepdims=True)
                acc[h] = acc[h] + att * vs[:, h * GLA_DV:(h + 1) * GLA_DV]
        o_scr[pl.ds(r0, ch), :] = jnp.concatenate(acc, axis=1)
        return carry

    lax.fori_loop(0, nsub, intra, 0)

    qe = q * jnp.exp(bc)
    kst = k * jnp.exp(b_last - bc)
    dec = jnp.exp(b_last)
    for h in range(GLA_HEADS):
        kst_t = kst[:, h * hk:(h + 1) * hk].T
        dec_t = dec[:, h * hk:(h + 1) * hk].T
        s_h = s_scr[h]
        for j in range(nsub):
            rs = slice(j * ch, (j + 1) * ch)
            o_scr[rs, h * GLA_DV:(h + 1) * GLA_DV] += jnp.dot(qe[rs, h * hk:(h + 1) * hk].astype(BF16), s_h.astype(BF16),
                                                              preferred_element_type=F32)
            s_h = s_h * dec_t[:, j * ch:j * ch + 1] + jnp.dot(kst_t[:, rs].astype(BF16),
                                                              v[rs, h * GLA_DV:(h + 1) * GLA_DV].astype(BF16),
                                                              preferred_element_type=F32)
        s_scr[h] = s_h

    g = _pad_rows(g_ref[...], GLA_BLOCK)
    outs = []
    for h in range(GLA_HEADS):
        o_h = o_scr[:, h * GLA_DV:(h + 1) * GLA_DV]
        o_h = o_h * lax.rsqrt(jnp.mean(o_h * o_h, -1, keepdims=True) + NORM_EPS) * nw_ref[...]
        outs.append(o_h)
    o = jnp.concatenate(outs, axis=1) * _silu(g)
    o_ref[...] = o[:rows_in].astype(o_ref.dtype)

    @pl.when(c == pl.num_programs(1) - 1)
    def _():
        s_out_ref[0] = s_scr[...]


def _gla(y, s0, gate_w2, gate_b, norm_w, bsz):
    m = y.shape[0]
    l = m // bsz
    rows_in = min(GLA_BLOCK, l)
    ch = GLA_CHUNK
    nblk = l // rows_in
    hdk = GLA_HEADS * GLA_DK
    w2 = jnp.zeros((LANE, hdk), BF16).at[ALR_OFF - SMALL_OFF:ALR_OFF - SMALL_OFF + GLA_RANK].set(gate_w2.astype(BF16))

    def rows(width, off):
        assert off % width == 0
        return pl.BlockSpec((rows_in, width), lambda b, c: (b * nblk + c, off // width))

    def const(shape):
        return pl.BlockSpec(shape, lambda b, c: (0,) * len(shape))

    state = pl.BlockSpec((1, GLA_HEADS, GLA_DK, GLA_DV), lambda b, c: (b, 0, 0, 0))
    return pl.pallas_call(
        functools.partial(_gla_kernel, rows_in=rows_in, ch=ch),
        grid=(bsz, nblk),
        in_specs=[rows(hdk, QG_OFF), rows(hdk, KG_OFF), rows(D_GLA, VG_OFF), rows(D_GLA, GG_OFF), rows(LANE, SMALL_OFF),
                  const((LANE, hdk)), const((1, hdk)), const((1, GLA_DV)), state],
        out_specs=[pl.BlockSpec((rows_in, D_GLA), lambda b, c: (b * nblk + c, 0)), state],
        out_shape=[jax.ShapeDtypeStruct((m, D_GLA), BF16), jax.ShapeDtypeStruct((bsz, GLA_HEADS, GLA_DK, GLA_DV), F32)],
        scratch_shapes=[pltpu.VMEM((GLA_HEADS, GLA_DK, GLA_DV), F32), pltpu.VMEM((GLA_BLOCK, hdk), F32),
                        pltpu.VMEM((GLA_BLOCK, hdk), F32), pltpu.VMEM((GLA_BLOCK, D_GLA), F32),
                        pltpu.VMEM((GLA_BLOCK, hdk), F32), pltpu.VMEM((GLA_BLOCK, D_GLA), F32)],
        compiler_params=pltpu.CompilerParams(dimension_semantics=('parallel', 'arbitrary'),
                                             vmem_limit_bytes=VMEM_LIMIT_BYTES),
        name='gla',
    )(y, y, y, y, y, w2, gate_b.reshape(1, hdk), norm_w.reshape(1, GLA_DV), s0)


SSD_R = SSD_HEADS // SSD_GROUPS
SSD_GW = SSD_R * SSD_HEAD_DIM
SSD_CW = SSD_GW + 2 * SSD_STATE
SSD_TAIL = 8


def _ssd_kernel(xs_ref, bm_ref, cm_ref, z_ref, dt_ref, conv0_ref, cw_ref, cb_ref, hp_ref, dskip_ref, nw_ref, h0_ref,
                o_ref, h_out_ref, xbuf, h_scr, *, rows_in):
    g = pl.program_id(1)
    c = pl.program_id(2)
    t = SSD_CHUNK

    @pl.when(c == 0)
    def _():
        xbuf[0:SSD_TAIL, :] = conv0_ref[0, 0]
        h_scr[...] = h0_ref[0, 0]

    xbuf[SSD_TAIL:SSD_TAIL + t, 0:SSD_GW] = _pad_rows(xs_ref[...], t)
    xbuf[SSD_TAIL:SSD_TAIL + t, SSD_GW:SSD_GW + SSD_STATE] = _pad_rows(bm_ref[...], t)
    xbuf[SSD_TAIL:SSD_TAIL + t, SSD_GW + SSD_STATE:SSD_CW] = _pad_rows(cm_ref[...], t)
    conv = cb_ref[0]
    for kk in range(SSD_CONV):
        conv = conv + cw_ref[0, kk:kk + 1, :] * xbuf[SSD_TAIL - (SSD_CONV - 1) + kk:SSD_TAIL - (SSD_CONV - 1) + kk + t, :]
    xbuf[0:SSD_TAIL, :] = xbuf[t:t + SSD_TAIL, :]
    xc = _silu(conv)
    xs = xc[:, 0:SSD_GW]
    bm = xc[:, SSD_GW:SSD_GW + SSD_STATE]
    cm = xc[:, SSD_GW + SSD_STATE:SSD_CW].astype(BF16)

    lane = lax.broadcasted_iota(jnp.int32, (t, LANE), 1)
    row = lax.broadcasted_iota(jnp.int32, (t, LANE), 0)
    dt = pltpu.roll(_pad_rows(dt_ref[...], t), (LANE - SSD_R * g) % LANE, 1)
    dt = jnp.where((lane < SSD_R) & (row < rows_in), _softplus(dt + hp_ref[0, 0:1, :]), 0.0)
    acs = _cumsum_rows(dt * -jnp.exp(hp_ref[0, 1:2, :]), t)
    acs_t = acs.T
    a_last = acs[t - 1:t, :]

    head_of_lane = lax.broadcasted_iota(jnp.int32, (1, SSD_GW), 1) // SSD_HEAD_DIM

    def spread(arr):
        return sum(jnp.where(head_of_lane == r, arr[:, r:r + 1], 0.0) for r in range(SSD_R))

    xdt = xs * spread(dt)
    cb = _dot_nt(cm, bm.astype(BF16))
    tri = lax.broadcasted_iota(jnp.int32, (t, t), 0) >= lax.broadcasted_iota(jnp.int32, (t, t), 1)
    y = jnp.zeros((t, SSD_GW), F32)
    for r in range(SSD_R):
        decay = jnp.where(tri, jnp.exp(jnp.minimum(acs[:, r:r + 1] - acs_t[r:r + 1, :], 0.0)), 0.0)
        x_r = jnp.where(head_of_lane == r, xdt, 0.0).astype(BF16)
        y = y + jnp.dot((cb * decay).astype(BF16), x_r, preferred_element_type=F32)
    h_prev = h_scr[...]
    y = y + jnp.dot(cm, h_prev.astype(BF16), preferred_element_type=F32) * jnp.exp(spread(acs))
    x_st = (xdt * jnp.exp(spread(a_last - acs))).astype(BF16)
    h_new = h_prev * jnp.exp(spread(a_last)) + jnp.dot(bm.T.astype(BF16), x_st, preferred_element_type=F32)
    h_scr[...] = h_new

    y = (y + xs * dskip_ref[0]) * _silu(_pad_rows(z_ref[...], t))
    y = y * lax.rsqrt(jnp.mean(y * y, -1, keepdims=True) + NORM_EPS) * nw_ref[0]
    o_ref[...] = y[:rows_in].astype(o_ref.dtype)

    @pl.when(c == pl.num_programs(2) - 1)
    def _():
        h_out_ref[0, 0] = h_new


def _ssd(y, conv0, h0, conv_w, conv_b, dt_bias, a_log, d_skip, norm_w, bsz):
    m = y.shape[0]
    l = m // bsz
    rows_in = min(SSD_CHUNK, l)
    nblk = l // rows_in
    ng = SSD_GROUPS

    def per_group(v):
        xs = v[..., :D_SSD].reshape(v.shape[:-1] + (ng, SSD_GW))
        bm = v[..., D_SSD:D_SSD + ng * SSD_STATE].reshape(v.shape[:-1] + (ng, SSD_STATE))
        cm = v[..., D_SSD + ng * SSD_STATE:].reshape(v.shape[:-1] + (ng, SSD_STATE))
        return jnp.moveaxis(jnp.concatenate([xs, bm, cm], axis=-1), -2, 0)

    conv0_g = jnp.moveaxis(per_group(jnp.pad(conv0, ((0, 0), (SSD_TAIL - (SSD_CONV - 1), 0), (0, 0)))), 0, 1)
    cw_g = per_group(conv_w)
    cb_g = per_group(conv_b[None])
    hp = jnp.zeros((ng, 8, LANE), F32)
    hp = hp.at[:, 0, :SSD_R].set(dt_bias.reshape(ng, SSD_R)).at[:, 1, :SSD_R].set(a_log.reshape(ng, SSD_R))
    dskip_g = jnp.repeat(d_skip, SSD_HEAD_DIM).reshape(ng, 1, SSD_GW)
    nw_g = norm_w.reshape(ng, 1, SSD_GW)
    h0_t = h0.reshape(bsz, ng, SSD_GW, SSD_STATE).transpose(0, 1, 3, 2)

    def rows(width, off):
        assert off % width == 0
        return pl.BlockSpec((rows_in, width), lambda b, g, c: (b * nblk + c, off // width + g))

    def grp(shape):
        return pl.BlockSpec((1,) + shape, lambda b, g, c: (g,) + (0,) * len(shape))

    state = pl.BlockSpec((1, 1, SSD_STATE, SSD_GW), lambda b, g, c: (b, g, 0, 0))
    out, h_t = pl.pallas_call(
        functools.partial(_ssd_kernel, rows_in=rows_in),
        grid=(bsz, ng, nblk),
        in_specs=[rows(SSD_GW, XBC_OFF), rows(SSD_STATE, XBC_OFF + D_SSD), rows(SSD_STATE, XBC_OFF + D_SSD + ng * SSD_STATE),
                  rows(SSD_GW, Z_OFF), pl.BlockSpec((rows_in, LANE), lambda b, g, c: (b * nblk + c, SMALL_OFF // LANE)),
                  pl.BlockSpec((1, 1, SSD_TAIL, SSD_CW), lambda b, g, c: (b, g, 0, 0)),
                  grp((SSD_CONV, SSD_CW)), grp((1, SSD_CW)), grp((8, LANE)), grp((1, SSD_GW)), grp((1, SSD_GW)), state],
        out_specs=[pl.BlockSpec((rows_in, SSD_GW), lambda b, g, c: (b * nblk + c, g)), state],
        out_shape=[jax.ShapeDtypeStruct((m, D_SSD), BF16), jax.ShapeDtypeStruct((bsz, ng, SSD_STATE, SSD_GW), F32)],
        scratch_shapes=[pltpu.VMEM((SSD_TAIL + SSD_CHUNK, SSD_CW), F32), pltpu.VMEM((SSD_STATE, SSD_GW), F32)],
        compiler_params=pltpu.CompilerParams(dimension_semantics=('parallel', 'parallel', 'arbitrary'),
                                             vmem_limit_bytes=VMEM_LIMIT_BYTES),
        name='ssd',
    )(y, y, y, y, y, conv0_g, cw_g, cb_g, hp, dskip_g, nw_g, h0_t)
    h_new = h_t.transpose(0, 1, 3, 2).reshape(bsz, SSD_HEADS, SSD_HEAD_DIM, SSD_STATE)
    return out, h_new


def _split(x, sizes):
    return jnp.split(x, np.cumsum(sizes)[:-1].tolist(), axis=-1)


def _layernorm(x, g, b):
    mu = jnp.mean(x, -1, keepdims=True)
    var = jnp.mean(jnp.square(x - mu), -1, keepdims=True)
    return (x - mu) * lax.rsqrt(var + LN_EPS) * g + b


def _rmsnorm(x, w):
    return x * lax.rsqrt(jnp.mean(x * x, -1, keepdims=True) + NORM_EPS) * w


def _masked_softmax(s, mask):
    s = jnp.where(mask, s.astype(F32), -jnp.inf)
    m = jnp.max(s, -1, keepdims=True)
    m = jnp.where(jnp.isfinite(m), m, 0.0)
    e = jnp.exp(s - m)
    d = jnp.sum(e, -1, keepdims=True)
    return e / jnp.where(d > 0, d, 1.0)


def _t5_bucket(dist):
    n = jnp.maximum(dist, 0)
    max_exact = N_BUCKETS // 2
    nf = jnp.maximum(n, 1).astype(F32)
    large = max_exact + (jnp.log(nf / max_exact) / math.log(MAX_DISTANCE / max_exact) * (N_BUCKETS - max_exact)).astype(jnp.int32)
    large = jnp.minimum(large, N_BUCKETS - 1)
    return jnp.where(n < max_exact, n, large)


def _ssd_scan(x, dt, a, bm, cm, h0):
    b, l, h, p = x.shape
    g, n = bm.shape[2], bm.shape[3]
    r = h // g
    lc = min(SSD_CHUNK, l)
    pad = (-l) % lc
    x, dt, bm, cm = [jnp.pad(t.astype(F32), ((0, 0), (0, pad)) + ((0, 0),) * (t.ndim - 2)) for t in (x, dt, bm, cm)]
    c = (l + pad) // lc
    xdt = (x * dt[..., None]).reshape(b, c, lc, g, r, p)
    acs = jnp.cumsum((dt * a).reshape(b, c, lc, g, r), axis=2)
    bm = bm.reshape(b, c, lc, g, n)
    cm = cm.reshape(b, c, lc, g, n)
    causal = np.tril(np.ones((lc, lc), bool))[None, None, :, :, None, None]
    decay = jnp.exp(jnp.where(causal, acs[:, :, :, None] - acs[:, :, None], -jnp.inf))
    cb = jnp.einsum('bctgn,bcsgn->bctsg', cm, bm)
    y_diag = jnp.einsum('bctsg,bctsgr,bcsgrp->bctgrp', cb, decay, xdt)
    decay_st = jnp.exp(acs[:, :, -1:] - acs)
    states = jnp.einsum('bcsgn,bcsgr,bcsgrp->bcgrpn', bm, decay_st, xdt)
    chunk_decay = jnp.exp(acs[:, :, -1])

    def step(hc, inp):
        st, dec = inp
        return hc * dec[..., None, None] + st, hc

    h_last, h_prev = lax.scan(step, h0.astype(F32).reshape(b, g, r, p, n), (jnp.moveaxis(states, 1, 0), jnp.moveaxis(chunk_decay, 1, 0)))
    h_prev = jnp.moveaxis(h_prev, 0, 1)
    y_off = jnp.einsum('bctgn,bcgrpn,bctgr->bctgrp', cm, h_prev, jnp.exp(acs))
    y = (y_diag + y_off).reshape(b, c * lc, h, p)[:, :l]
    return y, h_last.reshape(b, h, p, n)


def _ssd_mixer(z, xbc, dt, conv0, h0, conv_w, conv_b, dt_bias, a_log, d_skip, norm_w):
    b, l, _ = xbc.shape
    xpad = jnp.concatenate([conv0.astype(xbc.dtype), xbc], axis=1)
    conv = sum((xpad[:, k:k + l] * conv_w[k] for k in range(SSD_CONV)), conv_b)
    conv_new = xpad[:, -(SSD_CONV - 1):]
    xbc = jax.nn.silu(conv)
    xs, bm, cm = _split(xbc, (D_SSD, SSD_GROUPS * SSD_STATE, SSD_GROUPS * SSD_STATE))
    xs = xs.reshape(b, l, SSD_HEADS, SSD_HEAD_DIM)
    bm = bm.reshape(b, l, SSD_GROUPS, SSD_STATE)
    cm = cm.reshape(b, l, SSD_GROUPS, SSD_STATE)
    dt = jax.nn.softplus((dt + dt_bias).astype(F32))
    a = -jnp.exp(a_log.astype(F32))
    y, h_new = _ssd_scan(xs, dt, a, bm, cm, h0)
    y = y + xs.astype(F32) * d_skip.astype(F32)[:, None]
    y = (y.reshape(b, l, D_SSD) * jax.nn.silu(z.astype(F32))).reshape(b, l, SSD_GROUPS, D_SSD // SSD_GROUPS)
    y = _rmsnorm(y, norm_w.reshape(SSD_GROUPS, D_SSD // SSD_GROUPS)).reshape(b, l, D_SSD)
    return y, conv_new, h_new


def _gla_scan(q, k, v, logf, s0):
    b, l, h, dk = q.shape
    dv = v.shape[-1]
    lc = min(GLA_CHUNK, l)
    pad = (-l) % lc
    q, k, v, logf = [jnp.pad(t.astype(F32), ((0, 0), (0, pad), (0, 0), (0, 0))) for t in (q, k, v, logf)]
    c = (l + pad) // lc
    q, k, logf = [t.reshape(b, c, lc, h, dk) for t in (q, k, logf)]
    v = v.reshape(b, c, lc, h, dv)
    bc = jnp.cumsum(logf, axis=2)
    causal = np.tril(np.ones((lc, lc), bool))[None, None, :, :, None, None]
    rel = jnp.exp(jnp.where(causal, bc[:, :, :, None] - bc[:, :, None], -jnp.inf))
    att = jnp.einsum('bcthd,bcshd,bctshd->bchts', q, k, rel)
    o_intra = jnp.einsum('bchts,bcshv->bcthv', att, v)
    k_st = k * jnp.exp(bc[:, :, -1:] - bc)
    chunk_states = jnp.einsum('bcshd,bcshv->bchdv', k_st, v)
    chunk_decay = jnp.exp(bc[:, :, -1])

    def step(s, inp):
        st, dec = inp
        return s * dec[..., None] + st, s

    s_last, s_prev = lax.scan(step, s0.astype(F32), (jnp.moveaxis(chunk_states, 1, 0), jnp.moveaxis(chunk_decay, 1, 0)))
    s_prev = jnp.moveaxis(s_prev, 0, 1)
    o_inter = jnp.einsum('bcthd,bchdv->bcthv', q * jnp.exp(bc), s_prev)
    o = (o_intra + o_inter).reshape(b, c * lc, h, dv)[:, :l]
    return o, s_last


def _gla_mixer(q, k, v, g, a_lr, s0, gate_w2, gate_b, norm_w):
    b, l, _ = q.shape
    q = q.reshape(b, l, GLA_HEADS, GLA_DK) * (GLA_DK ** -0.5)
    k = k.reshape(b, l, GLA_HEADS, GLA_DK)
    v = v.reshape(b, l, GLA_HEADS, GLA_DV)
    logf = jax.nn.log_sigmoid((a_lr @ gate_w2 + gate_b).astype(F32)) / GLA_TAU
    o, s_new = _gla_scan(q, k, v, logf.reshape(b, l, GLA_HEADS, GLA_DK), s0)
    o = _rmsnorm(o, norm_w).reshape(b, l, D_GLA) * jax.nn.silu(g.astype(F32))
    return o, s_new


def _nsa_attend(q, gates, kv_full, win_ctx, q_pos0, w_pos0, cmp_pe, cmp_w, rel_bias):
    b, lq = q.shape[:2]
    t_len = kv_full.shape[1]
    G, R, HD = NSA_KV_HEADS, NSA_REP, NSA_HEAD_DIM
    kc, vc, ks, vs = [kv_full[:, :, i] for i in range(4)]
    n_cmp = (t_len - CMP_LEN) // CMP_STRIDE + 1
    cidx = (np.arange(n_cmp)[:, None] * CMP_STRIDE + np.arange(CMP_LEN)[None]).astype(np.int32)
    k_cmp = jnp.einsum('bnlgd,lde->bnge', kc[:, cidx] + cmp_pe[0][:, None], cmp_w[0]).astype(F32)
    v_cmp = jnp.einsum('bnlgd,lde->bnge', vc[:, cidx] + cmp_pe[1][:, None], cmp_w[1]).astype(F32)
    cmp_end = (np.arange(n_cmp) * CMP_STRIDE + CMP_LEN - 1).astype(np.int32)
    n_sel = -(-t_len // SEL_LEN)
    sel_pad = n_sel * SEL_LEN - t_len

    def sel_blocks(t):
        t = jnp.pad(t, ((0, 0), (0, sel_pad), (0, 0), (0, 0)))
        return t.reshape(b, n_sel, SEL_LEN, G, HD).transpose(0, 3, 1, 2, 4)

    ks_b, vs_b = sel_blocks(ks), sel_blocks(vs)
    c_lo = np.arange(n_cmp) * CMP_STRIDE
    s_lo = np.arange(n_sel) * SEL_LEN
    cover = jnp.asarray(((c_lo[:, None] < s_lo[None] + SEL_LEN) & (c_lo[:, None] + CMP_LEN > s_lo[None])).astype(np.float32))
    top_n = min(TOP_N, n_sel)
    kw_pad = jnp.pad(win_ctx[:, :, 0], ((0, 0), (WINDOW, 0), (0, 0), (0, 0)))
    vw_pad = jnp.pad(win_ctx[:, :, 1], ((0, 0), (WINDOW, 0), (0, 0), (0, 0)))
    qb = Q_BLOCK if lq % Q_BLOCK == 0 else lq
    nqb = lq // qb
    scale = HD ** -0.5
    bias_gr = rel_bias.reshape(N_BUCKETS, G, R)
    bi = jnp.arange(b)[:, None, None, None]
    gi = jnp.arange(G)[None, None, :, None]
    gi5 = jnp.arange(G)[None, None, :, None, None]
    blk = jnp.arange(n_sel)

    def block(args):
        i, qblk, gblk = args
        q0 = q_pos0 + i * qb
        tpos = q0 + jnp.arange(qb)
        qg = qblk.reshape(b, qb, G, R, HD).astype(F32) * scale
        bias_c = rel_bias[_t5_bucket(tpos[:, None] - cmp_end[None])].astype(F32).reshape(qb, n_cmp, G, R).transpose(0, 2, 3, 1)
        s_c = jnp.einsum('bqgrd,bngd->bqgrn', qg, k_cmp) + bias_c
        p_c = _masked_softmax(s_c, (cmp_end[None] <= tpos[:, None])[None, :, None, None, :])
        o_c = jnp.einsum('bqgrn,bngd->bqgrd', p_c, v_cmp)
        imp = jnp.einsum('bqgrn,nj->bqgj', p_c, cover)
        cur = (tpos // SEL_LEN)[:, None]
        forced = (blk == 0) | (blk == cur) | (blk == cur - 1)
        valid = blk * SEL_LEN <= tpos[:, None]
        score = jnp.where(forced[None, :, None, :], FORCE_SCORE, jnp.where(valid[None, :, None, :], imp, -FORCE_SCORE))
        _, sel = lax.top_k(score, top_n)
        k_sel = ks_b[bi, gi, sel].astype(F32)
        v_sel = vs_b[bi, gi, sel].astype(F32)
        spos = sel[..., None] * SEL_LEN + jnp.arange(SEL_LEN)
        dist_s = tpos[None, :, None, None, None] - spos
        bias_s = jnp.moveaxis(bias_gr[_t5_bucket(dist_s), gi5].astype(F32), -1, 3)
        s_s = jnp.einsum('bqgrd,bqgksd->bqgrks', qg, k_sel) + bias_s
        p_s = _masked_softmax(s_s.reshape(b, qb, G, R, top_n * SEL_LEN), (dist_s >= 0).reshape(b, qb, G, 1, top_n * SEL_LEN)).reshape(s_s.shape)
        o_s = jnp.einsum('bqgrks,bqgksd->bqgrd', p_s, v_sel)
        start = q0 - w_pos0
        k_w = lax.dynamic_slice_in_dim(kw_pad, start, WINDOW + qb, axis=1).astype(F32)
        v_w = lax.dynamic_slice_in_dim(vw_pad, start, WINDOW + qb, axis=1).astype(F32)
        wpos = q0 - WINDOW + jnp.arange(WINDOW + qb)
        dist_w = tpos[:, None] - wpos[None]
        mask_w = (dist_w >= 0) & (dist_w < WINDOW) & (wpos[None] >= w_pos0)
        bias_w = rel_bias[_t5_bucket(dist_w)].astype(F32).reshape(qb, WINDOW + qb, G, R).transpose(0, 2, 3, 1)
        s_w = jnp.einsum('bqgrd,bkgd->bqgrk', qg, k_w) + bias_w
        p_w = _masked_softmax(s_w, mask_w[None, :, None, None, :])
        o_w = jnp.einsum('bqgrk,bkgd->bqgrd', p_w, v_w)
        gg = gblk.reshape(b, qb, G, R, 3).astype(F32)
        return gg[..., 0:1] * o_c + gg[..., 1:2] * o_s + gg[..., 2:3] * o_w

    q_blocks = jnp.swapaxes(q.reshape(b, nqb, qb, NSA_HEADS, HD), 0, 1)
    g_blocks = jnp.swapaxes(gates.reshape(b, nqb, qb, NSA_HEADS, 3), 0, 1)
    outs = lax.map(block, (jnp.arange(nqb), q_blocks, g_blocks))
    return jnp.moveaxis(outs, 0, 1).reshape(b, lq, D_NSA)


def _mixer(x, lp, rel_bias, tables, conv0, ssd_h0, gla_s0, nsa_past, win_past, past_len):
    b, l, _ = x.shape
    y = _matmul(x.reshape(b * l, D_MODEL).astype(BF16), lp['w_in'], name='in_proj')
    y3 = y.reshape(b, l, D_IN_PAD)

    def part(off, width):
        return y3[..., off:off + width]

    z, xbc, dt = part(Z_OFF, D_SSD), part(XBC_OFF, SSD_CONV_DIM), part(DT_OFF, SSD_HEADS)
    q_g, k_g = part(QG_OFF, GLA_HEADS * GLA_DK), part(KG_OFF, GLA_HEADS * GLA_DK)
    v_g, g_g, a_g = part(VG_OFF, D_GLA), part(GG_OFF, D_GLA), part(ALR_OFF, GLA_RANK)
    y_ssd, conv_new, h_new = _ssd_mixer(z, xbc, dt, conv0, ssd_h0, lp['ssd_conv_w'], lp['ssd_conv_b'], lp['ssd_dt_bias'], lp['ssd_a_log'], lp['ssd_d'], lp['ssd_norm_w'])
    rows = part(KV_OFF, 4 * KVW).reshape(b, l, 4, NSA_KV_HEADS, NSA_HEAD_DIM)
    win_rows = part(KV_OFF + 4 * KVW, 2 * KVW).reshape(b, l, 2, NSA_KV_HEADS, NSA_HEAD_DIM)
    if nsa_past is None:
        win_new = win_rows[:, -min(WINDOW, l):]
        y_nsa = _nsa_prompt(y, _nsa_compress_prompt(y, lp['nsa_cmp_pe'], lp['nsa_cmp_w'], b), tables, b).reshape(b, l, D_NSA)
    else:
        kv_full = jnp.concatenate([nsa_past.astype(rows.dtype), rows], axis=1)
        win_ctx = jnp.concatenate([win_past.astype(rows.dtype), win_rows], axis=1)
        win_new = win_ctx[:, -win_past.shape[1]:]
        g_n = jnp.concatenate([part(GATE_OFF + g * LANE, 3 * NSA_REP) for g in range(NSA_KV_HEADS)], axis=-1)
        gates = jax.nn.sigmoid(g_n).reshape(b, l, NSA_HEADS, 3)
        q_n = part(QN_OFF, D_NSA).reshape(b, l, NSA_HEADS, NSA_HEAD_DIM)
        y_nsa = _nsa_attend(q_n, gates, kv_full, win_ctx, past_len, past_len - win_past.shape[1], lp['nsa_cmp_pe'], lp['nsa_cmp_w'], rel_bias)
    y_gla, s_new = _gla_mixer(q_g, k_g, v_g, g_g, a_g, gla_s0, lp['gla_gate_w2'], lp['gla_gate_b'], lp['gla_norm_w'])
    y_out = _proj(jnp.concatenate([y_ssd, y_nsa, y_gla], axis=-1), lp['w_out'], name='out_proj')
    return y_out, rows, win_new, conv_new, h_new, s_new


def _cross_attn(x, mem_kv, wq, wo):
    b, l, _ = x.shape
    q = _proj(x, wq, name='xattn_q').reshape(b, l, X_HEADS, X_HEAD_DIM) * (X_HEAD_DIM ** -0.5)
    s = jnp.einsum('blhd,bmhd->bhlm', q, mem_kv[:, :, 0].astype(F32))
    p = jax.nn.softmax(s, axis=-1)
    o = jnp.einsum('bhlm,bmhd->blhd', p, mem_kv[:, :, 1].astype(F32)).reshape(b, l, D_MODEL)
    return _proj(o, wo, name='xattn_o')


def _sqrelu_ffn(x, w1, w2):
    h = _proj(x, w1, out_dtype=BF16, act='sqrelu', name='ffn_up')
    return _proj(h, w2, name='ffn_down')


def _layer(x, lp, rel_bias, tables, mem_kv, conv0, ssd_h0, gla_s0, nsa_past, win_past, past_len):
    h, rows, win, conv, hs, sg = _mixer(x, lp, rel_bias, tables, conv0, ssd_h0, gla_s0, nsa_past, win_past, past_len)
    x = _layernorm(DN_ALPHA * x + h, lp['ln_g'][0], lp['ln_b'][0])
    x = _layernorm(DN_ALPHA * x + _cross_attn(x, mem_kv, lp['x_wq'], lp['x_wo']), lp['ln_g'][1], lp['ln_b'][1])
    x = _layernorm(DN_ALPHA * x + _sqrelu_ffn(x, lp['ffn_w1'], lp['ffn_w2']), lp['ln_g'][2], lp['ln_b'][2])
    return x, (rows, win, conv, hs, sg)


def _layer_flat(x, xb, bsz, lp, rel_bias, tables, mem_kv, conv0, ssd_h0, gla_s0, nsa_past, win_past, past_len):
    m = x.shape[0]
    l = m // bsz
    assert l >= SSD_CONV - 1
    y = _matmul(xb, lp['w_in'], name='in_proj')
    y3 = y.reshape(bsz, l, D_IN_PAD)
    y_ssd, h_new = _ssd(y, conv0, ssd_h0, lp['ssd_conv_w'], lp['ssd_conv_b'], lp['ssd_dt_bias'], lp['ssd_a_log'],
                        lp['ssd_d'], lp['ssd_norm_w'], bsz)
    conv_new = y3[:, l - (SSD_CONV - 1):, XBC_OFF:XBC_OFF + SSD_CONV_DIM]
    y_gla, s_new = _gla(y, gla_s0, lp['gla_gate_w2'], lp['gla_gate_b'], lp['gla_norm_w'], bsz)
    rows = y3[..., KV_OFF:KV_OFF + 4 * KVW].reshape(bsz, l, 4, NSA_KV_HEADS, NSA_HEAD_DIM)
    win_rows = y3[..., KV_OFF + 4 * KVW:KV_OFF + 6 * KVW].reshape(bsz, l, 2, NSA_KV_HEADS, NSA_HEAD_DIM)
    if nsa_past is None:
        win_new = win_rows[:, -min(WINDOW, l):]
        y_nsa = _nsa_prompt_t(y, _nsa_compress_prompt(y, lp['nsa_cmp_pe'], lp['nsa_cmp_w'], bsz), tables, bsz)
    else:
        cache_kv, layer, page_table = nsa_past
        win_new = jnp.concatenate([win_past, win_rows], axis=1)[:, -win_past.shape[1]:]
        y_nsa = _nsa_sample(y, cache_kv, layer, page_table, win_past, lp['nsa_cmp_pe'], lp['nsa_cmp_w'], rel_bias, bsz)
    h = _matmul(jnp.concatenate([y_ssd, y_nsa, y_gla], axis=-1), lp['w_out'], name='out_proj')
    x, xb = _add_layernorm(x, h, lp['ln_g'][0], lp['ln_b'][0])
    o = _cross_attention(_matmul(xb, lp['x_wq'], name='xattn_q'), mem_kv, bsz)
    x, xb = _add_layernorm(x, _matmul(o, lp['x_wo'], name='xattn_o'), lp['ln_g'][1], lp['ln_b'][1])
    hidden = _matmul(xb, lp['ffn_w1'], out_dtype=BF16, act='sqrelu', name='ffn_up')
    x, xb = _add_layernorm(x, _matmul(hidden, lp['ffn_w2'], name='ffn_down'), lp['ln_g'][2], lp['ln_b'][2])
    return x, xb, (rows, win_new, conv_new, h_new, s_new)


def kernel(x_prompt, x_sample, cache_nsa_kv, cache_nsa_win, state_ssd, state_ssd_conv, state_gla, cache_mem_kv,
           page_table, mem_prompt, w_in, ssd_conv_w, ssd_conv_b, ssd_dt_bias, ssd_a_log, ssd_d, ssd_norm_w,
           nsa_cmp_pe, nsa_cmp_w, rel_bias, gla_gate_w2, gla_gate_b, gla_norm_w, w_out, x_wq, x_wkv, x_wo,
           ffn_w1, ffn_w2, ln_g, ln_b):
    bp = x_prompt.shape[0]
    bs = x_sample.shape[0]
    past_len = page_table.shape[1] * cache_nsa_kv.shape[2]
    xp, xs = x_prompt.reshape(-1, D_MODEL), x_sample.reshape(-1, D_MODEL)
    xpb, xsb = xp.astype(BF16), xs.astype(BF16)
    st_p, st_s, mem_p = [], [], []
    tables = _nsa_prompt_bias_tables_t(rel_bias)
    for l in range(DEPTH):
        lp = dict(w_in=_reorder_w_in(w_in[l]), ssd_conv_w=ssd_conv_w[l], ssd_conv_b=ssd_conv_b[l],
                  ssd_dt_bias=ssd_dt_bias[l], ssd_a_log=ssd_a_log[l], ssd_d=ssd_d[l], ssd_norm_w=ssd_norm_w[l],
                  nsa_cmp_pe=nsa_cmp_pe[l], nsa_cmp_w=nsa_cmp_w[l], gla_gate_w2=gla_gate_w2[l], gla_gate_b=gla_gate_b[l],
                  gla_norm_w=gla_norm_w[l], w_out=w_out[l].astype(BF16), x_wq=x_wq[l].astype(BF16),
                  x_wo=x_wo[l].astype(BF16), ffn_w1=ffn_w1[l].astype(BF16), ffn_w2=ffn_w2[l].astype(BF16),
                  ln_g=ln_g[l], ln_b=ln_b[l])
        mem_kv_p = _proj(mem_prompt, x_wkv[l].astype(BF16), name='mem_kv')
        conv0 = jnp.zeros((bp, SSD_CONV - 1, SSD_CONV_DIM), F32)
        h0 = jnp.zeros((bp, SSD_HEADS, SSD_HEAD_DIM, SSD_STATE), F32)
        s0 = jnp.zeros((bp, GLA_HEADS, GLA_DK, GLA_DV), F32)
        xp, xpb, stp = _layer_flat(xp, xpb, bp, lp, rel_bias, tables, mem_kv_p, conv0, h0, s0, None, None, 0)
        st_p.append(stp)
        mem_p.append(mem_kv_p.reshape(bp, N_MEM, 2, X_HEADS, X_HEAD_DIM))
        xs, xsb, sts = _layer_flat(xs, xsb, bs, lp, rel_bias, tables, cache_mem_kv[l].reshape(bs, N_MEM, 2 * D_MODEL),
                                   state_ssd_conv[l], state_ssd[l], state_gla[l], (cache_nsa_kv, l, page_table),
                                   cache_nsa_win[l], past_len)
        st_s.append(sts)
    p_rows, p_win, p_conv, p_ssd, p_gla = [jnp.stack(s) for s in zip(*st_p)]
    s_rows, s_win, s_conv, s_ssd, s_gla = [jnp.stack(s) for s in zip(*st_s)]
    p_mem = jnp.stack(mem_p)
    return (xp.reshape(x_prompt.shape), xs.reshape(x_sample.shape), p_rows, s_rows, p_win, s_win, p_ssd, s_ssd,
            p_conv, s_conv, p_gla, s_gla, p_mem)
```

```python
import functools
import math

import jax
import jax.numpy as jnp
import numpy as np
from jax import lax
from jax.experimental import pallas as pl
from jax.experimental.pallas import tpu as pltpu

F32 = jnp.float32
BF16 = jnp.bfloat16

D_MODEL = 4096
DEPTH = 2
PAGE_SIZE = 128
D_MIX = D_MODEL
D_SSD = D_MIX // 2
SSD_HEAD_DIM = 64
SSD_HEADS = D_SSD // SSD_HEAD_DIM
SSD_GROUPS = 8
SSD_STATE = 128
SSD_CONV = 4
SSD_CHUNK = 128
SSD_CONV_DIM = D_SSD + 2 * SSD_GROUPS * SSD_STATE
D_NSA = D_MIX // 4
NSA_HEAD_DIM = 128
NSA_HEADS = D_NSA // NSA_HEAD_DIM
NSA_KV_HEADS = 2
NSA_REP = NSA_HEADS // NSA_KV_HEADS
CMP_LEN = 32
CMP_STRIDE = 16
SEL_LEN = 64
TOP_N = 16
WINDOW = 512
Q_BLOCK = 128
FORCE_SCORE = 1e4
D_GLA = D_MIX - D_SSD - D_NSA
GLA_HEADS = 4
GLA_DV = D_GLA // GLA_HEADS
GLA_DK = GLA_DV // 2
GLA_RANK = 16
GLA_TAU = 16.0
GLA_CHUNK = 16
N_MEM = 256
X_HEADS = 4
X_HEAD_DIM = D_MODEL // X_HEADS
D_FF = 4 * D_MODEL
N_BUCKETS = 32
MAX_DISTANCE = 128
LN_EPS = 1e-5
NORM_EPS = 1e-6
DN_ALPHA = (2 * DEPTH) ** 0.25
IN_SPLITS = (D_SSD, SSD_CONV_DIM, SSD_HEADS, NSA_HEADS * NSA_HEAD_DIM) + (NSA_KV_HEADS * NSA_HEAD_DIM,) * 6 + (
    3 * NSA_HEADS, GLA_HEADS * GLA_DK, GLA_HEADS * GLA_DK, D_GLA, D_GLA, GLA_RANK)
D_IN = sum(IN_SPLITS)

LANE = 128
VMEM_LIMIT_BYTES = 56 * 1024 * 1024

KVW = NSA_KV_HEADS * NSA_HEAD_DIM
Z_OFF = 0
XBC_OFF = Z_OFF + D_SSD
QN_OFF = XBC_OFF + SSD_CONV_DIM
VG_OFF = QN_OFF + NSA_HEADS * NSA_HEAD_DIM
GG_OFF = VG_OFF + D_GLA
KV_OFF = GG_OFF + D_GLA
QG_OFF = KV_OFF + 6 * KVW
KG_OFF = QG_OFF + GLA_HEADS * GLA_DK
SMALL_OFF = KG_OFF + GLA_HEADS * GLA_DK
DT_OFF = SMALL_OFF
ALR_OFF = SMALL_OFF + SSD_HEADS
GATE_OFF = SMALL_OFF + LANE
D_IN_PAD = 12288
NEG = -1e30


def _in_proj_column_map():
    off = np.cumsum((0,) + IN_SPLITS)
    z, xbc, dt, qn = off[0], off[1], off[2], off[3]
    kv0, gn, qg, kg, vg, gg, alr = off[4], off[10], off[11], off[12], off[13], off[14], off[15]
    cmap = np.full((D_IN_PAD,), -1, np.int64)
    cmap[Z_OFF:Z_OFF + D_SSD] = z + np.arange(D_SSD)
    cmap[XBC_OFF:XBC_OFF + SSD_CONV_DIM] = xbc + np.arange(SSD_CONV_DIM)
    cmap[QN_OFF:QN_OFF + D_NSA] = qn + np.arange(D_NSA)
    cmap[KV_OFF:KV_OFF + 6 * KVW] = kv0 + np.arange(6 * KVW)
    cmap[QG_OFF:QG_OFF + 512] = qg + np.arange(512)
    cmap[KG_OFF:KG_OFF + 512] = kg + np.arange(512)
    cmap[VG_OFF:VG_OFF + D_GLA] = vg + np.arange(D_GLA)
    cmap[GG_OFF:GG_OFF + D_GLA] = gg + np.arange(D_GLA)
    cmap[DT_OFF:DT_OFF + SSD_HEADS] = dt + np.arange(SSD_HEADS)
    cmap[ALR_OFF:ALR_OFF + GLA_RANK] = alr + np.arange(GLA_RANK)
    for g in range(NSA_KV_HEADS):
        cmap[GATE_OFF + g * LANE:GATE_OFF + g * LANE + 3 * NSA_REP] = gn + g * 3 * NSA_REP + np.arange(3 * NSA_REP)
    return cmap


def _reorder_w_in(w):
    cmap = _in_proj_column_map()
    cuts = [0] + [i for i in range(1, D_IN_PAD) if cmap[i] != cmap[i - 1] + 1 and not (cmap[i] == -1 and cmap[i - 1] == -1)] + [D_IN_PAD]
    pieces = []
    for a, b in zip(cuts[:-1], cuts[1:]):
        pieces.append(jnp.zeros((w.shape[0], b - a), BF16) if cmap[a] < 0 else w[:, cmap[a]:cmap[a] + b - a].astype(BF16))
    return jnp.concatenate(pieces, axis=1)


def _dot_nt(a, b):
    return lax.dot_general(a, b, (((1,), (1,)), ((), ())), preferred_element_type=F32)


NSA_L = 2048
NSA_QB = Q_BLOCK
NSA_NQB = NSA_L // NSA_QB
NSA_NCP = NSA_L // CMP_STRIDE
NSA_NSEL = NSA_L // SEL_LEN
CMP_HALF = CMP_STRIDE * NSA_HEAD_DIM
NSA_KSTEP = 512


def _t5_bucket_np(dist):
    n = np.maximum(dist, 0)
    max_exact = N_BUCKETS // 2
    nf = np.maximum(n, 1).astype(np.float32)
    large = max_exact + (np.log(nf / max_exact) / np.float32(math.log(MAX_DISTANCE / max_exact)) * (N_BUCKETS - max_exact)).astype(np.int32)
    return np.where(n < max_exact, n, np.minimum(large, N_BUCKETS - 1)).astype(np.int32)


def _nsa_prompt_bias_tables(rel_bias):
    tq = np.arange(NSA_QB)[:, None]
    ts = np.arange(NSA_QB)[None, :]

    def table(idx):
        t = rel_bias[jnp.asarray(idx)]
        t = jnp.moveaxis(t, -1, 0).reshape((NSA_KV_HEADS, NSA_REP) + idx.shape)
        t = jnp.moveaxis(t, 1, -3)
        return t.reshape(t.shape[:-3] + (NSA_REP * NSA_QB, idx.shape[-1]))

    b_diag = table(_t5_bucket_np(tq - ts))
    b_prev = table(_t5_bucket_np(NSA_QB + tq - ts))
    b_far = table(np.full((NSA_QB, NSA_QB), N_BUCKETS - 1, np.int32))
    i = np.arange(NSA_NQB)[:, None, None]
    n = np.arange(NSA_NCP)[None, None, :]
    b_cmp = table(_t5_bucket_np(i * NSA_QB + tq[None] - (n * CMP_STRIDE + CMP_LEN - 1)))
    return b_diag, b_prev, b_far, b_cmp


def _nsa_compress_kernel(a_ref, pe_ref, w_ref, o_ref):
    a = a_ref[0, 0, 0]
    lo = jnp.dot((a + pe_ref[0, 0]).astype(BF16), w_ref[0, 0], preferred_element_type=F32)
    hi = jnp.dot((a + pe_ref[0, 1]).astype(BF16), w_ref[0, 1], preferred_element_type=F32)
    o_ref[0, 0, 0] = lo + pltpu.roll(hi, NSA_NCP - 1, 0)


def _nsa_compress_prompt(y, cmp_pe, cmp_w, bsz):
    a = y[:, KV_OFF:KV_OFF + 2 * KVW].reshape(bsz, NSA_NCP, CMP_STRIDE, 2, NSA_KV_HEADS, NSA_HEAD_DIM)
    a = a.transpose(0, 3, 4, 1, 2, 5).reshape(bsz, 2, NSA_KV_HEADS, NSA_NCP, CMP_HALF)
    pe = cmp_pe.reshape(2, 2, 1, CMP_HALF)
    w = cmp_w.astype(BF16).reshape(2, 2, CMP_HALF, NSA_HEAD_DIM)
    return pl.pallas_call(
        _nsa_compress_kernel,
        grid=(bsz, 2, NSA_KV_HEADS),
        in_specs=[pl.BlockSpec((1, 1, 1, NSA_NCP, CMP_HALF), lambda b, c, g: (b, c, g, 0, 0)),
                  pl.BlockSpec((1, 2, 1, CMP_HALF), lambda b, c, g: (c, 0, 0, 0)),
                  pl.BlockSpec((1, 2, CMP_HALF, NSA_HEAD_DIM), lambda b, c, g: (c, 0, 0, 0))],
        out_specs=pl.BlockSpec((1, 1, 1, NSA_NCP, NSA_HEAD_DIM), lambda b, c, g: (b, c, g, 0, 0)),
        out_shape=jax.ShapeDtypeStruct((bsz, 2, NSA_KV_HEADS, NSA_NCP, NSA_HEAD_DIM), F32),
        compiler_params=pltpu.CompilerParams(dimension_semantics=('parallel', 'parallel', 'parallel')),
        name='nsa_compress',
    )(a, pe, w)


def _nsa_prompt_kernel(q_ref, gate_ref, cmp_ref, ks_ref, vs_ref, kw_ref, vw_ref, bdiag_ref, bprev_ref, bfar_ref,
                       bcmp_ref, covt_ref, expand_ref, o_ref, m_ref, l_ref, acc_ref):
    i = pl.program_id(2)
    rows = NSA_REP * NSA_QB
    qblk = q_ref[...] * (NSA_HEAD_DIM ** -0.5)
    q = jnp.concatenate([qblk[:, r * LANE:(r + 1) * LANE] for r in range(NSA_REP)], axis=0).astype(BF16)
    tq = lax.broadcasted_iota(jnp.int32, (rows, LANE), 0) % NSA_QB
    lane = lax.broadcasted_iota(jnp.int32, (rows, LANE), 1)
    t_abs = i * NSA_QB + tq

    k_c = cmp_ref[0, 0, 0].astype(BF16)
    v_c = cmp_ref[0, 1, 0].astype(BF16)
    mask_c = lane * CMP_STRIDE + (CMP_LEN - 1) <= t_abs
    s_c = jnp.where(mask_c, _dot_nt(q, k_c) + bcmp_ref[0, 0], NEG)
    e_c = jnp.where(mask_c, jnp.exp(s_c - jnp.max(s_c, -1, keepdims=True)), 0.0)
    d_c = jnp.sum(e_c, -1, keepdims=True)
    p_c = (e_c / jnp.where(d_c > 0, d_c, 1.0)).astype(BF16)
    o_c = jnp.dot(p_c, v_c, preferred_element_type=F32)

    imp4 = _dot_nt(covt_ref[...], p_c)
    imp = sum(imp4[:, r * NSA_QB:(r + 1) * NSA_QB] for r in range(NSA_REP))
    blk = lax.broadcasted_iota(jnp.int32, (LANE, NSA_QB), 0)
    t_row = i * NSA_QB + lax.broadcasted_iota(jnp.int32, (LANE, NSA_QB), 1)
    cur = t_row // SEL_LEN
    forced = (blk == 0) | (blk == cur) | (blk == cur - 1)
    score = jnp.where(forced, FORCE_SCORE, jnp.where(blk * SEL_LEN <= t_row, imp, -FORCE_SCORE))
    score = jnp.where(blk < NSA_NSEL, score, -3.0 * FORCE_SCORE)
    rank = jnp.zeros((LANE, NSA_QB), jnp.int32)
    for j in range(NSA_NSEL):
        row = score[j:j + 1, :]
        rank = rank + jnp.where((row > score) | ((row == score) & (blk > j)), 1, 0)
    sel_t = jnp.where((rank < TOP_N) & (blk < NSA_NSEL), 1.0, 0.0)
    sel = sel_t.T.astype(BF16)

    def init():
        m_ref[...] = jnp.full((rows, 1), NEG, F32)
        l_ref[...] = jnp.zeros((rows, 1), F32)
        acc_ref[...] = jnp.zeros((rows, NSA_HEAD_DIM), F32)

    def step(k_ref, v_ref, c, bias, mask):
        start = pl.multiple_of(c * NSA_QB, NSA_QB)
        k = k_ref[pl.ds(start, NSA_QB), :].astype(BF16)
        v = v_ref[pl.ds(start, NSA_QB), :].astype(BF16)
        s = _dot_nt(q, k) + bias
        if mask is not None:
            s = jnp.where(mask, s, NEG)
        m_old = m_ref[...]
        m_new = jnp.maximum(m_old, jnp.max(s, -1, keepdims=True))
        p = jnp.exp(s - m_new)
        if mask is not None:
            p = jnp.where(mask, p, 0.0)
        alpha = jnp.exp(m_old - m_new)
        l_ref[...] = alpha * l_ref[...] + jnp.sum(p, -1, keepdims=True)
        acc_ref[...] = alpha * acc_ref[...] + jnp.dot(p.astype(BF16), v, preferred_element_type=F32)
        m_ref[...] = m_new

    def finish():
        l = l_ref[...]
        return acc_ref[...] / jnp.where(l > 0, l, 1.0)

    def sel_mask(c):
        m1 = jnp.dot(sel, expand_ref[c], preferred_element_type=F32)
        return jnp.concatenate([m1] * NSA_REP, axis=0) > 0.5

    init()
    bfar = bfar_ref[0]

    def far_body(c, carry):
        step(ks_ref, vs_ref, c, bfar, sel_mask(c))
        return carry

    lax.fori_loop(0, jnp.maximum(i - 1, 0), far_body, 0)

    @pl.when(i >= 1)
    def _():
        step(ks_ref, vs_ref, i - 1, bprev_ref[0], sel_mask(i - 1))

    causal = lane <= tq
    step(ks_ref, vs_ref, i, bdiag_ref[0], sel_mask(i) & causal)
    o_s = finish()

    init()
    step(kw_ref, vw_ref, i, bdiag_ref[0], causal)

    @pl.when(i >= 1)
    def _():
        step(kw_ref, vw_ref, i - 1, bprev_ref[0], None)

    for d in range(2, WINDOW // NSA_QB):
        @pl.when(i >= d)
        def _(d=d):
            step(kw_ref, vw_ref, i - d, bfar, None)

    @pl.when(i >= WINDOW // NSA_QB)
    def _():
        step(kw_ref, vw_ref, i - WINDOW // NSA_QB, bfar, lane > tq)

    o_w = finish()

    gates = jax.nn.sigmoid(gate_ref[...])
    for r in range(NSA_REP):
        rs = slice(r * NSA_QB, (r + 1) * NSA_QB)
        o_ref[:, r * LANE:(r + 1) * LANE] = (gates[:, 3 * r:3 * r + 1] * o_c[rs] + gates[:, 3 * r + 1:3 * r + 2] * o_s[rs]
                                             + gates[:, 3 * r + 2:3 * r + 3] * o_w[rs]).astype(o_ref.dtype)


def _nsa_prompt(y, k_v_cmp, tables, bsz):
    b_diag, b_prev, b_far, b_cmp = tables
    rows = NSA_REP * NSA_QB
    c_lo = np.arange(NSA_NCP) * CMP_STRIDE
    s_lo = np.arange(LANE) * SEL_LEN
    cov_t = ((c_lo[None] < s_lo[:, None] + SEL_LEN) & (c_lo[None] + CMP_LEN > s_lo[:, None])
             & (np.arange(NSA_NCP)[None] < NSA_NCP - 1) & (np.arange(LANE)[:, None] < NSA_NSEL))
    key_blk = (np.arange(NSA_L) // SEL_LEN).reshape(NSA_NQB, 1, NSA_QB)
    expand = (np.arange(LANE)[None, :, None] == key_blk)
    kv_blk = KV_OFF // LANE
    grp = NSA_KV_HEADS

    def col(slot):
        return pl.BlockSpec((NSA_L, NSA_HEAD_DIM), lambda b, g, i: (b, kv_blk + slot * grp + g))

    tab = pl.BlockSpec((1, rows, LANE), lambda b, g, i: (g, 0, 0))
    return pl.pallas_call(
        _nsa_prompt_kernel,
        grid=(bsz, NSA_KV_HEADS, NSA_NQB),
        in_specs=[pl.BlockSpec((NSA_QB, NSA_REP * LANE), lambda b, g, i: (b * NSA_NQB + i, QN_OFF // (NSA_REP * LANE) + g)),
                  pl.BlockSpec((NSA_QB, LANE), lambda b, g, i: (b * NSA_NQB + i, GATE_OFF // LANE + g)),
                  pl.BlockSpec((1, 2, 1, NSA_NCP, NSA_HEAD_DIM), lambda b, g, i: (b, 0, g, 0, 0)),
                  col(2), col(3), col(4), col(5), tab, tab, tab,
                  pl.BlockSpec((1, 1, rows, NSA_NCP), lambda b, g, i: (g, i, 0, 0)),
                  pl.BlockSpec((LANE, NSA_NCP), lambda b, g, i: (0, 0)),
                  pl.BlockSpec((NSA_NQB, LANE, NSA_QB), lambda b, g, i: (0, 0, 0))],
        out_specs=pl.BlockSpec((NSA_QB, NSA_REP * LANE), lambda b, g, i: (b * NSA_NQB + i, g)),
        out_shape=jax.ShapeDtypeStruct((bsz * NSA_L, D_NSA), BF16),
        scratch_shapes=[pltpu.VMEM((rows, 1), F32), pltpu.VMEM((rows, 1), F32), pltpu.VMEM((rows, NSA_HEAD_DIM), F32)],
        compiler_params=pltpu.CompilerParams(dimension_semantics=('parallel', 'parallel', 'arbitrary'),
                                             vmem_limit_bytes=VMEM_LIMIT_BYTES),
        name='nsa_prompt',
    )(y, y, k_v_cmp, y, y, y, y, b_diag, b_prev, b_far, b_cmp,
      jnp.asarray(cov_t, BF16), jnp.asarray(expand, BF16))


def _bias_lookup(rel_bias, idx):
    idx = jnp.asarray(idx)[None]
    per_head = rel_bias.T.reshape((NSA_HEADS, N_BUCKETS) + (1,) * (idx.ndim - 1))
    return sum(jnp.where(idx == k, per_head[:, k], 0.0) for k in range(N_BUCKETS))


def _nsa_prompt_bias_tables_t(rel_bias):
    tq = np.arange(NSA_QB)[None, :]
    ts = np.arange(NSA_QB)[:, None]

    def table(idx):
        t = _bias_lookup(rel_bias, idx).reshape((NSA_KV_HEADS, NSA_REP) + idx.shape)
        t = jnp.moveaxis(t, 1, -2)
        return t.reshape(t.shape[:-2] + (NSA_REP * NSA_QB,))

    b_diag = table(_t5_bucket_np(tq - ts))
    b_prev = table(_t5_bucket_np(NSA_QB + tq - ts))
    b_far = table(np.full((8, NSA_QB), N_BUCKETS - 1, np.int32))
    i = np.arange(NSA_NQB)[:, None, None]
    n = np.arange(NSA_NCP)[None, :, None]
    b_cmp = table(_t5_bucket_np(i * NSA_QB + tq[None] - (n * CMP_STRIDE + CMP_LEN - 1)))
    return b_diag, b_prev, b_far, b_cmp


def _nsa_prompt_t_kernel(q_ref, gate_ref, cmp_ref, ks_ref, vs_ref, kw_ref, vw_ref, bdiag_ref, bprev_ref, bfar_ref,
                         bcmp_ref, covt_ref, expand_ref, o_ref, m_ref, l_ref, acc_ref, ksb, vst, kwb, vwt):
    i = pl.program_id(2)
    cols = NSA_REP * NSA_QB

    @pl.when(i == 0)
    def _():
        ksb[...] = ks_ref[...].astype(BF16)
        kwb[...] = kw_ref[...].astype(BF16)
        for c in range(NSA_L // NSA_KSTEP):
            vst[c] = vs_ref[c * NSA_KSTEP:(c + 1) * NSA_KSTEP, :].T.astype(BF16)
            vwt[c] = vw_ref[c * NSA_KSTEP:(c + 1) * NSA_KSTEP, :].T.astype(BF16)

    qblk = q_ref[...] * (NSA_HEAD_DIM ** -0.5)
    q_t = jnp.concatenate([qblk[:, r * LANE:(r + 1) * LANE].T for r in range(NSA_REP)], axis=1).astype(BF16)
    key = lax.broadcasted_iota(jnp.int32, (NSA_QB, cols), 0)
    tq = lax.broadcasted_iota(jnp.int32, (NSA_QB, cols), 1) % NSA_QB
    t_abs = i * NSA_QB + tq

    k_c = cmp_ref[0, 0, 0].astype(BF16)
    v_c_t = cmp_ref[0, 1, 0].T.astype(BF16)
    mask_c = key * CMP_STRIDE + (CMP_LEN - 1) <= t_abs
    s_c = jnp.where(mask_c, jnp.dot(k_c, q_t, preferred_element_type=F32) + bcmp_ref[0, 0], NEG)
    e_c = jnp.where(mask_c, jnp.exp(s_c - jnp.max(s_c, 0, keepdims=True)), 0.0)
    d_c = jnp.sum(e_c, 0, keepdims=True)
    p_c = (e_c / jnp.where(d_c > 0, d_c, 1.0)).astype(BF16)
    o_c = jnp.dot(v_c_t, p_c, preferred_element_type=F32)

    imp4 = jnp.dot(covt_ref[...], p_c, preferred_element_type=F32)
    imp = sum(imp4[:, r * NSA_QB:(r + 1) * NSA_QB] for r in range(NSA_REP))
    blk = lax.broadcasted_iota(jnp.int32, (LANE, NSA_QB), 0)
    t_row = i * NSA_QB + lax.broadcasted_iota(jnp.int32, (LANE, NSA_QB), 1)
    cur = t_row // SEL_LEN
    forced = (blk == 0) | (blk == cur) | (blk == cur - 1)
    score = jnp.where(forced, FORCE_SCORE, jnp.where(blk * SEL_LEN <= t_row, imp, -FORCE_SCORE))
    score = jnp.where(blk < NSA_NSEL, score, -3.0 * FORCE_SCORE)
    rank = jnp.zeros((LANE, NSA_QB), jnp.int32)
    for j in range(NSA_NSEL):
        row = score[j:j + 1, :]
        rank = rank + jnp.where((row > score) | ((row == score) & (blk > j)), 1, 0)
    sel_t = jnp.where((rank < TOP_N) & (blk < NSA_NSEL), 1.0, 0.0).astype(BF16)

    def init():
        m_ref[...] = jnp.full((1, cols), NEG, F32)
        l_ref[...] = jnp.zeros((1, cols), F32)
        acc_ref[...] = jnp.zeros((NSA_HEAD_DIM, cols), F32)

    dist0 = (lax.broadcasted_iota(jnp.int32, (NSA_KSTEP, cols), 1) % NSA_QB
             - lax.broadcasted_iota(jnp.int32, (NSA_KSTEP, cols), 0))
    per_step = NSA_KSTEP // NSA_QB
    bfar = bfar_ref[0, 0:1, :]

    def step(k_scr, v_scr, j, selected):
        start = pl.multiple_of(j * NSA_KSTEP, NSA_KSTEP)
        s = jnp.dot(k_scr[pl.ds(start, NSA_KSTEP), :], q_t, preferred_element_type=F32)
        bias = []
        for u in range(per_step):
            d = i - (j * per_step + u)
            bias.append(jnp.where(d == 0, bdiag_ref[0], jnp.where(d == 1, bprev_ref[0], bfar)))
        dist = dist0 + (i * NSA_QB - j * NSA_KSTEP)
        if selected:
            m1 = jnp.dot(expand_ref[j], sel_t, preferred_element_type=F32)
            mask = (jnp.concatenate([m1] * NSA_REP, axis=1) > 0.5) & (dist >= 0)
        else:
            mask = (dist >= 0) & (dist < WINDOW)
        s = jnp.where(mask, s + jnp.concatenate(bias, axis=0), NEG)
        m_old = m_ref[...]
        m_new = jnp.maximum(m_old, jnp.max(s, 0, keepdims=True))
        p = jnp.where(mask, jnp.exp(s - m_new), 0.0)
        alpha = jnp.exp(m_old - m_new)
        l_ref[...] = alpha * l_ref[...] + jnp.sum(p, 0, keepdims=True)
        acc_ref[...] = alpha * acc_ref[...] + jnp.dot(v_scr[j], p.astype(BF16), preferred_element_type=F32)
        m_ref[...] = m_new

    def finish():
        l = l_ref[...]
        return acc_ref[...] / jnp.where(l > 0, l, 1.0)

    j_diag = i // per_step

    init()

    def sel_body(j, carry):
        step(ksb, vst, j, True)
        return carry

    lax.fori_loop(0, j_diag + 1, sel_body, 0)
    o_s = finish()

    init()
    step(kwb, vwt, j_diag, False)

    @pl.when(j_diag >= 1)
    def _():
        step(kwb, vwt, j_diag - 1, False)

    o_w = finish()

    g_t = jax.nn.sigmoid(gate_ref[...]).T
    for r in range(NSA_REP):
        cs = slice(r * NSA_QB, (r + 1) * NSA_QB)
        o_r = g_t[3 * r:3 * r + 1, :] * o_c[:, cs] + g_t[3 * r + 1:3 * r + 2, :] * o_s[:, cs] + g_t[3 * r + 2:3 * r + 3, :] * o_w[:, cs]
        o_ref[:, r * LANE:(r + 1) * LANE] = o_r.T.astype(o_ref.dtype)


def _nsa_prompt_t(y, k_v_cmp, tables, bsz):
    b_diag, b_prev, b_far, b_cmp = tables
    cols = NSA_REP * NSA_QB
    nstep = NSA_L // NSA_KSTEP
    assert WINDOW <= NSA_KSTEP and NSA_KSTEP % NSA_QB == 0
    c_lo = np.arange(NSA_NCP) * CMP_STRIDE
    s_lo = np.arange(LANE) * SEL_LEN
    cov_t = ((c_lo[None] < s_lo[:, None] + SEL_LEN) & (c_lo[None] + CMP_LEN > s_lo[:, None])
             & (np.arange(NSA_NCP)[None] < NSA_NCP - 1) & (np.arange(LANE)[:, None] < NSA_NSEL))
    key_blk = (np.arange(NSA_L) // SEL_LEN).reshape(nstep, NSA_KSTEP, 1)
    expand = (np.arange(LANE)[None, None, :] == key_blk)
    kv_blk = KV_OFF // LANE
    grp = NSA_KV_HEADS

    def col(slot):
        return pl.BlockSpec((NSA_L, NSA_HEAD_DIM), lambda b, g, i: (b, kv_blk + slot * grp + g))

    tab = pl.BlockSpec((1, NSA_QB, cols), lambda b, g, i: (g, 0, 0))
    return pl.pallas_call(
        _nsa_prompt_t_kernel,
        grid=(bsz, NSA_KV_HEADS, NSA_NQB),
        in_specs=[pl.BlockSpec((NSA_QB, NSA_REP * LANE), lambda b, g, i: (b * NSA_NQB + i, QN_OFF // (NSA_REP * LANE) + g)),
                  pl.BlockSpec((NSA_QB, LANE), lambda b, g, i: (b * NSA_NQB + i, GATE_OFF // LANE + g)),
                  pl.BlockSpec((1, 2, 1, NSA_NCP, NSA_HEAD_DIM), lambda b, g, i: (b, 0, g, 0, 0)),
                  col(2), col(3), col(4), col(5), tab, tab,
                  pl.BlockSpec((1, 8, cols), lambda b, g, i: (g, 0, 0)),
                  pl.BlockSpec((1, 1, NSA_NCP, cols), lambda b, g, i: (g, i, 0, 0)),
                  pl.BlockSpec((LANE, NSA_NCP), lambda b, g, i: (0, 0)),
                  pl.BlockSpec((nstep, NSA_KSTEP, LANE), lambda b, g, i: (0, 0, 0))],
        out_specs=pl.BlockSpec((NSA_QB, NSA_REP * LANE), lambda b, g, i: (b * NSA_NQB + i, g)),
        out_shape=jax.ShapeDtypeStruct((bsz * NSA_L, D_NSA), BF16),
        scratch_shapes=[pltpu.VMEM((1, cols), F32), pltpu.VMEM((1, cols), F32), pltpu.VMEM((NSA_HEAD_DIM, cols), F32),
                        pltpu.VMEM((NSA_L, NSA_HEAD_DIM), BF16), pltpu.VMEM((nstep, NSA_HEAD_DIM, NSA_KSTEP), BF16),
                        pltpu.VMEM((NSA_L, NSA_HEAD_DIM), BF16), pltpu.VMEM((nstep, NSA_HEAD_DIM, NSA_KSTEP), BF16)],
        compiler_params=pltpu.CompilerParams(dimension_semantics=('parallel', 'parallel', 'arbitrary'),
                                             vmem_limit_bytes=VMEM_LIMIT_BYTES),
        name='nsa_prompt',
    )(y, y, k_v_cmp, y, y, y, y, b_diag, b_prev, b_far, b_cmp,
      jnp.asarray(cov_t, BF16), jnp.asarray(expand, BF16))


NS_SELW = 3 * LANE
NS_CHUNK = 2048


def _nsa_sample_tables(rel_bias, past_len, lq, win_len):
    tq = (past_len + np.arange(lq))[:, None]

    def table(idx):
        t = _bias_lookup(rel_bias, idx)
        return t.reshape(NSA_KV_HEADS, NSA_REP * lq, idx.shape[-1])

    ncp = past_len // CMP_STRIDE
    b_cmp = table(_t5_bucket_np(tq - (np.arange(ncp)[None] * CMP_STRIDE + CMP_LEN - 1)))
    b_last = table(_t5_bucket_np(tq - (past_len - LANE + np.arange(LANE)[None])))
    b_new = table(_t5_bucket_np(tq - (past_len + np.arange(LANE)[None])))
    b_far = table(np.full((lq, LANE), N_BUCKETS - 1, np.int32))
    wpos = np.concatenate([past_len - win_len + np.arange(win_len), past_len + np.arange(LANE)])
    b_win = table(_t5_bucket_np(tq - wpos[None]))
    return b_cmp, b_last, b_new, b_far, b_win


def _nsa_sample_kernel(pt_ref, q_ref, kvn_ref, gate0_ref, gate1_ref, win_ref, cache_ref, w_ref, pe_ref,
                       bcmp_ref, blast_ref, bnew_ref, bfar_ref, bwin_ref, cov_ref, expand_ref, o_ref,
                       buf, cmp_scr, s_scr, sem, *, n_pages, lq, past_len, win_len, layer):
    b = pl.program_id(0)
    rows = NSA_REP * lq
    ncp = past_len // CMP_STRIDE
    n_cmp = (past_len + lq - CMP_LEN) // CMP_STRIDE + 1
    n_sel = -(-(past_len + lq) // SEL_LEN)
    n_chunks = past_len // NS_CHUNK
    ncol = 2 * NSA_KV_HEADS

    def page_copy(j, pair, col):
        slot, g = pair * 2 + col // NSA_KV_HEADS, col % NSA_KV_HEADS
        return pltpu.make_async_copy(cache_ref.at[layer, pt_ref[b, j], :, slot, g, :],
                                     buf.at[col, pl.ds(j * PAGE_SIZE, PAGE_SIZE), :], sem)

    def gather_start(pair):
        def body(j, carry):
            for col in range(ncol):
                page_copy(j, pair, col).start()
            return carry
        lax.fori_loop(0, n_pages, body, 0)

    def gather_wait(pair):
        def body(j, carry):
            for col in range(ncol):
                page_copy(j, pair, col).wait()
            return carry
        lax.fori_loop(0, n_pages, body, 0)

    gather_start(0)
    gather_wait(0)
    for c in range(2):
        for g in range(NSA_KV_HEADS):
            lo = jnp.zeros((ncp, NSA_HEAD_DIM), F32)
            hi = jnp.zeros((ncp, NSA_HEAD_DIM), F32)
            for l in range(CMP_STRIDE):
                x = buf[c * NSA_KV_HEADS + g, pl.ds(l, ncp, stride=CMP_STRIDE), :]
                lo = lo + jnp.dot((x + pe_ref[c, l:l + 1, :]).astype(BF16), w_ref[c, l], preferred_element_type=F32)
                hi = hi + jnp.dot((x + pe_ref[c, CMP_STRIDE + l:CMP_STRIDE + l + 1, :]).astype(BF16),
                                  w_ref[c, CMP_STRIDE + l], preferred_element_type=F32)
            cmp_scr[c, g] = lo + pltpu.roll(hi, ncp - 1, 0)
    gather_start(1)

    qall = q_ref[...] * (NSA_HEAD_DIM ** -0.5)
    tq = lax.broadcasted_iota(jnp.int32, (rows, 1), 0) % lq
    t_abs = past_len + tq
    o_c, sel, q_g = [], [], []
    for g in range(NSA_KV_HEADS):
        q = jnp.concatenate([qall[:, (g * NSA_REP + r) * LANE:(g * NSA_REP + r + 1) * LANE] for r in range(NSA_REP)],
                            axis=0).astype(BF16)
        q_g.append(q)
        n_idx = lax.broadcasted_iota(jnp.int32, (rows, ncp), 1)
        mask_c = (n_idx < n_cmp) & (n_idx * CMP_STRIDE + (CMP_LEN - 1) <= t_abs)
        s_c = jnp.where(mask_c, _dot_nt(q, cmp_scr[0, g].astype(BF16)) + bcmp_ref[g], NEG)
        e_c = jnp.where(mask_c, jnp.exp(s_c - jnp.max(s_c, -1, keepdims=True)), 0.0)
        d_c = jnp.sum(e_c, -1, keepdims=True)
        p_c = (e_c / jnp.where(d_c > 0, d_c, 1.0)).astype(BF16)
        o_c.append(jnp.dot(p_c, cmp_scr[1, g].astype(BF16), preferred_element_type=F32))
        imp4 = jnp.dot(p_c, cov_ref[...], preferred_element_type=F32)
        imp = sum(imp4[r * lq:(r + 1) * lq] for r in range(NSA_REP))
        blk = lax.broadcasted_iota(jnp.int32, (lq, NS_SELW), 1)
        t_q = past_len + lax.broadcasted_iota(jnp.int32, (lq, NS_SELW), 0)
        cur = t_q // SEL_LEN
        forced = (blk == 0) | (blk == cur) | (blk == cur - 1)
        score = jnp.where(forced, FORCE_SCORE, jnp.where(blk * SEL_LEN <= t_q, imp, -FORCE_SCORE))
        score = jnp.where(blk < n_sel, score, -3.0 * FORCE_SCORE)
        rank = jnp.zeros((lq, NS_SELW), jnp.int32)
        for j in range(n_sel):
            col = score[:, j:j + 1]
            rank = rank + jnp.where((col > score) | ((col == score) & (blk > j)), 1, 0)
        sel_q = jnp.where((rank < min(TOP_N, n_sel)) & (blk < n_sel), 1.0, 0.0)
        sel.append(jnp.concatenate([sel_q] * NSA_REP, axis=0))

    gather_wait(1)
    kvn = _pad_rows(kvn_ref[...], LANE)
    lane = lax.broadcasted_iota(jnp.int32, (rows, LANE), 1)
    blocks_per_chunk = NS_CHUNK // SEL_LEN
    gates = [jax.nn.sigmoid(gate0_ref[...]), jax.nn.sigmoid(gate1_ref[...])]
    for g in range(NSA_KV_HEADS):
        q = q_g[g]
        bfar = bfar_ref[g][:, 0:1]

        def new_rows(slot):
            return kvn[:, (slot * NSA_KV_HEADS + g) * NSA_HEAD_DIM:(slot * NSA_KV_HEADS + g + 1) * NSA_HEAD_DIM].astype(BF16)

        sel_b = sel[g].astype(BF16)
        for c in range(n_chunks):
            k = buf[g, c * NS_CHUNK:(c + 1) * NS_CHUNK, :].astype(BF16)
            s = _dot_nt(q, k) + bfar
            if c == n_chunks - 1:
                fix = blast_ref[g] - bfar_ref[g]
                s = jnp.concatenate([s[:, :NS_CHUNK - LANE], s[:, NS_CHUNK - LANE:] + fix], axis=1)
            m_c = jnp.dot(sel_b[:, c * blocks_per_chunk:(c + 1) * blocks_per_chunk], expand_ref[...],
                          preferred_element_type=F32)
            s_scr[:, c * NS_CHUNK:(c + 1) * NS_CHUNK] = jnp.where(m_c > 0.5, s, NEG)
        s_new = _dot_nt(q, new_rows(2)) + bnew_ref[g]
        mask_new = (lane <= tq) & (lane < lq) & (sel[g][:, n_sel - 1:n_sel] > 0.5)
        s_scr[:, past_len:past_len + LANE] = jnp.where(mask_new, s_new, NEG)
        m = jnp.max(s_scr[...], -1, keepdims=True)
        acc = jnp.zeros((rows, NSA_HEAD_DIM), F32)
        den = jnp.zeros((rows, 1), F32)
        for c in range(n_chunks):
            sc = s_scr[:, c * NS_CHUNK:(c + 1) * NS_CHUNK]
            e = jnp.where(sc > 0.5 * NEG, jnp.exp(sc - m), 0.0)
            den = den + jnp.sum(e, -1, keepdims=True)
            v = buf[NSA_KV_HEADS + g, c * NS_CHUNK:(c + 1) * NS_CHUNK, :]
            acc = acc + jnp.dot(e.astype(BF16), v.astype(BF16), preferred_element_type=F32)
        sc = s_scr[:, past_len:past_len + LANE]
        e = jnp.where(sc > 0.5 * NEG, jnp.exp(sc - m), 0.0)
        den = den + jnp.sum(e, -1, keepdims=True)
        acc = acc + jnp.dot(e.astype(BF16), new_rows(3), preferred_element_type=F32)
        o_s = acc / jnp.where(den > 0, den, 1.0)

        kw = jnp.concatenate([win_ref[0, :, g * NSA_HEAD_DIM:(g + 1) * NSA_HEAD_DIM].astype(BF16), new_rows(4)], axis=0)
        vw = jnp.concatenate([win_ref[0, :, (NSA_KV_HEADS + g) * NSA_HEAD_DIM:(NSA_KV_HEADS + g + 1) * NSA_HEAD_DIM].astype(BF16),
                              new_rows(5)], axis=0)
        wl = lax.broadcasted_iota(jnp.int32, (rows, win_len + LANE), 1)
        dist = jnp.where(wl < win_len, win_len + tq - wl, tq - (wl - win_len))
        mask_w = (dist >= 0) & (dist < WINDOW) & (wl < win_len + lq)
        s_w = jnp.where(mask_w, _dot_nt(q, kw) + bwin_ref[g], NEG)
        e_w = jnp.where(mask_w, jnp.exp(s_w - jnp.max(s_w, -1, keepdims=True)), 0.0)
        d_w = jnp.sum(e_w, -1, keepdims=True)
        o_w = jnp.dot((e_w / jnp.where(d_w > 0, d_w, 1.0)).astype(BF16), vw, preferred_element_type=F32)

        for r in range(NSA_REP):
            rs = slice(r * lq, (r + 1) * lq)
            gt = gates[g]
            h = g * NSA_REP + r
            o_ref[:, h * LANE:(h + 1) * LANE] = (gt[:, 3 * r:3 * r + 1] * o_c[g][rs] + gt[:, 3 * r + 1:3 * r + 2] * o_s[rs]
                                                 + gt[:, 3 * r + 2:3 * r + 3] * o_w[rs]).astype(o_ref.dtype)


def _nsa_sample(y, cache_kv, layer, page_table, cache_win, cmp_pe, cmp_w, rel_bias, bsz):
    lq = y.shape[0] // bsz
    n_pages = page_table.shape[1]
    past_len = n_pages * PAGE_SIZE
    win_len = cache_win.shape[1]
    rows = NSA_REP * lq
    ncp = past_len // CMP_STRIDE
    n_cmp = (past_len + lq - CMP_LEN) // CMP_STRIDE + 1
    n_sel = -(-(past_len + lq) // SEL_LEN)
    assert lq % 8 == 0 and lq <= CMP_STRIDE and n_cmp <= ncp - 1 + lq // CMP_STRIDE and past_len % NS_CHUNK == 0
    assert past_len % SEL_LEN == 0 and n_sel <= NS_SELW and lq % Q_BLOCK != 0 and win_len == WINDOW
    tables = _nsa_sample_tables(rel_bias, past_len, lq, win_len)
    c_lo = np.arange(ncp) * CMP_STRIDE
    s_lo = np.arange(NS_SELW) * SEL_LEN
    cover = ((c_lo[:, None] < s_lo[None] + SEL_LEN) & (c_lo[:, None] + CMP_LEN > s_lo[None])
             & (np.arange(ncp)[:, None] < n_cmp) & (np.arange(NS_SELW)[None] < n_sel))
    expand = np.arange(NS_CHUNK // SEL_LEN)[:, None] == (np.arange(NS_CHUNK) // SEL_LEN)[None]
    assert cache_kv.shape[2:] == (PAGE_SIZE, 4, NSA_KV_HEADS, NSA_HEAD_DIM)
    win2 =cache_win.reshape(bsz, win_len, 2 * KVW)

    def full(shape):
        return pl.BlockSpec(shape, lambda b, pt: (0,) * len(shape))

    grid_spec = pltpu.PrefetchScalarGridSpec(
        num_scalar_prefetch=1, grid=(bsz,),
        in_specs=[pl.BlockSpec((lq, D_NSA), lambda b, pt: (b, QN_OFF // D_NSA)),
                  pl.BlockSpec((lq, 6 * KVW), lambda b, pt: (b, KV_OFF // (6 * KVW))),
                  pl.BlockSpec((lq, LANE), lambda b, pt: (b, GATE_OFF // LANE)),
                  pl.BlockSpec((lq, LANE), lambda b, pt: (b, GATE_OFF // LANE + 1)),
                  pl.BlockSpec((1, win_len, 2 * KVW), lambda b, pt: (b, 0, 0)),
                  pl.BlockSpec(memory_space=pl.ANY),
                  full((2, CMP_LEN, NSA_HEAD_DIM, NSA_HEAD_DIM)), full((2, CMP_LEN, NSA_HEAD_DIM)),
                  full((NSA_KV_HEADS, rows, ncp)), full((NSA_KV_HEADS, rows, LANE)), full((NSA_KV_HEADS, rows, LANE)),
                  full((NSA_KV_HEADS, rows, LANE)), full((NSA_KV_HEADS, rows, win_len + LANE)),
                  full((ncp, NS_SELW)), full((NS_CHUNK // SEL_LEN, NS_CHUNK))],
        out_specs=pl.BlockSpec((lq, D_NSA), lambda b, pt: (b, 0)),
        scratch_shapes=[pltpu.VMEM((2 * NSA_KV_HEADS, past_len, NSA_HEAD_DIM), F32),
                        pltpu.VMEM((2, NSA_KV_HEADS, ncp, NSA_HEAD_DIM), F32),
                        pltpu.VMEM((rows, past_len + LANE), F32), pltpu.SemaphoreType.DMA(())])
    assert QN_OFF % D_NSA == 0 and KV_OFF % (6 * KVW) == 0
    return pl.pallas_call(
        functools.partial(_nsa_sample_kernel, n_pages=n_pages, lq=lq, past_len=past_len, win_len=win_len,
                          layer=layer),
        grid_spec=grid_spec,
        out_shape=jax.ShapeDtypeStruct((bsz * lq, D_NSA), BF16),
        compiler_params=pltpu.CompilerParams(dimension_semantics=('arbitrary',), vmem_limit_bytes=VMEM_LIMIT_BYTES),
        name='nsa_sample',
    )(page_table, y, y, y, y, win2, cache_kv, cmp_w.astype(BF16), cmp_pe, *tables,
      jnp.asarray(cover, BF16), jnp.asarray(expand, BF16))


def _mm_kernel(x_ref, w_ref, o_ref, *scratch, nk, act):
    def finish(acc):
        if act == 'sqrelu':
            acc = jnp.square(jnp.maximum(acc, 0.0))
        o_ref[...] = acc.astype(o_ref.dtype)

    if nk == 1:
        finish(jnp.dot(x_ref[...], w_ref[...], preferred_element_type=F32))
        return
    acc_ref, = scratch
    k = pl.program_id(2)
    part = jnp.dot(x_ref[...], w_ref[...], preferred_element_type=F32)

    @pl.when(k == 0)
    def _():
        acc_ref[...] = part

    @pl.when(k > 0)
    def _():
        acc_ref[...] += part

    @pl.when(k == nk - 1)
    def _():
        finish(acc_ref[...])


def _pick(n, pref):
    for t in pref:
        if n % t == 0:
            return t
    return n


def _matmul(x, w, out_dtype=F32, act=None, name='matmul'):
    m, k = x.shape
    n = w.shape[1]
    tm = _pick(m, (1024, 512, 256, 128, 64))
    tn = _pick(n, (512, 256, 128))
    tk = _pick(k, (4096, 2048, 1024, 512))
    nk = k // tk
    scratch = [pltpu.VMEM((tm, tn), F32)] if nk > 1 else []
    return pl.pallas_call(
        functools.partial(_mm_kernel, nk=nk, act=act),
        grid=(m // tm, n // tn, nk),
        in_specs=[pl.BlockSpec((tm, tk), lambda i, j, kk: (i, kk)),
                  pl.BlockSpec((tk, tn), lambda i, j, kk: (kk, j))],
        out_specs=pl.BlockSpec((tm, tn), lambda i, j, kk: (i, j)),
        out_shape=jax.ShapeDtypeStruct((m, n), out_dtype),
        scratch_shapes=scratch,
        compiler_params=pltpu.CompilerParams(
            dimension_semantics=('parallel', 'parallel', 'arbitrary'),
            vmem_limit_bytes=VMEM_LIMIT_BYTES),
        name=name,
    )(x, w)


def _out_proj_kernel(a_ref, b_ref, c_ref, wa_ref, wb_ref, wc_ref, o_ref):
    o_ref[...] = (jnp.dot(a_ref[...], wa_ref[...], preferred_element_type=F32)
                  + jnp.dot(b_ref[...], wb_ref[...], preferred_element_type=F32)
                  + jnp.dot(c_ref[...], wc_ref[...], preferred_element_type=F32))


def _out_proj(y_ssd, y_nsa, y_gla, w):
    m = y_ssd.shape[0]
    n = w.shape[1]
    tm = _pick(m, (1024, 64))
    tn = _pick(n, (512,))
    assert D_SSD % D_NSA == 0 and D_NSA == D_GLA

    def act(width):
        return pl.BlockSpec((tm, width), lambda i, j: (i, 0))

    return pl.pallas_call(
        _out_proj_kernel, grid=(m // tm, n // tn),
        in_specs=[act(D_SSD), act(D_NSA), act(D_GLA),
                  pl.BlockSpec((D_SSD, tn), lambda i, j: (0, j)),
                  pl.BlockSpec((D_NSA, tn), lambda i, j: (D_SSD // D_NSA, j)),
                  pl.BlockSpec((D_GLA, tn), lambda i, j: (D_SSD // D_NSA + 1, j))],
        out_specs=pl.BlockSpec((tm, tn), lambda i, j: (i, j)),
        out_shape=jax.ShapeDtypeStruct((m, n), F32),
        compiler_params=pltpu.CompilerParams(dimension_semantics=('parallel', 'parallel'),
                                             vmem_limit_bytes=VMEM_LIMIT_BYTES),
        name='out_proj',
    )(y_ssd, y_nsa, y_gla, w, w, w)


def _proj(x, w_bf16, **kw):
    lead = x.shape[:-1]
    y = _matmul(x.reshape(-1, x.shape[-1]).astype(BF16), w_bf16, **kw)
    return y.reshape(lead + (w_bf16.shape[1],))


def _ln_kernel(x_ref, h_ref, g_ref, b_ref, o_ref, ob_ref):
    v = DN_ALPHA * x_ref[...] + h_ref[...]
    d = v - jnp.mean(v, -1, keepdims=True)
    y = d * lax.rsqrt(jnp.mean(d * d, -1, keepdims=True) + LN_EPS) * g_ref[...] + b_ref[...]
    o_ref[...] = y
    ob_ref[...] = y.astype(BF16)


def _add_layernorm(x, h, g, b):
    m, d = x.shape
    tm = _pick(m, (256, 64))
    row = pl.BlockSpec((tm, d), lambda i: (i, 0))
    vec = pl.BlockSpec((1, d), lambda i: (0, 0))
    return pl.pallas_call(
        _ln_kernel, grid=(m // tm,), in_specs=[row, row, vec, vec], out_specs=[row, row],
        out_shape=[jax.ShapeDtypeStruct((m, d), F32), jax.ShapeDtypeStruct((m, d), BF16)],
        compiler_params=pltpu.CompilerParams(dimension_semantics=('parallel',), vmem_limit_bytes=VMEM_LIMIT_BYTES),
        name='add_layernorm',
    )(x, h, g.reshape(1, d), b.reshape(1, d))


def _xattn_kernel(q_ref, k_ref, v_ref, o_ref):
    q = (q_ref[...] * (X_HEAD_DIM ** -0.5)).astype(BF16)
    s = _dot_nt(q, k_ref[0].astype(BF16))
    e = jnp.exp(s - jnp.max(s, -1, keepdims=True))
    p = e / jnp.sum(e, -1, keepdims=True)
    o_ref[...] = jnp.dot(p.astype(BF16), v_ref[0].astype(BF16), preferred_element_type=F32).astype(o_ref.dtype)


def _cross_attention(q, mem_kv, bsz):
    m = q.shape[0]
    l = m // bsz
    tq = _pick(l, (512, 8))
    nq = l // tq
    return pl.pallas_call(
        _xattn_kernel, grid=(bsz, X_HEADS, nq),
        in_specs=[pl.BlockSpec((tq, X_HEAD_DIM), lambda b, h, i: (b * nq + i, h)),
                  pl.BlockSpec((1, N_MEM, X_HEAD_DIM), lambda b, h, i: (b, 0, h)),
                  pl.BlockSpec((1, N_MEM, X_HEAD_DIM), lambda b, h, i: (b, 0, X_HEADS + h))],
        out_specs=pl.BlockSpec((tq, X_HEAD_DIM), lambda b, h, i: (b * nq + i, h)),
        out_shape=jax.ShapeDtypeStruct((m, D_MODEL), BF16),
        compiler_params=pltpu.CompilerParams(dimension_semantics=('parallel', 'parallel', 'parallel'),
                                             vmem_limit_bytes=VMEM_LIMIT_BYTES),
        name='cross_attention',
    )(q, mem_kv, mem_kv)


def _softplus(x):
    return jnp.maximum(x, 0.0) + jnp.log1p(jnp.exp(-jnp.abs(x)))


def _silu(x):
    return x * jax.nn.sigmoid(x)


def _pad_rows(x, rows):
    if x.shape[0] == rows:
        return x
    return jnp.concatenate([x, jnp.zeros((rows - x.shape[0],) + x.shape[1:], x.dtype)], axis=0)


def _cumsum_rows(x, seg):
    r = lax.broadcasted_iota(jnp.int32, x.shape, 0) % seg
    k = 1
    while k < seg:
        x = x + jnp.where(r >= k, pltpu.roll(x, k, 0), 0.0)
        k *= 2
    return x


def _segment_last(x, seg):
    n = x.shape[0]
    r = lax.broadcasted_iota(jnp.int32, x.shape, 0) % seg
    k = 1
    while k < seg:
        x = jnp.where(r < seg - k, pltpu.roll(x, n - k, 0), x)
        k *= 2
    return x


GLA_BLOCK = 128


def _gla_kernel(q_ref, k_ref, v_ref, g_ref, a_ref, w2_ref, gb_ref, nw_ref, s0_ref, o_ref, s_out_ref,
                s_scr, q_scr, k_scr, v_scr, bc_scr, o_scr, *, rows_in, ch):
    c = pl.program_id(1)
    nsub = -(-rows_in // ch)
    hk = GLA_DK

    @pl.when(c == 0)
    def _():
        s_scr[...] = s0_ref[0]

    valid = lax.broadcasted_iota(jnp.int32, (GLA_BLOCK, 1), 0) < rows_in
    a = _pad_rows(a_ref[...], GLA_BLOCK).astype(BF16)
    logf = -_softplus(-(jnp.dot(a, w2_ref[...], preferred_element_type=F32) + gb_ref[...])) / GLA_TAU
    logf = jnp.where(valid, logf, 0.0)
    bc = _cumsum_rows(logf, ch)
    b_last = _segment_last(bc, ch)
    q = _pad_rows(q_ref[...], GLA_BLOCK) * (GLA_DK ** -0.5)
    k = _pad_rows(k_ref[...], GLA_BLOCK)
    v = _pad_rows(v_ref[...], GLA_BLOCK)
    q_scr[...] = q
    k_scr[...] = k
    v_scr[...] = v
    bc_scr[...] = bc

    tt = lax.broadcasted_iota(jnp.int32, (ch, 1), 0)

    def intra(j, carry):
        r0 = pl.multiple_of(j * ch, ch)
        qj = q_scr[pl.ds(r0, ch), :]
        bj = bc_scr[pl.ds(r0, ch), :]
        acc = [jnp.zeros((ch, GLA_DV), F32) for _ in range(GLA_HEADS)]
        for s in range(ch):
            ks = k_scr[pl.ds(r0 + s, 1), :]
            bs = bc_scr[pl.ds(r0 + s, 1), :]
            vs = v_scr[pl.ds(r0 + s, 1), :]
            w = jnp.where(tt >= s, qj * ks * jnp.exp(jnp.minimum(bj - bs, 0.0)), 0.0)
            for h in range(GLA_HEADS):
                att = jnp.sum(w[:, h * hk:(h + 1) * hk], -1, keepdims=True)
                acc[h] = acc[h] + att * vs[:, h * GLA_DV:(h + 1) * GLA_DV]
        o_scr[pl.ds(r0, ch), :] = jnp.concatenate(acc, axis=1)
        return carry

    lax.fori_loop(0, nsub, intra, 0)

    qe = q * jnp.exp(bc)
    kst = k * jnp.exp(b_last - bc)
    dec = jnp.exp(b_last)
    o_intra = _pad_rows(o_scr[0:nsub * ch, :], GLA_BLOCK)
    chunk_of_col = lax.broadcasted_iota(jnp.int32, (hk, GLA_BLOCK), 1) // ch
    g = _pad_rows(g_ref[...], GLA_BLOCK)
    outs = []
    for h in range(GLA_HEADS):
        kst_t = kst[:, h * hk:(h + 1) * hk].T
        dec_t = dec[:, h * hk:(h + 1) * hk].T
        lhs = jnp.concatenate([jnp.where(chunk_of_col == j, kst_t, 0.0) for j in range(nsub)], axis=0).astype(BF16)
        kv = jnp.dot(lhs, v[:, h * GLA_DV:(h + 1) * GLA_DV].astype(BF16), preferred_element_type=F32)
        s_h = s_scr[h]
        s_before = []
        for j in range(nsub):
            s_before.append(s_h.astype(BF16))
            s_h = s_h * dec_t[:, j * ch:j * ch + 1] + kv[j * hk:(j + 1) * hk]
        s_scr[h] = s_h
        big = jnp.dot(qe[:, h * hk:(h + 1) * hk].astype(BF16), jnp.concatenate(s_before, axis=1), preferred_element_type=F32)
        o_inter = _pad_rows(jnp.concatenate([big[j * ch:(j + 1) * ch, j * GLA_DV:(j + 1) * GLA_DV] for j in range(nsub)], axis=0),
                            GLA_BLOCK)
        o_h = o_intra[:, h * GLA_DV:(h + 1) * GLA_DV] + o_inter
        o_h = o_h * lax.rsqrt(jnp.mean(o_h * o_h, -1, keepdims=True) + NORM_EPS) * nw_ref[...]
        outs.append(o_h)
    o = jnp.concatenate(outs, axis=1) * _silu(g)
    o_ref[...] = o[:rows_in].astype(o_ref.dtype)

    @pl.when(c == pl.num_programs(1) - 1)
    def _():
        s_out_ref[0] = s_scr[...]


def _gla(y, s0, gate_w2, gate_b, norm_w, bsz):
    m = y.shape[0]
    l = m // bsz
    rows_in = min(GLA_BLOCK, l)
    ch = GLA_CHUNK
    nblk = l // rows_in
    hdk = GLA_HEADS * GLA_DK
    w2 = jnp.zeros((LANE, hdk), BF16).at[ALR_OFF - SMALL_OFF:ALR_OFF - SMALL_OFF + GLA_RANK].set(gate_w2.astype(BF16))

    def rows(width, off):
        assert off % width == 0
        return pl.BlockSpec((rows_in, width), lambda b, c: (b * nblk + c, off // width))

    def const(shape):
        return pl.BlockSpec(shape, lambda b, c: (0,) * len(shape))

    state = pl.BlockSpec((1, GLA_HEADS, GLA_DK, GLA_DV), lambda b, c: (b, 0, 0, 0))
    return pl.pallas_call(
        functools.partial(_gla_kernel, rows_in=rows_in, ch=ch),
        grid=(bsz, nblk),
        in_specs=[rows(hdk, QG_OFF), rows(hdk, KG_OFF), rows(D_GLA, VG_OFF), rows(D_GLA, GG_OFF), rows(LANE, SMALL_OFF),
                  const((LANE, hdk)), const((1, hdk)), const((1, GLA_DV)), state],
        out_specs=[pl.BlockSpec((rows_in, D_GLA), lambda b, c: (b * nblk + c, 0)), state],
        out_shape=[jax.ShapeDtypeStruct((m, D_GLA), BF16), jax.ShapeDtypeStruct((bsz, GLA_HEADS, GLA_DK, GLA_DV), F32)],
        scratch_shapes=[pltpu.VMEM((GLA_HEADS, GLA_DK, GLA_DV), F32), pltpu.VMEM((GLA_BLOCK, hdk), F32),
                        pltpu.VMEM((GLA_BLOCK, hdk), F32), pltpu.VMEM((GLA_BLOCK, D_GLA), F32),
                        pltpu.VMEM((GLA_BLOCK, hdk), F32), pltpu.VMEM((GLA_BLOCK, D_GLA), F32)],
        compiler_params=pltpu.CompilerParams(dimension_semantics=('parallel', 'arbitrary'),
                                             vmem_limit_bytes=VMEM_LIMIT_BYTES),
        name='gla',
    )(y, y, y, y, y, w2, gate_b.reshape(1, hdk), norm_w.reshape(1, GLA_DV), s0)


SSD_R = SSD_HEADS // SSD_GROUPS
SSD_GW = SSD_R * SSD_HEAD_DIM
SSD_CW = SSD_GW + 2 * SSD_STATE
SSD_TAIL = 8


def _ssd_kernel(xs_ref, bm_ref, cm_ref, z_ref, dt_ref, conv0_ref, cw_ref, cb_ref, hp_ref, dskip_ref, nw_ref, h0_ref,
                o_ref, h_out_ref, xbuf, h_scr, *, rows_in):
    g = pl.program_id(1)
    c = pl.program_id(2)
    t = SSD_CHUNK

    @pl.when(c == 0)
    def _():
        xbuf[0:SSD_TAIL, :] = conv0_ref[0, 0]
        h_scr[...] = h0_ref[0, 0]

    xbuf[SSD_TAIL:SSD_TAIL + t, 0:SSD_GW] = _pad_rows(xs_ref[...], t)
    xbuf[SSD_TAIL:SSD_TAIL + t, SSD_GW:SSD_GW + SSD_STATE] = _pad_rows(bm_ref[...], t)
    xbuf[SSD_TAIL:SSD_TAIL + t, SSD_GW + SSD_STATE:SSD_CW] = _pad_rows(cm_ref[...], t)
    conv = cb_ref[0]
    for kk in range(SSD_CONV):
        conv = conv + cw_ref[0, kk:kk + 1, :] * xbuf[SSD_TAIL - (SSD_CONV - 1) + kk:SSD_TAIL - (SSD_CONV - 1) + kk + t, :]
    xbuf[0:SSD_TAIL, :] = xbuf[t:t + SSD_TAIL, :]
    xc = _silu(conv)
    xs = xc[:, 0:SSD_GW]
    bm = xc[:, SSD_GW:SSD_GW + SSD_STATE]
    cm = xc[:, SSD_GW + SSD_STATE:SSD_CW].astype(BF16)

    lane = lax.broadcasted_iota(jnp.int32, (t, LANE), 1)
    row = lax.broadcasted_iota(jnp.int32, (t, LANE), 0)
    dt = pltpu.roll(_pad_rows(dt_ref[...], t), (LANE - SSD_R * g) % LANE, 1)
    dt = jnp.where((lane < SSD_R) & (row < rows_in), _softplus(dt + hp_ref[0, 0:1, :]), 0.0)
    acs = _cumsum_rows(dt * -jnp.exp(hp_ref[0, 1:2, :]), t)
    acs_t = acs.T
    a_last = acs[t - 1:t, :]

    head_of_lane = lax.broadcasted_iota(jnp.int32, (1, SSD_GW), 1) // SSD_HEAD_DIM

    def spread(arr):
        return sum(jnp.where(head_of_lane == r, arr[:, r:r + 1], 0.0) for r in range(SSD_R))

    xdt = xs * spread(dt)
    cb = _dot_nt(cm, bm.astype(BF16))
    tri = lax.broadcasted_iota(jnp.int32, (t, t), 0) >= lax.broadcasted_iota(jnp.int32, (t, t), 1)
    y = jnp.zeros((t, SSD_GW), F32)
    for r in range(SSD_R):
        decay = jnp.where(tri, jnp.exp(jnp.minimum(acs[:, r:r + 1] - acs_t[r:r + 1, :], 0.0)), 0.0)
        x_r = jnp.where(head_of_lane == r, xdt, 0.0).astype(BF16)
        y = y + jnp.dot((cb * decay).astype(BF16), x_r, preferred_element_type=F32)
    h_prev = h_scr[...]
    y = y + jnp.dot(cm, h_prev.astype(BF16), preferred_element_type=F32) * jnp.exp(spread(acs))
    x_st = (xdt * jnp.exp(spread(a_last - acs))).astype(BF16)
    h_new = h_prev * jnp.exp(spread(a_last)) + jnp.dot(bm.T.astype(BF16), x_st, preferred_element_type=F32)
    h_scr[...] = h_new

    y = (y + xs * dskip_ref[0]) * _silu(_pad_rows(z_ref[...], t))
    y = y * lax.rsqrt(jnp.mean(y * y, -1, keepdims=True) + NORM_EPS) * nw_ref[0]
    o_ref[...] = y[:rows_in].astype(o_ref.dtype)

    @pl.when(c == pl.num_programs(2) - 1)
    def _():
        h_out_ref[0, 0] = h_new


def _ssd(y, conv0, h0, conv_w, conv_b, dt_bias, a_log, d_skip, norm_w, bsz):
    m = y.shape[0]
    l = m // bsz
    rows_in = min(SSD_CHUNK, l)
    nblk = l // rows_in
    ng = SSD_GROUPS

    def per_group(v):
        xs = v[..., :D_SSD].reshape(v.shape[:-1] + (ng, SSD_GW))
        bm = v[..., D_SSD:D_SSD + ng * SSD_STATE].reshape(v.shape[:-1] + (ng, SSD_STATE))
        cm = v[..., D_SSD + ng * SSD_STATE:].reshape(v.shape[:-1] + (ng, SSD_STATE))
        return jnp.moveaxis(jnp.concatenate([xs, bm, cm], axis=-1), -2, 0)

    conv0_g = jnp.moveaxis(per_group(jnp.pad(conv0, ((0, 0), (SSD_TAIL - (SSD_CONV - 1), 0), (0, 0)))), 0, 1)
    cw_g = per_group(conv_w)
    cb_g = per_group(conv_b[None])
    hp = jnp.zeros((ng, 8, LANE), F32)
    hp = hp.at[:, 0, :SSD_R].set(dt_bias.reshape(ng, SSD_R)).at[:, 1, :SSD_R].set(a_log.reshape(ng, SSD_R))
    dskip_g = jnp.repeat(d_skip, SSD_HEAD_DIM).reshape(ng, 1, SSD_GW)
    nw_g = norm_w.reshape(ng, 1, SSD_GW)
    h0_t = h0.reshape(bsz, ng, SSD_GW, SSD_STATE).transpose(0, 1, 3, 2)

    def rows(width, off):
        assert off % width == 0
        return pl.BlockSpec((rows_in, width), lambda b, g, c: (b * nblk + c, off // width + g))

    def grp(shape):
        return pl.BlockSpec((1,) + shape, lambda b, g, c: (g,) + (0,) * len(shape))

    state = pl.BlockSpec((1, 1, SSD_STATE, SSD_GW), lambda b, g, c: (b, g, 0, 0))
    out, h_t = pl.pallas_call(
        functools.partial(_ssd_kernel, rows_in=rows_in),
        grid=(bsz, ng, nblk),
        in_specs=[rows(SSD_GW, XBC_OFF), rows(SSD_STATE, XBC_OFF + D_SSD), rows(SSD_STATE, XBC_OFF + D_SSD + ng * SSD_STATE),
                  rows(SSD_GW, Z_OFF), pl.BlockSpec((rows_in, LANE), lambda b, g, c: (b * nblk + c, SMALL_OFF // LANE)),
                  pl.BlockSpec((1, 1, SSD_TAIL, SSD_CW), lambda b, g, c: (b, g, 0, 0)),
                  grp((SSD_CONV, SSD_CW)), grp((1, SSD_CW)), grp((8, LANE)), grp((1, SSD_GW)), grp((1, SSD_GW)), state],
        out_specs=[pl.BlockSpec((rows_in, SSD_GW), lambda b, g, c: (b * nblk + c, g)), state],
        out_shape=[jax.ShapeDtypeStruct((m, D_SSD), BF16), jax.ShapeDtypeStruct((bsz, ng, SSD_STATE, SSD_GW), F32)],
        scratch_shapes=[pltpu.VMEM((SSD_TAIL + SSD_CHUNK, SSD_CW), F32), pltpu.VMEM((SSD_STATE, SSD_GW), F32)],
        compiler_params=pltpu.CompilerParams(dimension_semantics=('parallel', 'parallel', 'arbitrary'),
                                             vmem_limit_bytes=VMEM_LIMIT_BYTES),
        name='ssd',
    )(y, y, y, y, y, conv0_g, cw_g, cb_g, hp, dskip_g, nw_g, h0_t)
    h_new = h_t.transpose(0, 1, 3, 2).reshape(bsz, SSD_HEADS, SSD_HEAD_DIM, SSD_STATE)
    return out, h_new


def _split(x, sizes):
    return jnp.split(x, np.cumsum(sizes)[:-1].tolist(), axis=-1)


def _layernorm(x, g, b):
    mu = jnp.mean(x, -1, keepdims=True)
    var = jnp.mean(jnp.square(x - mu), -1, keepdims=True)
    return (x - mu) * lax.rsqrt(var + LN_EPS) * g + b


def _rmsnorm(x, w):
    return x * lax.rsqrt(jnp.mean(x * x, -1, keepdims=True) + NORM_EPS) * w


def _masked_softmax(s, mask):
    s = jnp.where(mask, s.astype(F32), -jnp.inf)
    m = jnp.max(s, -1, keepdims=True)
    m = jnp.where(jnp.isfinite(m), m, 0.0)
    e = jnp.exp(s - m)
    d = jnp.sum(e, -1, keepdims=True)
    return e / jnp.where(d > 0, d, 1.0)


def _t5_bucket(dist):
    n = jnp.maximum(dist, 0)
    max_exact = N_BUCKETS // 2
    nf = jnp.maximum(n, 1).astype(F32)
    large = max_exact + (jnp.log(nf / max_exact) / math.log(MAX_DISTANCE / max_exact) * (N_BUCKETS - max_exact)).astype(jnp.int32)
    large = jnp.minimum(large, N_BUCKETS - 1)
    return jnp.where(n < max_exact, n, large)


def _ssd_scan(x, dt, a, bm, cm, h0):
    b, l, h, p = x.shape
    g, n = bm.shape[2], bm.shape[3]
    r = h // g
    lc = min(SSD_CHUNK, l)
    pad = (-l) % lc
    x, dt, bm, cm = [jnp.pad(t.astype(F32), ((0, 0), (0, pad)) + ((0, 0),) * (t.ndim - 2)) for t in (x, dt, bm, cm)]
    c = (l + pad) // lc
    xdt = (x * dt[..., None]).reshape(b, c, lc, g, r, p)
    acs = jnp.cumsum((dt * a).reshape(b, c, lc, g, r), axis=2)
    bm = bm.reshape(b, c, lc, g, n)
    cm = cm.reshape(b, c, lc, g, n)
    causal = np.tril(np.ones((lc, lc), bool))[None, None, :, :, None, None]
    decay = jnp.exp(jnp.where(causal, acs[:, :, :, None] - acs[:, :, None], -jnp.inf))
    cb = jnp.einsum('bctgn,bcsgn->bctsg', cm, bm)
    y_diag = jnp.einsum('bctsg,bctsgr,bcsgrp->bctgrp', cb, decay, xdt)
    decay_st = jnp.exp(acs[:, :, -1:] - acs)
    states = jnp.einsum('bcsgn,bcsgr,bcsgrp->bcgrpn', bm, decay_st, xdt)
    chunk_decay = jnp.exp(acs[:, :, -1])

    def step(hc, inp):
        st, dec = inp
        return hc * dec[..., None, None] + st, hc

    h_last, h_prev = lax.scan(step, h0.astype(F32).reshape(b, g, r, p, n), (jnp.moveaxis(states, 1, 0), jnp.moveaxis(chunk_decay, 1, 0)))
    h_prev = jnp.moveaxis(h_prev, 0, 1)
    y_off = jnp.einsum('bctgn,bcgrpn,bctgr->bctgrp', cm, h_prev, jnp.exp(acs))
    y = (y_diag + y_off).reshape(b, c * lc, h, p)[:, :l]
    return y, h_last.reshape(b, h, p, n)


def _ssd_mixer(z, xbc, dt, conv0, h0, conv_w, conv_b, dt_bias, a_log, d_skip, norm_w):
    b, l, _ = xbc.shape
    xpad = jnp.concatenate([conv0.astype(xbc.dtype), xbc], axis=1)
    conv = sum((xpad[:, k:k + l] * conv_w[k] for k in range(SSD_CONV)), conv_b)
    conv_new = xpad[:, -(SSD_CONV - 1):]
    xbc = jax.nn.silu(conv)
    xs, bm, cm = _split(xbc, (D_SSD, SSD_GROUPS * SSD_STATE, SSD_GROUPS * SSD_STATE))
    xs = xs.reshape(b, l, SSD_HEADS, SSD_HEAD_DIM)
    bm = bm.reshape(b, l, SSD_GROUPS, SSD_STATE)
    cm = cm.reshape(b, l, SSD_GROUPS, SSD_STATE)
    dt = jax.nn.softplus((dt + dt_bias).astype(F32))
    a = -jnp.exp(a_log.astype(F32))
    y, h_new = _ssd_scan(xs, dt, a, bm, cm, h0)
    y = y + xs.astype(F32) * d_skip.astype(F32)[:, None]
    y = (y.reshape(b, l, D_SSD) * jax.nn.silu(z.astype(F32))).reshape(b, l, SSD_GROUPS, D_SSD // SSD_GROUPS)
    y = _rmsnorm(y, norm_w.reshape(SSD_GROUPS, D_SSD // SSD_GROUPS)).reshape(b, l, D_SSD)
    return y, conv_new, h_new


def _gla_scan(q, k, v, logf, s0):
    b, l, h, dk = q.shape
    dv = v.shape[-1]
    lc = min(GLA_CHUNK, l)
    pad = (-l) % lc
    q, k, v, logf = [jnp.pad(t.astype(F32), ((0, 0), (0, pad), (0, 0), (0, 0))) for t in (q, k, v, logf)]
    c = (l + pad) // lc
    q, k, logf = [t.reshape(b, c, lc, h, dk) for t in (q, k, logf)]
    v = v.reshape(b, c, lc, h, dv)
    bc = jnp.cumsum(logf, axis=2)
    causal = np.tril(np.ones((lc, lc), bool))[None, None, :, :, None, None]
    rel = jnp.exp(jnp.where(causal, bc[:, :, :, None] - bc[:, :, None], -jnp.inf))
    att = jnp.einsum('bcthd,bcshd,bctshd->bchts', q, k, rel)
    o_intra = jnp.einsum('bchts,bcshv->bcthv', att, v)
    k_st = k * jnp.exp(bc[:, :, -1:] - bc)
    chunk_states = jnp.einsum('bcshd,bcshv->bchdv', k_st, v)
    chunk_decay = jnp.exp(bc[:, :, -1])

    def step(s, inp):
        st, dec = inp
        return s * dec[..., None] + st, s

    s_last, s_prev = lax.scan(step, s0.astype(F32), (jnp.moveaxis(chunk_states, 1, 0), jnp.moveaxis(chunk_decay, 1, 0)))
    s_prev = jnp.moveaxis(s_prev, 0, 1)
    o_inter = jnp.einsum('bcthd,bchdv->bcthv', q * jnp.exp(bc), s_prev)
    o = (o_intra + o_inter).reshape(b, c * lc, h, dv)[:, :l]
    return o, s_last


def _gla_mixer(q, k, v, g, a_lr, s0, gate_w2, gate_b, norm_w):
    b, l, _ = q.shape
    q = q.reshape(b, l, GLA_HEADS, GLA_DK) * (GLA_DK ** -0.5)
    k = k.reshape(b, l, GLA_HEADS, GLA_DK)
    v = v.reshape(b, l, GLA_HEADS, GLA_DV)
    logf = jax.nn.log_sigmoid((a_lr @ gate_w2 + gate_b).astype(F32)) / GLA_TAU
    o, s_new = _gla_scan(q, k, v, logf.reshape(b, l, GLA_HEADS, GLA_DK), s0)
    o = _rmsnorm(o, norm_w).reshape(b, l, D_GLA) * jax.nn.silu(g.astype(F32))
    return o, s_new


def _nsa_attend(q, gates, kv_full, win_ctx, q_pos0, w_pos0, cmp_pe, cmp_w, rel_bias):
    b, lq = q.shape[:2]
    t_len = kv_full.shape[1]
    G, R, HD = NSA_KV_HEADS, NSA_REP, NSA_HEAD_DIM
    kc, vc, ks, vs = [kv_full[:, :, i] for i in range(4)]
    n_cmp = (t_len - CMP_LEN) // CMP_STRIDE + 1
    cidx = (np.arange(n_cmp)[:, None] * CMP_STRIDE + np.arange(CMP_LEN)[None]).astype(np.int32)
    k_cmp = jnp.einsum('bnlgd,lde->bnge', kc[:, cidx] + cmp_pe[0][:, None], cmp_w[0]).astype(F32)
    v_cmp = jnp.einsum('bnlgd,lde->bnge', vc[:, cidx] + cmp_pe[1][:, None], cmp_w[1]).astype(F32)
    cmp_end = (np.arange(n_cmp) * CMP_STRIDE + CMP_LEN - 1).astype(np.int32)
    n_sel = -(-t_len // SEL_LEN)
    sel_pad = n_sel * SEL_LEN - t_len

    def sel_blocks(t):
        t = jnp.pad(t, ((0, 0), (0, sel_pad), (0, 0), (0, 0)))
        return t.reshape(b, n_sel, SEL_LEN, G, HD).transpose(0, 3, 1, 2, 4)

    ks_b, vs_b = sel_blocks(ks), sel_blocks(vs)
    c_lo = np.arange(n_cmp) * CMP_STRIDE
    s_lo = np.arange(n_sel) * SEL_LEN
    cover = jnp.asarray(((c_lo[:, None] < s_lo[None] + SEL_LEN) & (c_lo[:, None] + CMP_LEN > s_lo[None])).astype(np.float32))
    top_n = min(TOP_N, n_sel)
    kw_pad = jnp.pad(win_ctx[:, :, 0], ((0, 0), (WINDOW, 0), (0, 0), (0, 0)))
    vw_pad = jnp.pad(win_ctx[:, :, 1], ((0, 0), (WINDOW, 0), (0, 0), (0, 0)))
    qb = Q_BLOCK if lq % Q_BLOCK == 0 else lq
    nqb = lq // qb
    scale = HD ** -0.5
    bias_gr = rel_bias.reshape(N_BUCKETS, G, R)
    bi = jnp.arange(b)[:, None, None, None]
    gi = jnp.arange(G)[None, None, :, None]
    gi5 = jnp.arange(G)[None, None, :, None, None]
    blk = jnp.arange(n_sel)

    def block(args):
        i, qblk, gblk = args
        q0 = q_pos0 + i * qb
        tpos = q0 + jnp.arange(qb)
        qg = qblk.reshape(b, qb, G, R, HD).astype(F32) * scale
        bias_c = rel_bias[_t5_bucket(tpos[:, None] - cmp_end[None])].astype(F32).reshape(qb, n_cmp, G, R).transpose(0, 2, 3, 1)
        s_c = jnp.einsum('bqgrd,bngd->bqgrn', qg, k_cmp) + bias_c
        p_c = _masked_softmax(s_c, (cmp_end[None] <= tpos[:, None])[None, :, None, None, :])
        o_c = jnp.einsum('bqgrn,bngd->bqgrd', p_c, v_cmp)
        imp = jnp.einsum('bqgrn,nj->bqgj', p_c, cover)
        cur = (tpos // SEL_LEN)[:, None]
        forced = (blk == 0) | (blk == cur) | (blk == cur - 1)
        valid = blk * SEL_LEN <= tpos[:, None]
        score = jnp.where(forced[None, :, None, :], FORCE_SCORE, jnp.where(valid[None, :, None, :], imp, -FORCE_SCORE))
        _, sel = lax.top_k(score, top_n)
        k_sel = ks_b[bi, gi, sel].astype(F32)
        v_sel = vs_b[bi, gi, sel].astype(F32)
        spos = sel[..., None] * SEL_LEN + jnp.arange(SEL_LEN)
        dist_s = tpos[None, :, None, None, None] - spos
        bias_s = jnp.moveaxis(bias_gr[_t5_bucket(dist_s), gi5].astype(F32), -1, 3)
        s_s = jnp.einsum('bqgrd,bqgksd->bqgrks', qg, k_sel) + bias_s
        p_s = _masked_softmax(s_s.reshape(b, qb, G, R, top_n * SEL_LEN), (dist_s >= 0).reshape(b, qb, G, 1, top_n * SEL_LEN)).reshape(s_s.shape)
        o_s = jnp.einsum('bqgrks,bqgksd->bqgrd', p_s, v_sel)
        start = q0 - w_pos0
        k_w = lax.dynamic_slice_in_dim(kw_pad, start, WINDOW + qb, axis=1).astype(F32)
        v_w = lax.dynamic_slice_in_dim(vw_pad, start, WINDOW + qb, axis=1).astype(F32)
        wpos = q0 - WINDOW + jnp.arange(WINDOW + qb)
        dist_w = tpos[:, None] - wpos[None]
        mask_w = (dist_w >= 0) & (dist_w < WINDOW) & (wpos[None] >= w_pos0)
        bias_w = rel_bias[_t5_bucket(dist_w)].astype(F32).reshape(qb, WINDOW + qb, G, R).transpose(0, 2, 3, 1)
        s_w = jnp.einsum('bqgrd,bkgd->bqgrk', qg, k_w) + bias_w
        p_w = _masked_softmax(s_w, mask_w[None, :, None, None, :])
        o_w = jnp.einsum('bqgrk,bkgd->bqgrd', p_w, v_w)
        gg = gblk.reshape(b, qb, G, R, 3).astype(F32)
        return gg[..., 0:1] * o_c + gg[..., 1:2] * o_s + gg[..., 2:3] * o_w

    q_blocks = jnp.swapaxes(q.reshape(b, nqb, qb, NSA_HEADS, HD), 0, 1)
    g_blocks = jnp.swapaxes(gates.reshape(b, nqb, qb, NSA_HEADS, 3), 0, 1)
    outs = lax.map(block, (jnp.arange(nqb), q_blocks, g_blocks))
    return jnp.moveaxis(outs, 0, 1).reshape(b, lq, D_NSA)


def _mixer(x, lp, rel_bias, tables, conv0, ssd_h0, gla_s0, nsa_past, win_past, past_len):
    b, l, _ = x.shape
    y = _matmul(x.reshape(b * l, D_MODEL).astype(BF16), lp['w_in'], name='in_proj')
    y3 = y.reshape(b, l, D_IN_PAD)

    def part(off, width):
        return y3[..., off:off + width]

    z, xbc, dt = part(Z_OFF, D_SSD), part(XBC_OFF, SSD_CONV_DIM), part(DT_OFF, SSD_HEADS)
    q_g, k_g = part(QG_OFF, GLA_HEADS * GLA_DK), part(KG_OFF, GLA_HEADS * GLA_DK)
    v_g, g_g, a_g = part(VG_OFF, D_GLA), part(GG_OFF, D_GLA), part(ALR_OFF, GLA_RANK)
    y_ssd, conv_new, h_new = _ssd_mixer(z, xbc, dt, conv0, ssd_h0, lp['ssd_conv_w'], lp['ssd_conv_b'], lp['ssd_dt_bias'], lp['ssd_a_log'], lp['ssd_d'], lp['ssd_norm_w'])
    rows = part(KV_OFF, 4 * KVW).reshape(b, l, 4, NSA_KV_HEADS, NSA_HEAD_DIM)
    win_rows = part(KV_OFF + 4 * KVW, 2 * KVW).reshape(b, l, 2, NSA_KV_HEADS, NSA_HEAD_DIM)
    if nsa_past is None:
        win_new = win_rows[:, -min(WINDOW, l):]
        y_nsa = _nsa_prompt(y, _nsa_compress_prompt(y, lp['nsa_cmp_pe'], lp['nsa_cmp_w'], b), tables, b).reshape(b, l, D_NSA)
    else:
        kv_full = jnp.concatenate([nsa_past.astype(rows.dtype), rows], axis=1)
        win_ctx = jnp.concatenate([win_past.astype(rows.dtype), win_rows], axis=1)
        win_new = win_ctx[:, -win_past.shape[1]:]
        g_n = jnp.concatenate([part(GATE_OFF + g * LANE, 3 * NSA_REP) for g in range(NSA_KV_HEADS)], axis=-1)
        gates = jax.nn.sigmoid(g_n).reshape(b, l, NSA_HEADS, 3)
        q_n = part(QN_OFF, D_NSA).reshape(b, l, NSA_HEADS, NSA_HEAD_DIM)
        y_nsa = _nsa_attend(q_n, gates, kv_full, win_ctx, past_len, past_len - win_past.shape[1], lp['nsa_cmp_pe'], lp['nsa_cmp_w'], rel_bias)
    y_gla, s_new = _gla_mixer(q_g, k_g, v_g, g_g, a_g, gla_s0, lp['gla_gate_w2'], lp['gla_gate_b'], lp['gla_norm_w'])
    y_out = _proj(jnp.concatenate([y_ssd, y_nsa, y_gla], axis=-1), lp['w_out'], name='out_proj')
    return y_out, rows, win_new, conv_new, h_new, s_new


def _cross_attn(x, mem_kv, wq, wo):
    b, l, _ = x.shape
    q = _proj(x, wq, name='xattn_q').reshape(b, l, X_HEADS, X_HEAD_DIM) * (X_HEAD_DIM ** -0.5)
    s = jnp.einsum('blhd,bmhd->bhlm', q, mem_kv[:, :, 0].astype(F32))
    p = jax.nn.softmax(s, axis=-1)
    o = jnp.einsum('bhlm,bmhd->blhd', p, mem_kv[:, :, 1].astype(F32)).reshape(b, l, D_MODEL)
    return _proj(o, wo, name='xattn_o')


def _sqrelu_ffn(x, w1, w2):
    h = _proj(x, w1, out_dtype=BF16, act='sqrelu', name='ffn_up')
    return _proj(h, w2, name='ffn_down')


def _layer(x, lp, rel_bias, tables, mem_kv, conv0, ssd_h0, gla_s0, nsa_past, win_past, past_len):
    h, rows, win, conv, hs, sg = _mixer(x, lp, rel_bias, tables, conv0, ssd_h0, gla_s0, nsa_past, win_past, past_len)
    x = _layernorm(DN_ALPHA * x + h, lp['ln_g'][0], lp['ln_b'][0])
    x = _layernorm(DN_ALPHA * x + _cross_attn(x, mem_kv, lp['x_wq'], lp['x_wo']), lp['ln_g'][1], lp['ln_b'][1])
    x = _layernorm(DN_ALPHA * x + _sqrelu_ffn(x, lp['ffn_w1'], lp['ffn_w2']), lp['ln_g'][2], lp['ln_b'][2])
    return x, (rows, win, conv, hs, sg)


def _layer_flat(x, xb, bsz, lp, rel_bias, tables, mem_kv, conv0, ssd_h0, gla_s0, nsa_past, win_past, past_len):
    m = x.shape[0]
    l = m // bsz
    assert l >= SSD_CONV - 1
    y = _matmul(xb, lp['w_in'], name='in_proj')
    y3 = y.reshape(bsz, l, D_IN_PAD)
    y_ssd, h_new = _ssd(y, conv0, ssd_h0, lp['ssd_conv_w'], lp['ssd_conv_b'], lp['ssd_dt_bias'], lp['ssd_a_log'],
                        lp['ssd_d'], lp['ssd_norm_w'], bsz)
    conv_new = y3[:, l - (SSD_CONV - 1):, XBC_OFF:XBC_OFF + SSD_CONV_DIM]
    y_gla, s_new = _gla(y, gla_s0, lp['gla_gate_w2'], lp['gla_gate_b'], lp['gla_norm_w'], bsz)
    rows = y3[..., KV_OFF:KV_OFF + 4 * KVW].reshape(bsz, l, 4, NSA_KV_HEADS, NSA_HEAD_DIM)
    win_rows = y3[..., KV_OFF + 4 * KVW:KV_OFF + 6 * KVW].reshape(bsz, l, 2, NSA_KV_HEADS, NSA_HEAD_DIM)
    if nsa_past is None:
        win_new = win_rows[:, -min(WINDOW, l):]
        y_nsa = _nsa_prompt_t(y, _nsa_compress_prompt(y, lp['nsa_cmp_pe'], lp['nsa_cmp_w'], bsz), tables, bsz)
    else:
        cache_kv, layer, page_table = nsa_past
        win_new = jnp.concatenate([win_past, win_rows], axis=1)[:, -win_past.shape[1]:]
        y_nsa = _nsa_sample(y, cache_kv, layer, page_table, win_past, lp['nsa_cmp_pe'], lp['nsa_cmp_w'], rel_bias, bsz)
    h = _out_proj(y_ssd, y_nsa, y_gla, lp['w_out'])
    x, xb = _add_layernorm(x, h, lp['ln_g'][0], lp['ln_b'][0])
    o = _cross_attention(_matmul(xb, lp['x_wq'], name='xattn_q'), mem_kv, bsz)
    x, xb = _add_layernorm(x, _matmul(o, lp['x_wo'], name='xattn_o'), lp['ln_g'][1], lp['ln_b'][1])
    hidden = _matmul(xb, lp['ffn_w1'], out_dtype=BF16, act='sqrelu', name='ffn_up')
    x, xb = _add_layernorm(x, _matmul(hidden, lp['ffn_w2'], name='ffn_down'), lp['ln_g'][2], lp['ln_b'][2])
    return x, xb, (rows, win_new, conv_new, h_new, s_new)


def kernel(x_prompt, x_sample, cache_nsa_kv, cache_nsa_win, state_ssd, state_ssd_conv, state_gla, cache_mem_kv,
           page_table, mem_prompt, w_in, ssd_conv_w, ssd_conv_b, ssd_dt_bias, ssd_a_log, ssd_d, ssd_norm_w,
           nsa_cmp_pe, nsa_cmp_w, rel_bias, gla_gate_w2, gla_gate_b, gla_norm_w, w_out, x_wq, x_wkv, x_wo,
           ffn_w1, ffn_w2, ln_g, ln_b):
    bp = x_prompt.shape[0]
    bs = x_sample.shape[0]
    past_len = page_table.shape[1] * cache_nsa_kv.shape[2]
    xp, xs = x_prompt.reshape(-1, D_MODEL), x_sample.reshape(-1, D_MODEL)
    xpb, xsb = xp.astype(BF16), xs.astype(BF16)
    st_p, st_s, mem_p = [], [], []
    tables = _nsa_prompt_bias_tables_t(rel_bias)
    for l in range(DEPTH):
        lp = dict(w_in=_reorder_w_in(w_in[l]), ssd_conv_w=ssd_conv_w[l], ssd_conv_b=ssd_conv_b[l],
                  ssd_dt_bias=ssd_dt_bias[l], ssd_a_log=ssd_a_log[l], ssd_d=ssd_d[l], ssd_norm_w=ssd_norm_w[l],
                  nsa_cmp_pe=nsa_cmp_pe[l], nsa_cmp_w=nsa_cmp_w[l], gla_gate_w2=gla_gate_w2[l], gla_gate_b=gla_gate_b[l],
                  gla_norm_w=gla_norm_w[l], w_out=w_out[l].astype(BF16), x_wq=x_wq[l].astype(BF16),
                  x_wo=x_wo[l].astype(BF16), ffn_w1=ffn_w1[l].astype(BF16), ffn_w2=ffn_w2[l].astype(BF16),
                  ln_g=ln_g[l], ln_b=ln_b[l])
        mem_kv_p = _proj(mem_prompt, x_wkv[l].astype(BF16), name='mem_kv')
        conv0 = jnp.zeros((bp, SSD_CONV - 1, SSD_CONV_DIM), F32)
        h0 = jnp.zeros((bp, SSD_HEADS, SSD_HEAD_DIM, SSD_STATE), F32)
        s0 = jnp.zeros((bp, GLA_HEADS, GLA_DK, GLA_DV), F32)
        xp, xpb, stp = _layer_flat(xp, xpb, bp, lp, rel_bias, tables, mem_kv_p, conv0, h0, s0, None, None, 0)
        st_p.append(stp)
        mem_p.append(mem_kv_p.reshape(bp, N_MEM, 2, X_HEADS, X_HEAD_DIM))
        xs, xsb, sts = _layer_flat(xs, xsb, bs, lp, rel_bias, tables, cache_mem_kv[l].reshape(bs, N_MEM, 2 * D_MODEL),
                                   state_ssd_conv[l], state_ssd[l], state_gla[l], (cache_nsa_kv, l, page_table),
                                   cache_nsa_win[l], past_len)
        st_s.append(sts)
    p_rows, p_win, p_conv, p_ssd, p_gla = [jnp.stack(s) for s in zip(*st_p)]
    s_rows, s_win, s_conv, s_ssd, s_gla = [jnp.stack(s) for s in zip(*st_s)]
    p_mem = jnp.stack(mem_p)
    return (xp.reshape(x_prompt.shape), xs.reshape(x_sample.shape), p_rows, s_rows, p_win, s_win, p_ssd, s_ssd,
            p_conv, s_conv, p_gla, s_gla, p_mem)
```

```python
import functools
import math

import jax
import jax.numpy as jnp
import numpy as np
from jax import lax
from jax.experimental import pallas as pl
from jax.experimental.pallas import tpu as pltpu

F32 = jnp.float32
BF16 = jnp.bfloat16

D_MODEL = 4096
DEPTH = 2
PAGE_SIZE = 128
D_MIX = D_MODEL
D_SSD = D_MIX // 2
SSD_HEAD_DIM = 64
SSD_HEADS = D_SSD // SSD_HEAD_DIM
SSD_GROUPS = 8
SSD_STATE = 128
SSD_CONV = 4
SSD_CHUNK = 128
SSD_CONV_DIM = D_SSD + 2 * SSD_GROUPS * SSD_STATE
D_NSA = D_MIX // 4
NSA_HEAD_DIM = 128
NSA_HEADS = D_NSA // NSA_HEAD_DIM
NSA_KV_HEADS = 2
NSA_REP = NSA_HEADS // NSA_KV_HEADS
CMP_LEN = 32
CMP_STRIDE = 16
SEL_LEN = 64
TOP_N = 16
WINDOW = 512
Q_BLOCK = 128
FORCE_SCORE = 1e4
D_GLA = D_MIX - D_SSD - D_NSA
GLA_HEADS = 4
GLA_DV = D_GLA // GLA_HEADS
GLA_DK = GLA_DV // 2
GLA_RANK = 16
GLA_TAU = 16.0
GLA_CHUNK = 16
N_MEM = 256
X_HEADS = 4
X_HEAD_DIM = D_MODEL // X_HEADS
D_FF = 4 * D_MODEL
N_BUCKETS = 32
MAX_DISTANCE = 128
LN_EPS = 1e-5
NORM_EPS = 1e-6
DN_ALPHA = (2 * DEPTH) ** 0.25
IN_SPLITS = (D_SSD, SSD_CONV_DIM, SSD_HEADS, NSA_HEADS * NSA_HEAD_DIM) + (NSA_KV_HEADS * NSA_HEAD_DIM,) * 6 + (
    3 * NSA_HEADS, GLA_HEADS * GLA_DK, GLA_HEADS * GLA_DK, D_GLA, D_GLA, GLA_RANK)
D_IN = sum(IN_SPLITS)

LANE = 128
VMEM_LIMIT_BYTES = 56 * 1024 * 1024

KVW = NSA_KV_HEADS * NSA_HEAD_DIM
Z_OFF = 0
XBC_OFF = Z_OFF + D_SSD
QN_OFF = XBC_OFF + SSD_CONV_DIM
VG_OFF = QN_OFF + NSA_HEADS * NSA_HEAD_DIM
GG_OFF = VG_OFF + D_GLA
KV_OFF = GG_OFF + D_GLA
QG_OFF = KV_OFF + 6 * KVW
KG_OFF = QG_OFF + GLA_HEADS * GLA_DK
SMALL_OFF = KG_OFF + GLA_HEADS * GLA_DK
DT_OFF = SMALL_OFF
ALR_OFF = SMALL_OFF + SSD_HEADS
GATE_OFF = SMALL_OFF + LANE
D_IN_PAD = 12288
NEG = -1e30


def _in_proj_column_map():
    off = np.cumsum((0,) + IN_SPLITS)
    z, xbc, dt, qn = off[0], off[1], off[2], off[3]
    kv0, gn, qg, kg, vg, gg, alr = off[4], off[10], off[11], off[12], off[13], off[14], off[15]
    cmap = np.full((D_IN_PAD,), -1, np.int64)
    cmap[Z_OFF:Z_OFF + D_SSD] = z + np.arange(D_SSD)
    cmap[XBC_OFF:XBC_OFF + SSD_CONV_DIM] = xbc + np.arange(SSD_CONV_DIM)
    cmap[QN_OFF:QN_OFF + D_NSA] = qn + np.arange(D_NSA)
    cmap[KV_OFF:KV_OFF + 6 * KVW] = kv0 + np.arange(6 * KVW)
    cmap[QG_OFF:QG_OFF + 512] = qg + np.arange(512)
    cmap[KG_OFF:KG_OFF + 512] = kg + np.arange(512)
    cmap[VG_OFF:VG_OFF + D_GLA] = vg + np.arange(D_GLA)
    cmap[GG_OFF:GG_OFF + D_GLA] = gg + np.arange(D_GLA)
    cmap[DT_OFF:DT_OFF + SSD_HEADS] = dt + np.arange(SSD_HEADS)
    cmap[ALR_OFF:ALR_OFF + GLA_RANK] = alr + np.arange(GLA_RANK)
    for g in range(NSA_KV_HEADS):
        cmap[GATE_OFF + g * LANE:GATE_OFF + g * LANE + 3 * NSA_REP] = gn + g * 3 * NSA_REP + np.arange(3 * NSA_REP)
    return cmap


def _reorder_w_in(w):
    cmap = _in_proj_column_map()
    cuts = [0] + [i for i in range(1, D_IN_PAD) if cmap[i] != cmap[i - 1] + 1 and not (cmap[i] == -1 and cmap[i - 1] == -1)] + [D_IN_PAD]
    pieces = []
    for a, b in zip(cuts[:-1], cuts[1:]):
        pieces.append(jnp.zeros((w.shape[0], b - a), BF16) if cmap[a] < 0 else w[:, cmap[a]:cmap[a] + b - a].astype(BF16))
    return jnp.concatenate(pieces, axis=1)


def _dot_nt(a, b):
    return lax.dot_general(a, b, (((1,), (1,)), ((), ())), preferred_element_type=F32)


NSA_L = 2048
NSA_QB = Q_BLOCK
NSA_NQB = NSA_L // NSA_QB
NSA_NCP = NSA_L // CMP_STRIDE
NSA_NSEL = NSA_L // SEL_LEN
CMP_HALF = CMP_STRIDE * NSA_HEAD_DIM
NSA_KSTEP = 512


def _t5_bucket_np(dist):
    n = np.maximum(dist, 0)
    max_exact = N_BUCKETS // 2
    nf = np.maximum(n, 1).astype(np.float32)
    large = max_exact + (np.log(nf / max_exact) / np.float32(math.log(MAX_DISTANCE / max_exact)) * (N_BUCKETS - max_exact)).astype(np.int32)
    return np.where(n < max_exact, n, np.minimum(large, N_BUCKETS - 1)).astype(np.int32)


def _nsa_prompt_bias_tables(rel_bias):
    tq = np.arange(NSA_QB)[:, None]
    ts = np.arange(NSA_QB)[None, :]

    def table(idx):
        t = rel_bias[jnp.asarray(idx)]
        t = jnp.moveaxis(t, -1, 0).reshape((NSA_KV_HEADS, NSA_REP) + idx.shape)
        t = jnp.moveaxis(t, 1, -3)
        return t.reshape(t.shape[:-3] + (NSA_REP * NSA_QB, idx.shape[-1]))

    b_diag = table(_t5_bucket_np(tq - ts))
    b_prev = table(_t5_bucket_np(NSA_QB + tq - ts))
    b_far = table(np.full((NSA_QB, NSA_QB), N_BUCKETS - 1, np.int32))
    i = np.arange(NSA_NQB)[:, None, None]
    n = np.arange(NSA_NCP)[None, None, :]
    b_cmp = table(_t5_bucket_np(i * NSA_QB + tq[None] - (n * CMP_STRIDE + CMP_LEN - 1)))
    return b_diag, b_prev, b_far, b_cmp


def _nsa_compress_kernel(a_ref, pe_ref, w_ref, o_ref):
    a = a_ref[0, 0, 0]
    lo = jnp.dot((a + pe_ref[0, 0]).astype(BF16), w_ref[0, 0], preferred_element_type=F32)
    hi = jnp.dot((a + pe_ref[0, 1]).astype(BF16), w_ref[0, 1], preferred_element_type=F32)
    o_ref[0, 0, 0] = lo + pltpu.roll(hi, NSA_NCP - 1, 0)


def _nsa_compress_prompt(y, cmp_pe, cmp_w, bsz):
    a = y[:, KV_OFF:KV_OFF + 2 * KVW].reshape(bsz, NSA_NCP, CMP_STRIDE, 2, NSA_KV_HEADS, NSA_HEAD_DIM)
    a = a.transpose(0, 3, 4, 1, 2, 5).reshape(bsz, 2, NSA_KV_HEADS, NSA_NCP, CMP_HALF)
    pe = cmp_pe.reshape(2, 2, 1, CMP_HALF)
    w = cmp_w.astype(BF16).reshape(2, 2, CMP_HALF, NSA_HEAD_DIM)
    return pl.pallas_call(
        _nsa_compress_kernel,
        grid=(bsz, 2, NSA_KV_HEADS),
        in_specs=[pl.BlockSpec((1, 1, 1, NSA_NCP, CMP_HALF), lambda b, c, g: (b, c, g, 0, 0)),
                  pl.BlockSpec((1, 2, 1, CMP_HALF), lambda b, c, g: (c, 0, 0, 0)),
                  pl.BlockSpec((1, 2, CMP_HALF, NSA_HEAD_DIM), lambda b, c, g: (c, 0, 0, 0))],
        out_specs=pl.BlockSpec((1, 1, 1, NSA_NCP, NSA_HEAD_DIM), lambda b, c, g: (b, c, g, 0, 0)),
        out_shape=jax.ShapeDtypeStruct((bsz, 2, NSA_KV_HEADS, NSA_NCP, NSA_HEAD_DIM), F32),
        compiler_params=pltpu.CompilerParams(dimension_semantics=('parallel', 'parallel', 'parallel')),
        name='nsa_compress',
    )(a, pe, w)


def _nsa_prompt_kernel(q_ref, gate_ref, cmp_ref, ks_ref, vs_ref, kw_ref, vw_ref, bdiag_ref, bprev_ref, bfar_ref,
                       bcmp_ref, covt_ref, expand_ref, o_ref, m_ref, l_ref, acc_ref):
    i = pl.program_id(2)
    rows = NSA_REP * NSA_QB
    qblk = q_ref[...] * (NSA_HEAD_DIM ** -0.5)
    q = jnp.concatenate([qblk[:, r * LANE:(r + 1) * LANE] for r in range(NSA_REP)], axis=0).astype(BF16)
    tq = lax.broadcasted_iota(jnp.int32, (rows, LANE), 0) % NSA_QB
    lane = lax.broadcasted_iota(jnp.int32, (rows, LANE), 1)
    t_abs = i * NSA_QB + tq

    k_c = cmp_ref[0, 0, 0].astype(BF16)
    v_c = cmp_ref[0, 1, 0].astype(BF16)
    mask_c = lane * CMP_STRIDE + (CMP_LEN - 1) <= t_abs
    s_c = jnp.where(mask_c, _dot_nt(q, k_c) + bcmp_ref[0, 0], NEG)
    e_c = jnp.where(mask_c, jnp.exp(s_c - jnp.max(s_c, -1, keepdims=True)), 0.0)
    d_c = jnp.sum(e_c, -1, keepdims=True)
    p_c = (e_c / jnp.where(d_c > 0, d_c, 1.0)).astype(BF16)
    o_c = jnp.dot(p_c, v_c, preferred_element_type=F32)

    imp4 = _dot_nt(covt_ref[...], p_c)
    imp = sum(imp4[:, r * NSA_QB:(r + 1) * NSA_QB] for r in range(NSA_REP))
    blk = lax.broadcasted_iota(jnp.int32, (LANE, NSA_QB), 0)
    t_row = i * NSA_QB + lax.broadcasted_iota(jnp.int32, (LANE, NSA_QB), 1)
    cur = t_row // SEL_LEN
    forced = (blk == 0) | (blk == cur) | (blk == cur - 1)
    score = jnp.where(forced, FORCE_SCORE, jnp.where(blk * SEL_LEN <= t_row, imp, -FORCE_SCORE))
    score = jnp.where(blk < NSA_NSEL, score, -3.0 * FORCE_SCORE)
    rank = jnp.zeros((LANE, NSA_QB), jnp.int32)
    for j in range(NSA_NSEL):
        row = score[j:j + 1, :]
        rank = rank + jnp.where((row > score) | ((row == score) & (blk > j)), 1, 0)
    sel_t = jnp.where((rank < TOP_N) & (blk < NSA_NSEL), 1.0, 0.0)
    sel = sel_t.T.astype(BF16)

    def init():
        m_ref[...] = jnp.full((rows, 1), NEG, F32)
        l_ref[...] = jnp.zeros((rows, 1), F32)
        acc_ref[...] = jnp.zeros((rows, NSA_HEAD_DIM), F32)

    def step(k_ref, v_ref, c, bias, mask):
        start = pl.multiple_of(c * NSA_QB, NSA_QB)
        k = k_ref[pl.ds(start, NSA_QB), :].astype(BF16)
        v = v_ref[pl.ds(start, NSA_QB), :].astype(BF16)
        s = _dot_nt(q, k) + bias
        if mask is not None:
            s = jnp.where(mask, s, NEG)
        m_old = m_ref[...]
        m_new = jnp.maximum(m_old, jnp.max(s, -1, keepdims=True))
        p = jnp.exp(s - m_new)
        if mask is not None:
            p = jnp.where(mask, p, 0.0)
        alpha = jnp.exp(m_old - m_new)
        l_ref[...] = alpha * l_ref[...] + jnp.sum(p, -1, keepdims=True)
        acc_ref[...] = alpha * acc_ref[...] + jnp.dot(p.astype(BF16), v, preferred_element_type=F32)
        m_ref[...] = m_new

    def finish():
        l = l_ref[...]
        return acc_ref[...] / jnp.where(l > 0, l, 1.0)

    def sel_mask(c):
        m1 = jnp.dot(sel, expand_ref[c], preferred_element_type=F32)
        return jnp.concatenate([m1] * NSA_REP, axis=0) > 0.5

    init()
    bfar = bfar_ref[0]

    def far_body(c, carry):
        step(ks_ref, vs_ref, c, bfar, sel_mask(c))
        return carry

    lax.fori_loop(0, jnp.maximum(i - 1, 0), far_body, 0)

    @pl.when(i >= 1)
    def _():
        step(ks_ref, vs_ref, i - 1, bprev_ref[0], sel_mask(i - 1))

    causal = lane <= tq
    step(ks_ref, vs_ref, i, bdiag_ref[0], sel_mask(i) & causal)
    o_s = finish()

    init()
    step(kw_ref, vw_ref, i, bdiag_ref[0], causal)

    @pl.when(i >= 1)
    def _():
        step(kw_ref, vw_ref, i - 1, bprev_ref[0], None)

    for d in range(2, WINDOW // NSA_QB):
        @pl.when(i >= d)
        def _(d=d):
            step(kw_ref, vw_ref, i - d, bfar, None)

    @pl.when(i >= WINDOW // NSA_QB)
    def _():
        step(kw_ref, vw_ref, i - WINDOW // NSA_QB, bfar, lane > tq)

    o_w = finish()

    gates = jax.nn.sigmoid(gate_ref[...])
    for r in range(NSA_REP):
        rs = slice(r * NSA_QB, (r + 1) * NSA_QB)
        o_ref[:, r * LANE:(r + 1) * LANE] = (gates[:, 3 * r:3 * r + 1] * o_c[rs] + gates[:, 3 * r + 1:3 * r + 2] * o_s[rs]
                                             + gates[:, 3 * r + 2:3 * r + 3] * o_w[rs]).astype(o_ref.dtype)


def _nsa_prompt(y, k_v_cmp, tables, bsz):
    b_diag, b_prev, b_far, b_cmp = tables
    rows = NSA_REP * NSA_QB
    c_lo = np.arange(NSA_NCP) * CMP_STRIDE
    s_lo = np.arange(LANE) * SEL_LEN
    cov_t = ((c_lo[None] < s_lo[:, None] + SEL_LEN) & (c_lo[None] + CMP_LEN > s_lo[:, None])
             & (np.arange(NSA_NCP)[None] < NSA_NCP - 1) & (np.arange(LANE)[:, None] < NSA_NSEL))
    key_blk = (np.arange(NSA_L) // SEL_LEN).reshape(NSA_NQB, 1, NSA_QB)
    expand = (np.arange(LANE)[None, :, None] == key_blk)
    kv_blk = KV_OFF // LANE
    grp = NSA_KV_HEADS

    def col(slot):
        return pl.BlockSpec((NSA_L, NSA_HEAD_DIM), lambda b, g, i: (b, kv_blk + slot * grp + g))

    tab = pl.BlockSpec((1, rows, LANE), lambda b, g, i: (g, 0, 0))
    return pl.pallas_call(
        _nsa_prompt_kernel,
        grid=(bsz, NSA_KV_HEADS, NSA_NQB),
        in_specs=[pl.BlockSpec((NSA_QB, NSA_REP * LANE), lambda b, g, i: (b * NSA_NQB + i, QN_OFF // (NSA_REP * LANE) + g)),
                  pl.BlockSpec((NSA_QB, LANE), lambda b, g, i: (b * NSA_NQB + i, GATE_OFF // LANE + g)),
                  pl.BlockSpec((1, 2, 1, NSA_NCP, NSA_HEAD_DIM), lambda b, g, i: (b, 0, g, 0, 0)),
                  col(2), col(3), col(4), col(5), tab, tab, tab,
                  pl.BlockSpec((1, 1, rows, NSA_NCP), lambda b, g, i: (g, i, 0, 0)),
                  pl.BlockSpec((LANE, NSA_NCP), lambda b, g, i: (0, 0)),
                  pl.BlockSpec((NSA_NQB, LANE, NSA_QB), lambda b, g, i: (0, 0, 0))],
        out_specs=pl.BlockSpec((NSA_QB, NSA_REP * LANE), lambda b, g, i: (b * NSA_NQB + i, g)),
        out_shape=jax.ShapeDtypeStruct((bsz * NSA_L, D_NSA), BF16),
        scratch_shapes=[pltpu.VMEM((rows, 1), F32), pltpu.VMEM((rows, 1), F32), pltpu.VMEM((rows, NSA_HEAD_DIM), F32)],
        compiler_params=pltpu.CompilerParams(dimension_semantics=('parallel', 'parallel', 'arbitrary'),
                                             vmem_limit_bytes=VMEM_LIMIT_BYTES),
        name='nsa_prompt',
    )(y, y, k_v_cmp, y, y, y, y, b_diag, b_prev, b_far, b_cmp,
      jnp.asarray(cov_t, BF16), jnp.asarray(expand, BF16))


def _bias_lookup(rel_bias, idx):
    return rel_bias.T[:, jnp.asarray(idx)]


def _bucket_thresholds():
    buckets = _t5_bucket_np(np.arange(4 * MAX_DISTANCE))
    return [int(np.argmax(buckets >= k)) for k in range(N_BUCKETS)]


def _nsa_prompt_bias_tables_t(rel_bias):
    tq = np.arange(NSA_QB)[None, :]
    ts = np.arange(NSA_QB)[:, None]

    def table(idx):
        t = _bias_lookup(rel_bias, idx).reshape((NSA_KV_HEADS, NSA_REP) + idx.shape)
        t = jnp.moveaxis(t, 1, -2)
        return t.reshape(t.shape[:-2] + (NSA_REP * NSA_QB,))

    b_diag = table(_t5_bucket_np(tq - ts))
    b_prev = table(_t5_bucket_np(NSA_QB + tq - ts))
    by_bucket = jnp.repeat(rel_bias.reshape(N_BUCKETS, NSA_KV_HEADS, NSA_REP).transpose(1, 0, 2), NSA_QB, axis=2)
    return b_diag, b_prev, by_bucket


def _nsa_prompt_t_kernel(q_ref, gate_ref, cmp_ref, ks_ref, vs_ref, kw_ref, vw_ref, bdiag_ref, bprev_ref, bkt_ref,
                         covt_ref, expand_ref, o_ref, m_ref, l_ref, acc_ref, ksb, vst, kwb, vwt, bcmp_scr):
    i = pl.program_id(2)
    cols = NSA_REP * NSA_QB

    @pl.when(i == 0)
    def _():
        ksb[...] = ks_ref[...].astype(BF16)
        kwb[...] = kw_ref[...].astype(BF16)
        for c in range(NSA_L // NSA_KSTEP):
            vst[c] = vs_ref[c * NSA_KSTEP:(c + 1) * NSA_KSTEP, :].T.astype(BF16)
            vwt[c] = vw_ref[c * NSA_KSTEP:(c + 1) * NSA_KSTEP, :].T.astype(BF16)

    qblk = q_ref[...] * (NSA_HEAD_DIM ** -0.5)
    q_t = jnp.concatenate([qblk[:, r * LANE:(r + 1) * LANE].T for r in range(NSA_REP)], axis=1).astype(BF16)
    key = lax.broadcasted_iota(jnp.int32, (NSA_QB, cols), 0)
    tq = lax.broadcasted_iota(jnp.int32, (NSA_QB, cols), 1) % NSA_QB
    t_abs = i * NSA_QB + tq

    k_c = cmp_ref[0, 0, 0].astype(BF16)
    v_c_t = cmp_ref[0, 1, 0].T.astype(BF16)
    mask_c = key * CMP_STRIDE + (CMP_LEN - 1) <= t_abs
    bfar = bkt_ref[0, N_BUCKETS - 1:N_BUCKETS, :]
    per_qb = NSA_QB // CMP_STRIDE
    w0 = pl.multiple_of(jnp.maximum(i * per_qb - 2 * per_qb, 0), 8)
    tok = w0 + lax.broadcasted_iota(jnp.int32, (3 * per_qb, cols), 0)
    t_w = i * NSA_QB + lax.broadcasted_iota(jnp.int32, (3 * per_qb, cols), 1) % NSA_QB
    dist_c = t_w - (tok * CMP_STRIDE + (CMP_LEN - 1))
    bias_w = jnp.zeros((3 * per_qb, cols), F32) + bkt_ref[0, 0:1, :]
    for k, thr in enumerate(_bucket_thresholds()):
        if k > 0:
            bias_w = jnp.where(dist_c >= thr, bkt_ref[0, k:k + 1, :], bias_w)
    bcmp_scr[...] = jnp.zeros((NSA_NCP, cols), F32) + bfar
    bcmp_scr[pl.ds(w0, 3 * per_qb), :] = bias_w
    s_c = jnp.where(mask_c, jnp.dot(k_c, q_t, preferred_element_type=F32) + bcmp_scr[...], NEG)
    e_c = jnp.where(mask_c, jnp.exp(s_c - jnp.max(s_c, 0, keepdims=True)), 0.0)
    d_c = jnp.sum(e_c, 0, keepdims=True)
    p_c = (e_c / jnp.where(d_c > 0, d_c, 1.0)).astype(BF16)
    o_c = jnp.dot(v_c_t, p_c, preferred_element_type=F32)

    imp4 = jnp.dot(covt_ref[...], p_c, preferred_element_type=F32)
    imp = sum(imp4[:, r * NSA_QB:(r + 1) * NSA_QB] for r in range(NSA_REP))
    blk = lax.broadcasted_iota(jnp.int32, (LANE, NSA_QB), 0)
    t_row = i * NSA_QB + lax.broadcasted_iota(jnp.int32, (LANE, NSA_QB), 1)
    cur = t_row // SEL_LEN
    forced = (blk == 0) | (blk == cur) | (blk == cur - 1)
    score = jnp.where(forced, FORCE_SCORE, jnp.where(blk * SEL_LEN <= t_row, imp, -FORCE_SCORE))
    score = jnp.where(blk < NSA_NSEL, score, -3.0 * FORCE_SCORE)
    rank = jnp.zeros((LANE, NSA_QB), jnp.int32)
    for j in range(NSA_NSEL):
        row = score[j:j + 1, :]
        rank = rank + jnp.where((row > score) | ((row == score) & (blk > j)), 1, 0)
    sel_t = jnp.where((rank < TOP_N) & (blk < NSA_NSEL), 1.0, 0.0).astype(BF16)

    def init():
        m_ref[...] = jnp.full((1, cols), NEG, F32)
        l_ref[...] = jnp.zeros((1, cols), F32)
        acc_ref[...] = jnp.zeros((NSA_HEAD_DIM, cols), F32)

    dist0 = (lax.broadcasted_iota(jnp.int32, (NSA_KSTEP, cols), 1) % NSA_QB
             - lax.broadcasted_iota(jnp.int32, (NSA_KSTEP, cols), 0))
    per_step = NSA_KSTEP // NSA_QB

    def step(k_scr, v_scr, j, selected):
        start = pl.multiple_of(j * NSA_KSTEP, NSA_KSTEP)
        s = jnp.dot(k_scr[pl.ds(start, NSA_KSTEP), :], q_t, preferred_element_type=F32)
        bias = []
        for u in range(per_step):
            d = i - (j * per_step + u)
            bias.append(jnp.where(d == 0, bdiag_ref[0], jnp.where(d == 1, bprev_ref[0], bfar)))
        dist = dist0 + (i * NSA_QB - j * NSA_KSTEP)
        if selected:
            m1 = jnp.dot(expand_ref[j], sel_t, preferred_element_type=F32)
            mask = (jnp.concatenate([m1] * NSA_REP, axis=1) > 0.5) & (dist >= 0)
        else:
            mask = (dist >= 0) & (dist < WINDOW)
        s = jnp.where(mask, s + jnp.concatenate(bias, axis=0), NEG)
        m_old = m_ref[...]
        m_new = jnp.maximum(m_old, jnp.max(s, 0, keepdims=True))
        p = jnp.where(mask, jnp.exp(s - m_new), 0.0)
        alpha = jnp.exp(m_old - m_new)
        l_ref[...] = alpha * l_ref[...] + jnp.sum(p, 0, keepdims=True)
        acc_ref[...] = alpha * acc_ref[...] + jnp.dot(v_scr[j], p.astype(BF16), preferred_element_type=F32)
        m_ref[...] = m_new

    def finish():
        l = l_ref[...]
        return acc_ref[...] / jnp.where(l > 0, l, 1.0)

    j_diag = i // per_step

    init()

    def sel_body(j, carry):
        step(ksb, vst, j, True)
        return carry

    lax.fori_loop(0, j_diag + 1, sel_body, 0)
    o_s = finish()

    init()
    step(kwb, vwt, j_diag, False)

    @pl.when(j_diag >= 1)
    def _():
        step(kwb, vwt, j_diag - 1, False)

    o_w = finish()

    g_t = jax.nn.sigmoid(gate_ref[...]).T
    for r in range(NSA_REP):
        cs = slice(r * NSA_QB, (r + 1) * NSA_QB)
        o_r = g_t[3 * r:3 * r + 1, :] * o_c[:, cs] + g_t[3 * r + 1:3 * r + 2, :] * o_s[:, cs] + g_t[3 * r + 2:3 * r + 3, :] * o_w[:, cs]
        o_ref[:, r * LANE:(r + 1) * LANE] = o_r.T.astype(o_ref.dtype)


def _nsa_prompt_t(y, k_v_cmp, tables, bsz):
    b_diag, b_prev, by_bucket = tables
    cols = NSA_REP * NSA_QB
    nstep = NSA_L // NSA_KSTEP
    assert WINDOW <= NSA_KSTEP and NSA_KSTEP % NSA_QB == 0
    c_lo = np.arange(NSA_NCP) * CMP_STRIDE
    s_lo = np.arange(LANE) * SEL_LEN
    cov_t = ((c_lo[None] < s_lo[:, None] + SEL_LEN) & (c_lo[None] + CMP_LEN > s_lo[:, None])
             & (np.arange(NSA_NCP)[None] < NSA_NCP - 1) & (np.arange(LANE)[:, None] < NSA_NSEL))
    key_blk = (np.arange(NSA_L) // SEL_LEN).reshape(nstep, NSA_KSTEP, 1)
    expand = (np.arange(LANE)[None, None, :] == key_blk)
    kv_blk = KV_OFF // LANE
    grp = NSA_KV_HEADS

    def col(slot):
        return pl.BlockSpec((NSA_L, NSA_HEAD_DIM), lambda b, g, i: (b, kv_blk + slot * grp + g))

    tab = pl.BlockSpec((1, NSA_QB, cols), lambda b, g, i: (g, 0, 0))
    return pl.pallas_call(
        _nsa_prompt_t_kernel,
        grid=(bsz, NSA_KV_HEADS, NSA_NQB),
        in_specs=[pl.BlockSpec((NSA_QB, NSA_REP * LANE), lambda b, g, i: (b * NSA_NQB + i, QN_OFF // (NSA_REP * LANE) + g)),
                  pl.BlockSpec((NSA_QB, LANE), lambda b, g, i: (b * NSA_NQB + i, GATE_OFF // LANE + g)),
                  pl.BlockSpec((1, 2, 1, NSA_NCP, NSA_HEAD_DIM), lambda b, g, i: (b, 0, g, 0, 0)),
                  col(2), col(3), col(4), col(5), tab, tab,
                  pl.BlockSpec((1, N_BUCKETS, cols), lambda b, g, i: (g, 0, 0)),
                  pl.BlockSpec((LANE, NSA_NCP), lambda b, g, i: (0, 0)),
                  pl.BlockSpec((nstep, NSA_KSTEP, LANE), lambda b, g, i: (0, 0, 0))],
        out_specs=pl.BlockSpec((NSA_QB, NSA_REP * LANE), lambda b, g, i: (b * NSA_NQB + i, g)),
        out_shape=jax.ShapeDtypeStruct((bsz * NSA_L, D_NSA), BF16),
        scratch_shapes=[pltpu.VMEM((1, cols), F32), pltpu.VMEM((1, cols), F32), pltpu.VMEM((NSA_HEAD_DIM, cols), F32),
                        pltpu.VMEM((NSA_L, NSA_HEAD_DIM), BF16), pltpu.VMEM((nstep, NSA_HEAD_DIM, NSA_KSTEP), BF16),
                        pltpu.VMEM((NSA_L, NSA_HEAD_DIM), BF16), pltpu.VMEM((nstep, NSA_HEAD_DIM, NSA_KSTEP), BF16),
                        pltpu.VMEM((NSA_NCP, cols), F32)],
        compiler_params=pltpu.CompilerParams(dimension_semantics=('parallel', 'parallel', 'arbitrary'),
                                             vmem_limit_bytes=VMEM_LIMIT_BYTES),
        name='nsa_prompt',
    )(y, y, k_v_cmp, y, y, y, y, b_diag, b_prev, by_bucket,
      jnp.asarray(cov_t, BF16), jnp.asarray(expand, BF16))


NS_SELW = 3 * LANE
NS_CHUNK = 2048


def _nsa_sample_tables(rel_bias, past_len, lq, win_len):
    tq = (past_len + np.arange(lq))[:, None]

    def table(idx):
        t = _bias_lookup(rel_bias, idx)
        return t.reshape(NSA_KV_HEADS, NSA_REP * lq, idx.shape[-1])

    ncp = past_len // CMP_STRIDE
    b_cmp = table(_t5_bucket_np(tq - (np.arange(ncp)[None] * CMP_STRIDE + CMP_LEN - 1)))
    b_last = table(_t5_bucket_np(tq - (past_len - LANE + np.arange(LANE)[None])))
    b_new = table(_t5_bucket_np(tq - (past_len + np.arange(LANE)[None])))
    b_far = table(np.full((lq, LANE), N_BUCKETS - 1, np.int32))
    wpos = np.concatenate([past_len - win_len + np.arange(win_len), past_len + np.arange(LANE)])
    b_win = table(_t5_bucket_np(tq - wpos[None]))
    return b_cmp, b_last, b_new, b_far, b_win


def _nsa_sample_kernel(pt_ref, q_ref, kvn_ref, gate0_ref, gate1_ref, win_ref, cache_ref, w_ref, pe_ref,
                       bcmp_ref, blast_ref, bnew_ref, bfar_ref, bwin_ref, cov_ref, expand_ref, o_ref,
                       buf, cmp_scr, s_scr, sem, *, n_pages, lq, past_len, win_len, layer):
    b = pl.program_id(0)
    rows = NSA_REP * lq
    ncp = past_len // CMP_STRIDE
    n_cmp = (past_len + lq - CMP_LEN) // CMP_STRIDE + 1
    n_sel = -(-(past_len + lq) // SEL_LEN)
    n_chunks = past_len // NS_CHUNK
    ncol = 2 * NSA_KV_HEADS

    def page_copy(j, pair, col):
        slot, g = pair * 2 + col // NSA_KV_HEADS, col % NSA_KV_HEADS
        return pltpu.make_async_copy(cache_ref.at[layer, pt_ref[b, j], :, slot, g, :],
                                     buf.at[col, pl.ds(j * PAGE_SIZE, PAGE_SIZE), :], sem)

    def gather_start(pair):
        def body(j, carry):
            for col in range(ncol):
                page_copy(j, pair, col).start()
            return carry
        lax.fori_loop(0, n_pages, body, 0)

    def gather_wait(pair):
        def body(j, carry):
            for col in range(ncol):
                page_copy(j, pair, col).wait()
            return carry
        lax.fori_loop(0, n_pages, body, 0)

    gather_start(0)
    gather_wait(0)
    for c in range(2):
        for g in range(NSA_KV_HEADS):
            lo = jnp.zeros((ncp, NSA_HEAD_DIM), F32)
            hi = jnp.zeros((ncp, NSA_HEAD_DIM), F32)
            for l in range(CMP_STRIDE):
                x = buf[c * NSA_KV_HEADS + g, pl.ds(l, ncp, stride=CMP_STRIDE), :]
                lo = lo + jnp.dot((x + pe_ref[c, l:l + 1, :]).astype(BF16), w_ref[c, l], preferred_element_type=F32)
                hi = hi + jnp.dot((x + pe_ref[c, CMP_STRIDE + l:CMP_STRIDE + l + 1, :]).astype(BF16),
                                  w_ref[c, CMP_STRIDE + l], preferred_element_type=F32)
            cmp_scr[c, g] = lo + pltpu.roll(hi, ncp - 1, 0)
    gather_start(1)

    qall = q_ref[...] * (NSA_HEAD_DIM ** -0.5)
    tq = lax.broadcasted_iota(jnp.int32, (rows, 1), 0) % lq
    t_abs = past_len + tq
    o_c, sel, q_g = [], [], []
    for g in range(NSA_KV_HEADS):
        q = jnp.concatenate([qall[:, (g * NSA_REP + r) * LANE:(g * NSA_REP + r + 1) * LANE] for r in range(NSA_REP)],
                            axis=0).astype(BF16)
        q_g.append(q)
        n_idx = lax.broadcasted_iota(jnp.int32, (rows, ncp), 1)
        mask_c = (n_idx < n_cmp) & (n_idx * CMP_STRIDE + (CMP_LEN - 1) <= t_abs)
        s_c = jnp.where(mask_c, _dot_nt(q, cmp_scr[0, g].astype(BF16)) + bcmp_ref[g], NEG)
        e_c = jnp.where(mask_c, jnp.exp(s_c - jnp.max(s_c, -1, keepdims=True)), 0.0)
        d_c = jnp.sum(e_c, -1, keepdims=True)
        p_c = (e_c / jnp.where(d_c > 0, d_c, 1.0)).astype(BF16)
        o_c.append(jnp.dot(p_c, cmp_scr[1, g].astype(BF16), preferred_element_type=F32))
        imp4 = jnp.dot(p_c, cov_ref[...], preferred_element_type=F32)
        imp = sum(imp4[r * lq:(r + 1) * lq] for r in range(NSA_REP))
        blk = lax.broadcasted_iota(jnp.int32, (lq, NS_SELW), 1)
        t_q = past_len + lax.broadcasted_iota(jnp.int32, (lq, NS_SELW), 0)
        cur = t_q // SEL_LEN
        forced = (blk == 0) | (blk == cur) | (blk == cur - 1)
        score = jnp.where(forced, FORCE_SCORE, jnp.where(blk * SEL_LEN <= t_q, imp, -FORCE_SCORE))
        score = jnp.where(blk < n_sel, score, -3.0 * FORCE_SCORE)
        rank = jnp.zeros((lq, NS_SELW), jnp.int32)
        for j in range(n_sel):
            col = score[:, j:j + 1]
            rank = rank + jnp.where((col > score) | ((col == score) & (blk > j)), 1, 0)
        sel_q = jnp.where((rank < min(TOP_N, n_sel)) & (blk < n_sel), 1.0, 0.0)
        sel.append(jnp.concatenate([sel_q] * NSA_REP, axis=0))

    gather_wait(1)
    kvn = _pad_rows(kvn_ref[...], LANE)
    lane = lax.broadcasted_iota(jnp.int32, (rows, LANE), 1)
    blocks_per_chunk = NS_CHUNK // SEL_LEN
    gates = [jax.nn.sigmoid(gate0_ref[...]), jax.nn.sigmoid(gate1_ref[...])]
    for g in range(NSA_KV_HEADS):
        q = q_g[g]
        bfar = bfar_ref[g][:, 0:1]

        def new_rows(slot):
            return kvn[:, (slot * NSA_KV_HEADS + g) * NSA_HEAD_DIM:(slot * NSA_KV_HEADS + g + 1) * NSA_HEAD_DIM].astype(BF16)

        sel_b = sel[g].astype(BF16)
        for c in range(n_chunks):
            k = buf[g, c * NS_CHUNK:(c + 1) * NS_CHUNK, :].astype(BF16)
            s = _dot_nt(q, k) + bfar
            if c == n_chunks - 1:
                fix = blast_ref[g] - bfar_ref[g]
                s = jnp.concatenate([s[:, :NS_CHUNK - LANE], s[:, NS_CHUNK - LANE:] + fix], axis=1)
            m_c = jnp.dot(sel_b[:, c * blocks_per_chunk:(c + 1) * blocks_per_chunk], expand_ref[...],
                          preferred_element_type=F32)
            s_scr[:, c * NS_CHUNK:(c + 1) * NS_CHUNK] = jnp.where(m_c > 0.5, s, NEG)
        s_new = _dot_nt(q, new_rows(2)) + bnew_ref[g]
        mask_new = (lane <= tq) & (lane < lq) & (sel[g][:, n_sel - 1:n_sel] > 0.5)
        s_scr[:, past_len:past_len + LANE] = jnp.where(mask_new, s_new, NEG)
        m = jnp.max(s_scr[...], -1, keepdims=True)
        acc = jnp.zeros((rows, NSA_HEAD_DIM), F32)
        den = jnp.zeros((rows, 1), F32)
        for c in range(n_chunks):
            sc = s_scr[:, c * NS_CHUNK:(c + 1) * NS_CHUNK]
            e = jnp.where(sc > 0.5 * NEG, jnp.exp(sc - m), 0.0)
            den = den + jnp.sum(e, -1, keepdims=True)
            v = buf[NSA_KV_HEADS + g, c * NS_CHUNK:(c + 1) * NS_CHUNK, :]
            acc = acc + jnp.dot(e.astype(BF16), v.astype(BF16), preferred_element_type=F32)
        sc = s_scr[:, past_len:past_len + LANE]
        e = jnp.where(sc > 0.5 * NEG, jnp.exp(sc - m), 0.0)
        den = den + jnp.sum(e, -1, keepdims=True)
        acc = acc + jnp.dot(e.astype(BF16), new_rows(3), preferred_element_type=F32)
        o_s = acc / jnp.where(den > 0, den, 1.0)

        kw = jnp.concatenate([win_ref[0, :, g * NSA_HEAD_DIM:(g + 1) * NSA_HEAD_DIM].astype(BF16), new_rows(4)], axis=0)
        vw = jnp.concatenate([win_ref[0, :, (NSA_KV_HEADS + g) * NSA_HEAD_DIM:(NSA_KV_HEADS + g + 1) * NSA_HEAD_DIM].astype(BF16),
                              new_rows(5)], axis=0)
        wl = lax.broadcasted_iota(jnp.int32, (rows, win_len + LANE), 1)
        dist = jnp.where(wl < win_len, win_len + tq - wl, tq - (wl - win_len))
        mask_w = (dist >= 0) & (dist < WINDOW) & (wl < win_len + lq)
        s_w = jnp.where(mask_w, _dot_nt(q, kw) + bwin_ref[g], NEG)
        e_w = jnp.where(mask_w, jnp.exp(s_w - jnp.max(s_w, -1, keepdims=True)), 0.0)
        d_w = jnp.sum(e_w, -1, keepdims=True)
        o_w = jnp.dot((e_w / jnp.where(d_w > 0, d_w, 1.0)).astype(BF16), vw, preferred_element_type=F32)

        for r in range(NSA_REP):
            rs = slice(r * lq, (r + 1) * lq)
            gt = gates[g]
            h = g * NSA_REP + r
            o_ref[:, h * LANE:(h + 1) * LANE] = (gt[:, 3 * r:3 * r + 1] * o_c[g][rs] + gt[:, 3 * r + 1:3 * r + 2] * o_s[rs]
                                                 + gt[:, 3 * r + 2:3 * r + 3] * o_w[rs]).astype(o_ref.dtype)


def _nsa_sample(y, cache_kv, layer, page_table, cache_win, cmp_pe, cmp_w, rel_bias, bsz):
    lq = y.shape[0] // bsz
    n_pages = page_table.shape[1]
    past_len = n_pages * PAGE_SIZE
    win_len = cache_win.shape[1]
    rows = NSA_REP * lq
    ncp = past_len // CMP_STRIDE
    n_cmp = (past_len + lq - CMP_LEN) // CMP_STRIDE + 1
    n_sel = -(-(past_len + lq) // SEL_LEN)
    assert lq % 8 == 0 and lq <= CMP_STRIDE and n_cmp <= ncp - 1 + lq // CMP_STRIDE and past_len % NS_CHUNK == 0
    assert past_len % SEL_LEN == 0 and n_sel <= NS_SELW and lq % Q_BLOCK != 0 and win_len == WINDOW
    tables = _nsa_sample_tables(rel_bias, past_len, lq, win_len)
    c_lo = np.arange(ncp) * CMP_STRIDE
    s_lo = np.arange(NS_SELW) * SEL_LEN
    cover = ((c_lo[:, None] < s_lo[None] + SEL_LEN) & (c_lo[:, None] + CMP_LEN > s_lo[None])
             & (np.arange(ncp)[:, None] < n_cmp) & (np.arange(NS_SELW)[None] < n_sel))
    expand = np.arange(NS_CHUNK // SEL_LEN)[:, None] == (np.arange(NS_CHUNK) // SEL_LEN)[None]
    assert cache_kv.shape[2:] == (PAGE_SIZE, 4, NSA_KV_HEADS, NSA_HEAD_DIM)
    win2 =cache_win.reshape(bsz, win_len, 2 * KVW)

    def full(shape):
        return pl.BlockSpec(shape, lambda b, pt: (0,) * len(shape))

    grid_spec = pltpu.PrefetchScalarGridSpec(
        num_scalar_prefetch=1, grid=(bsz,),
        in_specs=[pl.BlockSpec((lq, D_NSA), lambda b, pt: (b, QN_OFF // D_NSA)),
                  pl.BlockSpec((lq, 6 * KVW), lambda b, pt: (b, KV_OFF // (6 * KVW))),
                  pl.BlockSpec((lq, LANE), lambda b, pt: (b, GATE_OFF // LANE)),
                  pl.BlockSpec((lq, LANE), lambda b, pt: (b, GATE_OFF // LANE + 1)),
                  pl.BlockSpec((1, win_len, 2 * KVW), lambda b, pt: (b, 0, 0)),
                  pl.BlockSpec(memory_space=pl.ANY),
                  full((2, CMP_LEN, NSA_HEAD_DIM, NSA_HEAD_DIM)), full((2, CMP_LEN, NSA_HEAD_DIM)),
                  full((NSA_KV_HEADS, rows, ncp)), full((NSA_KV_HEADS, rows, LANE)), full((NSA_KV_HEADS, rows, LANE)),
                  full((NSA_KV_HEADS, rows, LANE)), full((NSA_KV_HEADS, rows, win_len + LANE)),
                  full((ncp, NS_SELW)), full((NS_CHUNK // SEL_LEN, NS_CHUNK))],
        out_specs=pl.BlockSpec((lq, D_NSA), lambda b, pt: (b, 0)),
        scratch_shapes=[pltpu.VMEM((2 * NSA_KV_HEADS, past_len, NSA_HEAD_DIM), F32),
                        pltpu.VMEM((2, NSA_KV_HEADS, ncp, NSA_HEAD_DIM), F32),
                        pltpu.VMEM((rows, past_len + LANE), F32), pltpu.SemaphoreType.DMA(())])
    assert QN_OFF % D_NSA == 0 and KV_OFF % (6 * KVW) == 0
    return pl.pallas_call(
        functools.partial(_nsa_sample_kernel, n_pages=n_pages, lq=lq, past_len=past_len, win_len=win_len,
                          layer=layer),
        grid_spec=grid_spec,
        out_shape=jax.ShapeDtypeStruct((bsz * lq, D_NSA), BF16),
        compiler_params=pltpu.CompilerParams(dimension_semantics=('arbitrary',), vmem_limit_bytes=VMEM_LIMIT_BYTES),
        name='nsa_sample',
    )(page_table, y, y, y, y, win2, cache_kv, cmp_w.astype(BF16), cmp_pe, *tables,
      jnp.asarray(cover, BF16), jnp.asarray(expand, BF16))


def _mm_kernel(x_ref, w_ref, o_ref, *scratch, nk, act):
    def finish(acc):
        if act == 'sqrelu':
            acc = jnp.square(jnp.maximum(acc, 0.0))
        o_ref[...] = acc.astype(o_ref.dtype)

    if nk == 1:
        finish(jnp.dot(x_ref[...], w_ref[...], preferred_element_type=F32))
        return
    acc_ref, = scratch
    k = pl.program_id(2)
    part = jnp.dot(x_ref[...], w_ref[...], preferred_element_type=F32)

    @pl.when(k == 0)
    def _():
        acc_ref[...] = part

    @pl.when(k > 0)
    def _():
        acc_ref[...] += part

    @pl.when(k == nk - 1)
    def _():
        finish(acc_ref[...])


def _pick(n, pref):
    for t in pref:
        if n % t == 0:
            return t
    return n


def _matmul(x, w, out_dtype=F32, act=None, name='matmul'):
    m, k = x.shape
    n = w.shape[1]
    tm = _pick(m, (1024, 512, 256, 128, 64))
    tn = _pick(n, (512, 256, 128))
    tk = _pick(k, (4096, 2048, 1024, 512))
    nk = k // tk
    scratch = [pltpu.VMEM((tm, tn), F32)] if nk > 1 else []
    return pl.pallas_call(
        functools.partial(_mm_kernel, nk=nk, act=act),
        grid=(m // tm, n // tn, nk),
        in_specs=[pl.BlockSpec((tm, tk), lambda i, j, kk: (i, kk)),
                  pl.BlockSpec((tk, tn), lambda i, j, kk: (kk, j))],
        out_specs=pl.BlockSpec((tm, tn), lambda i, j, kk: (i, j)),
        out_shape=jax.ShapeDtypeStruct((m, n), out_dtype),
        scratch_shapes=scratch,
        compiler_params=pltpu.CompilerParams(
            dimension_semantics=('parallel', 'parallel', 'arbitrary'),
            vmem_limit_bytes=VMEM_LIMIT_BYTES),
        name=name,
    )(x, w)


CAST_BLOCK_BYTES = 8 * 1024 * 1024


def _cast_kernel(w_ref, o_ref):
    o_ref[...] = w_ref[0].astype(o_ref.dtype)


def _to_bf16(w, layer):
    _, k, n = w.shape
    tk = CAST_BLOCK_BYTES // (4 * n)
    assert tk % 16 == 0 and k % tk == 0
    return pl.pallas_call(
        _cast_kernel, grid=(k // tk,),
        in_specs=[pl.BlockSpec((1, tk, n), lambda i: (layer, i, 0))], out_specs=pl.BlockSpec((tk, n), lambda i: (i, 0)),
        out_shape=jax.ShapeDtypeStruct((k, n), BF16),
        compiler_params=pltpu.CompilerParams(dimension_semantics=('parallel',), vmem_limit_bytes=VMEM_LIMIT_BYTES),
        name='weight_cast',
    )(w)


def _out_proj_kernel(a_ref, b_ref, c_ref, wa_ref, wb_ref, wc_ref, o_ref):
    o_ref[...] = (jnp.dot(a_ref[...], wa_ref[...], preferred_element_type=F32)
                  + jnp.dot(b_ref[...], wb_ref[...], preferred_element_type=F32)
                  + jnp.dot(c_ref[...], wc_ref[...], preferred_element_type=F32))


def _out_proj(y_ssd, y_nsa, y_gla, w):
    m = y_ssd.shape[0]
    n = w.shape[1]
    tm = _pick(m, (1024, 64))
    tn = _pick(n, (512,))
    assert D_SSD % D_NSA == 0 and D_NSA == D_GLA

    def act(width):
        return pl.BlockSpec((tm, width), lambda i, j: (i, 0))

    return pl.pallas_call(
        _out_proj_kernel, grid=(m // tm, n // tn),
        in_specs=[act(D_SSD), act(D_NSA), act(D_GLA),
                  pl.BlockSpec((D_SSD, tn), lambda i, j: (0, j)),
                  pl.BlockSpec((D_NSA, tn), lambda i, j: (D_SSD // D_NSA, j)),
                  pl.BlockSpec((D_GLA, tn), lambda i, j: (D_SSD // D_NSA + 1, j))],
        out_specs=pl.BlockSpec((tm, tn), lambda i, j: (i, j)),
        out_shape=jax.ShapeDtypeStruct((m, n), F32),
        compiler_params=pltpu.CompilerParams(dimension_semantics=('parallel', 'parallel'),
                                             vmem_limit_bytes=VMEM_LIMIT_BYTES),
        name='out_proj',
    )(y_ssd, y_nsa, y_gla, w, w, w)


def _proj(x, w_bf16, **kw):
    lead = x.shape[:-1]
    y = _matmul(x.reshape(-1, x.shape[-1]).astype(BF16), w_bf16, **kw)
    return y.reshape(lead + (w_bf16.shape[1],))


def _ln_kernel(x_ref, h_ref, g_ref, b_ref, o_ref, ob_ref):
    v = DN_ALPHA * x_ref[...] + h_ref[...]
    d = v - jnp.mean(v, -1, keepdims=True)
    y = d * lax.rsqrt(jnp.mean(d * d, -1, keepdims=True) + LN_EPS) * g_ref[...] + b_ref[...]
    o_ref[...] = y
    ob_ref[...] = y.astype(BF16)


def _add_layernorm(x, h, g, b):
    m, d = x.shape
    tm = _pick(m, (256, 64))
    row = pl.BlockSpec((tm, d), lambda i: (i, 0))
    vec = pl.BlockSpec((1, d), lambda i: (0, 0))
    return pl.pallas_call(
        _ln_kernel, grid=(m // tm,), in_specs=[row, row, vec, vec], out_specs=[row, row],
        out_shape=[jax.ShapeDtypeStruct((m, d), F32), jax.ShapeDtypeStruct((m, d), BF16)],
        compiler_params=pltpu.CompilerParams(dimension_semantics=('parallel',), vmem_limit_bytes=VMEM_LIMIT_BYTES),
        name='add_layernorm',
    )(x, h, g.reshape(1, d), b.reshape(1, d))


def _xattn_kernel(q_ref, k_ref, v_ref, o_ref):
    q = (q_ref[...] * (X_HEAD_DIM ** -0.5)).astype(BF16)
    s = _dot_nt(q, k_ref[0].astype(BF16))
    e = jnp.exp(s - jnp.max(s, -1, keepdims=True))
    p = e / jnp.sum(e, -1, keepdims=True)
    o_ref[...] = jnp.dot(p.astype(BF16), v_ref[0].astype(BF16), preferred_element_type=F32).astype(o_ref.dtype)


def _cross_attention(q, mem_kv, bsz):
    m = q.shape[0]
    l = m // bsz
    tq = _pick(l, (512, 8))
    nq = l // tq
    return pl.pallas_call(
        _xattn_kernel, grid=(bsz, X_HEADS, nq),
        in_specs=[pl.BlockSpec((tq, X_HEAD_DIM), lambda b, h, i: (b * nq + i, h)),
                  pl.BlockSpec((1, N_MEM, X_HEAD_DIM), lambda b, h, i: (b, 0, h)),
                  pl.BlockSpec((1, N_MEM, X_HEAD_DIM), lambda b, h, i: (b, 0, X_HEADS + h))],
        out_specs=pl.BlockSpec((tq, X_HEAD_DIM), lambda b, h, i: (b * nq + i, h)),
        out_shape=jax.ShapeDtypeStruct((m, D_MODEL), BF16),
        compiler_params=pltpu.CompilerParams(dimension_semantics=('parallel', 'parallel', 'parallel'),
                                             vmem_limit_bytes=VMEM_LIMIT_BYTES),
        name='cross_attention',
    )(q, mem_kv, mem_kv)


def _softplus(x):
    return jnp.maximum(x, 0.0) + jnp.log1p(jnp.exp(-jnp.abs(x)))


def _silu(x):
    return x * jax.nn.sigmoid(x)


def _pad_rows(x, rows):
    if x.shape[0] == rows:
        return x
    return jnp.concatenate([x, jnp.zeros((rows - x.shape[0],) + x.shape[1:], x.dtype)], axis=0)


def _cumsum_rows(x, seg):
    r = lax.broadcasted_iota(jnp.int32, x.shape, 0) % seg
    k = 1
    while k < seg:
        x = x + jnp.where(r >= k, pltpu.roll(x, k, 0), 0.0)
        k *= 2
    return x


def _segment_last(x, seg):
    n = x.shape[0]
    r = lax.broadcasted_iota(jnp.int32, x.shape, 0) % seg
    k = 1
    while k < seg:
        x = jnp.where(r < seg - k, pltpu.roll(x, n - k, 0), x)
        k *= 2
    return x


GLA_BLOCK = 128


def _gla_kernel(q_ref, k_ref, v_ref, g_ref, a_ref, w2_ref, gb_ref, nw_ref, s0_ref, o_ref, s_out_ref,
                s_scr, q_scr, k_scr, v_scr, bc_scr, o_scr, *, rows_in, ch):
    c = pl.program_id(1)
    nsub = -(-rows_in // ch)
    hk = GLA_DK

    @pl.when(c == 0)
    def _():
        s_scr[...] = s0_ref[0]

    valid = lax.broadcasted_iota(jnp.int32, (GLA_BLOCK, 1), 0) < rows_in
    a = _pad_rows(a_ref[...], GLA_BLOCK).astype(BF16)
    logf = -_softplus(-(jnp.dot(a, w2_ref[...], preferred_element_type=F32) + gb_ref[...])) / GLA_TAU
    logf = jnp.where(valid, logf, 0.0)
    bc = _cumsum_rows(logf, ch)
    b_last = _segment_last(bc, ch)
    q = _pad_rows(q_ref[...], GLA_BLOCK) * (GLA_DK ** -0.5)
    k = _pad_rows(k_ref[...], GLA_BLOCK)
    v = _pad_rows(v_ref[...], GLA_BLOCK)
    q_scr[...] = q
    k_scr[...] = k
    v_scr[...] = v
    bc_scr[...] = bc

    tt = lax.broadcasted_iota(jnp.int32, (ch, 1), 0)

    def intra(j, carry):
        r0 = pl.multiple_of(j * ch, ch)
        qj = q_scr[pl.ds(r0, ch), :]
        bj = bc_scr[pl.ds(r0, ch), :]
        acc = [jnp.zeros((ch, GLA_DV), F32) for _ in range(GLA_HEADS)]
        for s in range(ch):
            ks = k_scr[pl.ds(r0 + s, 1), :]
            bs = bc_scr[pl.ds(r0 + s, 1), :]
            vs = v_scr[pl.ds(r0 + s, 1), :]
            w = jnp.where(tt >= s, qj * ks * jnp.exp(jnp.minimum(bj - bs, 0.0)), 0.0)
            for h in range(GLA_HEADS):
                att = jnp.sum(w[:, h * hk:(h + 1) * hk], -1, keepdims=True)
                acc[h] = acc[h] + att * vs[:, h * GLA_DV:(h + 1) * GLA_DV]
        o_scr[pl.ds(r0, ch), :] = jnp.concatenate(acc, axis=1)
        return carry

    lax.fori_loop(0, nsub, intra, 0)

    qe = q * jnp.exp(bc)
    kst = k * jnp.exp(b_last - bc)
    dec = jnp.exp(b_last)
    o_intra = _pad_rows(o_scr[0:nsub * ch, :], GLA_BLOCK)
    chunk_of_col = lax.broadcasted_iota(jnp.int32, (hk, GLA_BLOCK), 1) // ch
    g = _pad_rows(g_ref[...], GLA_BLOCK)
    outs = []
    for h in range(GLA_HEADS):
        kst_t = kst[:, h * hk:(h + 1) * hk].T
        dec_t = dec[:, h * hk:(h + 1) * hk].T
        lhs = jnp.concatenate([jnp.where(chunk_of_col == j, kst_t, 0.0) for j in range(nsub)], axis=0).astype(BF16)
        kv = jnp.dot(lhs, v[:, h * GLA_DV:(h + 1) * GLA_DV].astype(BF16), preferred_element_type=F32)
        s_h = s_scr[h]
        s_before = []
        for j in range(nsub):
            s_before.append(s_h.astype(BF16))
            s_h = s_h * dec_t[:, j * ch:j * ch + 1] + kv[j * hk:(j + 1) * hk]
        s_scr[h] = s_h
        big = jnp.dot(qe[:, h * hk:(h + 1) * hk].astype(BF16), jnp.concatenate(s_before, axis=1), preferred_element_type=F32)
        o_inter = _pad_rows(jnp.concatenate([big[j * ch:(j + 1) * ch, j * GLA_DV:(j + 1) * GLA_DV] for j in range(nsub)], axis=0),
                            GLA_BLOCK)
        o_h = o_intra[:, h * GLA_DV:(h + 1) * GLA_DV] + o_inter
        o_h = o_h * lax.rsqrt(jnp.mean(o_h * o_h, -1, keepdims=True) + NORM_EPS) * nw_ref[...]
        outs.append(o_h)
    o = jnp.concatenate(outs, axis=1) * _silu(g)
    o_ref[...] = o[:rows_in].astype(o_ref.dtype)

    @pl.when(c == pl.num_programs(1) - 1)
    def _():
        s_out_ref[0] = s_scr[...]


def _gla(y, s0, gate_w2, gate_b, norm_w, bsz):
    m = y.shape[0]
    l = m // bsz
    rows_in = min(GLA_BLOCK, l)
    ch = GLA_CHUNK
    nblk = l // rows_in
    hdk = GLA_HEADS * GLA_DK
    w2 = jnp.zeros((LANE, hdk), BF16).at[ALR_OFF - SMALL_OFF:ALR_OFF - SMALL_OFF + GLA_RANK].set(gate_w2.astype(BF16))

    def rows(width, off):
        assert off % width == 0
        return pl.BlockSpec((rows_in, width), lambda b, c: (b * nblk + c, off // width))

    def const(shape):
        return pl.BlockSpec(shape, lambda b, c: (0,) * len(shape))

    state = pl.BlockSpec((1, GLA_HEADS, GLA_DK, GLA_DV), lambda b, c: (b, 0, 0, 0))
    return pl.pallas_call(
        functools.partial(_gla_kernel, rows_in=rows_in, ch=ch),
        grid=(bsz, nblk),
        in_specs=[rows(hdk, QG_OFF), rows(hdk, KG_OFF), rows(D_GLA, VG_OFF), rows(D_GLA, GG_OFF), rows(LANE, SMALL_OFF),
                  const((LANE, hdk)), const((1, hdk)), const((1, GLA_DV)), state],
        out_specs=[pl.BlockSpec((rows_in, D_GLA), lambda b, c: (b * nblk + c, 0)), state],
        out_shape=[jax.ShapeDtypeStruct((m, D_GLA), BF16), jax.ShapeDtypeStruct((bsz, GLA_HEADS, GLA_DK, GLA_DV), F32)],
        scratch_shapes=[pltpu.VMEM((GLA_HEADS, GLA_DK, GLA_DV), F32), pltpu.VMEM((GLA_BLOCK, hdk), F32),
                        pltpu.VMEM((GLA_BLOCK, hdk), F32), pltpu.VMEM((GLA_BLOCK, D_GLA), F32),
                        pltpu.VMEM((GLA_BLOCK, hdk), F32), pltpu.VMEM((GLA_BLOCK, D_GLA), F32)],
        compiler_params=pltpu.CompilerParams(dimension_semantics=('parallel', 'arbitrary'),
                                             vmem_limit_bytes=VMEM_LIMIT_BYTES),
        name='gla',
    )(y, y, y, y, y, w2, gate_b.reshape(1, hdk), norm_w.reshape(1, GLA_DV), s0)


SSD_R = SSD_HEADS // SSD_GROUPS
SSD_GW = SSD_R * SSD_HEAD_DIM
SSD_CW = SSD_GW + 2 * SSD_STATE
SSD_TAIL = 8


def _ssd_kernel(xs_ref, bm_ref, cm_ref, z_ref, dt_ref, conv0_ref, cw_ref, cb_ref, hp_ref, dskip_ref, nw_ref, h0_ref,
                o_ref, h_out_ref, xbuf, h_scr, *, rows_in):
    g = pl.program_id(1)
    c = pl.program_id(2)
    t = SSD_CHUNK

    @pl.when(c == 0)
    def _():
        xbuf[0:SSD_TAIL, :] = conv0_ref[0, 0]
        h_scr[...] = h0_ref[0, 0]

    xbuf[SSD_TAIL:SSD_TAIL + t, 0:SSD_GW] = _pad_rows(xs_ref[...], t)
    xbuf[SSD_TAIL:SSD_TAIL + t, SSD_GW:SSD_GW + SSD_STATE] = _pad_rows(bm_ref[...], t)
    xbuf[SSD_TAIL:SSD_TAIL + t, SSD_GW + SSD_STATE:SSD_CW] = _pad_rows(cm_ref[...], t)
    conv = cb_ref[0]
    for kk in range(SSD_CONV):
        conv = conv + cw_ref[0, kk:kk + 1, :] * xbuf[SSD_TAIL - (SSD_CONV - 1) + kk:SSD_TAIL - (SSD_CONV - 1) + kk + t, :]
    xbuf[0:SSD_TAIL, :] = xbuf[t:t + SSD_TAIL, :]
    xc = _silu(conv)
    xs = xc[:, 0:SSD_GW]
    bm = xc[:, SSD_GW:SSD_GW + SSD_STATE]
    cm = xc[:, SSD_GW + SSD_STATE:SSD_CW].astype(BF16)

    lane = lax.broadcasted_iota(jnp.int32, (t, LANE), 1)
    row = lax.broadcasted_iota(jnp.int32, (t, LANE), 0)
    dt = pltpu.roll(_pad_rows(dt_ref[...], t), (LANE - SSD_R * g) % LANE, 1)
    dt = jnp.where((lane < SSD_R) & (row < rows_in), _softplus(dt + hp_ref[0, 0:1, :]), 0.0)
    acs = _cumsum_rows(dt * -jnp.exp(hp_ref[0, 1:2, :]), t)
    acs_t = acs.T
    a_last = acs[t - 1:t, :]

    head_of_lane = lax.broadcasted_iota(jnp.int32, (1, SSD_GW), 1) // SSD_HEAD_DIM

    def spread(arr):
        return sum(jnp.where(head_of_lane == r, arr[:, r:r + 1], 0.0) for r in range(SSD_R))

    xdt = xs * spread(dt)
    cb = _dot_nt(cm, bm.astype(BF16))
    tri = lax.broadcasted_iota(jnp.int32, (t, t), 0) >= lax.broadcasted_iota(jnp.int32, (t, t), 1)
    y = jnp.zeros((t, SSD_GW), F32)
    for r in range(SSD_R):
        decay = jnp.where(tri, jnp.exp(jnp.minimum(acs[:, r:r + 1] - acs_t[r:r + 1, :], 0.0)), 0.0)
        x_r = jnp.where(head_of_lane == r, xdt, 0.0).astype(BF16)
        y = y + jnp.dot((cb * decay).astype(BF16), x_r, preferred_element_type=F32)
    h_prev = h_scr[...]
    y = y + jnp.dot(cm, h_prev.astype(BF16), preferred_element_type=F32) * jnp.exp(spread(acs))
    x_st = (xdt * jnp.exp(spread(a_last - acs))).astype(BF16)
    h_new = h_prev * jnp.exp(spread(a_last)) + jnp.dot(bm.T.astype(BF16), x_st, preferred_element_type=F32)
    h_scr[...] = h_new

    y = (y + xs * dskip_ref[0]) * _silu(_pad_rows(z_ref[...], t))
    y = y * lax.rsqrt(jnp.mean(y * y, -1, keepdims=True) + NORM_EPS) * nw_ref[0]
    o_ref[...] = y[:rows_in].astype(o_ref.dtype)

    @pl.when(c == pl.num_programs(2) - 1)
    def _():
        h_out_ref[0, 0] = h_new


def _ssd(y, conv0, h0, conv_w, conv_b, dt_bias, a_log, d_skip, norm_w, bsz):
    m = y.shape[0]
    l = m // bsz
    rows_in = min(SSD_CHUNK, l)
    nblk = l // rows_in
    ng = SSD_GROUPS

    def per_group(v):
        xs = v[..., :D_SSD].reshape(v.shape[:-1] + (ng, SSD_GW))
        bm = v[..., D_SSD:D_SSD + ng * SSD_STATE].reshape(v.shape[:-1] + (ng, SSD_STATE))
        cm = v[..., D_SSD + ng * SSD_STATE:].reshape(v.shape[:-1] + (ng, SSD_STATE))
        return jnp.moveaxis(jnp.concatenate([xs, bm, cm], axis=-1), -2, 0)

    conv0_g = jnp.moveaxis(per_group(jnp.pad(conv0, ((0, 0), (SSD_TAIL - (SSD_CONV - 1), 0), (0, 0)))), 0, 1)
    cw_g = per_group(conv_w)
    cb_g = per_group(conv_b[None])
    hp = jnp.zeros((ng, 8, LANE), F32)
    hp = hp.at[:, 0, :SSD_R].set(dt_bias.reshape(ng, SSD_R)).at[:, 1, :SSD_R].set(a_log.reshape(ng, SSD_R))
    dskip_g = jnp.repeat(d_skip, SSD_HEAD_DIM).reshape(ng, 1, SSD_GW)
    nw_g = norm_w.reshape(ng, 1, SSD_GW)
    h0_t = h0.reshape(bsz, ng, SSD_GW, SSD_STATE).transpose(0, 1, 3, 2)

    def rows(width, off):
        assert off % width == 0
        return pl.BlockSpec((rows_in, width), lambda b, g, c: (b * nblk + c, off // width + g))

    def grp(shape):
        return pl.BlockSpec((1,) + shape, lambda b, g, c: (g,) + (0,) * len(shape))

    state = pl.BlockSpec((1, 1, SSD_STATE, SSD_GW), lambda b, g, c: (b, g, 0, 0))
    out, h_t = pl.pallas_call(
        functools.partial(_ssd_kernel, rows_in=rows_in),
        grid=(bsz, ng, nblk),
        in_specs=[rows(SSD_GW, XBC_OFF), rows(SSD_STATE, XBC_OFF + D_SSD), rows(SSD_STATE, XBC_OFF + D_SSD + ng * SSD_STATE),
                  rows(SSD_GW, Z_OFF), pl.BlockSpec((rows_in, LANE), lambda b, g, c: (b * nblk + c, SMALL_OFF // LANE)),
                  pl.BlockSpec((1, 1, SSD_TAIL, SSD_CW), lambda b, g, c: (b, g, 0, 0)),
                  grp((SSD_CONV, SSD_CW)), grp((1, SSD_CW)), grp((8, LANE)), grp((1, SSD_GW)), grp((1, SSD_GW)), state],
        out_specs=[pl.BlockSpec((rows_in, SSD_GW), lambda b, g, c: (b * nblk + c, g)), state],
        out_shape=[jax.ShapeDtypeStruct((m, D_SSD), BF16), jax.ShapeDtypeStruct((bsz, ng, SSD_STATE, SSD_GW), F32)],
        scratch_shapes=[pltpu.VMEM((SSD_TAIL + SSD_CHUNK, SSD_CW), F32), pltpu.VMEM((SSD_STATE, SSD_GW), F32)],
        compiler_params=pltpu.CompilerParams(dimension_semantics=('parallel', 'parallel', 'arbitrary'),
                                             vmem_limit_bytes=VMEM_LIMIT_BYTES),
        name='ssd',
    )(y, y, y, y, y, conv0_g, cw_g, cb_g, hp, dskip_g, nw_g, h0_t)
    h_new = h_t.transpose(0, 1, 3, 2).reshape(bsz, SSD_HEADS, SSD_HEAD_DIM, SSD_STATE)
    return out, h_new


def _split(x, sizes):
    return jnp.split(x, np.cumsum(sizes)[:-1].tolist(), axis=-1)


def _layernorm(x, g, b):
    mu = jnp.mean(x, -1, keepdims=True)
    var = jnp.mean(jnp.square(x - mu), -1, keepdims=True)
    return (x - mu) * lax.rsqrt(var + LN_EPS) * g + b


def _rmsnorm(x, w):
    return x * lax.rsqrt(jnp.mean(x * x, -1, keepdims=True) + NORM_EPS) * w


def _masked_softmax(s, mask):
    s = jnp.where(mask, s.astype(F32), -jnp.inf)
    m = jnp.max(s, -1, keepdims=True)
    m = jnp.where(jnp.isfinite(m), m, 0.0)
    e = jnp.exp(s - m)
    d = jnp.sum(e, -1, keepdims=True)
    return e / jnp.where(d > 0, d, 1.0)


def _t5_bucket(dist):
    n = jnp.maximum(dist, 0)
    max_exact = N_BUCKETS // 2
    nf = jnp.maximum(n, 1).astype(F32)
    large = max_exact + (jnp.log(nf / max_exact) / math.log(MAX_DISTANCE / max_exact) * (N_BUCKETS - max_exact)).astype(jnp.int32)
    large = jnp.minimum(large, N_BUCKETS - 1)
    return jnp.where(n < max_exact, n, large)


def _ssd_scan(x, dt, a, bm, cm, h0):
    b, l, h, p = x.shape
    g, n = bm.shape[2], bm.shape[3]
    r = h // g
    lc = min(SSD_CHUNK, l)
    pad = (-l) % lc
    x, dt, bm, cm = [jnp.pad(t.astype(F32), ((0, 0), (0, pad)) + ((0, 0),) * (t.ndim - 2)) for t in (x, dt, bm, cm)]
    c = (l + pad) // lc
    xdt = (x * dt[..., None]).reshape(b, c, lc, g, r, p)
    acs = jnp.cumsum((dt * a).reshape(b, c, lc, g, r), axis=2)
    bm = bm.reshape(b, c, lc, g, n)
    cm = cm.reshape(b, c, lc, g, n)
    causal = np.tril(np.ones((lc, lc), bool))[None, None, :, :, None, None]
    decay = jnp.exp(jnp.where(causal, acs[:, :, :, None] - acs[:, :, None], -jnp.inf))
    cb = jnp.einsum('bctgn,bcsgn->bctsg', cm, bm)
    y_diag = jnp.einsum('bctsg,bctsgr,bcsgrp->bctgrp', cb, decay, xdt)
    decay_st = jnp.exp(acs[:, :, -1:] - acs)
    states = jnp.einsum('bcsgn,bcsgr,bcsgrp->bcgrpn', bm, decay_st, xdt)
    chunk_decay = jnp.exp(acs[:, :, -1])

    def step(hc, inp):
        st, dec = inp
        return hc * dec[..., None, None] + st, hc

    h_last, h_prev = lax.scan(step, h0.astype(F32).reshape(b, g, r, p, n), (jnp.moveaxis(states, 1, 0), jnp.moveaxis(chunk_decay, 1, 0)))
    h_prev = jnp.moveaxis(h_prev, 0, 1)
    y_off = jnp.einsum('bctgn,bcgrpn,bctgr->bctgrp', cm, h_prev, jnp.exp(acs))
    y = (y_diag + y_off).reshape(b, c * lc, h, p)[:, :l]
    return y, h_last.reshape(b, h, p, n)


def _ssd_mixer(z, xbc, dt, conv0, h0, conv_w, conv_b, dt_bias, a_log, d_skip, norm_w):
    b, l, _ = xbc.shape
    xpad = jnp.concatenate([conv0.astype(xbc.dtype), xbc], axis=1)
    conv = sum((xpad[:, k:k + l] * conv_w[k] for k in range(SSD_CONV)), conv_b)
    conv_new = xpad[:, -(SSD_CONV - 1):]
    xbc = jax.nn.silu(conv)
    xs, bm, cm = _split(xbc, (D_SSD, SSD_GROUPS * SSD_STATE, SSD_GROUPS * SSD_STATE))
    xs = xs.reshape(b, l, SSD_HEADS, SSD_HEAD_DIM)
    bm = bm.reshape(b, l, SSD_GROUPS, SSD_STATE)
    cm = cm.reshape(b, l, SSD_GROUPS, SSD_STATE)
    dt = jax.nn.softplus((dt + dt_bias).astype(F32))
    a = -jnp.exp(a_log.astype(F32))
    y, h_new = _ssd_scan(xs, dt, a, bm, cm, h0)
    y = y + xs.astype(F32) * d_skip.astype(F32)[:, None]
    y = (y.reshape(b, l, D_SSD) * jax.nn.silu(z.astype(F32))).reshape(b, l, SSD_GROUPS, D_SSD // SSD_GROUPS)
    y = _rmsnorm(y, norm_w.reshape(SSD_GROUPS, D_SSD // SSD_GROUPS)).reshape(b, l, D_SSD)
    return y, conv_new, h_new


def _gla_scan(q, k, v, logf, s0):
    b, l, h, dk = q.shape
    dv = v.shape[-1]
    lc = min(GLA_CHUNK, l)
    pad = (-l) % lc
    q, k, v, logf = [jnp.pad(t.astype(F32), ((0, 0), (0, pad), (0, 0), (0, 0))) for t in (q, k, v, logf)]
    c = (l + pad) // lc
    q, k, logf = [t.reshape(b, c, lc, h, dk) for t in (q, k, logf)]
    v = v.reshape(b, c, lc, h, dv)
    bc = jnp.cumsum(logf, axis=2)
    causal = np.tril(np.ones((lc, lc), bool))[None, None, :, :, None, None]
    rel = jnp.exp(jnp.where(causal, bc[:, :, :, None] - bc[:, :, None], -jnp.inf))
    att = jnp.einsum('bcthd,bcshd,bctshd->bchts', q, k, rel)
    o_intra = jnp.einsum('bchts,bcshv->bcthv', att, v)
    k_st = k * jnp.exp(bc[:, :, -1:] - bc)
    chunk_states = jnp.einsum('bcshd,bcshv->bchdv', k_st, v)
    chunk_decay = jnp.exp(bc[:, :, -1])

    def step(s, inp):
        st, dec = inp
        return s * dec[..., None] + st, s

    s_last, s_prev = lax.scan(step, s0.astype(F32), (jnp.moveaxis(chunk_states, 1, 0), jnp.moveaxis(chunk_decay, 1, 0)))
    s_prev = jnp.moveaxis(s_prev, 0, 1)
    o_inter = jnp.einsum('bcthd,bchdv->bcthv', q * jnp.exp(bc), s_prev)
    o = (o_intra + o_inter).reshape(b, c * lc, h, dv)[:, :l]
    return o, s_last


def _gla_mixer(q, k, v, g, a_lr, s0, gate_w2, gate_b, norm_w):
    b, l, _ = q.shape
    q = q.reshape(b, l, GLA_HEADS, GLA_DK) * (GLA_DK ** -0.5)
    k = k.reshape(b, l, GLA_HEADS, GLA_DK)
    v = v.reshape(b, l, GLA_HEADS, GLA_DV)
    logf = jax.nn.log_sigmoid((a_lr @ gate_w2 + gate_b).astype(F32)) / GLA_TAU
    o, s_new = _gla_scan(q, k, v, logf.reshape(b, l, GLA_HEADS, GLA_DK), s0)
    o = _rmsnorm(o, norm_w).reshape(b, l, D_GLA) * jax.nn.silu(g.astype(F32))
    return o, s_new


def _nsa_attend(q, gates, kv_full, win_ctx, q_pos0, w_pos0, cmp_pe, cmp_w, rel_bias):
    b, lq = q.shape[:2]
    t_len = kv_full.shape[1]
    G, R, HD = NSA_KV_HEADS, NSA_REP, NSA_HEAD_DIM
    kc, vc, ks, vs = [kv_full[:, :, i] for i in range(4)]
    n_cmp = (t_len - CMP_LEN) // CMP_STRIDE + 1
    cidx = (np.arange(n_cmp)[:, None] * CMP_STRIDE + np.arange(CMP_LEN)[None]).astype(np.int32)
    k_cmp = jnp.einsum('bnlgd,lde->bnge', kc[:, cidx] + cmp_pe[0][:, None], cmp_w[0]).astype(F32)
    v_cmp = jnp.einsum('bnlgd,lde->bnge', vc[:, cidx] + cmp_pe[1][:, None], cmp_w[1]).astype(F32)
    cmp_end = (np.arange(n_cmp) * CMP_STRIDE + CMP_LEN - 1).astype(np.int32)
    n_sel = -(-t_len // SEL_LEN)
    sel_pad = n_sel * SEL_LEN - t_len

    def sel_blocks(t):
        t = jnp.pad(t, ((0, 0), (0, sel_pad), (0, 0), (0, 0)))
        return t.reshape(b, n_sel, SEL_LEN, G, HD).transpose(0, 3, 1, 2, 4)

    ks_b, vs_b = sel_blocks(ks), sel_blocks(vs)
    c_lo = np.arange(n_cmp) * CMP_STRIDE
    s_lo = np.arange(n_sel) * SEL_LEN
    cover = jnp.asarray(((c_lo[:, None] < s_lo[None] + SEL_LEN) & (c_lo[:, None] + CMP_LEN > s_lo[None])).astype(np.float32))
    top_n = min(TOP_N, n_sel)
    kw_pad = jnp.pad(win_ctx[:, :, 0], ((0, 0), (WINDOW, 0), (0, 0), (0, 0)))
    vw_pad = jnp.pad(win_ctx[:, :, 1], ((0, 0), (WINDOW, 0), (0, 0), (0, 0)))
    qb = Q_BLOCK if lq % Q_BLOCK == 0 else lq
    nqb = lq // qb
    scale = HD ** -0.5
    bias_gr = rel_bias.reshape(N_BUCKETS, G, R)
    bi = jnp.arange(b)[:, None, None, None]
    gi = jnp.arange(G)[None, None, :, None]
    gi5 = jnp.arange(G)[None, None, :, None, None]
    blk = jnp.arange(n_sel)

    def block(args):
        i, qblk, gblk = args
        q0 = q_pos0 + i * qb
        tpos = q0 + jnp.arange(qb)
        qg = qblk.reshape(b, qb, G, R, HD).astype(F32) * scale
        bias_c = rel_bias[_t5_bucket(tpos[:, None] - cmp_end[None])].astype(F32).reshape(qb, n_cmp, G, R).transpose(0, 2, 3, 1)
        s_c = jnp.einsum('bqgrd,bngd->bqgrn', qg, k_cmp) + bias_c
        p_c = _masked_softmax(s_c, (cmp_end[None] <= tpos[:, None])[None, :, None, None, :])
        o_c = jnp.einsum('bqgrn,bngd->bqgrd', p_c, v_cmp)
        imp = jnp.einsum('bqgrn,nj->bqgj', p_c, cover)
        cur = (tpos // SEL_LEN)[:, None]
        forced = (blk == 0) | (blk == cur) | (blk == cur - 1)
        valid = blk * SEL_LEN <= tpos[:, None]
        score = jnp.where(forced[None, :, None, :], FORCE_SCORE, jnp.where(valid[None, :, None, :], imp, -FORCE_SCORE))
        _, sel = lax.top_k(score, top_n)
        k_sel = ks_b[bi, gi, sel].astype(F32)
        v_sel = vs_b[bi, gi, sel].astype(F32)
        spos = sel[..., None] * SEL_LEN + jnp.arange(SEL_LEN)
        dist_s = tpos[None, :, None, None, None] - spos
        bias_s = jnp.moveaxis(bias_gr[_t5_bucket(dist_s), gi5].astype(F32), -1, 3)
        s_s = jnp.einsum('bqgrd,bqgksd->bqgrks', qg, k_sel) + bias_s
        p_s = _masked_softmax(s_s.reshape(b, qb, G, R, top_n * SEL_LEN), (dist_s >= 0).reshape(b, qb, G, 1, top_n * SEL_LEN)).reshape(s_s.shape)
        o_s = jnp.einsum('bqgrks,bqgksd->bqgrd', p_s, v_sel)
        start = q0 - w_pos0
        k_w = lax.dynamic_slice_in_dim(kw_pad, start, WINDOW + qb, axis=1).astype(F32)
        v_w = lax.dynamic_slice_in_dim(vw_pad, start, WINDOW + qb, axis=1).astype(F32)
        wpos = q0 - WINDOW + jnp.arange(WINDOW + qb)
        dist_w = tpos[:, None] - wpos[None]
        mask_w = (dist_w >= 0) & (dist_w < WINDOW) & (wpos[None] >= w_pos0)
        bias_w = rel_bias[_t5_bucket(dist_w)].astype(F32).reshape(qb, WINDOW + qb, G, R).transpose(0, 2, 3, 1)
        s_w = jnp.einsum('bqgrd,bkgd->bqgrk', qg, k_w) + bias_w
        p_w = _masked_softmax(s_w, mask_w[None, :, None, None, :])
        o_w = jnp.einsum('bqgrk,bkgd->bqgrd', p_w, v_w)
        gg = gblk.reshape(b, qb, G, R, 3).astype(F32)
        return gg[..., 0:1] * o_c + gg[..., 1:2] * o_s + gg[..., 2:3] * o_w

    q_blocks = jnp.swapaxes(q.reshape(b, nqb, qb, NSA_HEADS, HD), 0, 1)
    g_blocks = jnp.swapaxes(gates.reshape(b, nqb, qb, NSA_HEADS, 3), 0, 1)
    outs = lax.map(block, (jnp.arange(nqb), q_blocks, g_blocks))
    return jnp.moveaxis(outs, 0, 1).reshape(b, lq, D_NSA)


def _mixer(x, lp, rel_bias, tables, conv0, ssd_h0, gla_s0, nsa_past, win_past, past_len):
    b, l, _ = x.shape
    y = _matmul(x.reshape(b * l, D_MODEL).astype(BF16), lp['w_in'], name='in_proj')
    y3 = y.reshape(b, l, D_IN_PAD)

    def part(off, width):
        return y3[..., off:off + width]

    z, xbc, dt = part(Z_OFF, D_SSD), part(XBC_OFF, SSD_CONV_DIM), part(DT_OFF, SSD_HEADS)
    q_g, k_g = part(QG_OFF, GLA_HEADS * GLA_DK), part(KG_OFF, GLA_HEADS * GLA_DK)
    v_g, g_g, a_g = part(VG_OFF, D_GLA), part(GG_OFF, D_GLA), part(ALR_OFF, GLA_RANK)
    y_ssd, conv_new, h_new = _ssd_mixer(z, xbc, dt, conv0, ssd_h0, lp['ssd_conv_w'], lp['ssd_conv_b'], lp['ssd_dt_bias'], lp['ssd_a_log'], lp['ssd_d'], lp['ssd_norm_w'])
    rows = part(KV_OFF, 4 * KVW).reshape(b, l, 4, NSA_KV_HEADS, NSA_HEAD_DIM)
    win_rows = part(KV_OFF + 4 * KVW, 2 * KVW).reshape(b, l, 2, NSA_KV_HEADS, NSA_HEAD_DIM)
    if nsa_past is None:
        win_new = win_rows[:, -min(WINDOW, l):]
        y_nsa = _nsa_prompt(y, _nsa_compress_prompt(y, lp['nsa_cmp_pe'], lp['nsa_cmp_w'], b), tables, b).reshape(b, l, D_NSA)
    else:
        kv_full = jnp.concatenate([nsa_past.astype(rows.dtype), rows], axis=1)
        win_ctx = jnp.concatenate([win_past.astype(rows.dtype), win_rows], axis=1)
        win_new = win_ctx[:, -win_past.shape[1]:]
        g_n = jnp.concatenate([part(GATE_OFF + g * LANE, 3 * NSA_REP) for g in range(NSA_KV_HEADS)], axis=-1)
        gates = jax.nn.sigmoid(g_n).reshape(b, l, NSA_HEADS, 3)
        q_n = part(QN_OFF, D_NSA).reshape(b, l, NSA_HEADS, NSA_HEAD_DIM)
        y_nsa = _nsa_attend(q_n, gates, kv_full, win_ctx, past_len, past_len - win_past.shape[1], lp['nsa_cmp_pe'], lp['nsa_cmp_w'], rel_bias)
    y_gla, s_new = _gla_mixer(q_g, k_g, v_g, g_g, a_g, gla_s0, lp['gla_gate_w2'], lp['gla_gate_b'], lp['gla_norm_w'])
    y_out = _proj(jnp.concatenate([y_ssd, y_nsa, y_gla], axis=-1), lp['w_out'], name='out_proj')
    return y_out, rows, win_new, conv_new, h_new, s_new


def _cross_attn(x, mem_kv, wq, wo):
    b, l, _ = x.shape
    q = _proj(x, wq, name='xattn_q').reshape(b, l, X_HEADS, X_HEAD_DIM) * (X_HEAD_DIM ** -0.5)
    s = jnp.einsum('blhd,bmhd->bhlm', q, mem_kv[:, :, 0].astype(F32))
    p = jax.nn.softmax(s, axis=-1)
    o = jnp.einsum('bhlm,bmhd->blhd', p, mem_kv[:, :, 1].astype(F32)).reshape(b, l, D_MODEL)
    return _proj(o, wo, name='xattn_o')


def _sqrelu_ffn(x, w1, w2):
    h = _proj(x, w1, out_dtype=BF16, act='sqrelu', name='ffn_up')
    return _proj(h, w2, name='ffn_down')


def _layer(x, lp, rel_bias, tables, mem_kv, conv0, ssd_h0, gla_s0, nsa_past, win_past, past_len):
    h, rows, win, conv, hs, sg = _mixer(x, lp, rel_bias, tables, conv0, ssd_h0, gla_s0, nsa_past, win_past, past_len)
    x = _layernorm(DN_ALPHA * x + h, lp['ln_g'][0], lp['ln_b'][0])
    x = _layernorm(DN_ALPHA * x + _cross_attn(x, mem_kv, lp['x_wq'], lp['x_wo']), lp['ln_g'][1], lp['ln_b'][1])
    x = _layernorm(DN_ALPHA * x + _sqrelu_ffn(x, lp['ffn_w1'], lp['ffn_w2']), lp['ln_g'][2], lp['ln_b'][2])
    return x, (rows, win, conv, hs, sg)


def _layer_flat(x, xb, bsz, lp, rel_bias, tables, mem_kv, conv0, ssd_h0, gla_s0, nsa_past, win_past, past_len):
    m = x.shape[0]
    l = m // bsz
    assert l >= SSD_CONV - 1
    y = _matmul(xb, lp['w_in'], name='in_proj')
    y3 = y.reshape(bsz, l, D_IN_PAD)
    y_ssd, h_new = _ssd(y, conv0, ssd_h0, lp['ssd_conv_w'], lp['ssd_conv_b'], lp['ssd_dt_bias'], lp['ssd_a_log'],
                        lp['ssd_d'], lp['ssd_norm_w'], bsz)
    conv_new = y3[:, l - (SSD_CONV - 1):, XBC_OFF:XBC_OFF + SSD_CONV_DIM]
    y_gla, s_new = _gla(y, gla_s0, lp['gla_gate_w2'], lp['gla_gate_b'], lp['gla_norm_w'], bsz)
    rows = y3[..., KV_OFF:KV_OFF + 4 * KVW].reshape(bsz, l, 4, NSA_KV_HEADS, NSA_HEAD_DIM)
    win_rows = y3[..., KV_OFF + 4 * KVW:KV_OFF + 6 * KVW].reshape(bsz, l, 2, NSA_KV_HEADS, NSA_HEAD_DIM)
    if nsa_past is None:
        win_new = win_rows[:, -min(WINDOW, l):]
        y_nsa = _nsa_prompt_t(y, _nsa_compress_prompt(y, lp['nsa_cmp_pe'], lp['nsa_cmp_w'], bsz), tables, bsz)
    else:
        cache_kv, layer, page_table = nsa_past
        win_new = jnp.concatenate([win_past, win_rows], axis=1)[:, -win_past.shape[1]:]
        y_nsa = _nsa_sample(y, cache_kv, layer, page_table, win_past, lp['nsa_cmp_pe'], lp['nsa_cmp_w'], rel_bias, bsz)
    h = _out_proj(y_ssd, y_nsa, y_gla, lp['w_out'])
    x, xb = _add_layernorm(x, h, lp['ln_g'][0], lp['ln_b'][0])
    o = _cross_attention(_matmul(xb, lp['x_wq'], name='xattn_q'), mem_kv, bsz)
    x, xb = _add_layernorm(x, _matmul(o, lp['x_wo'], name='xattn_o'), lp['ln_g'][1], lp['ln_b'][1])
    hidden = _matmul(xb, lp['ffn_w1'], out_dtype=BF16, act='sqrelu', name='ffn_up')
    x, xb = _add_layernorm(x, _matmul(hidden, lp['ffn_w2'], name='ffn_down'), lp['ln_g'][2], lp['ln_b'][2])
    return x, xb, (rows, win_new, conv_new, h_new, s_new)


def kernel(x_prompt, x_sample, cache_nsa_kv, cache_nsa_win, state_ssd, state_ssd_conv, state_gla, cache_mem_kv,
           page_table, mem_prompt, w_in, ssd_conv_w, ssd_conv_b, ssd_dt_bias, ssd_a_log, ssd_d, ssd_norm_w,
           nsa_cmp_pe, nsa_cmp_w, rel_bias, gla_gate_w2, gla_gate_b, gla_norm_w, w_out, x_wq, x_wkv, x_wo,
           ffn_w1, ffn_w2, ln_g, ln_b):
    bp = x_prompt.shape[0]
    bs = x_sample.shape[0]
    past_len = page_table.shape[1] * cache_nsa_kv.shape[2]
    xp, xs = x_prompt.reshape(-1, D_MODEL), x_sample.reshape(-1, D_MODEL)
    xpb, xsb = xp.astype(BF16), xs.astype(BF16)
    st_p, st_s, mem_p = [], [], []
    tables = _nsa_prompt_bias_tables_t(rel_bias)
    for l in range(DEPTH):
        lp = dict(w_in=_reorder_w_in(w_in[l]), ssd_conv_w=ssd_conv_w[l], ssd_conv_b=ssd_conv_b[l],
                  ssd_dt_bias=ssd_dt_bias[l], ssd_a_log=ssd_a_log[l], ssd_d=ssd_d[l], ssd_norm_w=ssd_norm_w[l],
                  nsa_cmp_pe=nsa_cmp_pe[l], nsa_cmp_w=nsa_cmp_w[l], gla_gate_w2=gla_gate_w2[l], gla_gate_b=gla_gate_b[l],
                  gla_norm_w=gla_norm_w[l], w_out=_to_bf16(w_out, l), x_wq=_to_bf16(x_wq, l),
                  x_wo=_to_bf16(x_wo, l), ffn_w1=_to_bf16(ffn_w1, l), ffn_w2=_to_bf16(ffn_w2, l),
                  ln_g=ln_g[l], ln_b=ln_b[l])
        mem_kv_p = _proj(mem_prompt, _to_bf16(x_wkv, l), name='mem_kv')
        conv0 = jnp.zeros((bp, SSD_CONV - 1, SSD_CONV_DIM), F32)
        h0 = jnp.zeros((bp, SSD_HEADS, SSD_HEAD_DIM, SSD_STATE), F32)
        s0 = jnp.zeros((bp, GLA_HEADS, GLA_DK, GLA_DV), F32)
        xp, xpb, stp = _layer_flat(xp, xpb, bp, lp, rel_bias, tables, mem_kv_p, conv0, h0, s0, None, None, 0)
        st_p.append(stp)
        mem_p.append(mem_kv_p.reshape(bp, N_MEM, 2, X_HEADS, X_HEAD_DIM))
        xs, xsb, sts = _layer_flat(xs, xsb, bs, lp, rel_bias, tables, cache_mem_kv[l].reshape(bs, N_MEM, 2 * D_MODEL),
                                   state_ssd_conv[l], state_ssd[l], state_gla[l], (cache_nsa_kv, l, page_table),
                                   cache_nsa_win[l], past_len)
        st_s.append(sts)
    p_rows, p_win, p_conv, p_ssd, p_gla = [jnp.stack(s) for s in zip(*st_p)]
    s_rows, s_win, s_conv, s_ssd, s_gla = [jnp.stack(s) for s in zip(*st_s)]
    p_mem = jnp.stack(mem_p)
    return (xp.reshape(x_prompt.shape), xs.reshape(x_sample.shape), p_rows, s_rows, p_win, s_win, p_ssd, s_ssd,
            p_conv, s_conv, p_gla, s_gla, p_mem)
```

```python
import functools
import math

import jax
import jax.numpy as jnp
import numpy as np
from jax import lax
from jax.experimental import pallas as pl
from jax.experimental.pallas import tpu as pltpu

F32 = jnp.float32
BF16 = jnp.bfloat16

D_MODEL = 4096
DEPTH = 2
PAGE_SIZE = 128
D_MIX = D_MODEL
D_SSD = D_MIX // 2
SSD_HEAD_DIM = 64
SSD_HEADS = D_SSD // SSD_HEAD_DIM
SSD_GROUPS = 8
SSD_STATE = 128
SSD_CONV = 4
SSD_CHUNK = 128
SSD_CONV_DIM = D_SSD + 2 * SSD_GROUPS * SSD_STATE
D_NSA = D_MIX // 4
NSA_HEAD_DIM = 128
NSA_HEADS = D_NSA // NSA_HEAD_DIM
NSA_KV_HEADS = 2
NSA_REP = NSA_HEADS // NSA_KV_HEADS
CMP_LEN = 32
CMP_STRIDE = 16
SEL_LEN = 64
TOP_N = 16
WINDOW = 512
Q_BLOCK = 128
FORCE_SCORE = 1e4
D_GLA = D_MIX - D_SSD - D_NSA
GLA_HEADS = 4
GLA_DV = D_GLA // GLA_HEADS
GLA_DK = GLA_DV // 2
GLA_RANK = 16
GLA_TAU = 16.0
GLA_CHUNK = 16
N_MEM = 256
X_HEADS = 4
X_HEAD_DIM = D_MODEL // X_HEADS
D_FF = 4 * D_MODEL
N_BUCKETS = 32
MAX_DISTANCE = 128
LN_EPS = 1e-5
NORM_EPS = 1e-6
DN_ALPHA = (2 * DEPTH) ** 0.25
IN_SPLITS = (D_SSD, SSD_CONV_DIM, SSD_HEADS, NSA_HEADS * NSA_HEAD_DIM) + (NSA_KV_HEADS * NSA_HEAD_DIM,) * 6 + (
    3 * NSA_HEADS, GLA_HEADS * GLA_DK, GLA_HEADS * GLA_DK, D_GLA, D_GLA, GLA_RANK)
D_IN = sum(IN_SPLITS)

LANE = 128
VMEM_LIMIT_BYTES = 56 * 1024 * 1024

KVW = NSA_KV_HEADS * NSA_HEAD_DIM
Z_OFF = 0
XBC_OFF = Z_OFF + D_SSD
QN_OFF = XBC_OFF + SSD_CONV_DIM
VG_OFF = QN_OFF + NSA_HEADS * NSA_HEAD_DIM
GG_OFF = VG_OFF + D_GLA
KV_OFF = GG_OFF + D_GLA
QG_OFF = KV_OFF + 6 * KVW
KG_OFF = QG_OFF + GLA_HEADS * GLA_DK
SMALL_OFF = KG_OFF + GLA_HEADS * GLA_DK
DT_OFF = SMALL_OFF
ALR_OFF = SMALL_OFF + SSD_HEADS
GATE_OFF = SMALL_OFF + LANE
D_IN_PAD = 12288
NEG = -1e30


def _in_proj_column_map():
    off = np.cumsum((0,) + IN_SPLITS)
    z, xbc, dt, qn = off[0], off[1], off[2], off[3]
    kv0, gn, qg, kg, vg, gg, alr = off[4], off[10], off[11], off[12], off[13], off[14], off[15]
    cmap = np.full((D_IN_PAD,), -1, np.int64)
    cmap[Z_OFF:Z_OFF + D_SSD] = z + np.arange(D_SSD)
    cmap[XBC_OFF:XBC_OFF + SSD_CONV_DIM] = xbc + np.arange(SSD_CONV_DIM)
    cmap[QN_OFF:QN_OFF + D_NSA] = qn + np.arange(D_NSA)
    cmap[KV_OFF:KV_OFF + 6 * KVW] = kv0 + np.arange(6 * KVW)
    cmap[QG_OFF:QG_OFF + 512] = qg + np.arange(512)
    cmap[KG_OFF:KG_OFF + 512] = kg + np.arange(512)
    cmap[VG_OFF:VG_OFF + D_GLA] = vg + np.arange(D_GLA)
    cmap[GG_OFF:GG_OFF + D_GLA] = gg + np.arange(D_GLA)
    cmap[DT_OFF:DT_OFF + SSD_HEADS] = dt + np.arange(SSD_HEADS)
    cmap[ALR_OFF:ALR_OFF + GLA_RANK] = alr + np.arange(GLA_RANK)
    for g in range(NSA_KV_HEADS):
        cmap[GATE_OFF + g * LANE:GATE_OFF + g * LANE + 3 * NSA_REP] = gn + g * 3 * NSA_REP + np.arange(3 * NSA_REP)
    return cmap


def _reorder_w_in(w):
    cmap = _in_proj_column_map()
    cuts = [0] + [i for i in range(1, D_IN_PAD) if cmap[i] != cmap[i - 1] + 1 and not (cmap[i] == -1 and cmap[i - 1] == -1)] + [D_IN_PAD]
    pieces = []
    for a, b in zip(cuts[:-1], cuts[1:]):
        pieces.append(jnp.zeros((w.shape[0], b - a), w.dtype) if cmap[a] < 0 else w[:, cmap[a]:cmap[a] + b - a])
    return jnp.concatenate(pieces, axis=1).astype(BF16)


def _dot_nt(a, b):
    return lax.dot_general(a, b, (((1,), (1,)), ((), ())), preferred_element_type=F32)


NSA_L = 2048
NSA_QB = Q_BLOCK
NSA_NQB = NSA_L // NSA_QB
NSA_NCP = NSA_L // CMP_STRIDE
NSA_NSEL = NSA_L // SEL_LEN
CMP_HALF = CMP_STRIDE * NSA_HEAD_DIM
NSA_KSTEP = 512


def _t5_bucket_np(dist):
    n = np.maximum(dist, 0)
    max_exact = N_BUCKETS // 2
    nf = np.maximum(n, 1).astype(np.float32)
    large = max_exact + (np.log(nf / max_exact) / np.float32(math.log(MAX_DISTANCE / max_exact)) * (N_BUCKETS - max_exact)).astype(np.int32)
    return np.where(n < max_exact, n, np.minimum(large, N_BUCKETS - 1)).astype(np.int32)


def _nsa_compress_kernel(a_ref, pe_ref, w_ref, o_ref):
    a = a_ref[0, 0, 0]
    lo = jnp.dot((a + pe_ref[0, 0]).astype(BF16), w_ref[0, 0], preferred_element_type=F32)
    hi = jnp.dot((a + pe_ref[0, 1]).astype(BF16), w_ref[0, 1], preferred_element_type=F32)
    o_ref[0, 0, 0] = lo + pltpu.roll(hi, NSA_NCP - 1, 0)


def _nsa_compress_prompt(y, cmp_pe, cmp_w, bsz):
    a = y[:, KV_OFF:KV_OFF + 2 * KVW].reshape(bsz, NSA_NCP, CMP_STRIDE, 2, NSA_KV_HEADS, NSA_HEAD_DIM)
    a = a.transpose(0, 3, 4, 1, 2, 5).reshape(bsz, 2, NSA_KV_HEADS, NSA_NCP, CMP_HALF)
    pe = cmp_pe.reshape(2, 2, 1, CMP_HALF)
    w = cmp_w.astype(BF16).reshape(2, 2, CMP_HALF, NSA_HEAD_DIM)
    return pl.pallas_call(
        _nsa_compress_kernel,
        grid=(bsz, 2, NSA_KV_HEADS),
        in_specs=[pl.BlockSpec((1, 1, 1, NSA_NCP, CMP_HALF), lambda b, c, g: (b, c, g, 0, 0)),
                  pl.BlockSpec((1, 2, 1, CMP_HALF), lambda b, c, g: (c, 0, 0, 0)),
                  pl.BlockSpec((1, 2, CMP_HALF, NSA_HEAD_DIM), lambda b, c, g: (c, 0, 0, 0))],
        out_specs=pl.BlockSpec((1, 1, 1, NSA_NCP, NSA_HEAD_DIM), lambda b, c, g: (b, c, g, 0, 0)),
        out_shape=jax.ShapeDtypeStruct((bsz, 2, NSA_KV_HEADS, NSA_NCP, NSA_HEAD_DIM), F32),
        compiler_params=pltpu.CompilerParams(dimension_semantics=('parallel', 'parallel', 'parallel')),
        name='nsa_compress',
    )(a, pe, w)


def _bias_lookup(rel_bias, idx):
    return rel_bias.T[:, jnp.asarray(idx)]


def _bucket_thresholds():
    buckets = _t5_bucket_np(np.arange(4 * MAX_DISTANCE))
    return [int(np.argmax(buckets >= k)) for k in range(N_BUCKETS)]


def _nsa_prompt_bias_tables_t(rel_bias):
    tq = np.arange(NSA_QB)[None, :]
    ts = np.arange(NSA_QB)[:, None]

    def table(idx):
        t = _bias_lookup(rel_bias, idx).reshape((NSA_KV_HEADS, NSA_REP) + idx.shape)
        t = jnp.moveaxis(t, 1, -2)
        return t.reshape(t.shape[:-2] + (NSA_REP * NSA_QB,))

    b_diag = table(_t5_bucket_np(tq - ts))
    b_prev = table(_t5_bucket_np(NSA_QB + tq - ts))
    by_bucket = jnp.repeat(rel_bias.reshape(N_BUCKETS, NSA_KV_HEADS, NSA_REP).transpose(1, 0, 2), NSA_QB, axis=2)
    return b_diag, b_prev, by_bucket


def _nsa_prompt_t_kernel(q_ref, gate_ref, cmp_ref, ks_ref, vs_ref, kw_ref, vw_ref, bdiag_ref, bprev_ref, bkt_ref,
                         covt_ref, expand_ref, o_ref, m_ref, l_ref, acc_ref, ksb, vst, kwb, vwt, bcmp_scr):
    i = pl.program_id(2)
    cols = NSA_REP * NSA_QB

    @pl.when(i == 0)
    def _():
        ksb[...] = ks_ref[...].astype(BF16)
        kwb[...] = kw_ref[...].astype(BF16)
        for c in range(NSA_L // NSA_KSTEP):
            vst[c] = vs_ref[c * NSA_KSTEP:(c + 1) * NSA_KSTEP, :].T.astype(BF16)
            vwt[c] = vw_ref[c * NSA_KSTEP:(c + 1) * NSA_KSTEP, :].T.astype(BF16)

    qblk = q_ref[...] * (NSA_HEAD_DIM ** -0.5)
    q_t = jnp.concatenate([qblk[:, r * LANE:(r + 1) * LANE].T for r in range(NSA_REP)], axis=1).astype(BF16)
    key = lax.broadcasted_iota(jnp.int32, (NSA_QB, cols), 0)
    tq = lax.broadcasted_iota(jnp.int32, (NSA_QB, cols), 1) % NSA_QB
    t_abs = i * NSA_QB + tq

    k_c = cmp_ref[0, 0, 0].astype(BF16)
    v_c_t = cmp_ref[0, 1, 0].T.astype(BF16)
    mask_c = key * CMP_STRIDE + (CMP_LEN - 1) <= t_abs
    bfar = bkt_ref[0, N_BUCKETS - 1:N_BUCKETS, :]
    per_qb = NSA_QB // CMP_STRIDE
    w0 = pl.multiple_of(jnp.maximum(i * per_qb - 2 * per_qb, 0), 8)
    tok = w0 + lax.broadcasted_iota(jnp.int32, (3 * per_qb, cols), 0)
    t_w = i * NSA_QB + lax.broadcasted_iota(jnp.int32, (3 * per_qb, cols), 1) % NSA_QB
    dist_c = t_w - (tok * CMP_STRIDE + (CMP_LEN - 1))
    bias_w = jnp.zeros((3 * per_qb, cols), F32) + bkt_ref[0, 0:1, :]
    for k, thr in enumerate(_bucket_thresholds()):
        if k > 0:
            bias_w = jnp.where(dist_c >= thr, bkt_ref[0, k:k + 1, :], bias_w)
    bcmp_scr[...] = jnp.zeros((NSA_NCP, cols), F32) + bfar
    bcmp_scr[pl.ds(w0, 3 * per_qb), :] = bias_w
    s_c = jnp.where(mask_c, jnp.dot(k_c, q_t, preferred_element_type=F32) + bcmp_scr[...], NEG)
    e_c = jnp.where(mask_c, jnp.exp(s_c - jnp.max(s_c, 0, keepdims=True)), 0.0)
    d_c = jnp.sum(e_c, 0, keepdims=True)
    p_c = (e_c / jnp.where(d_c > 0, d_c, 1.0)).astype(BF16)
    o_c = jnp.dot(v_c_t, p_c, preferred_element_type=F32)

    imp4 = jnp.dot(covt_ref[...], p_c, preferred_element_type=F32)
    imp = sum(imp4[:, r * NSA_QB:(r + 1) * NSA_QB] for r in range(NSA_REP))
    blk = lax.broadcasted_iota(jnp.int32, (LANE, NSA_QB), 0)
    t_row = i * NSA_QB + lax.broadcasted_iota(jnp.int32, (LANE, NSA_QB), 1)
    cur = t_row // SEL_LEN
    forced = (blk == 0) | (blk == cur) | (blk == cur - 1)
    score = jnp.where(forced, FORCE_SCORE, jnp.where(blk * SEL_LEN <= t_row, imp, -FORCE_SCORE))
    score = jnp.where(blk < NSA_NSEL, score, -3.0 * FORCE_SCORE)
    rank = jnp.zeros((LANE, NSA_QB), jnp.int32)
    for j in range(NSA_NSEL):
        row = score[j:j + 1, :]
        rank = rank + jnp.where((row > score) | ((row == score) & (blk > j)), 1, 0)
    sel_t = jnp.where((rank < TOP_N) & (blk < NSA_NSEL), 1.0, 0.0).astype(BF16)

    def init():
        m_ref[...] = jnp.full((1, cols), NEG, F32)
        l_ref[...] = jnp.zeros((1, cols), F32)
        acc_ref[...] = jnp.zeros((NSA_HEAD_DIM, cols), F32)

    dist0 = (lax.broadcasted_iota(jnp.int32, (NSA_KSTEP, cols), 1) % NSA_QB
             - lax.broadcasted_iota(jnp.int32, (NSA_KSTEP, cols), 0))
    per_step = NSA_KSTEP // NSA_QB

    def step(k_scr, v_scr, j, selected):
        start = pl.multiple_of(j * NSA_KSTEP, NSA_KSTEP)
        s = jnp.dot(k_scr[pl.ds(start, NSA_KSTEP), :], q_t, preferred_element_type=F32)
        bias = []
        for u in range(per_step):
            d = i - (j * per_step + u)
            bias.append(jnp.where(d == 0, bdiag_ref[0], jnp.where(d == 1, bprev_ref[0], bfar)))
        dist = dist0 + (i * NSA_QB - j * NSA_KSTEP)
        if selected:
            m1 = jnp.dot(expand_ref[j], sel_t, preferred_element_type=F32)
            mask = (jnp.concatenate([m1] * NSA_REP, axis=1) > 0.5) & (dist >= 0)
        else:
            mask = (dist >= 0) & (dist < WINDOW)
        s = jnp.where(mask, s + jnp.concatenate(bias, axis=0), NEG)
        m_old = m_ref[...]
        m_new = jnp.maximum(m_old, jnp.max(s, 0, keepdims=True))
        p = jnp.where(mask, jnp.exp(s - m_new), 0.0)
        alpha = jnp.exp(m_old - m_new)
        l_ref[...] = alpha * l_ref[...] + jnp.sum(p, 0, keepdims=True)
        acc_ref[...] = alpha * acc_ref[...] + jnp.dot(v_scr[j], p.astype(BF16), preferred_element_type=F32)
        m_ref[...] = m_new

    def finish():
        l = l_ref[...]
        return acc_ref[...] / jnp.where(l > 0, l, 1.0)

    j_diag = i // per_step

    init()

    def sel_body(j, carry):
        step(ksb, vst, j, True)
        return carry

    lax.fori_loop(0, j_diag + 1, sel_body, 0)
    o_s = finish()

    init()
    step(kwb, vwt, j_diag, False)

    @pl.when(j_diag >= 1)
    def _():
        step(kwb, vwt, j_diag - 1, False)

    o_w = finish()

    g_t = jax.nn.sigmoid(gate_ref[...]).T
    for r in range(NSA_REP):
        cs = slice(r * NSA_QB, (r + 1) * NSA_QB)
        o_r = g_t[3 * r:3 * r + 1, :] * o_c[:, cs] + g_t[3 * r + 1:3 * r + 2, :] * o_s[:, cs] + g_t[3 * r + 2:3 * r + 3, :] * o_w[:, cs]
        o_ref[:, r * LANE:(r + 1) * LANE] = o_r.T.astype(o_ref.dtype)


def _nsa_prompt_t(y, k_v_cmp, tables, bsz):
    b_diag, b_prev, by_bucket = tables
    cols = NSA_REP * NSA_QB
    nstep = NSA_L // NSA_KSTEP
    assert WINDOW <= NSA_KSTEP and NSA_KSTEP % NSA_QB == 0
    c_lo = np.arange(NSA_NCP) * CMP_STRIDE
    s_lo = np.arange(LANE) * SEL_LEN
    cov_t = ((c_lo[None] < s_lo[:, None] + SEL_LEN) & (c_lo[None] + CMP_LEN > s_lo[:, None])
             & (np.arange(NSA_NCP)[None] < NSA_NCP - 1) & (np.arange(LANE)[:, None] < NSA_NSEL))
    key_blk = (np.arange(NSA_L) // SEL_LEN).reshape(nstep, NSA_KSTEP, 1)
    expand = (np.arange(LANE)[None, None, :] == key_blk)
    kv_blk = KV_OFF // LANE
    grp = NSA_KV_HEADS

    def col(slot):
        return pl.BlockSpec((NSA_L, NSA_HEAD_DIM), lambda b, g, i: (b, kv_blk + slot * grp + g))

    tab = pl.BlockSpec((1, NSA_QB, cols), lambda b, g, i: (g, 0, 0))
    return pl.pallas_call(
        _nsa_prompt_t_kernel,
        grid=(bsz, NSA_KV_HEADS, NSA_NQB),
        in_specs=[pl.BlockSpec((NSA_QB, NSA_REP * LANE), lambda b, g, i: (b * NSA_NQB + i, QN_OFF // (NSA_REP * LANE) + g)),
                  pl.BlockSpec((NSA_QB, LANE), lambda b, g, i: (b * NSA_NQB + i, GATE_OFF // LANE + g)),
                  pl.BlockSpec((1, 2, 1, NSA_NCP, NSA_HEAD_DIM), lambda b, g, i: (b, 0, g, 0, 0)),
                  col(2), col(3), col(4), col(5), tab, tab,
                  pl.BlockSpec((1, N_BUCKETS, cols), lambda b, g, i: (g, 0, 0)),
                  pl.BlockSpec((LANE, NSA_NCP), lambda b, g, i: (0, 0)),
                  pl.BlockSpec((nstep, NSA_KSTEP, LANE), lambda b, g, i: (0, 0, 0))],
        out_specs=pl.BlockSpec((NSA_QB, NSA_REP * LANE), lambda b, g, i: (b * NSA_NQB + i, g)),
        out_shape=jax.ShapeDtypeStruct((bsz * NSA_L, D_NSA), BF16),
        scratch_shapes=[pltpu.VMEM((1, cols), F32), pltpu.VMEM((1, cols), F32), pltpu.VMEM((NSA_HEAD_DIM, cols), F32),
                        pltpu.VMEM((NSA_L, NSA_HEAD_DIM), BF16), pltpu.VMEM((nstep, NSA_HEAD_DIM, NSA_KSTEP), BF16),
                        pltpu.VMEM((NSA_L, NSA_HEAD_DIM), BF16), pltpu.VMEM((nstep, NSA_HEAD_DIM, NSA_KSTEP), BF16),
                        pltpu.VMEM((NSA_NCP, cols), F32)],
        compiler_params=pltpu.CompilerParams(dimension_semantics=('parallel', 'parallel', 'arbitrary'),
                                             vmem_limit_bytes=VMEM_LIMIT_BYTES),
        name='nsa_prompt',
    )(y, y, k_v_cmp, y, y, y, y, b_diag, b_prev, by_bucket,
      jnp.asarray(cov_t, BF16), jnp.asarray(expand, BF16))


NS_SELW = 3 * LANE
NS_CHUNK = 2048


def _nsa_sample_tables(rel_bias, past_len, lq, win_len):
    tq = (past_len + np.arange(lq))[:, None]

    def table(idx):
        t = _bias_lookup(rel_bias, idx)
        return t.reshape(NSA_KV_HEADS, NSA_REP * lq, idx.shape[-1])

    ncp = past_len // CMP_STRIDE
    b_cmp = table(_t5_bucket_np(tq - (np.arange(ncp)[None] * CMP_STRIDE + CMP_LEN - 1)))
    b_last = table(_t5_bucket_np(tq - (past_len - LANE + np.arange(LANE)[None])))
    b_new = table(_t5_bucket_np(tq - (past_len + np.arange(LANE)[None])))
    b_far = table(np.full((lq, LANE), N_BUCKETS - 1, np.int32))
    wpos = np.concatenate([past_len - win_len + np.arange(win_len), past_len + np.arange(LANE)])
    b_win = table(_t5_bucket_np(tq - wpos[None]))
    return b_cmp, b_last, b_new, b_far, b_win


def _nsa_sample_kernel(pt_ref, q_ref, kvn_ref, gate0_ref, gate1_ref, win_ref, cache_ref, w_ref, pe_ref,
                       bcmp_ref, blast_ref, bnew_ref, bfar_ref, bwin_ref, cov_ref, expand_ref, o_ref,
                       buf, cmp_scr, s_scr, sem, *, n_pages, lq, past_len, win_len, layer):
    b = pl.program_id(0)
    rows = NSA_REP * lq
    ncp = past_len // CMP_STRIDE
    n_cmp = (past_len + lq - CMP_LEN) // CMP_STRIDE + 1
    n_sel = -(-(past_len + lq) // SEL_LEN)
    n_chunks = past_len // NS_CHUNK
    ncol = 2 * NSA_KV_HEADS

    def page_copy(j, pair, col):
        slot, g = pair * 2 + col // NSA_KV_HEADS, col % NSA_KV_HEADS
        return pltpu.make_async_copy(cache_ref.at[layer, pt_ref[b, j], :, slot, g, :],
                                     buf.at[col, pl.ds(j * PAGE_SIZE, PAGE_SIZE), :], sem)

    def gather_start(pair):
        def body(j, carry):
            for col in range(ncol):
                page_copy(j, pair, col).start()
            return carry
        lax.fori_loop(0, n_pages, body, 0)

    def gather_wait(pair):
        def body(j, carry):
            for col in range(ncol):
                page_copy(j, pair, col).wait()
            return carry
        lax.fori_loop(0, n_pages, body, 0)

    gather_start(0)
    gather_wait(0)
    for c in range(2):
        for g in range(NSA_KV_HEADS):
            lo = jnp.zeros((ncp, NSA_HEAD_DIM), F32)
            hi = jnp.zeros((ncp, NSA_HEAD_DIM), F32)
            for l in range(CMP_STRIDE):
                x = buf[c * NSA_KV_HEADS + g, pl.ds(l, ncp, stride=CMP_STRIDE), :]
                lo = lo + jnp.dot((x + pe_ref[c, l:l + 1, :]).astype(BF16), w_ref[c, l], preferred_element_type=F32)
                hi = hi + jnp.dot((x + pe_ref[c, CMP_STRIDE + l:CMP_STRIDE + l + 1, :]).astype(BF16),
                                  w_ref[c, CMP_STRIDE + l], preferred_element_type=F32)
            cmp_scr[c, g] = lo + pltpu.roll(hi, ncp - 1, 0)
    gather_start(1)

    qall = q_ref[...] * (NSA_HEAD_DIM ** -0.5)
    tq = lax.broadcasted_iota(jnp.int32, (rows, 1), 0) % lq
    t_abs = past_len + tq
    o_c, sel, q_g = [], [], []
    for g in range(NSA_KV_HEADS):
        q = jnp.concatenate([qall[:, (g * NSA_REP + r) * LANE:(g * NSA_REP + r + 1) * LANE] for r in range(NSA_REP)],
                            axis=0).astype(BF16)
        q_g.append(q)
        n_idx = lax.broadcasted_iota(jnp.int32, (rows, ncp), 1)
        mask_c = (n_idx < n_cmp) & (n_idx * CMP_STRIDE + (CMP_LEN - 1) <= t_abs)
        s_c = jnp.where(mask_c, _dot_nt(q, cmp_scr[0, g].astype(BF16)) + bcmp_ref[g], NEG)
        e_c = jnp.where(mask_c, jnp.exp(s_c - jnp.max(s_c, -1, keepdims=True)), 0.0)
        d_c = jnp.sum(e_c, -1, keepdims=True)
        p_c = (e_c / jnp.where(d_c > 0, d_c, 1.0)).astype(BF16)
        o_c.append(jnp.dot(p_c, cmp_scr[1, g].astype(BF16), preferred_element_type=F32))
        imp4 = jnp.dot(p_c, cov_ref[...], preferred_element_type=F32)
        imp = sum(imp4[r * lq:(r + 1) * lq] for r in range(NSA_REP))
        blk = lax.broadcasted_iota(jnp.int32, (lq, NS_SELW), 1)
        t_q = past_len + lax.broadcasted_iota(jnp.int32, (lq, NS_SELW), 0)
        cur = t_q // SEL_LEN
        forced = (blk == 0) | (blk == cur) | (blk == cur - 1)
        score = jnp.where(forced, FORCE_SCORE, jnp.where(blk * SEL_LEN <= t_q, imp, -FORCE_SCORE))
        score = jnp.where(blk < n_sel, score, -3.0 * FORCE_SCORE)
        rank = jnp.zeros((lq, NS_SELW), jnp.int32)
        for j in range(n_sel):
            col = score[:, j:j + 1]
            rank = rank + jnp.where((col > score) | ((col == score) & (blk > j)), 1, 0)
        sel_q = jnp.where((rank < min(TOP_N, n_sel)) & (blk < n_sel), 1.0, 0.0)
        sel.append(jnp.concatenate([sel_q] * NSA_REP, axis=0))

    gather_wait(1)
    kvn = _pad_rows(kvn_ref[...], LANE)
    lane = lax.broadcasted_iota(jnp.int32, (rows, LANE), 1)
    blocks_per_chunk = NS_CHUNK // SEL_LEN
    gates = [jax.nn.sigmoid(gate0_ref[...]), jax.nn.sigmoid(gate1_ref[...])]
    for g in range(NSA_KV_HEADS):
        q = q_g[g]
        bfar = bfar_ref[g][:, 0:1]

        def new_rows(slot):
            return kvn[:, (slot * NSA_KV_HEADS + g) * NSA_HEAD_DIM:(slot * NSA_KV_HEADS + g + 1) * NSA_HEAD_DIM].astype(BF16)

        sel_b = sel[g].astype(BF16)
        for c in range(n_chunks):
            k = buf[g, c * NS_CHUNK:(c + 1) * NS_CHUNK, :].astype(BF16)
            s = _dot_nt(q, k) + bfar
            if c == n_chunks - 1:
                fix = blast_ref[g] - bfar_ref[g]
                s = jnp.concatenate([s[:, :NS_CHUNK - LANE], s[:, NS_CHUNK - LANE:] + fix], axis=1)
            m_c = jnp.dot(sel_b[:, c * blocks_per_chunk:(c + 1) * blocks_per_chunk], expand_ref[...],
                          preferred_element_type=F32)
            s_scr[:, c * NS_CHUNK:(c + 1) * NS_CHUNK] = jnp.where(m_c > 0.5, s, NEG)
        s_new = _dot_nt(q, new_rows(2)) + bnew_ref[g]
        mask_new = (lane <= tq) & (lane < lq) & (sel[g][:, n_sel - 1:n_sel] > 0.5)
        s_scr[:, past_len:past_len + LANE] = jnp.where(mask_new, s_new, NEG)
        m = jnp.max(s_scr[...], -1, keepdims=True)
        acc = jnp.zeros((rows, NSA_HEAD_DIM), F32)
        den = jnp.zeros((rows, 1), F32)
        for c in range(n_chunks):
            sc = s_scr[:, c * NS_CHUNK:(c + 1) * NS_CHUNK]
            e = jnp.where(sc > 0.5 * NEG, jnp.exp(sc - m), 0.0)
            den = den + jnp.sum(e, -1, keepdims=True)
            v = buf[NSA_KV_HEADS + g, c * NS_CHUNK:(c + 1) * NS_CHUNK, :]
            acc = acc + jnp.dot(e.astype(BF16), v.astype(BF16), preferred_element_type=F32)
        sc = s_scr[:, past_len:past_len + LANE]
        e = jnp.where(sc > 0.5 * NEG, jnp.exp(sc - m), 0.0)
        den = den + jnp.sum(e, -1, keepdims=True)
        acc = acc + jnp.dot(e.astype(BF16), new_rows(3), preferred_element_type=F32)
        o_s = acc / jnp.where(den > 0, den, 1.0)

        kw = jnp.concatenate([win_ref[0, :, g * NSA_HEAD_DIM:(g + 1) * NSA_HEAD_DIM].astype(BF16), new_rows(4)], axis=0)
        vw = jnp.concatenate([win_ref[0, :, (NSA_KV_HEADS + g) * NSA_HEAD_DIM:(NSA_KV_HEADS + g + 1) * NSA_HEAD_DIM].astype(BF16),
                              new_rows(5)], axis=0)
        wl = lax.broadcasted_iota(jnp.int32, (rows, win_len + LANE), 1)
        dist = jnp.where(wl < win_len, win_len + tq - wl, tq - (wl - win_len))
        mask_w = (dist >= 0) & (dist < WINDOW) & (wl < win_len + lq)
        s_w = jnp.where(mask_w, _dot_nt(q, kw) + bwin_ref[g], NEG)
        e_w = jnp.where(mask_w, jnp.exp(s_w - jnp.max(s_w, -1, keepdims=True)), 0.0)
        d_w = jnp.sum(e_w, -1, keepdims=True)
        o_w = jnp.dot((e_w / jnp.where(d_w > 0, d_w, 1.0)).astype(BF16), vw, preferred_element_type=F32)

        for r in range(NSA_REP):
            rs = slice(r * lq, (r + 1) * lq)
            gt = gates[g]
            h = g * NSA_REP + r
            o_ref[:, h * LANE:(h + 1) * LANE] = (gt[:, 3 * r:3 * r + 1] * o_c[g][rs] + gt[:, 3 * r + 1:3 * r + 2] * o_s[rs]
                                                 + gt[:, 3 * r + 2:3 * r + 3] * o_w[rs]).astype(o_ref.dtype)


def _nsa_sample(y, cache_kv, layer, page_table, cache_win, cmp_pe, cmp_w, rel_bias, bsz):
    lq = y.shape[0] // bsz
    n_pages = page_table.shape[1]
    past_len = n_pages * PAGE_SIZE
    win_len = cache_win.shape[1]
    rows = NSA_REP * lq
    ncp = past_len // CMP_STRIDE
    n_cmp = (past_len + lq - CMP_LEN) // CMP_STRIDE + 1
    n_sel = -(-(past_len + lq) // SEL_LEN)
    assert lq % 8 == 0 and lq <= CMP_STRIDE and n_cmp <= ncp - 1 + lq // CMP_STRIDE and past_len % NS_CHUNK == 0
    assert past_len % SEL_LEN == 0 and n_sel <= NS_SELW and lq % Q_BLOCK != 0 and win_len == WINDOW
    tables = _nsa_sample_tables(rel_bias, past_len, lq, win_len)
    c_lo = np.arange(ncp) * CMP_STRIDE
    s_lo = np.arange(NS_SELW) * SEL_LEN
    cover = ((c_lo[:, None] < s_lo[None] + SEL_LEN) & (c_lo[:, None] + CMP_LEN > s_lo[None])
             & (np.arange(ncp)[:, None] < n_cmp) & (np.arange(NS_SELW)[None] < n_sel))
    expand = np.arange(NS_CHUNK // SEL_LEN)[:, None] == (np.arange(NS_CHUNK) // SEL_LEN)[None]
    assert cache_kv.shape[2:] == (PAGE_SIZE, 4, NSA_KV_HEADS, NSA_HEAD_DIM)
    win2 =cache_win.reshape(bsz, win_len, 2 * KVW)

    def full(shape):
        return pl.BlockSpec(shape, lambda b, pt: (0,) * len(shape))

    grid_spec = pltpu.PrefetchScalarGridSpec(
        num_scalar_prefetch=1, grid=(bsz,),
        in_specs=[pl.BlockSpec((lq, D_NSA), lambda b, pt: (b, QN_OFF // D_NSA)),
                  pl.BlockSpec((lq, 6 * KVW), lambda b, pt: (b, KV_OFF // (6 * KVW))),
                  pl.BlockSpec((lq, LANE), lambda b, pt: (b, GATE_OFF // LANE)),
                  pl.BlockSpec((lq, LANE), lambda b, pt: (b, GATE_OFF // LANE + 1)),
                  pl.BlockSpec((1, win_len, 2 * KVW), lambda b, pt: (b, 0, 0)),
                  pl.BlockSpec(memory_space=pl.ANY),
                  full((2, CMP_LEN, NSA_HEAD_DIM, NSA_HEAD_DIM)), full((2, CMP_LEN, NSA_HEAD_DIM)),
                  full((NSA_KV_HEADS, rows, ncp)), full((NSA_KV_HEADS, rows, LANE)), full((NSA_KV_HEADS, rows, LANE)),
                  full((NSA_KV_HEADS, rows, LANE)), full((NSA_KV_HEADS, rows, win_len + LANE)),
                  full((ncp, NS_SELW)), full((NS_CHUNK // SEL_LEN, NS_CHUNK))],
        out_specs=pl.BlockSpec((lq, D_NSA), lambda b, pt: (b, 0)),
        scratch_shapes=[pltpu.VMEM((2 * NSA_KV_HEADS, past_len, NSA_HEAD_DIM), F32),
                        pltpu.VMEM((2, NSA_KV_HEADS, ncp, NSA_HEAD_DIM), F32),
                        pltpu.VMEM((rows, past_len + LANE), F32), pltpu.SemaphoreType.DMA(())])
    assert QN_OFF % D_NSA == 0 and KV_OFF % (6 * KVW) == 0
    return pl.pallas_call(
        functools.partial(_nsa_sample_kernel, n_pages=n_pages, lq=lq, past_len=past_len, win_len=win_len,
                          layer=layer),
        grid_spec=grid_spec,
        out_shape=jax.ShapeDtypeStruct((bsz * lq, D_NSA), BF16),
        compiler_params=pltpu.CompilerParams(dimension_semantics=('arbitrary',), vmem_limit_bytes=VMEM_LIMIT_BYTES),
        name='nsa_sample',
    )(page_table, y, y, y, y, win2, cache_kv, cmp_w.astype(BF16), cmp_pe, *tables,
      jnp.asarray(cover, BF16), jnp.asarray(expand, BF16))


def _mm_kernel(x_ref, w_ref, o_ref, *scratch, nk, act):
    def finish(acc):
        if act == 'sqrelu':
            acc = jnp.square(jnp.maximum(acc, 0.0))
        o_ref[...] = acc.astype(o_ref.dtype)

    if nk == 1:
        finish(jnp.dot(x_ref[...], w_ref[...], preferred_element_type=F32))
        return
    acc_ref, = scratch
    k = pl.program_id(2)
    part = jnp.dot(x_ref[...], w_ref[...], preferred_element_type=F32)

    @pl.when(k == 0)
    def _():
        acc_ref[...] = part

    @pl.when(k > 0)
    def _():
        acc_ref[...] += part

    @pl.when(k == nk - 1)
    def _():
        finish(acc_ref[...])


def _pick(n, pref):
    for t in pref:
        if n % t == 0:
            return t
    return n


def _matmul(x, w, out_dtype=F32, act=None, name='matmul'):
    m, k = x.shape
    n = w.shape[1]
    tm = _pick(m, (1024, 512, 256, 128, 64))
    tn = _pick(n, (512, 256, 128))
    tk = _pick(k, (4096, 2048, 1024, 512))
    nk = k // tk
    scratch = [pltpu.VMEM((tm, tn), F32)] if nk > 1 else []
    return pl.pallas_call(
        functools.partial(_mm_kernel, nk=nk, act=act),
        grid=(m // tm, n // tn, nk),
        in_specs=[pl.BlockSpec((tm, tk), lambda i, j, kk: (i, kk)),
                  pl.BlockSpec((tk, tn), lambda i, j, kk: (kk, j))],
        out_specs=pl.BlockSpec((tm, tn), lambda i, j, kk: (i, j)),
        out_shape=jax.ShapeDtypeStruct((m, n), out_dtype),
        scratch_shapes=scratch,
        compiler_params=pltpu.CompilerParams(
            dimension_semantics=('parallel', 'parallel', 'arbitrary'),
            vmem_limit_bytes=VMEM_LIMIT_BYTES),
        name=name,
    )(x, w)


def _mm_ws_kernel(x_ref, w_ref, o_ref, wb_ref, *, act):
    @pl.when(pl.program_id(1) == 0)
    def _():
        wb_ref[...] = w_ref[0].astype(BF16)

    acc = jnp.dot(x_ref[...], wb_ref[...], preferred_element_type=F32)
    if act == 'sqrelu':
        acc = jnp.square(jnp.maximum(acc, 0.0))
    o_ref[...] = acc.astype(o_ref.dtype)


def _matmul_cast(x, w, layer, out_dtype=F32, act=None, name='matmul'):
    m, k = x.shape
    n = w.shape[2]
    tm = _pick(m, (1024,))
    tn = _pick(n, (512,))
    return pl.pallas_call(
        functools.partial(_mm_ws_kernel, act=act),
        grid=(n // tn, m // tm),
        in_specs=[pl.BlockSpec((tm, k), lambda j, i: (i, 0)),
                  pl.BlockSpec((1, k, tn), lambda j, i: (layer, 0, j))],
        out_specs=[pl.BlockSpec((tm, tn), lambda j, i: (i, j)), pl.BlockSpec((k, tn), lambda j, i: (0, j))],
        out_shape=[jax.ShapeDtypeStruct((m, n), out_dtype), jax.ShapeDtypeStruct((k, n), BF16)],
        compiler_params=pltpu.CompilerParams(dimension_semantics=('parallel', 'arbitrary'),
                                             vmem_limit_bytes=VMEM_LIMIT_BYTES),
        name=name,
    )(x, w)


CAST_BLOCK_BYTES = 8 * 1024 * 1024


def _cast_kernel(w_ref, o_ref):
    o_ref[...] = w_ref[0].astype(o_ref.dtype)


def _to_bf16(w, layer):
    _, k, n = w.shape
    tk = CAST_BLOCK_BYTES // (4 * n)
    assert tk % 16 == 0 and k % tk == 0
    return pl.pallas_call(
        _cast_kernel, grid=(k // tk,),
        in_specs=[pl.BlockSpec((1, tk, n), lambda i: (layer, i, 0))], out_specs=pl.BlockSpec((tk, n), lambda i: (i, 0)),
        out_shape=jax.ShapeDtypeStruct((k, n), BF16),
        compiler_params=pltpu.CompilerParams(dimension_semantics=('parallel',), vmem_limit_bytes=VMEM_LIMIT_BYTES),
        name='weight_cast',
    )(w)


def _out_proj_kernel(a_ref, b_ref, c_ref, wa_ref, wb_ref, wc_ref, o_ref):
    o_ref[...] = (jnp.dot(a_ref[...], wa_ref[...], preferred_element_type=F32)
                  + jnp.dot(b_ref[...], wb_ref[...], preferred_element_type=F32)
                  + jnp.dot(c_ref[...], wc_ref[...], preferred_element_type=F32))


def _out_proj(y_ssd, y_nsa, y_gla, w):
    m = y_ssd.shape[0]
    n = w.shape[1]
    tm = _pick(m, (1024, 64))
    tn = _pick(n, (512,))
    assert D_SSD % D_NSA == 0 and D_NSA == D_GLA

    def act(width):
        return pl.BlockSpec((tm, width), lambda i, j: (i, 0))

    return pl.pallas_call(
        _out_proj_kernel, grid=(m // tm, n // tn),
        in_specs=[act(D_SSD), act(D_NSA), act(D_GLA),
                  pl.BlockSpec((D_SSD, tn), lambda i, j: (0, j)),
                  pl.BlockSpec((D_NSA, tn), lambda i, j: (D_SSD // D_NSA, j)),
                  pl.BlockSpec((D_GLA, tn), lambda i, j: (D_SSD // D_NSA + 1, j))],
        out_specs=pl.BlockSpec((tm, tn), lambda i, j: (i, j)),
        out_shape=jax.ShapeDtypeStruct((m, n), F32),
        compiler_params=pltpu.CompilerParams(dimension_semantics=('parallel', 'parallel'),
                                             vmem_limit_bytes=VMEM_LIMIT_BYTES),
        name='out_proj',
    )(y_ssd, y_nsa, y_gla, w, w, w)


def _out_proj_cast_kernel(a_ref, b_ref, c_ref, w_ref, o_ref, wb_ref):
    @pl.when(pl.program_id(1) == 0)
    def _():
        wb_ref[...] = w_ref[0].astype(BF16)

    o_ref[...] = (jnp.dot(a_ref[...], wb_ref[0:D_SSD, :], preferred_element_type=F32)
                  + jnp.dot(b_ref[...], wb_ref[D_SSD:D_SSD + D_NSA, :], preferred_element_type=F32)
                  + jnp.dot(c_ref[...], wb_ref[D_SSD + D_NSA:D_MIX, :], preferred_element_type=F32))


def _out_proj_cast(y_ssd, y_nsa, y_gla, w, layer):
    m = y_ssd.shape[0]
    n = w.shape[2]
    tm = _pick(m, (1024,))
    tn = _pick(n, (512,))

    def act(width):
        return pl.BlockSpec((tm, width), lambda j, i: (i, 0))

    return pl.pallas_call(
        _out_proj_cast_kernel, grid=(n // tn, m // tm),
        in_specs=[act(D_SSD), act(D_NSA), act(D_GLA), pl.BlockSpec((1, D_MIX, tn), lambda j, i: (layer, 0, j))],
        out_specs=[pl.BlockSpec((tm, tn), lambda j, i: (i, j)), pl.BlockSpec((D_MIX, tn), lambda j, i: (0, j))],
        out_shape=[jax.ShapeDtypeStruct((m, n), F32), jax.ShapeDtypeStruct((D_MIX, n), BF16)],
        compiler_params=pltpu.CompilerParams(dimension_semantics=('parallel', 'arbitrary'),
                                             vmem_limit_bytes=VMEM_LIMIT_BYTES),
        name='out_proj',
    )(y_ssd, y_nsa, y_gla, w)


def _ln_kernel(x_ref, h_ref, g_ref, b_ref, o_ref, ob_ref):
    v = DN_ALPHA * x_ref[...] + h_ref[...]
    d = v - jnp.mean(v, -1, keepdims=True)
    y = d * lax.rsqrt(jnp.mean(d * d, -1, keepdims=True) + LN_EPS) * g_ref[...] + b_ref[...]
    o_ref[...] = y
    ob_ref[...] = y.astype(BF16)


def _add_layernorm(x, h, g, b):
    m, d = x.shape
    tm = _pick(m, (256, 64))
    row = pl.BlockSpec((tm, d), lambda i: (i, 0))
    vec = pl.BlockSpec((1, d), lambda i: (0, 0))
    return pl.pallas_call(
        _ln_kernel, grid=(m // tm,), in_specs=[row, row, vec, vec], out_specs=[row, row],
        out_shape=[jax.ShapeDtypeStruct((m, d), F32), jax.ShapeDtypeStruct((m, d), BF16)],
        compiler_params=pltpu.CompilerParams(dimension_semantics=('parallel',), vmem_limit_bytes=VMEM_LIMIT_BYTES),
        name='add_layernorm',
    )(x, h, g.reshape(1, d), b.reshape(1, d))


def _xattn_kernel(q_ref, k_ref, v_ref, o_ref):
    q = (q_ref[...] * (X_HEAD_DIM ** -0.5)).astype(BF16)
    s = _dot_nt(q, k_ref[0].astype(BF16))
    e = jnp.exp(s - jnp.max(s, -1, keepdims=True))
    p = e / jnp.sum(e, -1, keepdims=True)
    o_ref[...] = jnp.dot(p.astype(BF16), v_ref[0].astype(BF16), preferred_element_type=F32).astype(o_ref.dtype)


def _cross_attention(q, mem_kv, bsz):
    m = q.shape[0]
    l = m // bsz
    tq = _pick(l, (512, 8))
    nq = l // tq
    return pl.pallas_call(
        _xattn_kernel, grid=(bsz, X_HEADS, nq),
        in_specs=[pl.BlockSpec((tq, X_HEAD_DIM), lambda b, h, i: (b * nq + i, h)),
                  pl.BlockSpec((1, N_MEM, X_HEAD_DIM), lambda b, h, i: (b, 0, h)),
                  pl.BlockSpec((1, N_MEM, X_HEAD_DIM), lambda b, h, i: (b, 0, X_HEADS + h))],
        out_specs=pl.BlockSpec((tq, X_HEAD_DIM), lambda b, h, i: (b * nq + i, h)),
        out_shape=jax.ShapeDtypeStruct((m, D_MODEL), BF16),
        compiler_params=pltpu.CompilerParams(dimension_semantics=('parallel', 'parallel', 'parallel'),
                                             vmem_limit_bytes=VMEM_LIMIT_BYTES),
        name='cross_attention',
    )(q, mem_kv, mem_kv)


def _softplus(x):
    return jnp.maximum(x, 0.0) + jnp.log1p(jnp.exp(-jnp.abs(x)))


def _silu(x):
    return x * jax.nn.sigmoid(x)


def _pad_rows(x, rows):
    if x.shape[0] == rows:
        return x
    return jnp.concatenate([x, jnp.zeros((rows - x.shape[0],) + x.shape[1:], x.dtype)], axis=0)


def _cumsum_rows(x, seg):
    r = lax.broadcasted_iota(jnp.int32, x.shape, 0) % seg
    k = 1
    while k < seg:
        x = x + jnp.where(r >= k, pltpu.roll(x, k, 0), 0.0)
        k *= 2
    return x


def _segment_last(x, seg):
    n = x.shape[0]
    r = lax.broadcasted_iota(jnp.int32, x.shape, 0) % seg
    k = 1
    while k < seg:
        x = jnp.where(r < seg - k, pltpu.roll(x, n - k, 0), x)
        k *= 2
    return x


GLA_BLOCK = 128


def _gla_kernel(q_ref, k_ref, v_ref, g_ref, a_ref, w2_ref, gb_ref, nw_ref, s0_ref, o_ref, s_out_ref,
                s_scr, q_scr, k_scr, v_scr, bc_scr, o_scr, *, rows_in, ch):
    c = pl.program_id(1)
    nsub = -(-rows_in // ch)
    hk = GLA_DK

    @pl.when(c == 0)
    def _():
        s_scr[...] = s0_ref[0]

    valid = lax.broadcasted_iota(jnp.int32, (GLA_BLOCK, 1), 0) < rows_in
    a = _pad_rows(a_ref[...], GLA_BLOCK).astype(BF16)
    logf = -_softplus(-(jnp.dot(a, w2_ref[...], preferred_element_type=F32) + gb_ref[...])) / GLA_TAU
    logf = jnp.where(valid, logf, 0.0)
    bc = _cumsum_rows(logf, ch)
    b_last = _segment_last(bc, ch)
    q = _pad_rows(q_ref[...], GLA_BLOCK) * (GLA_DK ** -0.5)
    k = _pad_rows(k_ref[...], GLA_BLOCK)
    v = _pad_rows(v_ref[...], GLA_BLOCK)
    q_scr[...] = q
    k_scr[...] = k
    v_scr[...] = v
    bc_scr[...] = bc

    tt = lax.broadcasted_iota(jnp.int32, (ch, 1), 0)

    def intra(j, carry):
        r0 = pl.multiple_of(j * ch, ch)
        qj = q_scr[pl.ds(r0, ch), :]
        bj = bc_scr[pl.ds(r0, ch), :]
        acc = [jnp.zeros((ch, GLA_DV), F32) for _ in range(GLA_HEADS)]
        for s in range(ch):
            ks = k_scr[pl.ds(r0 + s, 1), :]
            bs = bc_scr[pl.ds(r0 + s, 1), :]
            vs = v_scr[pl.ds(r0 + s, 1), :]
            w = jnp.where(tt >= s, qj * ks * jnp.exp(jnp.minimum(bj - bs, 0.0)), 0.0)
            for h in range(GLA_HEADS):
                att = jnp.sum(w[:, h * hk:(h + 1) * hk], -1, keepdims=True)
                acc[h] = acc[h] + att * vs[:, h * GLA_DV:(h + 1) * GLA_DV]
        o_scr[pl.ds(r0, ch), :] = jnp.concatenate(acc, axis=1)
        return carry

    lax.fori_loop(0, nsub, intra, 0)

    qe = q * jnp.exp(bc)
    kst = k * jnp.exp(b_last - bc)
    dec = jnp.exp(b_last)
    o_intra = _pad_rows(o_scr[0:nsub * ch, :], GLA_BLOCK)
    chunk_of_col = lax.broadcasted_iota(jnp.int32, (hk, GLA_BLOCK), 1) // ch
    g = _pad_rows(g_ref[...], GLA_BLOCK)
    outs = []
    for h in range(GLA_HEADS):
        kst_t = kst[:, h * hk:(h + 1) * hk].T
        dec_t = dec[:, h * hk:(h + 1) * hk].T
        lhs = jnp.concatenate([jnp.where(chunk_of_col == j, kst_t, 0.0) for j in range(nsub)], axis=0).astype(BF16)
        kv = jnp.dot(lhs, v[:, h * GLA_DV:(h + 1) * GLA_DV].astype(BF16), preferred_element_type=F32)
        s_h = s_scr[h]
        s_before = []
        for j in range(nsub):
            s_before.append(s_h.astype(BF16))
            s_h = s_h * dec_t[:, j * ch:j * ch + 1] + kv[j * hk:(j + 1) * hk]
        s_scr[h] = s_h
        big = jnp.dot(qe[:, h * hk:(h + 1) * hk].astype(BF16), jnp.concatenate(s_before, axis=1), preferred_element_type=F32)
        o_inter = _pad_rows(jnp.concatenate([big[j * ch:(j + 1) * ch, j * GLA_DV:(j + 1) * GLA_DV] for j in range(nsub)], axis=0),
                            GLA_BLOCK)
        o_h = o_intra[:, h * GLA_DV:(h + 1) * GLA_DV] + o_inter
        o_h = o_h * lax.rsqrt(jnp.mean(o_h * o_h, -1, keepdims=True) + NORM_EPS) * nw_ref[...]
        outs.append(o_h)
    o = jnp.concatenate(outs, axis=1) * _silu(g)
    o_ref[...] = o[:rows_in].astype(o_ref.dtype)

    @pl.when(c == pl.num_programs(1) - 1)
    def _():
        s_out_ref[0] = s_scr[...]


def _gla(y, s0, gate_w2, gate_b, norm_w, bsz):
    m = y.shape[0]
    l = m // bsz
    rows_in = min(GLA_BLOCK, l)
    ch = GLA_CHUNK
    nblk = l // rows_in
    hdk = GLA_HEADS * GLA_DK
    w2 = jnp.zeros((LANE, hdk), BF16).at[ALR_OFF - SMALL_OFF:ALR_OFF - SMALL_OFF + GLA_RANK].set(gate_w2.astype(BF16))

    def rows(width, off):
        assert off % width == 0
        return pl.BlockSpec((rows_in, width), lambda b, c: (b * nblk + c, off // width))

    def const(shape):
        return pl.BlockSpec(shape, lambda b, c: (0,) * len(shape))

    state = pl.BlockSpec((1, GLA_HEADS, GLA_DK, GLA_DV), lambda b, c: (b, 0, 0, 0))
    return pl.pallas_call(
        functools.partial(_gla_kernel, rows_in=rows_in, ch=ch),
        grid=(bsz, nblk),
        in_specs=[rows(hdk, QG_OFF), rows(hdk, KG_OFF), rows(D_GLA, VG_OFF), rows(D_GLA, GG_OFF), rows(LANE, SMALL_OFF),
                  const((LANE, hdk)), const((1, hdk)), const((1, GLA_DV)), state],
        out_specs=[pl.BlockSpec((rows_in, D_GLA), lambda b, c: (b * nblk + c, 0)), state],
        out_shape=[jax.ShapeDtypeStruct((m, D_GLA), BF16), jax.ShapeDtypeStruct((bsz, GLA_HEADS, GLA_DK, GLA_DV), F32)],
        scratch_shapes=[pltpu.VMEM((GLA_HEADS, GLA_DK, GLA_DV), F32), pltpu.VMEM((GLA_BLOCK, hdk), F32),
                        pltpu.VMEM((GLA_BLOCK, hdk), F32), pltpu.VMEM((GLA_BLOCK, D_GLA), F32),
                        pltpu.VMEM((GLA_BLOCK, hdk), F32), pltpu.VMEM((GLA_BLOCK, D_GLA), F32)],
        compiler_params=pltpu.CompilerParams(dimension_semantics=('parallel', 'arbitrary'),
                                             vmem_limit_bytes=VMEM_LIMIT_BYTES),
        name='gla',
    )(y, y, y, y, y, w2, gate_b.reshape(1, hdk), norm_w.reshape(1, GLA_DV), s0)


SSD_R = SSD_HEADS // SSD_GROUPS
SSD_GW = SSD_R * SSD_HEAD_DIM
SSD_CW = SSD_GW + 2 * SSD_STATE
SSD_TAIL = 8


def _ssd_kernel(xs_ref, bm_ref, cm_ref, z_ref, dt_ref, conv0_ref, cw_ref, cb_ref, hp_ref, dskip_ref, nw_ref, h0_ref,
                o_ref, h_out_ref, xbuf, h_scr, *, rows_in):
    g = pl.program_id(1)
    c = pl.program_id(2)
    t = SSD_CHUNK

    @pl.when(c == 0)
    def _():
        xbuf[0:SSD_TAIL, :] = conv0_ref[0, 0]
        h_scr[...] = h0_ref[0, 0]

    xbuf[SSD_TAIL:SSD_TAIL + t, 0:SSD_GW] = _pad_rows(xs_ref[...], t)
    xbuf[SSD_TAIL:SSD_TAIL + t, SSD_GW:SSD_GW + SSD_STATE] = _pad_rows(bm_ref[...], t)
    xbuf[SSD_TAIL:SSD_TAIL + t, SSD_GW + SSD_STATE:SSD_CW] = _pad_rows(cm_ref[...], t)
    conv = cb_ref[0]
    for kk in range(SSD_CONV):
        conv = conv + cw_ref[0, kk:kk + 1, :] * xbuf[SSD_TAIL - (SSD_CONV - 1) + kk:SSD_TAIL - (SSD_CONV - 1) + kk + t, :]
    xbuf[0:SSD_TAIL, :] = xbuf[t:t + SSD_TAIL, :]
    xc = _silu(conv)
    xs = xc[:, 0:SSD_GW]
    bm = xc[:, SSD_GW:SSD_GW + SSD_STATE]
    cm = xc[:, SSD_GW + SSD_STATE:SSD_CW].astype(BF16)

    lane = lax.broadcasted_iota(jnp.int32, (t, LANE), 1)
    row = lax.broadcasted_iota(jnp.int32, (t, LANE), 0)
    dt = pltpu.roll(_pad_rows(dt_ref[...], t), (LANE - SSD_R * g) % LANE, 1)
    dt = jnp.where((lane < SSD_R) & (row < rows_in), _softplus(dt + hp_ref[0, 0:1, :]), 0.0)
    acs = _cumsum_rows(dt * -jnp.exp(hp_ref[0, 1:2, :]), t)
    acs_t = acs.T
    a_last = acs[t - 1:t, :]

    head_of_lane = lax.broadcasted_iota(jnp.int32, (1, SSD_GW), 1) // SSD_HEAD_DIM

    def spread(arr):
        return sum(jnp.where(head_of_lane == r, arr[:, r:r + 1], 0.0) for r in range(SSD_R))

    xdt = xs * spread(dt)
    cb = _dot_nt(cm, bm.astype(BF16))
    tri = lax.broadcasted_iota(jnp.int32, (t, t), 0) >= lax.broadcasted_iota(jnp.int32, (t, t), 1)
    y = jnp.zeros((t, SSD_GW), F32)
    for r in range(SSD_R):
        decay = jnp.where(tri, jnp.exp(jnp.minimum(acs[:, r:r + 1] - acs_t[r:r + 1, :], 0.0)), 0.0)
        x_r = jnp.where(head_of_lane == r, xdt, 0.0).astype(BF16)
        y = y + jnp.dot((cb * decay).astype(BF16), x_r, preferred_element_type=F32)
    h_prev = h_scr[...]
    y = y + jnp.dot(cm, h_prev.astype(BF16), preferred_element_type=F32) * jnp.exp(spread(acs))
    x_st = (xdt * jnp.exp(spread(a_last - acs))).astype(BF16)
    h_new = h_prev * jnp.exp(spread(a_last)) + jnp.dot(bm.T.astype(BF16), x_st, preferred_element_type=F32)
    h_scr[...] = h_new

    y = (y + xs * dskip_ref[0]) * _silu(_pad_rows(z_ref[...], t))
    y = y * lax.rsqrt(jnp.mean(y * y, -1, keepdims=True) + NORM_EPS) * nw_ref[0]
    o_ref[...] = y[:rows_in].astype(o_ref.dtype)

    @pl.when(c == pl.num_programs(2) - 1)
    def _():
        h_out_ref[0, 0] = h_new


def _ssd(y, conv0, h0, conv_w, conv_b, dt_bias, a_log, d_skip, norm_w, bsz):
    m = y.shape[0]
    l = m // bsz
    rows_in = min(SSD_CHUNK, l)
    nblk = l // rows_in
    ng = SSD_GROUPS

    def per_group(v):
        xs = v[..., :D_SSD].reshape(v.shape[:-1] + (ng, SSD_GW))
        bm = v[..., D_SSD:D_SSD + ng * SSD_STATE].reshape(v.shape[:-1] + (ng, SSD_STATE))
        cm = v[..., D_SSD + ng * SSD_STATE:].reshape(v.shape[:-1] + (ng, SSD_STATE))
        return jnp.moveaxis(jnp.concatenate([xs, bm, cm], axis=-1), -2, 0)

    conv0_g = jnp.moveaxis(per_group(jnp.pad(conv0, ((0, 0), (SSD_TAIL - (SSD_CONV - 1), 0), (0, 0)))), 0, 1)
    cw_g = per_group(conv_w)
    cb_g = per_group(conv_b[None])
    hp = jnp.zeros((ng, 8, LANE), F32)
    hp = hp.at[:, 0, :SSD_R].set(dt_bias.reshape(ng, SSD_R)).at[:, 1, :SSD_R].set(a_log.reshape(ng, SSD_R))
    dskip_g = jnp.repeat(d_skip, SSD_HEAD_DIM).reshape(ng, 1, SSD_GW)
    nw_g = norm_w.reshape(ng, 1, SSD_GW)
    h0_t = h0.reshape(bsz, ng, SSD_GW, SSD_STATE).transpose(0, 1, 3, 2)

    def rows(width, off):
        assert off % width == 0
        return pl.BlockSpec((rows_in, width), lambda b, g, c: (b * nblk + c, off // width + g))

    def grp(shape):
        return pl.BlockSpec((1,) + shape, lambda b, g, c: (g,) + (0,) * len(shape))

    state = pl.BlockSpec((1, 1, SSD_STATE, SSD_GW), lambda b, g, c: (b, g, 0, 0))
    out, h_t = pl.pallas_call(
        functools.partial(_ssd_kernel, rows_in=rows_in),
        grid=(bsz, ng, nblk),
        in_specs=[rows(SSD_GW, XBC_OFF), rows(SSD_STATE, XBC_OFF + D_SSD), rows(SSD_STATE, XBC_OFF + D_SSD + ng * SSD_STATE),
                  rows(SSD_GW, Z_OFF), pl.BlockSpec((rows_in, LANE), lambda b, g, c: (b * nblk + c, SMALL_OFF // LANE)),
                  pl.BlockSpec((1, 1, SSD_TAIL, SSD_CW), lambda b, g, c: (b, g, 0, 0)),
                  grp((SSD_CONV, SSD_CW)), grp((1, SSD_CW)), grp((8, LANE)), grp((1, SSD_GW)), grp((1, SSD_GW)), state],
        out_specs=[pl.BlockSpec((rows_in, SSD_GW), lambda b, g, c: (b * nblk + c, g)), state],
        out_shape=[jax.ShapeDtypeStruct((m, D_SSD), BF16), jax.ShapeDtypeStruct((bsz, ng, SSD_STATE, SSD_GW), F32)],
        scratch_shapes=[pltpu.VMEM((SSD_TAIL + SSD_CHUNK, SSD_CW), F32), pltpu.VMEM((SSD_STATE, SSD_GW), F32)],
        compiler_params=pltpu.CompilerParams(dimension_semantics=('parallel', 'parallel', 'arbitrary'),
                                             vmem_limit_bytes=VMEM_LIMIT_BYTES),
        name='ssd',
    )(y, y, y, y, y, conv0_g, cw_g, cb_g, hp, dskip_g, nw_g, h0_t)
    h_new = h_t.transpose(0, 1, 3, 2).reshape(bsz, SSD_HEADS, SSD_HEAD_DIM, SSD_STATE)
    return out, h_new


def _layer_flat(x, xb, bsz, lp, f32_weights, layer_idx, rel_bias, tables, mem_kv, conv0, ssd_h0, gla_s0, nsa_past, win_past):
    m = x.shape[0]
    l = m // bsz
    assert l >= SSD_CONV - 1
    y = _matmul(xb, lp['w_in'], name='in_proj')
    y3 = y.reshape(bsz, l, D_IN_PAD)
    y_ssd, h_new = _ssd(y, conv0, ssd_h0, lp['ssd_conv_w'], lp['ssd_conv_b'], lp['ssd_dt_bias'], lp['ssd_a_log'],
                        lp['ssd_d'], lp['ssd_norm_w'], bsz)
    conv_new = y3[:, l - (SSD_CONV - 1):, XBC_OFF:XBC_OFF + SSD_CONV_DIM]
    y_gla, s_new = _gla(y, gla_s0, lp['gla_gate_w2'], lp['gla_gate_b'], lp['gla_norm_w'], bsz)
    rows = y3[..., KV_OFF:KV_OFF + 4 * KVW].reshape(bsz, l, 4, NSA_KV_HEADS, NSA_HEAD_DIM)
    win_rows = y3[..., KV_OFF + 4 * KVW:KV_OFF + 6 * KVW].reshape(bsz, l, 2, NSA_KV_HEADS, NSA_HEAD_DIM)
    if nsa_past is None:
        win_new = win_rows[:, -min(WINDOW, l):]
        y_nsa = _nsa_prompt_t(y, _nsa_compress_prompt(y, lp['nsa_cmp_pe'], lp['nsa_cmp_w'], bsz), tables, bsz)
    else:
        cache_kv, layer, page_table = nsa_past
        win_new = jnp.concatenate([win_past, win_rows], axis=1)[:, -win_past.shape[1]:]
        y_nsa = _nsa_sample(y, cache_kv, layer, page_table, win_past, lp['nsa_cmp_pe'], lp['nsa_cmp_w'], rel_bias, bsz)
    def mm(key, a, **kw):
        if key not in lp:
            out, lp[key] = _matmul_cast(a, f32_weights[key], layer_idx, **kw)
            return out
        return _matmul(a, lp[key], **kw)

    if 'w_out' not in lp:
        h, lp['w_out'] = _out_proj_cast(y_ssd, y_nsa, y_gla, f32_weights['w_out'], layer_idx)
    else:
        h = _out_proj(y_ssd, y_nsa, y_gla, lp['w_out'])
    x, xb = _add_layernorm(x, h, lp['ln_g'][0], lp['ln_b'][0])
    o = _cross_attention(mm('x_wq', xb, name='xattn_q'), mem_kv, bsz)
    x, xb = _add_layernorm(x, mm('x_wo', o, name='xattn_o'), lp['ln_g'][1], lp['ln_b'][1])
    hidden = mm('ffn_w1', xb, out_dtype=BF16, act='sqrelu', name='ffn_up')
    x, xb = _add_layernorm(x, _matmul(hidden, lp['ffn_w2'], name='ffn_down'), lp['ln_g'][2], lp['ln_b'][2])
    return x, xb, (rows, win_new, conv_new, h_new, s_new)


def kernel(x_prompt, x_sample, cache_nsa_kv, cache_nsa_win, state_ssd, state_ssd_conv, state_gla, cache_mem_kv,
           page_table, mem_prompt, w_in, ssd_conv_w, ssd_conv_b, ssd_dt_bias, ssd_a_log, ssd_d, ssd_norm_w,
           nsa_cmp_pe, nsa_cmp_w, rel_bias, gla_gate_w2, gla_gate_b, gla_norm_w, w_out, x_wq, x_wkv, x_wo,
           ffn_w1, ffn_w2, ln_g, ln_b):
    bp = x_prompt.shape[0]
    bs = x_sample.shape[0]
    xp, xs = x_prompt.reshape(-1, D_MODEL), x_sample.reshape(-1, D_MODEL)
    xpb, xsb = xp.astype(BF16), xs.astype(BF16)
    st_p, st_s, mem_p = [], [], []
    tables = _nsa_prompt_bias_tables_t(rel_bias)
    f32_weights = dict(w_out=w_out, x_wq=x_wq, x_wo=x_wo, ffn_w1=ffn_w1)
    mem_b = mem_prompt.reshape(-1, D_MODEL).astype(BF16)
    for l in range(DEPTH):
        lp = dict(w_in=_reorder_w_in(w_in[l]), ssd_conv_w=ssd_conv_w[l], ssd_conv_b=ssd_conv_b[l],
                  ssd_dt_bias=ssd_dt_bias[l], ssd_a_log=ssd_a_log[l], ssd_d=ssd_d[l], ssd_norm_w=ssd_norm_w[l],
                  nsa_cmp_pe=nsa_cmp_pe[l], nsa_cmp_w=nsa_cmp_w[l], gla_gate_w2=gla_gate_w2[l], gla_gate_b=gla_gate_b[l],
                  gla_norm_w=gla_norm_w[l], ffn_w2=_to_bf16(ffn_w2, l), ln_g=ln_g[l], ln_b=ln_b[l])
        mem_kv_p, _ = _matmul_cast(mem_b, x_wkv, l, name='mem_kv')
        mem_kv_p = mem_kv_p.reshape(bp, N_MEM, 2 * D_MODEL)
        conv0 = jnp.zeros((bp, SSD_CONV - 1, SSD_CONV_DIM), F32)
        h0 = jnp.zeros((bp, SSD_HEADS, SSD_HEAD_DIM, SSD_STATE), F32)
        s0 = jnp.zeros((bp, GLA_HEADS, GLA_DK, GLA_DV), F32)
        xp, xpb, stp = _layer_flat(xp, xpb, bp, lp, f32_weights, l, rel_bias, tables, mem_kv_p, conv0, h0, s0, None, None)
        st_p.append(stp)
        mem_p.append(mem_kv_p.reshape(bp, N_MEM, 2, X_HEADS, X_HEAD_DIM))
        xs, xsb, sts = _layer_flat(xs, xsb, bs, lp, f32_weights, l, rel_bias, tables,
                                   cache_mem_kv[l].reshape(bs, N_MEM, 2 * D_MODEL), state_ssd_conv[l], state_ssd[l],
                                   state_gla[l], (cache_nsa_kv, l, page_table), cache_nsa_win[l])
        st_s.append(sts)
    p_rows, p_win, p_conv, p_ssd, p_gla = [jnp.stack(s) for s in zip(*st_p)]
    s_rows, s_win, s_conv, s_ssd, s_gla = [jnp.stack(s) for s in zip(*st_s)]
    p_mem = jnp.stack(mem_p)
    return (xp.reshape(x_prompt.shape), xs.reshape(x_sample.shape), p_rows, s_rows, p_win, s_win, p_ssd, s_ssd,
            p_conv, s_conv, p_gla, s_gla, p_mem)
```

```python
import functools
import math

import jax
import jax.numpy as jnp
import numpy as np
from jax import lax
from jax.experimental import pallas as pl
from jax.experimental.pallas import tpu as pltpu

F32 = jnp.float32
BF16 = jnp.bfloat16

D_MODEL = 4096
DEPTH = 2
PAGE_SIZE = 128
D_MIX = D_MODEL
D_SSD = D_MIX // 2
SSD_HEAD_DIM = 64
SSD_HEADS = D_SSD // SSD_HEAD_DIM
SSD_GROUPS = 8
SSD_STATE = 128
SSD_CONV = 4
SSD_CHUNK = 128
SSD_CONV_DIM = D_SSD + 2 * SSD_GROUPS * SSD_STATE
D_NSA = D_MIX // 4
NSA_HEAD_DIM = 128
NSA_HEADS = D_NSA // NSA_HEAD_DIM
NSA_KV_HEADS = 2
NSA_REP = NSA_HEADS // NSA_KV_HEADS
CMP_LEN = 32
CMP_STRIDE = 16
SEL_LEN = 64
TOP_N = 16
WINDOW = 512
Q_BLOCK = 128
FORCE_SCORE = 1e4
D_GLA = D_MIX - D_SSD - D_NSA
GLA_HEADS = 4
GLA_DV = D_GLA // GLA_HEADS
GLA_DK = GLA_DV // 2
GLA_RANK = 16
GLA_TAU = 16.0
GLA_CHUNK = 16
N_MEM = 256
X_HEADS = 4
X_HEAD_DIM = D_MODEL // X_HEADS
D_FF = 4 * D_MODEL
N_BUCKETS = 32
MAX_DISTANCE = 128
LN_EPS = 1e-5
NORM_EPS = 1e-6
DN_ALPHA = (2 * DEPTH) ** 0.25
IN_SPLITS = (D_SSD, SSD_CONV_DIM, SSD_HEADS, NSA_HEADS * NSA_HEAD_DIM) + (NSA_KV_HEADS * NSA_HEAD_DIM,) * 6 + (
    3 * NSA_HEADS, GLA_HEADS * GLA_DK, GLA_HEADS * GLA_DK, D_GLA, D_GLA, GLA_RANK)
D_IN = sum(IN_SPLITS)

LANE = 128
VMEM_LIMIT_BYTES = 56 * 1024 * 1024

KVW = NSA_KV_HEADS * NSA_HEAD_DIM
Z_OFF = 0
XBC_OFF = Z_OFF + D_SSD
QN_OFF = XBC_OFF + SSD_CONV_DIM
VG_OFF = QN_OFF + NSA_HEADS * NSA_HEAD_DIM
GG_OFF = VG_OFF + D_GLA
KV_OFF = GG_OFF + D_GLA
QG_OFF = KV_OFF + 6 * KVW
KG_OFF = QG_OFF + GLA_HEADS * GLA_DK
SMALL_OFF = KG_OFF + GLA_HEADS * GLA_DK
DT_OFF = SMALL_OFF
ALR_OFF = SMALL_OFF + SSD_HEADS
GATE_OFF = SMALL_OFF + LANE
D_IN_PAD = 12288
NEG = -1e30


def _in_proj_column_map():
    off = np.cumsum((0,) + IN_SPLITS)
    z, xbc, dt, qn = off[0], off[1], off[2], off[3]
    kv0, gn, qg, kg, vg, gg, alr = off[4], off[10], off[11], off[12], off[13], off[14], off[15]
    cmap = np.full((D_IN_PAD,), -1, np.int64)
    cmap[Z_OFF:Z_OFF + D_SSD] = z + np.arange(D_SSD)
    cmap[XBC_OFF:XBC_OFF + SSD_CONV_DIM] = xbc + np.arange(SSD_CONV_DIM)
    cmap[QN_OFF:QN_OFF + D_NSA] = qn + np.arange(D_NSA)
    cmap[KV_OFF:KV_OFF + 6 * KVW] = kv0 + np.arange(6 * KVW)
    cmap[QG_OFF:QG_OFF + 512] = qg + np.arange(512)
    cmap[KG_OFF:KG_OFF + 512] = kg + np.arange(512)
    cmap[VG_OFF:VG_OFF + D_GLA] = vg + np.arange(D_GLA)
    cmap[GG_OFF:GG_OFF + D_GLA] = gg + np.arange(D_GLA)
    cmap[DT_OFF:DT_OFF + SSD_HEADS] = dt + np.arange(SSD_HEADS)
    cmap[ALR_OFF:ALR_OFF + GLA_RANK] = alr + np.arange(GLA_RANK)
    for g in range(NSA_KV_HEADS):
        cmap[GATE_OFF + g * LANE:GATE_OFF + g * LANE + 3 * NSA_REP] = gn + g * 3 * NSA_REP + np.arange(3 * NSA_REP)
    return cmap


def _reorder_w_in(w):
    cmap = _in_proj_column_map()
    cuts = [0] + [i for i in range(1, D_IN_PAD) if cmap[i] != cmap[i - 1] + 1 and not (cmap[i] == -1 and cmap[i - 1] == -1)] + [D_IN_PAD]
    pieces = []
    for a, b in zip(cuts[:-1], cuts[1:]):
        pieces.append(jnp.zeros((w.shape[0], b - a), w.dtype) if cmap[a] < 0 else w[:, cmap[a]:cmap[a] + b - a])
    return jnp.concatenate(pieces, axis=1).astype(BF16)


def _dot_nt(a, b):
    return lax.dot_general(a, b, (((1,), (1,)), ((), ())), preferred_element_type=F32)


NSA_L = 2048
NSA_QB = Q_BLOCK
NSA_NQB = NSA_L // NSA_QB
NSA_NCP = NSA_L // CMP_STRIDE
NSA_NSEL = NSA_L // SEL_LEN
CMP_HALF = CMP_STRIDE * NSA_HEAD_DIM
NSA_KSTEP = 512


def _t5_bucket_np(dist):
    n = np.maximum(dist, 0)
    max_exact = N_BUCKETS // 2
    nf = np.maximum(n, 1).astype(np.float32)
    large = max_exact + (np.log(nf / max_exact) / np.float32(math.log(MAX_DISTANCE / max_exact)) * (N_BUCKETS - max_exact)).astype(np.int32)
    return np.where(n < max_exact, n, np.minimum(large, N_BUCKETS - 1)).astype(np.int32)


def _nsa_compress_kernel(a_ref, pe_ref, w_ref, o_ref):
    a = a_ref[0, 0, 0]
    lo = jnp.dot((a + pe_ref[0, 0]).astype(BF16), w_ref[0, 0], preferred_element_type=F32)
    hi = jnp.dot((a + pe_ref[0, 1]).astype(BF16), w_ref[0, 1], preferred_element_type=F32)
    o_ref[0, 0, 0] = lo + pltpu.roll(hi, NSA_NCP - 1, 0)


def _nsa_compress_prompt(y, cmp_pe, cmp_w, bsz):
    a = y[:, KV_OFF:KV_OFF + 2 * KVW].reshape(bsz, NSA_NCP, CMP_STRIDE, 2, NSA_KV_HEADS, NSA_HEAD_DIM)
    a = a.transpose(0, 3, 4, 1, 2, 5).reshape(bsz, 2, NSA_KV_HEADS, NSA_NCP, CMP_HALF)
    pe = cmp_pe.reshape(2, 2, 1, CMP_HALF)
    w = cmp_w.astype(BF16).reshape(2, 2, CMP_HALF, NSA_HEAD_DIM)
    return pl.pallas_call(
        _nsa_compress_kernel,
        grid=(bsz, 2, NSA_KV_HEADS),
        in_specs=[pl.BlockSpec((1, 1, 1, NSA_NCP, CMP_HALF), lambda b, c, g: (b, c, g, 0, 0)),
                  pl.BlockSpec((1, 2, 1, CMP_HALF), lambda b, c, g: (c, 0, 0, 0)),
                  pl.BlockSpec((1, 2, CMP_HALF, NSA_HEAD_DIM), lambda b, c, g: (c, 0, 0, 0))],
        out_specs=pl.BlockSpec((1, 1, 1, NSA_NCP, NSA_HEAD_DIM), lambda b, c, g: (b, c, g, 0, 0)),
        out_shape=jax.ShapeDtypeStruct((bsz, 2, NSA_KV_HEADS, NSA_NCP, NSA_HEAD_DIM), F32),
        compiler_params=pltpu.CompilerParams(dimension_semantics=('parallel', 'parallel', 'parallel')),
        name='nsa_compress',
    )(a, pe, w)


def _bias_lookup(rel_bias, idx):
    return rel_bias.T[:, jnp.asarray(idx)]


def _bucket_thresholds():
    buckets = _t5_bucket_np(np.arange(4 * MAX_DISTANCE))
    return [int(np.argmax(buckets >= k)) for k in range(N_BUCKETS)]


def _nsa_prompt_bias_tables_t(rel_bias):
    tq = np.arange(NSA_QB)[None, :]
    ts = np.arange(NSA_QB)[:, None]

    def table(idx):
        t = _bias_lookup(rel_bias, idx).reshape((NSA_KV_HEADS, NSA_REP) + idx.shape)
        t = jnp.moveaxis(t, 1, -2)
        return t.reshape(t.shape[:-2] + (NSA_REP * NSA_QB,))

    b_diag = table(_t5_bucket_np(tq - ts))
    b_prev = table(_t5_bucket_np(NSA_QB + tq - ts))
    by_bucket = jnp.repeat(rel_bias.reshape(N_BUCKETS, NSA_KV_HEADS, NSA_REP).transpose(1, 0, 2), NSA_QB, axis=2)
    return b_diag, b_prev, by_bucket


def _nsa_prompt_t_kernel(q_ref, gate_ref, cmp_ref, ks_ref, vs_ref, kw_ref, vw_ref, bdiag_ref, bprev_ref, bkt_ref,
                         covt_ref, expand_ref, o_ref, m_ref, l_ref, acc_ref, ksb, vst, kwb, vwt, bcmp_scr):
    i = pl.program_id(2)
    cols = NSA_REP * NSA_QB

    @pl.when(i == 0)
    def _():
        ksb[...] = ks_ref[...].astype(BF16)
        kwb[...] = kw_ref[...].astype(BF16)
        for c in range(NSA_L // NSA_KSTEP):
            vst[c] = vs_ref[c * NSA_KSTEP:(c + 1) * NSA_KSTEP, :].T.astype(BF16)
            vwt[c] = vw_ref[c * NSA_KSTEP:(c + 1) * NSA_KSTEP, :].T.astype(BF16)

    qblk = q_ref[...] * (NSA_HEAD_DIM ** -0.5)
    q_t = jnp.concatenate([qblk[:, r * LANE:(r + 1) * LANE].T for r in range(NSA_REP)], axis=1).astype(BF16)
    key = lax.broadcasted_iota(jnp.int32, (NSA_QB, cols), 0)
    tq = lax.broadcasted_iota(jnp.int32, (NSA_QB, cols), 1) % NSA_QB
    t_abs = i * NSA_QB + tq

    k_c = cmp_ref[0, 0, 0].astype(BF16)
    v_c_t = cmp_ref[0, 1, 0].T.astype(BF16)
    mask_c = key * CMP_STRIDE + (CMP_LEN - 1) <= t_abs
    bfar = bkt_ref[0, N_BUCKETS - 1:N_BUCKETS, :]
    per_qb = NSA_QB // CMP_STRIDE
    w0 = pl.multiple_of(jnp.maximum(i * per_qb - 2 * per_qb, 0), 8)
    tok = w0 + lax.broadcasted_iota(jnp.int32, (3 * per_qb, cols), 0)
    t_w = i * NSA_QB + lax.broadcasted_iota(jnp.int32, (3 * per_qb, cols), 1) % NSA_QB
    dist_c = t_w - (tok * CMP_STRIDE + (CMP_LEN - 1))
    bias_w = jnp.zeros((3 * per_qb, cols), F32) + bkt_ref[0, 0:1, :]
    for k, thr in enumerate(_bucket_thresholds()):
        if k > 0:
            bias_w = jnp.where(dist_c >= thr, bkt_ref[0, k:k + 1, :], bias_w)
    bcmp_scr[...] = jnp.zeros((NSA_NCP, cols), F32) + bfar
    bcmp_scr[pl.ds(w0, 3 * per_qb), :] = bias_w
    s_c = jnp.where(mask_c, jnp.dot(k_c, q_t, preferred_element_type=F32) + bcmp_scr[...], NEG)
    e_c = jnp.where(mask_c, jnp.exp(s_c - jnp.max(s_c, 0, keepdims=True)), 0.0)
    d_c = jnp.sum(e_c, 0, keepdims=True)
    p_c = (e_c / jnp.where(d_c > 0, d_c, 1.0)).astype(BF16)
    o_c = jnp.dot(v_c_t, p_c, preferred_element_type=F32)

    imp4 = jnp.dot(covt_ref[...], p_c, preferred_element_type=F32)
    imp = sum(imp4[:, r * NSA_QB:(r + 1) * NSA_QB] for r in range(NSA_REP))
    blk = lax.broadcasted_iota(jnp.int32, (LANE, NSA_QB), 0)
    t_row = i * NSA_QB + lax.broadcasted_iota(jnp.int32, (LANE, NSA_QB), 1)
    cur = t_row // SEL_LEN
    forced = (blk == 0) | (blk == cur) | (blk == cur - 1)
    score = jnp.where(forced, FORCE_SCORE, jnp.where(blk * SEL_LEN <= t_row, imp, -FORCE_SCORE))
    score = jnp.where(blk < NSA_NSEL, score, -3.0 * FORCE_SCORE)
    rank = jnp.zeros((LANE, NSA_QB), jnp.int32)
    for j in range(NSA_NSEL):
        row = score[j:j + 1, :]
        rank = rank + jnp.where((row > score) | ((row == score) & (blk > j)), 1, 0)
    sel_t = jnp.where((rank < TOP_N) & (blk < NSA_NSEL), 1.0, 0.0).astype(BF16)

    def init():
        m_ref[...] = jnp.full((1, cols), NEG, F32)
        l_ref[...] = jnp.zeros((1, cols), F32)
        acc_ref[...] = jnp.zeros((NSA_HEAD_DIM, cols), F32)

    dist0 = (lax.broadcasted_iota(jnp.int32, (NSA_KSTEP, cols), 1) % NSA_QB
             - lax.broadcasted_iota(jnp.int32, (NSA_KSTEP, cols), 0))
    per_step = NSA_KSTEP // NSA_QB

    def step(k_scr, v_scr, j, selected):
        start = pl.multiple_of(j * NSA_KSTEP, NSA_KSTEP)
        s = jnp.dot(k_scr[pl.ds(start, NSA_KSTEP), :], q_t, preferred_element_type=F32)
        bias = []
        for u in range(per_step):
            d = i - (j * per_step + u)
            bias.append(jnp.where(d == 0, bdiag_ref[0], jnp.where(d == 1, bprev_ref[0], bfar)))
        dist = dist0 + (i * NSA_QB - j * NSA_KSTEP)
        if selected:
            m1 = jnp.dot(expand_ref[j], sel_t, preferred_element_type=F32)
            mask = (jnp.concatenate([m1] * NSA_REP, axis=1) > 0.5) & (dist >= 0)
        else:
            mask = (dist >= 0) & (dist < WINDOW)
        s = jnp.where(mask, s + jnp.concatenate(bias, axis=0), NEG)
        m_old = m_ref[...]
        m_new = jnp.maximum(m_old, jnp.max(s, 0, keepdims=True))
        p = jnp.where(mask, jnp.exp(s - m_new), 0.0)
        alpha = jnp.exp(m_old - m_new)
        l_ref[...] = alpha * l_ref[...] + jnp.sum(p, 0, keepdims=True)
        acc_ref[...] = alpha * acc_ref[...] + jnp.dot(v_scr[j], p.astype(BF16), preferred_element_type=F32)
        m_ref[...] = m_new

    def finish():
        l = l_ref[...]
        return acc_ref[...] / jnp.where(l > 0, l, 1.0)

    j_diag = i // per_step

    def far_step(j):
        start = pl.multiple_of(j * NSA_KSTEP, NSA_KSTEP)
        m1 = jnp.dot(expand_ref[j], sel_t, preferred_element_type=F32)
        off = (1.0 - m1) * NEG
        s = jnp.dot(ksb[pl.ds(start, NSA_KSTEP), :], q_t, preferred_element_type=F32) + bfar + jnp.concatenate([off] * NSA_REP, axis=1)
        m_old = m_ref[...]
        m_new = jnp.maximum(m_old, jnp.max(s, 0, keepdims=True))
        p = jnp.exp(s - m_new)
        alpha = jnp.exp(m_old - m_new)
        l_ref[...] = alpha * l_ref[...] + jnp.sum(p, 0, keepdims=True)
        acc_ref[...] = alpha * acc_ref[...] + jnp.dot(vst[j], p.astype(BF16), preferred_element_type=F32)
        m_ref[...] = m_new

    init()
    n_far = jnp.maximum(i - 1, 0) // per_step

    def far_body(j, carry):
        far_step(j)
        return carry

    def sel_body(j, carry):
        step(ksb, vst, j, True)
        return carry

    lax.fori_loop(0, n_far, far_body, 0)
    lax.fori_loop(n_far, j_diag + 1, sel_body, 0)
    o_s = finish()

    init()
    step(kwb, vwt, j_diag, False)

    @pl.when(j_diag >= 1)
    def _():
        step(kwb, vwt, j_diag - 1, False)

    o_w = finish()

    g_t = jax.nn.sigmoid(gate_ref[...]).T
    for r in range(NSA_REP):
        cs = slice(r * NSA_QB, (r + 1) * NSA_QB)
        o_r = g_t[3 * r:3 * r + 1, :] * o_c[:, cs] + g_t[3 * r + 1:3 * r + 2, :] * o_s[:, cs] + g_t[3 * r + 2:3 * r + 3, :] * o_w[:, cs]
        o_ref[:, r * LANE:(r + 1) * LANE] = o_r.T.astype(o_ref.dtype)


def _nsa_prompt_t(y, k_v_cmp, tables, bsz):
    b_diag, b_prev, by_bucket = tables
    cols = NSA_REP * NSA_QB
    nstep = NSA_L // NSA_KSTEP
    assert WINDOW <= NSA_KSTEP and NSA_KSTEP % NSA_QB == 0
    c_lo = np.arange(NSA_NCP) * CMP_STRIDE
    s_lo = np.arange(LANE) * SEL_LEN
    cov_t = ((c_lo[None] < s_lo[:, None] + SEL_LEN) & (c_lo[None] + CMP_LEN > s_lo[:, None])
             & (np.arange(NSA_NCP)[None] < NSA_NCP - 1) & (np.arange(LANE)[:, None] < NSA_NSEL))
    key_blk = (np.arange(NSA_L) // SEL_LEN).reshape(nstep, NSA_KSTEP, 1)
    expand = (np.arange(LANE)[None, None, :] == key_blk)
    kv_blk = KV_OFF // LANE
    grp = NSA_KV_HEADS

    def col(slot):
        return pl.BlockSpec((NSA_L, NSA_HEAD_DIM), lambda b, g, i: (b, kv_blk + slot * grp + g))

    tab = pl.BlockSpec((1, NSA_QB, cols), lambda b, g, i: (g, 0, 0))
    return pl.pallas_call(
        _nsa_prompt_t_kernel,
        grid=(bsz, NSA_KV_HEADS, NSA_NQB),
        in_specs=[pl.BlockSpec((NSA_QB, NSA_REP * LANE), lambda b, g, i: (b * NSA_NQB + i, QN_OFF // (NSA_REP * LANE) + g)),
                  pl.BlockSpec((NSA_QB, LANE), lambda b, g, i: (b * NSA_NQB + i, GATE_OFF // LANE + g)),
                  pl.BlockSpec((1, 2, 1, NSA_NCP, NSA_HEAD_DIM), lambda b, g, i: (b, 0, g, 0, 0)),
                  col(2), col(3), col(4), col(5), tab, tab,
                  pl.BlockSpec((1, N_BUCKETS, cols), lambda b, g, i: (g, 0, 0)),
                  pl.BlockSpec((LANE, NSA_NCP), lambda b, g, i: (0, 0)),
                  pl.BlockSpec((nstep, NSA_KSTEP, LANE), lambda b, g, i: (0, 0, 0))],
        out_specs=pl.BlockSpec((NSA_QB, NSA_REP * LANE), lambda b, g, i: (b * NSA_NQB + i, g)),
        out_shape=jax.ShapeDtypeStruct((bsz * NSA_L, D_NSA), BF16),
        scratch_shapes=[pltpu.VMEM((1, cols), F32), pltpu.VMEM((1, cols), F32), pltpu.VMEM((NSA_HEAD_DIM, cols), F32),
                        pltpu.VMEM((NSA_L, NSA_HEAD_DIM), BF16), pltpu.VMEM((nstep, NSA_HEAD_DIM, NSA_KSTEP), BF16),
                        pltpu.VMEM((NSA_L, NSA_HEAD_DIM), BF16), pltpu.VMEM((nstep, NSA_HEAD_DIM, NSA_KSTEP), BF16),
                        pltpu.VMEM((NSA_NCP, cols), F32)],
        compiler_params=pltpu.CompilerParams(dimension_semantics=('parallel', 'parallel', 'arbitrary'),
                                             vmem_limit_bytes=VMEM_LIMIT_BYTES),
        name='nsa_prompt',
    )(y, y, k_v_cmp, y, y, y, y, b_diag, b_prev, by_bucket,
      jnp.asarray(cov_t, BF16), jnp.asarray(expand, BF16))


NS_SELW = 3 * LANE
NS_CHUNK = 2048


def _nsa_sample_tables(rel_bias, past_len, lq, win_len):
    tq = (past_len + np.arange(lq))[:, None]

    def table(idx):
        t = _bias_lookup(rel_bias, idx)
        return t.reshape(NSA_KV_HEADS, NSA_REP * lq, idx.shape[-1])

    ncp = past_len // CMP_STRIDE
    b_cmp = table(_t5_bucket_np(tq - (np.arange(ncp)[None] * CMP_STRIDE + CMP_LEN - 1)))
    b_last = table(_t5_bucket_np(tq - (past_len - LANE + np.arange(LANE)[None])))
    b_new = table(_t5_bucket_np(tq - (past_len + np.arange(LANE)[None])))
    b_far = table(np.full((lq, LANE), N_BUCKETS - 1, np.int32))
    wpos = np.concatenate([past_len - win_len + np.arange(win_len), past_len + np.arange(LANE)])
    b_win = table(_t5_bucket_np(tq - wpos[None]))
    return b_cmp, b_last, b_new, b_far, b_win


def _nsa_sample_kernel(pt_ref, q_ref, kvn_ref, gate0_ref, gate1_ref, win_ref, cache_ref, w_ref, pe_ref,
                       bcmp_ref, blast_ref, bnew_ref, bfar_ref, bwin_ref, cov_ref, expand_ref, o_ref,
                       buf, cmp_scr, s_scr, sem, *, n_pages, lq, past_len, win_len, layer):
    b = pl.program_id(0)
    rows = NSA_REP * lq
    ncp = past_len // CMP_STRIDE
    n_cmp = (past_len + lq - CMP_LEN) // CMP_STRIDE + 1
    n_sel = -(-(past_len + lq) // SEL_LEN)
    n_chunks = past_len // NS_CHUNK
    ncol = 2 * NSA_KV_HEADS

    def page_copy(j, pair, col):
        slot, g = pair * 2 + col // NSA_KV_HEADS, col % NSA_KV_HEADS
        return pltpu.make_async_copy(cache_ref.at[layer, pt_ref[b, j], :, slot, g, :],
                                     buf.at[col, pl.ds(j * PAGE_SIZE, PAGE_SIZE), :], sem)

    def gather_start(pair):
        def body(j, carry):
            for col in range(ncol):
                page_copy(j, pair, col).start()
            return carry
        lax.fori_loop(0, n_pages, body, 0)

    def gather_wait(pair):
        def body(j, carry):
            for col in range(ncol):
                page_copy(j, pair, col).wait()
            return carry
        lax.fori_loop(0, n_pages, body, 0)

    gather_start(0)
    gather_wait(0)
    for c in range(2):
        for g in range(NSA_KV_HEADS):
            lo = jnp.zeros((ncp, NSA_HEAD_DIM), F32)
            hi = jnp.zeros((ncp, NSA_HEAD_DIM), F32)
            for l in range(CMP_STRIDE):
                x = buf[c * NSA_KV_HEADS + g, pl.ds(l, ncp, stride=CMP_STRIDE), :]
                lo = lo + jnp.dot((x + pe_ref[c, l:l + 1, :]).astype(BF16), w_ref[c, l], preferred_element_type=F32)
                hi = hi + jnp.dot((x + pe_ref[c, CMP_STRIDE + l:CMP_STRIDE + l + 1, :]).astype(BF16),
                                  w_ref[c, CMP_STRIDE + l], preferred_element_type=F32)
            cmp_scr[c, g] = lo + pltpu.roll(hi, ncp - 1, 0)
    gather_start(1)

    qall = q_ref[...] * (NSA_HEAD_DIM ** -0.5)
    tq = lax.broadcasted_iota(jnp.int32, (rows, 1), 0) % lq
    t_abs = past_len + tq
    o_c, sel, q_g = [], [], []
    for g in range(NSA_KV_HEADS):
        q = jnp.concatenate([qall[:, (g * NSA_REP + r) * LANE:(g * NSA_REP + r + 1) * LANE] for r in range(NSA_REP)],
                            axis=0).astype(BF16)
        q_g.append(q)
        n_idx = lax.broadcasted_iota(jnp.int32, (rows, ncp), 1)
        mask_c = (n_idx < n_cmp) & (n_idx * CMP_STRIDE + (CMP_LEN - 1) <= t_abs)
        s_c = jnp.where(mask_c, _dot_nt(q, cmp_scr[0, g].astype(BF16)) + bcmp_ref[g], NEG)
        e_c = jnp.where(mask_c, jnp.exp(s_c - jnp.max(s_c, -1, keepdims=True)), 0.0)
        d_c = jnp.sum(e_c, -1, keepdims=True)
        p_c = (e_c / jnp.where(d_c > 0, d_c, 1.0)).astype(BF16)
        o_c.append(jnp.dot(p_c, cmp_scr[1, g].astype(BF16), preferred_element_type=F32))
        imp4 = jnp.dot(p_c, cov_ref[...], preferred_element_type=F32)
        imp = sum(imp4[r * lq:(r + 1) * lq] for r in range(NSA_REP))
        blk = lax.broadcasted_iota(jnp.int32, (lq, NS_SELW), 1)
        t_q = past_len + lax.broadcasted_iota(jnp.int32, (lq, NS_SELW), 0)
        cur = t_q // SEL_LEN
        forced = (blk == 0) | (blk == cur) | (blk == cur - 1)
        score = jnp.where(forced, FORCE_SCORE, jnp.where(blk * SEL_LEN <= t_q, imp, -FORCE_SCORE))
        score = jnp.where(blk < n_sel, score, -3.0 * FORCE_SCORE)
        rank = jnp.zeros((lq, NS_SELW), jnp.int32)
        for j in range(n_sel):
            col = score[:, j:j + 1]
            rank = rank + jnp.where((col > score) | ((col == score) & (blk > j)), 1, 0)
        sel_q = jnp.where((rank < min(TOP_N, n_sel)) & (blk < n_sel), 1.0, 0.0)
        sel.append(jnp.concatenate([sel_q] * NSA_REP, axis=0))

    gather_wait(1)
    kvn = _pad_rows(kvn_ref[...], LANE)
    lane = lax.broadcasted_iota(jnp.int32, (rows, LANE), 1)
    blocks_per_chunk = NS_CHUNK // SEL_LEN
    gates = [jax.nn.sigmoid(gate0_ref[...]), jax.nn.sigmoid(gate1_ref[...])]
    for g in range(NSA_KV_HEADS):
        q = q_g[g]
        bfar = bfar_ref[g][:, 0:1]

        def new_rows(slot):
            return kvn[:, (slot * NSA_KV_HEADS + g) * NSA_HEAD_DIM:(slot * NSA_KV_HEADS + g + 1) * NSA_HEAD_DIM].astype(BF16)

        sel_b = sel[g].astype(BF16)
        for c in range(n_chunks):
            k = buf[g, c * NS_CHUNK:(c + 1) * NS_CHUNK, :].astype(BF16)
            s = _dot_nt(q, k) + bfar
            if c == n_chunks - 1:
                fix = blast_ref[g] - bfar_ref[g]
                s = jnp.concatenate([s[:, :NS_CHUNK - LANE], s[:, NS_CHUNK - LANE:] + fix], axis=1)
            m_c = jnp.dot(sel_b[:, c * blocks_per_chunk:(c + 1) * blocks_per_chunk], expand_ref[...],
                          preferred_element_type=F32)
            s_scr[:, c * NS_CHUNK:(c + 1) * NS_CHUNK] = jnp.where(m_c > 0.5, s, NEG)
        s_new = _dot_nt(q, new_rows(2)) + bnew_ref[g]
        mask_new = (lane <= tq) & (lane < lq) & (sel[g][:, n_sel - 1:n_sel] > 0.5)
        s_scr[:, past_len:past_len + LANE] = jnp.where(mask_new, s_new, NEG)
        m = jnp.max(s_scr[...], -1, keepdims=True)
        acc = jnp.zeros((rows, NSA_HEAD_DIM), F32)
        den = jnp.zeros((rows, 1), F32)
        for c in range(n_chunks):
            sc = s_scr[:, c * NS_CHUNK:(c + 1) * NS_CHUNK]
            e = jnp.where(sc > 0.5 * NEG, jnp.exp(sc - m), 0.0)
            den = den + jnp.sum(e, -1, keepdims=True)
            v = buf[NSA_KV_HEADS + g, c * NS_CHUNK:(c + 1) * NS_CHUNK, :]
            acc = acc + jnp.dot(e.astype(BF16), v.astype(BF16), preferred_element_type=F32)
        sc = s_scr[:, past_len:past_len + LANE]
        e = jnp.where(sc > 0.5 * NEG, jnp.exp(sc - m), 0.0)
        den = den + jnp.sum(e, -1, keepdims=True)
        acc = acc + jnp.dot(e.astype(BF16), new_rows(3), preferred_element_type=F32)
        o_s = acc / jnp.where(den > 0, den, 1.0)

        kw = jnp.concatenate([win_ref[0, :, g * NSA_HEAD_DIM:(g + 1) * NSA_HEAD_DIM].astype(BF16), new_rows(4)], axis=0)
        vw = jnp.concatenate([win_ref[0, :, (NSA_KV_HEADS + g) * NSA_HEAD_DIM:(NSA_KV_HEADS + g + 1) * NSA_HEAD_DIM].astype(BF16),
                              new_rows(5)], axis=0)
        wl = lax.broadcasted_iota(jnp.int32, (rows, win_len + LANE), 1)
        dist = jnp.where(wl < win_len, win_len + tq - wl, tq - (wl - win_len))
        mask_w = (dist >= 0) & (dist < WINDOW) & (wl < win_len + lq)
        s_w = jnp.where(mask_w, _dot_nt(q, kw) + bwin_ref[g], NEG)
        e_w = jnp.where(mask_w, jnp.exp(s_w - jnp.max(s_w, -1, keepdims=True)), 0.0)
        d_w = jnp.sum(e_w, -1, keepdims=True)
        o_w = jnp.dot((e_w / jnp.where(d_w > 0, d_w, 1.0)).astype(BF16), vw, preferred_element_type=F32)

        for r in range(NSA_REP):
            rs = slice(r * lq, (r + 1) * lq)
            gt = gates[g]
            h = g * NSA_REP + r
            o_ref[:, h * LANE:(h + 1) * LANE] = (gt[:, 3 * r:3 * r + 1] * o_c[g][rs] + gt[:, 3 * r + 1:3 * r + 2] * o_s[rs]
                                                 + gt[:, 3 * r + 2:3 * r + 3] * o_w[rs]).astype(o_ref.dtype)


def _nsa_sample(y, cache_kv, layer, page_table, cache_win, cmp_pe, cmp_w, rel_bias, bsz):
    lq = y.shape[0] // bsz
    n_pages = page_table.shape[1]
    past_len = n_pages * PAGE_SIZE
    win_len = cache_win.shape[1]
    rows = NSA_REP * lq
    ncp = past_len // CMP_STRIDE
    n_cmp = (past_len + lq - CMP_LEN) // CMP_STRIDE + 1
    n_sel = -(-(past_len + lq) // SEL_LEN)
    assert lq % 8 == 0 and lq <= CMP_STRIDE and n_cmp <= ncp - 1 + lq // CMP_STRIDE and past_len % NS_CHUNK == 0
    assert past_len % SEL_LEN == 0 and n_sel <= NS_SELW and lq % Q_BLOCK != 0 and win_len == WINDOW
    tables = _nsa_sample_tables(rel_bias, past_len, lq, win_len)
    c_lo = np.arange(ncp) * CMP_STRIDE
    s_lo = np.arange(NS_SELW) * SEL_LEN
    cover = ((c_lo[:, None] < s_lo[None] + SEL_LEN) & (c_lo[:, None] + CMP_LEN > s_lo[None])
             & (np.arange(ncp)[:, None] < n_cmp) & (np.arange(NS_SELW)[None] < n_sel))
    expand = np.arange(NS_CHUNK // SEL_LEN)[:, None] == (np.arange(NS_CHUNK) // SEL_LEN)[None]
    assert cache_kv.shape[2:] == (PAGE_SIZE, 4, NSA_KV_HEADS, NSA_HEAD_DIM)
    win2 =cache_win.reshape(bsz, win_len, 2 * KVW)

    def full(shape):
        return pl.BlockSpec(shape, lambda b, pt: (0,) * len(shape))

    grid_spec = pltpu.PrefetchScalarGridSpec(
        num_scalar_prefetch=1, grid=(bsz,),
        in_specs=[pl.BlockSpec((lq, D_NSA), lambda b, pt: (b, QN_OFF // D_NSA)),
                  pl.BlockSpec((lq, 6 * KVW), lambda b, pt: (b, KV_OFF // (6 * KVW))),
                  pl.BlockSpec((lq, LANE), lambda b, pt: (b, GATE_OFF // LANE)),
                  pl.BlockSpec((lq, LANE), lambda b, pt: (b, GATE_OFF // LANE + 1)),
                  pl.BlockSpec((1, win_len, 2 * KVW), lambda b, pt: (b, 0, 0)),
                  pl.BlockSpec(memory_space=pl.ANY),
                  full((2, CMP_LEN, NSA_HEAD_DIM, NSA_HEAD_DIM)), full((2, CMP_LEN, NSA_HEAD_DIM)),
                  full((NSA_KV_HEADS, rows, ncp)), full((NSA_KV_HEADS, rows, LANE)), full((NSA_KV_HEADS, rows, LANE)),
                  full((NSA_KV_HEADS, rows, LANE)), full((NSA_KV_HEADS, rows, win_len + LANE)),
                  full((ncp, NS_SELW)), full((NS_CHUNK // SEL_LEN, NS_CHUNK))],
        out_specs=pl.BlockSpec((lq, D_NSA), lambda b, pt: (b, 0)),
        scratch_shapes=[pltpu.VMEM((2 * NSA_KV_HEADS, past_len, NSA_HEAD_DIM), F32),
                        pltpu.VMEM((2, NSA_KV_HEADS, ncp, NSA_HEAD_DIM), F32),
                        pltpu.VMEM((rows, past_len + LANE), F32), pltpu.SemaphoreType.DMA(())])
    assert QN_OFF % D_NSA == 0 and KV_OFF % (6 * KVW) == 0
    return pl.pallas_call(
        functools.partial(_nsa_sample_kernel, n_pages=n_pages, lq=lq, past_len=past_len, win_len=win_len,
                          layer=layer),
        grid_spec=grid_spec,
        out_shape=jax.ShapeDtypeStruct((bsz * lq, D_NSA), BF16),
        compiler_params=pltpu.CompilerParams(dimension_semantics=('arbitrary',), vmem_limit_bytes=VMEM_LIMIT_BYTES),
        name='nsa_sample',
    )(page_table, y, y, y, y, win2, cache_kv, cmp_w.astype(BF16), cmp_pe, *tables,
      jnp.asarray(cover, BF16), jnp.asarray(expand, BF16))


def _mm_kernel(x_ref, w_ref, o_ref, *scratch, nk, act):
    def finish(acc):
        if act == 'sqrelu':
            acc = jnp.square(jnp.maximum(acc, 0.0))
        o_ref[...] = acc.astype(o_ref.dtype)

    if nk == 1:
        finish(jnp.dot(x_ref[...], w_ref[...], preferred_element_type=F32))
        return
    acc_ref, = scratch
    k = pl.program_id(2)
    part = jnp.dot(x_ref[...], w_ref[...], preferred_element_type=F32)

    @pl.when(k == 0)
    def _():
        acc_ref[...] = part

    @pl.when(k > 0)
    def _():
        acc_ref[...] += part

    @pl.when(k == nk - 1)
    def _():
        finish(acc_ref[...])


def _pick(n, pref):
    for t in pref:
        if n % t == 0:
            return t
    return n


def _matmul(x, w, out_dtype=F32, act=None, name='matmul'):
    m, k = x.shape
    n = w.shape[1]
    tm = _pick(m, (1024, 512, 256, 128, 64))
    tn = _pick(n, (512, 256, 128))
    tk = _pick(k, (4096, 2048, 1024, 512))
    nk = k // tk
    scratch = [pltpu.VMEM((tm, tn), F32)] if nk > 1 else []
    return pl.pallas_call(
        functools.partial(_mm_kernel, nk=nk, act=act),
        grid=(m // tm, n // tn, nk),
        in_specs=[pl.BlockSpec((tm, tk), lambda i, j, kk: (i, kk)),
                  pl.BlockSpec((tk, tn), lambda i, j, kk: (kk, j))],
        out_specs=pl.BlockSpec((tm, tn), lambda i, j, kk: (i, j)),
        out_shape=jax.ShapeDtypeStruct((m, n), out_dtype),
        scratch_shapes=scratch,
        compiler_params=pltpu.CompilerParams(
            dimension_semantics=('parallel', 'parallel', 'arbitrary'),
            vmem_limit_bytes=VMEM_LIMIT_BYTES),
        name=name,
    )(x, w)


def _mm_ws_kernel(x_ref, w_ref, o_ref, wb_ref, *, act):
    @pl.when(pl.program_id(1) == 0)
    def _():
        wb_ref[...] = w_ref[0].astype(BF16)

    acc = jnp.dot(x_ref[...], wb_ref[...], preferred_element_type=F32)
    if act == 'sqrelu':
        acc = jnp.square(jnp.maximum(acc, 0.0))
    o_ref[...] = acc.astype(o_ref.dtype)


def _matmul_cast(x, w, layer, out_dtype=F32, act=None, name='matmul'):
    m, k = x.shape
    n = w.shape[2]
    tm = _pick(m, (1024,))
    tn = _pick(n, (512,))
    return pl.pallas_call(
        functools.partial(_mm_ws_kernel, act=act),
        grid=(n // tn, m // tm),
        in_specs=[pl.BlockSpec((tm, k), lambda j, i: (i, 0)),
                  pl.BlockSpec((1, k, tn), lambda j, i: (layer, 0, j))],
        out_specs=[pl.BlockSpec((tm, tn), lambda j, i: (i, j)), pl.BlockSpec((k, tn), lambda j, i: (0, j))],
        out_shape=[jax.ShapeDtypeStruct((m, n), out_dtype), jax.ShapeDtypeStruct((k, n), BF16)],
        compiler_params=pltpu.CompilerParams(dimension_semantics=('parallel', 'arbitrary'),
                                             vmem_limit_bytes=VMEM_LIMIT_BYTES),
        name=name,
    )(x, w)


CAST_BLOCK_BYTES = 8 * 1024 * 1024


def _cast_kernel(w_ref, o_ref):
    o_ref[...] = w_ref[0].astype(o_ref.dtype)


def _to_bf16(w, layer):
    _, k, n = w.shape
    tk = CAST_BLOCK_BYTES // (4 * n)
    assert tk % 16 == 0 and k % tk == 0
    return pl.pallas_call(
        _cast_kernel, grid=(k // tk,),
        in_specs=[pl.BlockSpec((1, tk, n), lambda i: (layer, i, 0))], out_specs=pl.BlockSpec((tk, n), lambda i: (i, 0)),
        out_shape=jax.ShapeDtypeStruct((k, n), BF16),
        compiler_params=pltpu.CompilerParams(dimension_semantics=('parallel',), vmem_limit_bytes=VMEM_LIMIT_BYTES),
        name='weight_cast',
    )(w)


def _out_proj_kernel(a_ref, b_ref, c_ref, wa_ref, wb_ref, wc_ref, o_ref):
    o_ref[...] = (jnp.dot(a_ref[...], wa_ref[...], preferred_element_type=F32)
                  + jnp.dot(b_ref[...], wb_ref[...], preferred_element_type=F32)
                  + jnp.dot(c_ref[...], wc_ref[...], preferred_element_type=F32))


def _out_proj(y_ssd, y_nsa, y_gla, w):
    m = y_ssd.shape[0]
    n = w.shape[1]
    tm = _pick(m, (1024, 64))
    tn = _pick(n, (512,))
    assert D_SSD % D_NSA == 0 and D_NSA == D_GLA

    def act(width):
        return pl.BlockSpec((tm, width), lambda i, j: (i, 0))

    return pl.pallas_call(
        _out_proj_kernel, grid=(m // tm, n // tn),
        in_specs=[act(D_SSD), act(D_NSA), act(D_GLA),
                  pl.BlockSpec((D_SSD, tn), lambda i, j: (0, j)),
                  pl.BlockSpec((D_NSA, tn), lambda i, j: (D_SSD // D_NSA, j)),
                  pl.BlockSpec((D_GLA, tn), lambda i, j: (D_SSD // D_NSA + 1, j))],
        out_specs=pl.BlockSpec((tm, tn), lambda i, j: (i, j)),
        out_shape=jax.ShapeDtypeStruct((m, n), F32),
        compiler_params=pltpu.CompilerParams(dimension_semantics=('parallel', 'parallel'),
                                             vmem_limit_bytes=VMEM_LIMIT_BYTES),
        name='out_proj',
    )(y_ssd, y_nsa, y_gla, w, w, w)


def _out_proj_cast_kernel(a_ref, b_ref, c_ref, w_ref, o_ref, wb_ref):
    @pl.when(pl.program_id(1) == 0)
    def _():
        wb_ref[...] = w_ref[0].astype(BF16)

    o_ref[...] = (jnp.dot(a_ref[...], wb_ref[0:D_SSD, :], preferred_element_type=F32)
                  + jnp.dot(b_ref[...], wb_ref[D_SSD:D_SSD + D_NSA, :], preferred_element_type=F32)
                  + jnp.dot(c_ref[...], wb_ref[D_SSD + D_NSA:D_MIX, :], preferred_element_type=F32))


def _out_proj_cast(y_ssd, y_nsa, y_gla, w, layer):
    m = y_ssd.shape[0]
    n = w.shape[2]
    tm = _pick(m, (1024,))
    tn = _pick(n, (512,))

    def act(width):
        return pl.BlockSpec((tm, width), lambda j, i: (i, 0))

    return pl.pallas_call(
        _out_proj_cast_kernel, grid=(n // tn, m // tm),
        in_specs=[act(D_SSD), act(D_NSA), act(D_GLA), pl.BlockSpec((1, D_MIX, tn), lambda j, i: (layer, 0, j))],
        out_specs=[pl.BlockSpec((tm, tn), lambda j, i: (i, j)), pl.BlockSpec((D_MIX, tn), lambda j, i: (0, j))],
        out_shape=[jax.ShapeDtypeStruct((m, n), F32), jax.ShapeDtypeStruct((D_MIX, n), BF16)],
        compiler_params=pltpu.CompilerParams(dimension_semantics=('parallel', 'arbitrary'),
                                             vmem_limit_bytes=VMEM_LIMIT_BYTES),
        name='out_proj',
    )(y_ssd, y_nsa, y_gla, w)


def _ln_kernel(x_ref, h_ref, g_ref, b_ref, o_ref, ob_ref):
    v = DN_ALPHA * x_ref[...] + h_ref[...]
    d = v - jnp.mean(v, -1, keepdims=True)
    y = d * lax.rsqrt(jnp.mean(d * d, -1, keepdims=True) + LN_EPS) * g_ref[...] + b_ref[...]
    o_ref[...] = y
    ob_ref[...] = y.astype(BF16)


def _add_layernorm(x, h, g, b):
    m, d = x.shape
    tm = _pick(m, (256, 64))
    row = pl.BlockSpec((tm, d), lambda i: (i, 0))
    vec = pl.BlockSpec((1, d), lambda i: (0, 0))
    return pl.pallas_call(
        _ln_kernel, grid=(m // tm,), in_specs=[row, row, vec, vec], out_specs=[row, row],
        out_shape=[jax.ShapeDtypeStruct((m, d), F32), jax.ShapeDtypeStruct((m, d), BF16)],
        compiler_params=pltpu.CompilerParams(dimension_semantics=('parallel',), vmem_limit_bytes=VMEM_LIMIT_BYTES),
        name='add_layernorm',
    )(x, h, g.reshape(1, d), b.reshape(1, d))


def _xattn_kernel(q_ref, k_ref, v_ref, o_ref):
    q = (q_ref[...] * (X_HEAD_DIM ** -0.5)).astype(BF16)
    s = _dot_nt(q, k_ref[0].astype(BF16))
    e = jnp.exp(s - jnp.max(s, -1, keepdims=True))
    p = e / jnp.sum(e, -1, keepdims=True)
    o_ref[...] = jnp.dot(p.astype(BF16), v_ref[0].astype(BF16), preferred_element_type=F32).astype(o_ref.dtype)


def _cross_attention(q, mem_kv, bsz):
    m = q.shape[0]
    l = m // bsz
    tq = _pick(l, (512, 8))
    nq = l // tq
    return pl.pallas_call(
        _xattn_kernel, grid=(bsz, X_HEADS, nq),
        in_specs=[pl.BlockSpec((tq, X_HEAD_DIM), lambda b, h, i: (b * nq + i, h)),
                  pl.BlockSpec((1, N_MEM, X_HEAD_DIM), lambda b, h, i: (b, 0, h)),
                  pl.BlockSpec((1, N_MEM, X_HEAD_DIM), lambda b, h, i: (b, 0, X_HEADS + h))],
        out_specs=pl.BlockSpec((tq, X_HEAD_DIM), lambda b, h, i: (b * nq + i, h)),
        out_shape=jax.ShapeDtypeStruct((m, D_MODEL), BF16),
        compiler_params=pltpu.CompilerParams(dimension_semantics=('parallel', 'parallel', 'parallel'),
                                             vmem_limit_bytes=VMEM_LIMIT_BYTES),
        name='cross_attention',
    )(q, mem_kv, mem_kv)


def _softplus(x):
    return jnp.maximum(x, 0.0) + jnp.log1p(jnp.exp(-jnp.abs(x)))


def _silu(x):
    return x * jax.nn.sigmoid(x)


def _pad_rows(x, rows):
    if x.shape[0] == rows:
        return x
    return jnp.concatenate([x, jnp.zeros((rows - x.shape[0],) + x.shape[1:], x.dtype)], axis=0)


def _cumsum_rows(x, seg):
    r = lax.broadcasted_iota(jnp.int32, x.shape, 0) % seg
    k = 1
    while k < seg:
        x = x + jnp.where(r >= k, pltpu.roll(x, k, 0), 0.0)
        k *= 2
    return x


def _segment_last(x, seg):
    n = x.shape[0]
    r = lax.broadcasted_iota(jnp.int32, x.shape, 0) % seg
    k = 1
    while k < seg:
        x = jnp.where(r < seg - k, pltpu.roll(x, n - k, 0), x)
        k *= 2
    return x


GLA_BLOCK = 128


def _gla_kernel(q_ref, k_ref, v_ref, g_ref, a_ref, w2_ref, gb_ref, nw_ref, s0_ref, o_ref, s_out_ref,
                s_scr, q_scr, k_scr, v_scr, bc_scr, o_scr, *, rows_in, ch):
    c = pl.program_id(1)
    nsub = -(-rows_in // ch)
    hk = GLA_DK

    @pl.when(c == 0)
    def _():
        s_scr[...] = s0_ref[0]

    valid = lax.broadcasted_iota(jnp.int32, (GLA_BLOCK, 1), 0) < rows_in
    a = _pad_rows(a_ref[...], GLA_BLOCK).astype(BF16)
    logf = -_softplus(-(jnp.dot(a, w2_ref[...], preferred_element_type=F32) + gb_ref[...])) / GLA_TAU
    logf = jnp.where(valid, logf, 0.0)
    bc = _cumsum_rows(logf, ch)
    b_last = _segment_last(bc, ch)
    q = _pad_rows(q_ref[...], GLA_BLOCK) * (GLA_DK ** -0.5)
    k = _pad_rows(k_ref[...], GLA_BLOCK)
    v = _pad_rows(v_ref[...], GLA_BLOCK)
    q_scr[...] = q
    k_scr[...] = k
    v_scr[...] = v
    bc_scr[...] = bc

    tt = lax.broadcasted_iota(jnp.int32, (ch, 1), 0)

    def intra(j, carry):
        r0 = pl.multiple_of(j * ch, ch)
        qj = q_scr[pl.ds(r0, ch), :]
        bj = bc_scr[pl.ds(r0, ch), :]
        acc = [jnp.zeros((ch, GLA_DV), F32) for _ in range(GLA_HEADS)]
        for s in range(ch):
            ks = k_scr[pl.ds(r0 + s, 1), :]
            bs = bc_scr[pl.ds(r0 + s, 1), :]
            vs = v_scr[pl.ds(r0 + s, 1), :]
            w = jnp.where(tt >= s, qj * ks * jnp.exp(jnp.minimum(bj - bs, 0.0)), 0.0)
            for h in range(GLA_HEADS):
                att = jnp.sum(w[:, h * hk:(h + 1) * hk], -1, keepdims=True)
                acc[h] = acc[h] + att * vs[:, h * GLA_DV:(h + 1) * GLA_DV]
        o_scr[pl.ds(r0, ch), :] = jnp.concatenate(acc, axis=1)
        return carry

    lax.fori_loop(0, nsub, intra, 0)

    qe = q * jnp.exp(bc)
    kst = k * jnp.exp(b_last - bc)
    dec = jnp.exp(b_last)
    o_intra = _pad_rows(o_scr[0:nsub * ch, :], GLA_BLOCK)
    chunk_of_col = lax.broadcasted_iota(jnp.int32, (hk, GLA_BLOCK), 1) // ch
    g = _pad_rows(g_ref[...], GLA_BLOCK)
    outs = []
    for h in range(GLA_HEADS):
        kst_t = kst[:, h * hk:(h + 1) * hk].T
        dec_t = dec[:, h * hk:(h + 1) * hk].T
        lhs = jnp.concatenate([jnp.where(chunk_of_col == j, kst_t, 0.0) for j in range(nsub)], axis=0).astype(BF16)
        kv = jnp.dot(lhs, v[:, h * GLA_DV:(h + 1) * GLA_DV].astype(BF16), preferred_element_type=F32)
        s_h = s_scr[h]
        s_before = []
        for j in range(nsub):
            s_before.append(s_h.astype(BF16))
            s_h = s_h * dec_t[:, j * ch:j * ch + 1] + kv[j * hk:(j + 1) * hk]
        s_scr[h] = s_h
        big = jnp.dot(qe[:, h * hk:(h + 1) * hk].astype(BF16), jnp.concatenate(s_before, axis=1), preferred_element_type=F32)
        o_inter = _pad_rows(jnp.concatenate([big[j * ch:(j + 1) * ch, j * GLA_DV:(j + 1) * GLA_DV] for j in range(nsub)], axis=0),
                            GLA_BLOCK)
        o_h = o_intra[:, h * GLA_DV:(h + 1) * GLA_DV] + o_inter
        o_h = o_h * lax.rsqrt(jnp.mean(o_h * o_h, -1, keepdims=True) + NORM_EPS) * nw_ref[...]
        outs.append(o_h)
    o = jnp.concatenate(outs, axis=1) * _silu(g)
    o_ref[...] = o[:rows_in].astype(o_ref.dtype)

    @pl.when(c == pl.num_programs(1) - 1)
    def _():
        s_out_ref[0] = s_scr[...]


def _gla(y, s0, gate_w2, gate_b, norm_w, bsz):
    m = y.shape[0]
    l = m // bsz
    rows_in = min(GLA_BLOCK, l)
    ch = GLA_CHUNK
    nblk = l // rows_in
    hdk = GLA_HEADS * GLA_DK
    w2 = jnp.zeros((LANE, hdk), BF16).at[ALR_OFF - SMALL_OFF:ALR_OFF - SMALL_OFF + GLA_RANK].set(gate_w2.astype(BF16))

    def rows(width, off):
        assert off % width == 0
        return pl.BlockSpec((rows_in, width), lambda b, c: (b * nblk + c, off // width))

    def const(shape):
        return pl.BlockSpec(shape, lambda b, c: (0,) * len(shape))

    state = pl.BlockSpec((1, GLA_HEADS, GLA_DK, GLA_DV), lambda b, c: (b, 0, 0, 0))
    return pl.pallas_call(
        functools.partial(_gla_kernel, rows_in=rows_in, ch=ch),
        grid=(bsz, nblk),
        in_specs=[rows(hdk, QG_OFF), rows(hdk, KG_OFF), rows(D_GLA, VG_OFF), rows(D_GLA, GG_OFF), rows(LANE, SMALL_OFF),
                  const((LANE, hdk)), const((1, hdk)), const((1, GLA_DV)), state],
        out_specs=[pl.BlockSpec((rows_in, D_GLA), lambda b, c: (b * nblk + c, 0)), state],
        out_shape=[jax.ShapeDtypeStruct((m, D_GLA), BF16), jax.ShapeDtypeStruct((bsz, GLA_HEADS, GLA_DK, GLA_DV), F32)],
        scratch_shapes=[pltpu.VMEM((GLA_HEADS, GLA_DK, GLA_DV), F32), pltpu.VMEM((GLA_BLOCK, hdk), F32),
                        pltpu.VMEM((GLA_BLOCK, hdk), F32), pltpu.VMEM((GLA_BLOCK, D_GLA), F32),
                        pltpu.VMEM((GLA_BLOCK, hdk), F32), pltpu.VMEM((GLA_BLOCK, D_GLA), F32)],
        compiler_params=pltpu.CompilerParams(dimension_semantics=('parallel', 'arbitrary'),
                                             vmem_limit_bytes=VMEM_LIMIT_BYTES),
        name='gla',
    )(y, y, y, y, y, w2, gate_b.reshape(1, hdk), norm_w.reshape(1, GLA_DV), s0)


SSD_R = SSD_HEADS // SSD_GROUPS
SSD_GW = SSD_R * SSD_HEAD_DIM
SSD_CW = SSD_GW + 2 * SSD_STATE
SSD_TAIL = 8


def _ssd_kernel(xs_ref, bm_ref, cm_ref, z_ref, dt_ref, conv0_ref, cw_ref, cb_ref, hp_ref, dskip_ref, nw_ref, h0_ref,
                o_ref, h_out_ref, xbuf, h_scr, *, rows_in):
    g = pl.program_id(1)
    c = pl.program_id(2)
    t = SSD_CHUNK

    @pl.when(c == 0)
    def _():
        xbuf[0:SSD_TAIL, :] = conv0_ref[0, 0]
        h_scr[...] = h0_ref[0, 0]

    xbuf[SSD_TAIL:SSD_TAIL + t, 0:SSD_GW] = _pad_rows(xs_ref[...], t)
    xbuf[SSD_TAIL:SSD_TAIL + t, SSD_GW:SSD_GW + SSD_STATE] = _pad_rows(bm_ref[...], t)
    xbuf[SSD_TAIL:SSD_TAIL + t, SSD_GW + SSD_STATE:SSD_CW] = _pad_rows(cm_ref[...], t)
    conv = cb_ref[0]
    for kk in range(SSD_CONV):
        conv = conv + cw_ref[0, kk:kk + 1, :] * xbuf[SSD_TAIL - (SSD_CONV - 1) + kk:SSD_TAIL - (SSD_CONV - 1) + kk + t, :]
    xbuf[0:SSD_TAIL, :] = xbuf[t:t + SSD_TAIL, :]
    xc = _silu(conv)
    xs = xc[:, 0:SSD_GW]
    bm = xc[:, SSD_GW:SSD_GW + SSD_STATE]
    cm = xc[:, SSD_GW + SSD_STATE:SSD_CW].astype(BF16)

    lane = lax.broadcasted_iota(jnp.int32, (t, LANE), 1)
    row = lax.broadcasted_iota(jnp.int32, (t, LANE), 0)
    dt = pltpu.roll(_pad_rows(dt_ref[...], t), (LANE - SSD_R * g) % LANE, 1)
    dt = jnp.where((lane < SSD_R) & (row < rows_in), _softplus(dt + hp_ref[0, 0:1, :]), 0.0)
    acs = _cumsum_rows(dt * -jnp.exp(hp_ref[0, 1:2, :]), t)
    acs_t = acs.T
    a_last = acs[t - 1:t, :]

    head_of_lane = lax.broadcasted_iota(jnp.int32, (1, SSD_GW), 1) // SSD_HEAD_DIM

    def spread(arr):
        return sum(jnp.where(head_of_lane == r, arr[:, r:r + 1], 0.0) for r in range(SSD_R))

    xdt = xs * spread(dt)
    cb = _dot_nt(cm, bm.astype(BF16))
    tri = lax.broadcasted_iota(jnp.int32, (t, t), 0) >= lax.broadcasted_iota(jnp.int32, (t, t), 1)
    y = jnp.zeros((t, SSD_GW), F32)
    for r in range(SSD_R):
        decay = jnp.where(tri, jnp.exp(jnp.minimum(acs[:, r:r + 1] - acs_t[r:r + 1, :], 0.0)), 0.0)
        x_r = jnp.where(head_of_lane == r, xdt, 0.0).astype(BF16)
        y = y + jnp.dot((cb * decay).astype(BF16), x_r, preferred_element_type=F32)
    h_prev = h_scr[...]
    y = y + jnp.dot(cm, h_prev.astype(BF16), preferred_element_type=F32) * jnp.exp(spread(acs))
    x_st = (xdt * jnp.exp(spread(a_last - acs))).astype(BF16)
    h_new = h_prev * jnp.exp(spread(a_last)) + jnp.dot(bm.T.astype(BF16), x_st, preferred_element_type=F32)
    h_scr[...] = h_new

    y = (y + xs * dskip_ref[0]) * _silu(_pad_rows(z_ref[...], t))
    y = y * lax.rsqrt(jnp.mean(y * y, -1, keepdims=True) + NORM_EPS) * nw_ref[0]
    o_ref[...] = y[:rows_in].astype(o_ref.dtype)

    @pl.when(c == pl.num_programs(2) - 1)
    def _():
        h_out_ref[0, 0] = h_new


def _ssd(y, conv0, h0, conv_w, conv_b, dt_bias, a_log, d_skip, norm_w, bsz):
    m = y.shape[0]
    l = m // bsz
    rows_in = min(SSD_CHUNK, l)
    nblk = l // rows_in
    ng = SSD_GROUPS

    def per_group(v):
        xs = v[..., :D_SSD].reshape(v.shape[:-1] + (ng, SSD_GW))
        bm = v[..., D_SSD:D_SSD + ng * SSD_STATE].reshape(v.shape[:-1] + (ng, SSD_STATE))
        cm = v[..., D_SSD + ng * SSD_STATE:].reshape(v.shape[:-1] + (ng, SSD_STATE))
        return jnp.moveaxis(jnp.concatenate([xs, bm, cm], axis=-1), -2, 0)

    conv0_g = jnp.moveaxis(per_group(jnp.pad(conv0, ((0, 0), (SSD_TAIL - (SSD_CONV - 1), 0), (0, 0)))), 0, 1)
    cw_g = per_group(conv_w)
    cb_g = per_group(conv_b[None])
    hp = jnp.zeros((ng, 8, LANE), F32)
    hp = hp.at[:, 0, :SSD_R].set(dt_bias.reshape(ng, SSD_R)).at[:, 1, :SSD_R].set(a_log.reshape(ng, SSD_R))
    dskip_g = jnp.repeat(d_skip, SSD_HEAD_DIM).reshape(ng, 1, SSD_GW)
    nw_g = norm_w.reshape(ng, 1, SSD_GW)
    h0_t = h0.reshape(bsz, ng, SSD_GW, SSD_STATE).transpose(0, 1, 3, 2)

    def rows(width, off):
        assert off % width == 0
        return pl.BlockSpec((rows_in, width), lambda b, g, c: (b * nblk + c, off // width + g))

    def grp(shape):
        return pl.BlockSpec((1,) + shape, lambda b, g, c: (g,) + (0,) * len(shape))

    state = pl.BlockSpec((1, 1, SSD_STATE, SSD_GW), lambda b, g, c: (b, g, 0, 0))
    out, h_t = pl.pallas_call(
        functools.partial(_ssd_kernel, rows_in=rows_in),
        grid=(bsz, ng, nblk),
        in_specs=[rows(SSD_GW, XBC_OFF), rows(SSD_STATE, XBC_OFF + D_SSD), rows(SSD_STATE, XBC_OFF + D_SSD + ng * SSD_STATE),
                  rows(SSD_GW, Z_OFF), pl.BlockSpec((rows_in, LANE), lambda b, g, c: (b * nblk + c, SMALL_OFF // LANE)),
                  pl.BlockSpec((1, 1, SSD_TAIL, SSD_CW), lambda b, g, c: (b, g, 0, 0)),
                  grp((SSD_CONV, SSD_CW)), grp((1, SSD_CW)), grp((8, LANE)), grp((1, SSD_GW)), grp((1, SSD_GW)), state],
        out_specs=[pl.BlockSpec((rows_in, SSD_GW), lambda b, g, c: (b * nblk + c, g)), state],
        out_shape=[jax.ShapeDtypeStruct((m, D_SSD), BF16), jax.ShapeDtypeStruct((bsz, ng, SSD_STATE, SSD_GW), F32)],
        scratch_shapes=[pltpu.VMEM((SSD_TAIL + SSD_CHUNK, SSD_CW), F32), pltpu.VMEM((SSD_STATE, SSD_GW), F32)],
        compiler_params=pltpu.CompilerParams(dimension_semantics=('parallel', 'parallel', 'arbitrary'),
                                             vmem_limit_bytes=VMEM_LIMIT_BYTES),
        name='ssd',
    )(y, y, y, y, y, conv0_g, cw_g, cb_g, hp, dskip_g, nw_g, h0_t)
    h_new = h_t.transpose(0, 1, 3, 2).reshape(bsz, SSD_HEADS, SSD_HEAD_DIM, SSD_STATE)
    return out, h_new


def _layer_flat(x, xb, bsz, lp, f32_weights, layer_idx, rel_bias, tables, mem_kv, conv0, ssd_h0, gla_s0, nsa_past, win_past):
    m = x.shape[0]
    l = m // bsz
    assert l >= SSD_CONV - 1
    y = _matmul(xb, lp['w_in'], name='in_proj')
    y3 = y.reshape(bsz, l, D_IN_PAD)
    y_ssd, h_new = _ssd(y, conv0, ssd_h0, lp['ssd_conv_w'], lp['ssd_conv_b'], lp['ssd_dt_bias'], lp['ssd_a_log'],
                        lp['ssd_d'], lp['ssd_norm_w'], bsz)
    conv_new = y3[:, l - (SSD_CONV - 1):, XBC_OFF:XBC_OFF + SSD_CONV_DIM]
    y_gla, s_new = _gla(y, gla_s0, lp['gla_gate_w2'], lp['gla_gate_b'], lp['gla_norm_w'], bsz)
    rows = y3[..., KV_OFF:KV_OFF + 4 * KVW].reshape(bsz, l, 4, NSA_KV_HEADS, NSA_HEAD_DIM)
    win_rows = y3[..., KV_OFF + 4 * KVW:KV_OFF + 6 * KVW].reshape(bsz, l, 2, NSA_KV_HEADS, NSA_HEAD_DIM)
    if nsa_past is None:
        win_new = win_rows[:, -min(WINDOW, l):]
        y_nsa = _nsa_prompt_t(y, _nsa_compress_prompt(y, lp['nsa_cmp_pe'], lp['nsa_cmp_w'], bsz), tables, bsz)
    else:
        cache_kv, layer, page_table = nsa_past
        win_new = jnp.concatenate([win_past, win_rows], axis=1)[:, -win_past.shape[1]:]
        y_nsa = _nsa_sample(y, cache_kv, layer, page_table, win_past, lp['nsa_cmp_pe'], lp['nsa_cmp_w'], rel_bias, bsz)
    def mm(key, a, **kw):
        if key not in lp:
            out, lp[key] = _matmul_cast(a, f32_weights[key], layer_idx, **kw)
            return out
        return _matmul(a, lp[key], **kw)

    if 'w_out' not in lp:
        h, lp['w_out'] = _out_proj_cast(y_ssd, y_nsa, y_gla, f32_weights['w_out'], layer_idx)
    else:
        h = _out_proj(y_ssd, y_nsa, y_gla, lp['w_out'])
    x, xb = _add_layernorm(x, h, lp['ln_g'][0], lp['ln_b'][0])
    o = _cross_attention(mm('x_wq', xb, name='xattn_q'), mem_kv, bsz)
    x, xb = _add_layernorm(x, mm('x_wo', o, name='xattn_o'), lp['ln_g'][1], lp['ln_b'][1])
    hidden = mm('ffn_w1', xb, out_dtype=BF16, act='sqrelu', name='ffn_up')
    x, xb = _add_layernorm(x, _matmul(hidden, lp['ffn_w2'], name='ffn_down'), lp['ln_g'][2], lp['ln_b'][2])
    return x, xb, (rows, win_new, conv_new, h_new, s_new)


def kernel(x_prompt, x_sample, cache_nsa_kv, cache_nsa_win, state_ssd, state_ssd_conv, state_gla, cache_mem_kv,
           page_table, mem_prompt, w_in, ssd_conv_w, ssd_conv_b, ssd_dt_bias, ssd_a_log, ssd_d, ssd_norm_w,
           nsa_cmp_pe, nsa_cmp_w, rel_bias, gla_gate_w2, gla_gate_b, gla_norm_w, w_out, x_wq, x_wkv, x_wo,
           ffn_w1, ffn_w2, ln_g, ln_b):
    bp = x_prompt.shape[0]
    bs = x_sample.shape[0]
    xp, xs = x_prompt.reshape(-1, D_MODEL), x_sample.reshape(-1, D_MODEL)
    xpb, xsb = xp.astype(BF16), xs.astype(BF16)
    st_p, st_s, mem_p = [], [], []
    tables = _nsa_prompt_bias_tables_t(rel_bias)
    f32_weights = dict(w_out=w_out, x_wq=x_wq, x_wo=x_wo, ffn_w1=ffn_w1)
    mem_b = mem_prompt.reshape(-1, D_MODEL).astype(BF16)
    for l in range(DEPTH):
        lp = dict(w_in=_reorder_w_in(w_in[l]), ssd_conv_w=ssd_conv_w[l], ssd_conv_b=ssd_conv_b[l],
                  ssd_dt_bias=ssd_dt_bias[l], ssd_a_log=ssd_a_log[l], ssd_d=ssd_d[l], ssd_norm_w=ssd_norm_w[l],
                  nsa_cmp_pe=nsa_cmp_pe[l], nsa_cmp_w=nsa_cmp_w[l], gla_gate_w2=gla_gate_w2[l], gla_gate_b=gla_gate_b[l],
                  gla_norm_w=gla_norm_w[l], ffn_w2=_to_bf16(ffn_w2, l), ln_g=ln_g[l], ln_b=ln_b[l])
        mem_kv_p, _ = _matmul_cast(mem_b, x_wkv, l, name='mem_kv')
        mem_kv_p = mem_kv_p.reshape(bp, N_MEM, 2 * D_MODEL)
        conv0 = jnp.zeros((bp, SSD_CONV - 1, SSD_CONV_DIM), F32)
        h0 = jnp.zeros((bp, SSD_HEADS, SSD_HEAD_DIM, SSD_STATE), F32)
        s0 = jnp.zeros((bp, GLA_HEADS, GLA_DK, GLA_DV), F32)
        xp, xpb, stp = _layer_flat(xp, xpb, bp, lp, f32_weights, l, rel_bias, tables, mem_kv_p, conv0, h0, s0, None, None)
        st_p.append(stp)
        mem_p.append(mem_kv_p.reshape(bp, N_MEM, 2, X_HEADS, X_HEAD_DIM))
        xs, xsb, sts = _layer_flat(xs, xsb, bs, lp, f32_weights, l, rel_bias, tables,
                                   cache_mem_kv[l].reshape(bs, N_MEM, 2 * D_MODEL), state_ssd_conv[l], state_ssd[l],
                                   state_gla[l], (cache_nsa_kv, l, page_table), cache_nsa_win[l])
        st_s.append(sts)
    p_rows, p_win, p_conv, p_ssd, p_gla = [jnp.stack(s) for s in zip(*st_p)]
    s_rows, s_win, s_conv, s_ssd, s_gla = [jnp.stack(s) for s in zip(*st_s)]
    p_mem = jnp.stack(mem_p)
    return (xp.reshape(x_prompt.shape), xs.reshape(x_sample.shape), p_rows, s_rows, p_win, s_win, p_ssd, s_ssd,
            p_conv, s_conv, p_gla, s_gla, p_mem)
```

```python
import functools
import math

import jax
import jax.numpy as jnp
import numpy as np
from jax import lax
from jax.experimental import pallas as pl
from jax.experimental.pallas import tpu as pltpu

F32 = jnp.float32
BF16 = jnp.bfloat16

D_MODEL = 4096
DEPTH = 2
PAGE_SIZE = 128
D_MIX = D_MODEL
D_SSD = D_MIX // 2
SSD_HEAD_DIM = 64
SSD_HEADS = D_SSD // SSD_HEAD_DIM
SSD_GROUPS = 8
SSD_STATE = 128
SSD_CONV = 4
SSD_CHUNK = 128
SSD_CONV_DIM = D_SSD + 2 * SSD_GROUPS * SSD_STATE
D_NSA = D_MIX // 4
NSA_HEAD_DIM = 128
NSA_HEADS = D_NSA // NSA_HEAD_DIM
NSA_KV_HEADS = 2
NSA_REP = NSA_HEADS // NSA_KV_HEADS
CMP_LEN = 32
CMP_STRIDE = 16
SEL_LEN = 64
TOP_N = 16
WINDOW = 512
Q_BLOCK = 128
FORCE_SCORE = 1e4
D_GLA = D_MIX - D_SSD - D_NSA
GLA_HEADS = 4
GLA_DV = D_GLA // GLA_HEADS
GLA_DK = GLA_DV // 2
GLA_RANK = 16
GLA_TAU = 16.0
GLA_CHUNK = 16
N_MEM = 256
X_HEADS = 4
X_HEAD_DIM = D_MODEL // X_HEADS
D_FF = 4 * D_MODEL
N_BUCKETS = 32
MAX_DISTANCE = 128
LN_EPS = 1e-5
NORM_EPS = 1e-6
DN_ALPHA = (2 * DEPTH) ** 0.25
IN_SPLITS = (D_SSD, SSD_CONV_DIM, SSD_HEADS, NSA_HEADS * NSA_HEAD_DIM) + (NSA_KV_HEADS * NSA_HEAD_DIM,) * 6 + (
    3 * NSA_HEADS, GLA_HEADS * GLA_DK, GLA_HEADS * GLA_DK, D_GLA, D_GLA, GLA_RANK)
D_IN = sum(IN_SPLITS)

LANE = 128
VMEM_LIMIT_BYTES = 56 * 1024 * 1024

KVW = NSA_KV_HEADS * NSA_HEAD_DIM
Z_OFF = 0
XBC_OFF = Z_OFF + D_SSD
QN_OFF = XBC_OFF + SSD_CONV_DIM
VG_OFF = QN_OFF + NSA_HEADS * NSA_HEAD_DIM
GG_OFF = VG_OFF + D_GLA
KV_OFF = GG_OFF + D_GLA
QG_OFF = KV_OFF + 6 * KVW
KG_OFF = QG_OFF + GLA_HEADS * GLA_DK
SMALL_OFF = KG_OFF + GLA_HEADS * GLA_DK
DT_OFF = SMALL_OFF
ALR_OFF = SMALL_OFF + SSD_HEADS
GATE_OFF = SMALL_OFF + LANE
D_IN_PAD = 12288
NEG = -1e30


def _in_proj_column_map():
    off = np.cumsum((0,) + IN_SPLITS)
    z, xbc, dt, qn = off[0], off[1], off[2], off[3]
    kv0, gn, qg, kg, vg, gg, alr = off[4], off[10], off[11], off[12], off[13], off[14], off[15]
    cmap = np.full((D_IN_PAD,), -1, np.int64)
    cmap[Z_OFF:Z_OFF + D_SSD] = z + np.arange(D_SSD)
    cmap[XBC_OFF:XBC_OFF + SSD_CONV_DIM] = xbc + np.arange(SSD_CONV_DIM)
    cmap[QN_OFF:QN_OFF + D_NSA] = qn + np.arange(D_NSA)
    cmap[KV_OFF:KV_OFF + 6 * KVW] = kv0 + np.arange(6 * KVW)
    cmap[QG_OFF:QG_OFF + 512] = qg + np.arange(512)
    cmap[KG_OFF:KG_OFF + 512] = kg + np.arange(512)
    cmap[VG_OFF:VG_OFF + D_GLA] = vg + np.arange(D_GLA)
    cmap[GG_OFF:GG_OFF + D_GLA] = gg + np.arange(D_GLA)
    cmap[DT_OFF:DT_OFF + SSD_HEADS] = dt + np.arange(SSD_HEADS)
    cmap[ALR_OFF:ALR_OFF + GLA_RANK] = alr + np.arange(GLA_RANK)
    for g in range(NSA_KV_HEADS):
        cmap[GATE_OFF + g * LANE:GATE_OFF + g * LANE + 3 * NSA_REP] = gn + g * 3 * NSA_REP + np.arange(3 * NSA_REP)
    return cmap


def _reorder_w_in(w):
    cmap = _in_proj_column_map()
    cuts = [0] + [i for i in range(1, D_IN_PAD) if cmap[i] != cmap[i - 1] + 1 and not (cmap[i] == -1 and cmap[i - 1] == -1)] + [D_IN_PAD]
    pieces = []
    for a, b in zip(cuts[:-1], cuts[1:]):
        pieces.append(jnp.zeros((w.shape[0], b - a), w.dtype) if cmap[a] < 0 else w[:, cmap[a]:cmap[a] + b - a])
    return jnp.concatenate(pieces, axis=1).astype(BF16)


def _dot_nt(a, b):
    return lax.dot_general(a, b, (((1,), (1,)), ((), ())), preferred_element_type=F32)


NSA_L = 2048
NSA_QB = Q_BLOCK
NSA_NQB = NSA_L // NSA_QB
NSA_NCP = NSA_L // CMP_STRIDE
NSA_NSEL = NSA_L // SEL_LEN
CMP_HALF = CMP_STRIDE * NSA_HEAD_DIM
NSA_KSTEP = 512


def _t5_bucket_np(dist):
    n = np.maximum(dist, 0)
    max_exact = N_BUCKETS // 2
    nf = np.maximum(n, 1).astype(np.float32)
    large = max_exact + (np.log(nf / max_exact) / np.float32(math.log(MAX_DISTANCE / max_exact)) * (N_BUCKETS - max_exact)).astype(np.int32)
    return np.where(n < max_exact, n, np.minimum(large, N_BUCKETS - 1)).astype(np.int32)


def _nsa_compress_kernel(a_ref, pe_ref, w_ref, o_ref):
    a = a_ref[0, 0, 0]
    lo = jnp.dot((a + pe_ref[0, 0]).astype(BF16), w_ref[0, 0], preferred_element_type=F32)
    hi = jnp.dot((a + pe_ref[0, 1]).astype(BF16), w_ref[0, 1], preferred_element_type=F32)
    o_ref[0, 0, 0] = lo + pltpu.roll(hi, NSA_NCP - 1, 0)


def _nsa_compress_prompt(y, cmp_pe, cmp_w, bsz):
    a = y[:, KV_OFF:KV_OFF + 2 * KVW].reshape(bsz, NSA_NCP, CMP_STRIDE, 2, NSA_KV_HEADS, NSA_HEAD_DIM)
    a = a.transpose(0, 3, 4, 1, 2, 5).reshape(bsz, 2, NSA_KV_HEADS, NSA_NCP, CMP_HALF)
    pe = cmp_pe.reshape(2, 2, 1, CMP_HALF)
    w = cmp_w.astype(BF16).reshape(2, 2, CMP_HALF, NSA_HEAD_DIM)
    return pl.pallas_call(
        _nsa_compress_kernel,
        grid=(bsz, 2, NSA_KV_HEADS),
        in_specs=[pl.BlockSpec((1, 1, 1, NSA_NCP, CMP_HALF), lambda b, c, g: (b, c, g, 0, 0)),
                  pl.BlockSpec((1, 2, 1, CMP_HALF), lambda b, c, g: (c, 0, 0, 0)),
                  pl.BlockSpec((1, 2, CMP_HALF, NSA_HEAD_DIM), lambda b, c, g: (c, 0, 0, 0))],
        out_specs=pl.BlockSpec((1, 1, 1, NSA_NCP, NSA_HEAD_DIM), lambda b, c, g: (b, c, g, 0, 0)),
        out_shape=jax.ShapeDtypeStruct((bsz, 2, NSA_KV_HEADS, NSA_NCP, NSA_HEAD_DIM), F32),
        compiler_params=pltpu.CompilerParams(dimension_semantics=('parallel', 'parallel', 'parallel')),
        name='nsa_compress',
    )(a, pe, w)


def _bias_lookup(rel_bias, idx):
    return rel_bias.T[:, jnp.asarray(idx)]


def _bucket_thresholds():
    buckets = _t5_bucket_np(np.arange(4 * MAX_DISTANCE))
    return [int(np.argmax(buckets >= k)) for k in range(N_BUCKETS)]


def _nsa_prompt_bias_tables_t(rel_bias):
    tq = np.arange(NSA_QB)[None, :]
    ts = np.arange(NSA_QB)[:, None]

    def table(idx):
        t = _bias_lookup(rel_bias, idx).reshape((NSA_KV_HEADS, NSA_REP) + idx.shape)
        t = jnp.moveaxis(t, 1, -2)
        return t.reshape(t.shape[:-2] + (NSA_REP * NSA_QB,))

    b_diag = table(_t5_bucket_np(tq - ts))
    b_prev = table(_t5_bucket_np(NSA_QB + tq - ts))
    by_bucket = jnp.repeat(rel_bias.reshape(N_BUCKETS, NSA_KV_HEADS, NSA_REP).transpose(1, 0, 2), NSA_QB, axis=2)
    return b_diag, b_prev, by_bucket


def _nsa_prompt_t_kernel(q_ref, gate_ref, cmp_ref, ks_ref, vs_ref, kw_ref, vw_ref, bdiag_ref, bprev_ref, bkt_ref,
                         covt_ref, expand_ref, o_ref, m_ref, l_ref, acc_ref, ksb, vst, kwb, vwt, bcmp_scr):
    i = pl.program_id(2)
    cols = NSA_REP * NSA_QB

    @pl.when(i == 0)
    def _():
        ksb[...] = ks_ref[...].astype(BF16)
        kwb[...] = kw_ref[...].astype(BF16)
        for c in range(NSA_L // NSA_KSTEP):
            vst[c] = vs_ref[c * NSA_KSTEP:(c + 1) * NSA_KSTEP, :].T.astype(BF16)
            vwt[c] = vw_ref[c * NSA_KSTEP:(c + 1) * NSA_KSTEP, :].T.astype(BF16)

    qblk = q_ref[...] * (NSA_HEAD_DIM ** -0.5)
    q_t = jnp.concatenate([qblk[:, r * LANE:(r + 1) * LANE].T for r in range(NSA_REP)], axis=1).astype(BF16)
    key = lax.broadcasted_iota(jnp.int32, (NSA_QB, cols), 0)
    tq = lax.broadcasted_iota(jnp.int32, (NSA_QB, cols), 1) % NSA_QB
    t_abs = i * NSA_QB + tq

    k_c = cmp_ref[0, 0, 0].astype(BF16)
    v_c_t = cmp_ref[0, 1, 0].T.astype(BF16)
    mask_c = key * CMP_STRIDE + (CMP_LEN - 1) <= t_abs
    bfar = bkt_ref[0, N_BUCKETS - 1:N_BUCKETS, :]
    per_qb = NSA_QB // CMP_STRIDE
    w0 = pl.multiple_of(jnp.maximum(i * per_qb - 2 * per_qb, 0), 8)
    tok = w0 + lax.broadcasted_iota(jnp.int32, (3 * per_qb, cols), 0)
    t_w = i * NSA_QB + lax.broadcasted_iota(jnp.int32, (3 * per_qb, cols), 1) % NSA_QB
    dist_c = t_w - (tok * CMP_STRIDE + (CMP_LEN - 1))
    bias_w = jnp.zeros((3 * per_qb, cols), F32) + bkt_ref[0, 0:1, :]
    for k, thr in enumerate(_bucket_thresholds()):
        if k > 0:
            bias_w = jnp.where(dist_c >= thr, bkt_ref[0, k:k + 1, :], bias_w)
    bcmp_scr[...] = jnp.zeros((NSA_NCP, cols), F32) + bfar
    bcmp_scr[pl.ds(w0, 3 * per_qb), :] = bias_w
    s_c = jnp.where(mask_c, jnp.dot(k_c, q_t, preferred_element_type=F32) + bcmp_scr[...], NEG)
    e_c = jnp.where(mask_c, jnp.exp(s_c - jnp.max(s_c, 0, keepdims=True)), 0.0)
    d_c = jnp.sum(e_c, 0, keepdims=True)
    p_c = (e_c / jnp.where(d_c > 0, d_c, 1.0)).astype(BF16)
    o_c = jnp.dot(v_c_t, p_c, preferred_element_type=F32)

    imp4 = jnp.dot(covt_ref[...], p_c, preferred_element_type=F32)
    imp = sum(imp4[:, r * NSA_QB:(r + 1) * NSA_QB] for r in range(NSA_REP))
    blk = lax.broadcasted_iota(jnp.int32, (LANE, NSA_QB), 0)
    t_row = i * NSA_QB + lax.broadcasted_iota(jnp.int32, (LANE, NSA_QB), 1)
    cur = t_row // SEL_LEN
    forced = (blk == 0) | (blk == cur) | (blk == cur - 1)
    score = jnp.where(forced, FORCE_SCORE, jnp.where(blk * SEL_LEN <= t_row, imp, -FORCE_SCORE))
    score = jnp.where(blk < NSA_NSEL, score, -3.0 * FORCE_SCORE)
    rank = jnp.zeros((LANE, NSA_QB), jnp.int32)
    for j in range(NSA_NSEL):
        row = score[j:j + 1, :]
        rank = rank + jnp.where((row > score) | ((row == score) & (blk > j)), 1, 0)
    sel_t = jnp.where((rank < TOP_N) & (blk < NSA_NSEL), 1.0, 0.0).astype(BF16)

    def init():
        m_ref[...] = jnp.full((1, cols), NEG, F32)
        l_ref[...] = jnp.zeros((1, cols), F32)
        acc_ref[...] = jnp.zeros((NSA_HEAD_DIM, cols), F32)

    dist0 = (lax.broadcasted_iota(jnp.int32, (NSA_KSTEP, cols), 1) % NSA_QB
             - lax.broadcasted_iota(jnp.int32, (NSA_KSTEP, cols), 0))
    per_step = NSA_KSTEP // NSA_QB

    def step(k_scr, v_scr, j, selected):
        start = pl.multiple_of(j * NSA_KSTEP, NSA_KSTEP)
        s = jnp.dot(k_scr[pl.ds(start, NSA_KSTEP), :], q_t, preferred_element_type=F32)
        bias = []
        for u in range(per_step):
            d = i - (j * per_step + u)
            bias.append(jnp.where(d == 0, bdiag_ref[0], jnp.where(d == 1, bprev_ref[0], bfar)))
        dist = dist0 + (i * NSA_QB - j * NSA_KSTEP)
        if selected:
            m1 = jnp.dot(expand_ref[j], sel_t, preferred_element_type=F32)
            mask = (jnp.concatenate([m1] * NSA_REP, axis=1) > 0.5) & (dist >= 0)
        else:
            mask = (dist >= 0) & (dist < WINDOW)
        s = jnp.where(mask, s + jnp.concatenate(bias, axis=0), NEG)
        m_old = m_ref[...]
        m_new = jnp.maximum(m_old, jnp.max(s, 0, keepdims=True))
        p = jnp.where(mask, jnp.exp(s - m_new), 0.0)
        alpha = jnp.exp(m_old - m_new)
        l_ref[...] = alpha * l_ref[...] + jnp.sum(p, 0, keepdims=True)
        acc_ref[...] = alpha * acc_ref[...] + jnp.dot(v_scr[j], p.astype(BF16), preferred_element_type=F32)
        m_ref[...] = m_new

    def finish():
        l = l_ref[...]
        return acc_ref[...] / jnp.where(l > 0, l, 1.0)

    j_diag = i // per_step

    def far_step(j):
        start = pl.multiple_of(j * NSA_KSTEP, NSA_KSTEP)
        m1 = jnp.dot(expand_ref[j], sel_t, preferred_element_type=F32)
        off = (1.0 - m1) * NEG
        s = jnp.dot(ksb[pl.ds(start, NSA_KSTEP), :], q_t, preferred_element_type=F32) + bfar + jnp.concatenate([off] * NSA_REP, axis=1)
        m_old = m_ref[...]
        m_new = jnp.maximum(m_old, jnp.max(s, 0, keepdims=True))
        p = jnp.exp(s - m_new)
        alpha = jnp.exp(m_old - m_new)
        l_ref[...] = alpha * l_ref[...] + jnp.sum(p, 0, keepdims=True)
        acc_ref[...] = alpha * acc_ref[...] + jnp.dot(vst[j], p.astype(BF16), preferred_element_type=F32)
        m_ref[...] = m_new

    init()
    n_far = jnp.maximum(i - 1, 0) // per_step

    def far_body(j, carry):
        far_step(j)
        return carry

    def sel_body(j, carry):
        step(ksb, vst, j, True)
        return carry

    lax.fori_loop(0, n_far, far_body, 0)
    lax.fori_loop(n_far, j_diag + 1, sel_body, 0)
    o_s = finish()

    init()
    step(kwb, vwt, j_diag, False)

    @pl.when(j_diag >= 1)
    def _():
        step(kwb, vwt, j_diag - 1, False)

    o_w = finish()

    g_t = jax.nn.sigmoid(gate_ref[...]).T
    for r in range(NSA_REP):
        cs = slice(r * NSA_QB, (r + 1) * NSA_QB)
        o_r = g_t[3 * r:3 * r + 1, :] * o_c[:, cs] + g_t[3 * r + 1:3 * r + 2, :] * o_s[:, cs] + g_t[3 * r + 2:3 * r + 3, :] * o_w[:, cs]
        o_ref[:, r * LANE:(r + 1) * LANE] = o_r.T.astype(o_ref.dtype)


def _nsa_prompt_t(y, k_v_cmp, tables, bsz):
    b_diag, b_prev, by_bucket = tables
    cols = NSA_REP * NSA_QB
    nstep = NSA_L // NSA_KSTEP
    assert WINDOW <= NSA_KSTEP and NSA_KSTEP % NSA_QB == 0
    c_lo = np.arange(NSA_NCP) * CMP_STRIDE
    s_lo = np.arange(LANE) * SEL_LEN
    cov_t = ((c_lo[None] < s_lo[:, None] + SEL_LEN) & (c_lo[None] + CMP_LEN > s_lo[:, None])
             & (np.arange(NSA_NCP)[None] < NSA_NCP - 1) & (np.arange(LANE)[:, None] < NSA_NSEL))
    key_blk = (np.arange(NSA_L) // SEL_LEN).reshape(nstep, NSA_KSTEP, 1)
    expand = (np.arange(LANE)[None, None, :] == key_blk)
    kv_blk = KV_OFF // LANE
    grp = NSA_KV_HEADS

    def col(slot):
        return pl.BlockSpec((NSA_L, NSA_HEAD_DIM), lambda b, g, i: (b, kv_blk + slot * grp + g))

    tab = pl.BlockSpec((1, NSA_QB, cols), lambda b, g, i: (g, 0, 0))
    return pl.pallas_call(
        _nsa_prompt_t_kernel,
        grid=(bsz, NSA_KV_HEADS, NSA_NQB),
        in_specs=[pl.BlockSpec((NSA_QB, NSA_REP * LANE), lambda b, g, i: (b * NSA_NQB + i, QN_OFF // (NSA_REP * LANE) + g)),
                  pl.BlockSpec((NSA_QB, LANE), lambda b, g, i: (b * NSA_NQB + i, GATE_OFF // LANE + g)),
                  pl.BlockSpec((1, 2, 1, NSA_NCP, NSA_HEAD_DIM), lambda b, g, i: (b, 0, g, 0, 0)),
                  col(2), col(3), col(4), col(5), tab, tab,
                  pl.BlockSpec((1, N_BUCKETS, cols), lambda b, g, i: (g, 0, 0)),
                  pl.BlockSpec((LANE, NSA_NCP), lambda b, g, i: (0, 0)),
                  pl.BlockSpec((nstep, NSA_KSTEP, LANE), lambda b, g, i: (0, 0, 0))],
        out_specs=pl.BlockSpec((NSA_QB, NSA_REP * LANE), lambda b, g, i: (b * NSA_NQB + i, g)),
        out_shape=jax.ShapeDtypeStruct((bsz * NSA_L, D_NSA), BF16),
        scratch_shapes=[pltpu.VMEM((1, cols), F32), pltpu.VMEM((1, cols), F32), pltpu.VMEM((NSA_HEAD_DIM, cols), F32),
                        pltpu.VMEM((NSA_L, NSA_HEAD_DIM), BF16), pltpu.VMEM((nstep, NSA_HEAD_DIM, NSA_KSTEP), BF16),
                        pltpu.VMEM((NSA_L, NSA_HEAD_DIM), BF16), pltpu.VMEM((nstep, NSA_HEAD_DIM, NSA_KSTEP), BF16),
                        pltpu.VMEM((NSA_NCP, cols), F32)],
        compiler_params=pltpu.CompilerParams(dimension_semantics=('parallel', 'parallel', 'arbitrary'),
                                             vmem_limit_bytes=VMEM_LIMIT_BYTES),
        name='nsa_prompt',
    )(y, y, k_v_cmp, y, y, y, y, b_diag, b_prev, by_bucket,
      jnp.asarray(cov_t, BF16), jnp.asarray(expand, BF16))


NS_SELW = 3 * LANE
NS_CHUNK = 2048


def _nsa_sample_tables(rel_bias, past_len, lq, win_len):
    tq = (past_len + np.arange(lq))[:, None]

    def table(idx):
        t = _bias_lookup(rel_bias, idx)
        return t.reshape(NSA_KV_HEADS, NSA_REP * lq, idx.shape[-1])

    ncp = past_len // CMP_STRIDE
    b_cmp = table(_t5_bucket_np(tq - (np.arange(ncp)[None] * CMP_STRIDE + CMP_LEN - 1)))
    b_last = table(_t5_bucket_np(tq - (past_len - LANE + np.arange(LANE)[None])))
    b_new = table(_t5_bucket_np(tq - (past_len + np.arange(LANE)[None])))
    b_far = table(np.full((lq, LANE), N_BUCKETS - 1, np.int32))
    wpos = np.concatenate([past_len - win_len + np.arange(win_len), past_len + np.arange(LANE)])
    b_win = table(_t5_bucket_np(tq - wpos[None]))
    return b_cmp, b_last, b_new, b_far, b_win


def _nsa_sample_kernel(pt_ref, q_ref, kvn_ref, gate0_ref, gate1_ref, win_ref, cache_ref, w_ref, pe_ref,
                       bcmp_ref, blast_ref, bnew_ref, bfar_ref, bwin_ref, cov_ref, expand_ref, o_ref,
                       buf, cmp_scr, s_scr, sem, *, n_pages, lq, past_len, win_len, layer):
    b = pl.program_id(0)
    rows = NSA_REP * lq
    ncp = past_len // CMP_STRIDE
    n_cmp = (past_len + lq - CMP_LEN) // CMP_STRIDE + 1
    n_sel = -(-(past_len + lq) // SEL_LEN)
    n_chunks = past_len // NS_CHUNK
    ncol = 2 * NSA_KV_HEADS

    def page_copy(j, pair, col):
        slot, g = pair * 2 + col // NSA_KV_HEADS, col % NSA_KV_HEADS
        return pltpu.make_async_copy(cache_ref.at[layer, pt_ref[b, j], :, slot, g, :],
                                     buf.at[col, pl.ds(j * PAGE_SIZE, PAGE_SIZE), :], sem)

    def gather_start(pair):
        def body(j, carry):
            for col in range(ncol):
                page_copy(j, pair, col).start()
            return carry
        lax.fori_loop(0, n_pages, body, 0)

    def gather_wait(pair):
        def body(j, carry):
            for col in range(ncol):
                page_copy(j, pair, col).wait()
            return carry
        lax.fori_loop(0, n_pages, body, 0)

    gather_start(0)
    gather_wait(0)
    for c in range(2):
        for g in range(NSA_KV_HEADS):
            lo = jnp.zeros((ncp, NSA_HEAD_DIM), F32)
            hi = jnp.zeros((ncp, NSA_HEAD_DIM), F32)
            for l in range(CMP_STRIDE):
                x = buf[c * NSA_KV_HEADS + g, pl.ds(l, ncp, stride=CMP_STRIDE), :]
                lo = lo + jnp.dot((x + pe_ref[c, l:l + 1, :]).astype(BF16), w_ref[c, l], preferred_element_type=F32)
                hi = hi + jnp.dot((x + pe_ref[c, CMP_STRIDE + l:CMP_STRIDE + l + 1, :]).astype(BF16),
                                  w_ref[c, CMP_STRIDE + l], preferred_element_type=F32)
            cmp_scr[c, g] = lo + pltpu.roll(hi, ncp - 1, 0)
    gather_start(1)

    qall = q_ref[...] * (NSA_HEAD_DIM ** -0.5)
    tq = lax.broadcasted_iota(jnp.int32, (rows, 1), 0) % lq
    t_abs = past_len + tq
    o_c, sel, q_g = [], [], []
    for g in range(NSA_KV_HEADS):
        q = jnp.concatenate([qall[:, (g * NSA_REP + r) * LANE:(g * NSA_REP + r + 1) * LANE] for r in range(NSA_REP)],
                            axis=0).astype(BF16)
        q_g.append(q)
        n_idx = lax.broadcasted_iota(jnp.int32, (rows, ncp), 1)
        mask_c = (n_idx < n_cmp) & (n_idx * CMP_STRIDE + (CMP_LEN - 1) <= t_abs)
        s_c = jnp.where(mask_c, _dot_nt(q, cmp_scr[0, g].astype(BF16)) + bcmp_ref[g], NEG)
        e_c = jnp.where(mask_c, jnp.exp(s_c - jnp.max(s_c, -1, keepdims=True)), 0.0)
        d_c = jnp.sum(e_c, -1, keepdims=True)
        p_c = (e_c / jnp.where(d_c > 0, d_c, 1.0)).astype(BF16)
        o_c.append(jnp.dot(p_c, cmp_scr[1, g].astype(BF16), preferred_element_type=F32))
        imp4 = jnp.dot(p_c, cov_ref[...], preferred_element_type=F32)
        imp = sum(imp4[r * lq:(r + 1) * lq] for r in range(NSA_REP))
        blk = lax.broadcasted_iota(jnp.int32, (lq, NS_SELW), 1)
        t_q = past_len + lax.broadcasted_iota(jnp.int32, (lq, NS_SELW), 0)
        cur = t_q // SEL_LEN
        forced = (blk == 0) | (blk == cur) | (blk == cur - 1)
        score = jnp.where(forced, FORCE_SCORE, jnp.where(blk * SEL_LEN <= t_q, imp, -FORCE_SCORE))
        score = jnp.where(blk < n_sel, score, -3.0 * FORCE_SCORE)
        rank = jnp.zeros((lq, NS_SELW), jnp.int32)
        for j in range(n_sel):
            col = score[:, j:j + 1]
            rank = rank + jnp.where((col > score) | ((col == score) & (blk > j)), 1, 0)
        sel_q = jnp.where((rank < min(TOP_N, n_sel)) & (blk < n_sel), 1.0, 0.0)
        sel.append(jnp.concatenate([sel_q] * NSA_REP, axis=0))

    gather_wait(1)
    kvn = _pad_rows(kvn_ref[...], LANE)
    lane = lax.broadcasted_iota(jnp.int32, (rows, LANE), 1)
    blocks_per_chunk = NS_CHUNK // SEL_LEN
    gates = [jax.nn.sigmoid(gate0_ref[...]), jax.nn.sigmoid(gate1_ref[...])]
    for g in range(NSA_KV_HEADS):
        q = q_g[g]
        bfar = bfar_ref[g][:, 0:1]

        def new_rows(slot):
            return kvn[:, (slot * NSA_KV_HEADS + g) * NSA_HEAD_DIM:(slot * NSA_KV_HEADS + g + 1) * NSA_HEAD_DIM].astype(BF16)

        sel_b = sel[g].astype(BF16)
        for c in range(n_chunks):
            k = buf[g, c * NS_CHUNK:(c + 1) * NS_CHUNK, :].astype(BF16)
            s = _dot_nt(q, k) + bfar
            if c == n_chunks - 1:
                fix = blast_ref[g] - bfar_ref[g]
                s = jnp.concatenate([s[:, :NS_CHUNK - LANE], s[:, NS_CHUNK - LANE:] + fix], axis=1)
            m_c = jnp.dot(sel_b[:, c * blocks_per_chunk:(c + 1) * blocks_per_chunk], expand_ref[...],
                          preferred_element_type=F32)
            s_scr[:, c * NS_CHUNK:(c + 1) * NS_CHUNK] = jnp.where(m_c > 0.5, s, NEG)
        s_new = _dot_nt(q, new_rows(2)) + bnew_ref[g]
        mask_new = (lane <= tq) & (lane < lq) & (sel[g][:, n_sel - 1:n_sel] > 0.5)
        s_scr[:, past_len:past_len + LANE] = jnp.where(mask_new, s_new, NEG)
        m = jnp.max(s_scr[...], -1, keepdims=True)
        acc = jnp.zeros((rows, NSA_HEAD_DIM), F32)
        den = jnp.zeros((rows, 1), F32)
        for c in range(n_chunks):
            sc = s_scr[:, c * NS_CHUNK:(c + 1) * NS_CHUNK]
            e = jnp.where(sc > 0.5 * NEG, jnp.exp(sc - m), 0.0)
            den = den + jnp.sum(e, -1, keepdims=True)
            v = buf[NSA_KV_HEADS + g, c * NS_CHUNK:(c + 1) * NS_CHUNK, :]
            acc = acc + jnp.dot(e.astype(BF16), v.astype(BF16), preferred_element_type=F32)
        sc = s_scr[:, past_len:past_len + LANE]
        e = jnp.where(sc > 0.5 * NEG, jnp.exp(sc - m), 0.0)
        den = den + jnp.sum(e, -1, keepdims=True)
        acc = acc + jnp.dot(e.astype(BF16), new_rows(3), preferred_element_type=F32)
        o_s = acc / jnp.where(den > 0, den, 1.0)

        kw = jnp.concatenate([win_ref[0, :, g * NSA_HEAD_DIM:(g + 1) * NSA_HEAD_DIM].astype(BF16), new_rows(4)], axis=0)
        vw = jnp.concatenate([win_ref[0, :, (NSA_KV_HEADS + g) * NSA_HEAD_DIM:(NSA_KV_HEADS + g + 1) * NSA_HEAD_DIM].astype(BF16),
                              new_rows(5)], axis=0)
        wl = lax.broadcasted_iota(jnp.int32, (rows, win_len + LANE), 1)
        dist = jnp.where(wl < win_len, win_len + tq - wl, tq - (wl - win_len))
        mask_w = (dist >= 0) & (dist < WINDOW) & (wl < win_len + lq)
        s_w = jnp.where(mask_w, _dot_nt(q, kw) + bwin_ref[g], NEG)
        e_w = jnp.where(mask_w, jnp.exp(s_w - jnp.max(s_w, -1, keepdims=True)), 0.0)
        d_w = jnp.sum(e_w, -1, keepdims=True)
        o_w = jnp.dot((e_w / jnp.where(d_w > 0, d_w, 1.0)).astype(BF16), vw, preferred_element_type=F32)

        for r in range(NSA_REP):
            rs = slice(r * lq, (r + 1) * lq)
            gt = gates[g]
            h = g * NSA_REP + r
            o_ref[:, h * LANE:(h + 1) * LANE] = (gt[:, 3 * r:3 * r + 1] * o_c[g][rs] + gt[:, 3 * r + 1:3 * r + 2] * o_s[rs]
                                                 + gt[:, 3 * r + 2:3 * r + 3] * o_w[rs]).astype(o_ref.dtype)


def _nsa_sample(y, cache_kv, layer, page_table, cache_win, cmp_pe, cmp_w, rel_bias, bsz):
    lq = y.shape[0] // bsz
    n_pages = page_table.shape[1]
    past_len = n_pages * PAGE_SIZE
    win_len = cache_win.shape[1]
    rows = NSA_REP * lq
    ncp = past_len // CMP_STRIDE
    n_cmp = (past_len + lq - CMP_LEN) // CMP_STRIDE + 1
    n_sel = -(-(past_len + lq) // SEL_LEN)
    assert lq % 8 == 0 and lq <= CMP_STRIDE and n_cmp <= ncp - 1 + lq // CMP_STRIDE and past_len % NS_CHUNK == 0
    assert past_len % SEL_LEN == 0 and n_sel <= NS_SELW and lq % Q_BLOCK != 0 and win_len == WINDOW
    tables = _nsa_sample_tables(rel_bias, past_len, lq, win_len)
    c_lo = np.arange(ncp) * CMP_STRIDE
    s_lo = np.arange(NS_SELW) * SEL_LEN
    cover = ((c_lo[:, None] < s_lo[None] + SEL_LEN) & (c_lo[:, None] + CMP_LEN > s_lo[None])
             & (np.arange(ncp)[:, None] < n_cmp) & (np.arange(NS_SELW)[None] < n_sel))
    expand = np.arange(NS_CHUNK // SEL_LEN)[:, None] == (np.arange(NS_CHUNK) // SEL_LEN)[None]
    assert cache_kv.shape[2:] == (PAGE_SIZE, 4, NSA_KV_HEADS, NSA_HEAD_DIM)
    win2 =cache_win.reshape(bsz, win_len, 2 * KVW)

    def full(shape):
        return pl.BlockSpec(shape, lambda b, pt: (0,) * len(shape))

    grid_spec = pltpu.PrefetchScalarGridSpec(
        num_scalar_prefetch=1, grid=(bsz,),
        in_specs=[pl.BlockSpec((lq, D_NSA), lambda b, pt: (b, QN_OFF // D_NSA)),
                  pl.BlockSpec((lq, 6 * KVW), lambda b, pt: (b, KV_OFF // (6 * KVW))),
                  pl.BlockSpec((lq, LANE), lambda b, pt: (b, GATE_OFF // LANE)),
                  pl.BlockSpec((lq, LANE), lambda b, pt: (b, GATE_OFF // LANE + 1)),
                  pl.BlockSpec((1, win_len, 2 * KVW), lambda b, pt: (b, 0, 0)),
                  pl.BlockSpec(memory_space=pl.ANY),
                  full((2, CMP_LEN, NSA_HEAD_DIM, NSA_HEAD_DIM)), full((2, CMP_LEN, NSA_HEAD_DIM)),
                  full((NSA_KV_HEADS, rows, ncp)), full((NSA_KV_HEADS, rows, LANE)), full((NSA_KV_HEADS, rows, LANE)),
                  full((NSA_KV_HEADS, rows, LANE)), full((NSA_KV_HEADS, rows, win_len + LANE)),
                  full((ncp, NS_SELW)), full((NS_CHUNK // SEL_LEN, NS_CHUNK))],
        out_specs=pl.BlockSpec((lq, D_NSA), lambda b, pt: (b, 0)),
        scratch_shapes=[pltpu.VMEM((2 * NSA_KV_HEADS, past_len, NSA_HEAD_DIM), F32),
                        pltpu.VMEM((2, NSA_KV_HEADS, ncp, NSA_HEAD_DIM), F32),
                        pltpu.VMEM((rows, past_len + LANE), F32), pltpu.SemaphoreType.DMA(())])
    assert QN_OFF % D_NSA == 0 and KV_OFF % (6 * KVW) == 0
    return pl.pallas_call(
        functools.partial(_nsa_sample_kernel, n_pages=n_pages, lq=lq, past_len=past_len, win_len=win_len,
                          layer=layer),
        grid_spec=grid_spec,
        out_shape=jax.ShapeDtypeStruct((bsz * lq, D_NSA), BF16),
        compiler_params=pltpu.CompilerParams(dimension_semantics=('arbitrary',), vmem_limit_bytes=VMEM_LIMIT_BYTES),
        name='nsa_sample',
    )(page_table, y, y, y, y, win2, cache_kv, cmp_w.astype(BF16), cmp_pe, *tables,
      jnp.asarray(cover, BF16), jnp.asarray(expand, BF16))


def _mm_kernel(x_ref, w_ref, o_ref, *scratch, nk, act):
    def finish(acc):
        if act == 'sqrelu':
            acc = jnp.square(jnp.maximum(acc, 0.0))
        o_ref[...] = acc.astype(o_ref.dtype)

    if nk == 1:
        finish(jnp.dot(x_ref[...], w_ref[...], preferred_element_type=F32))
        return
    acc_ref, = scratch
    k = pl.program_id(2)
    part = jnp.dot(x_ref[...], w_ref[...], preferred_element_type=F32)

    @pl.when(k == 0)
    def _():
        acc_ref[...] = part

    @pl.when(k > 0)
    def _():
        acc_ref[...] += part

    @pl.when(k == nk - 1)
    def _():
        finish(acc_ref[...])


def _pick(n, pref):
    for t in pref:
        if n % t == 0:
            return t
    return n


def _matmul(x, w, out_dtype=F32, act=None, name='matmul'):
    m, k = x.shape
    n = w.shape[1]
    tm = _pick(m, (1024, 512, 256, 128, 64))
    tn = _pick(n, (512, 256, 128))
    tk = _pick(k, (4096, 2048, 1024, 512))
    nk = k // tk
    scratch = [pltpu.VMEM((tm, tn), F32)] if nk > 1 else []
    return pl.pallas_call(
        functools.partial(_mm_kernel, nk=nk, act=act),
        grid=(m // tm, n // tn, nk),
        in_specs=[pl.BlockSpec((tm, tk), lambda i, j, kk: (i, kk)),
                  pl.BlockSpec((tk, tn), lambda i, j, kk: (kk, j))],
        out_specs=pl.BlockSpec((tm, tn), lambda i, j, kk: (i, j)),
        out_shape=jax.ShapeDtypeStruct((m, n), out_dtype),
        scratch_shapes=scratch,
        compiler_params=pltpu.CompilerParams(
            dimension_semantics=('parallel', 'parallel', 'arbitrary'),
            vmem_limit_bytes=VMEM_LIMIT_BYTES),
        name=name,
    )(x, w)


CAST_CHUNK = 512


def _mm_ws_kernel(x_ref, w_ref, o_ref, wb_ref, *, act):
    def finish(acc):
        if act == 'sqrelu':
            acc = jnp.square(jnp.maximum(acc, 0.0))
        o_ref[...] = acc.astype(o_ref.dtype)

    @pl.when(pl.program_id(1) == 0)
    def _():
        acc = jnp.zeros(o_ref.shape, F32)
        for c in range(w_ref.shape[1] // CAST_CHUNK):
            rows = slice(c * CAST_CHUNK, (c + 1) * CAST_CHUNK)
            wc = w_ref[0, rows, :].astype(BF16)
            wb_ref[rows, :] = wc
            acc = acc + jnp.dot(x_ref[:, rows], wc, preferred_element_type=F32)
        finish(acc)

    @pl.when(pl.program_id(1) > 0)
    def _():
        finish(jnp.dot(x_ref[...], wb_ref[...], preferred_element_type=F32))


def _matmul_cast(x, w, layer, out_dtype=F32, act=None, name='matmul'):
    m, k = x.shape
    n = w.shape[2]
    tm = _pick(m, (1024,))
    tn = _pick(n, (512,))
    return pl.pallas_call(
        functools.partial(_mm_ws_kernel, act=act),
        grid=(n // tn, m // tm),
        in_specs=[pl.BlockSpec((tm, k), lambda j, i: (i, 0)),
                  pl.BlockSpec((1, k, tn), lambda j, i: (layer, 0, j))],
        out_specs=[pl.BlockSpec((tm, tn), lambda j, i: (i, j)), pl.BlockSpec((k, tn), lambda j, i: (0, j))],
        out_shape=[jax.ShapeDtypeStruct((m, n), out_dtype), jax.ShapeDtypeStruct((k, n), BF16)],
        compiler_params=pltpu.CompilerParams(dimension_semantics=('parallel', 'arbitrary'),
                                             vmem_limit_bytes=VMEM_LIMIT_BYTES),
        name=name,
    )(x, w)


CAST_BLOCK_BYTES = 8 * 1024 * 1024


def _cast_kernel(w_ref, o_ref):
    o_ref[...] = w_ref[0].astype(o_ref.dtype)


def _to_bf16(w, layer):
    _, k, n = w.shape
    tk = CAST_BLOCK_BYTES // (4 * n)
    assert tk % 16 == 0 and k % tk == 0
    return pl.pallas_call(
        _cast_kernel, grid=(k // tk,),
        in_specs=[pl.BlockSpec((1, tk, n), lambda i: (layer, i, 0))], out_specs=pl.BlockSpec((tk, n), lambda i: (i, 0)),
        out_shape=jax.ShapeDtypeStruct((k, n), BF16),
        compiler_params=pltpu.CompilerParams(dimension_semantics=('parallel',), vmem_limit_bytes=VMEM_LIMIT_BYTES),
        name='weight_cast',
    )(w)


def _out_proj_kernel(a_ref, b_ref, c_ref, wa_ref, wb_ref, wc_ref, o_ref):
    o_ref[...] = (jnp.dot(a_ref[...], wa_ref[...], preferred_element_type=F32)
                  + jnp.dot(b_ref[...], wb_ref[...], preferred_element_type=F32)
                  + jnp.dot(c_ref[...], wc_ref[...], preferred_element_type=F32))


def _out_proj(y_ssd, y_nsa, y_gla, w):
    m = y_ssd.shape[0]
    n = w.shape[1]
    tm = _pick(m, (1024, 64))
    tn = _pick(n, (512,))
    assert D_SSD % D_NSA == 0 and D_NSA == D_GLA

    def act(width):
        return pl.BlockSpec((tm, width), lambda i, j: (i, 0))

    return pl.pallas_call(
        _out_proj_kernel, grid=(m // tm, n // tn),
        in_specs=[act(D_SSD), act(D_NSA), act(D_GLA),
                  pl.BlockSpec((D_SSD, tn), lambda i, j: (0, j)),
                  pl.BlockSpec((D_NSA, tn), lambda i, j: (D_SSD // D_NSA, j)),
                  pl.BlockSpec((D_GLA, tn), lambda i, j: (D_SSD // D_NSA + 1, j))],
        out_specs=pl.BlockSpec((tm, tn), lambda i, j: (i, j)),
        out_shape=jax.ShapeDtypeStruct((m, n), F32),
        compiler_params=pltpu.CompilerParams(dimension_semantics=('parallel', 'parallel'),
                                             vmem_limit_bytes=VMEM_LIMIT_BYTES),
        name='out_proj',
    )(y_ssd, y_nsa, y_gla, w, w, w)


def _out_proj_cast_kernel(a_ref, b_ref, c_ref, w_ref, o_ref, wb_ref):
    parts = ((a_ref, 0), (b_ref, D_SSD), (c_ref, D_SSD + D_NSA))

    @pl.when(pl.program_id(1) == 0)
    def _():
        acc = jnp.zeros(o_ref.shape, F32)
        for x_ref, row0 in parts:
            for c in range(x_ref.shape[1] // CAST_CHUNK):
                rows = slice(row0 + c * CAST_CHUNK, row0 + (c + 1) * CAST_CHUNK)
                wc = w_ref[0, rows, :].astype(BF16)
                wb_ref[rows, :] = wc
                acc = acc + jnp.dot(x_ref[:, c * CAST_CHUNK:(c + 1) * CAST_CHUNK], wc, preferred_element_type=F32)
        o_ref[...] = acc

    @pl.when(pl.program_id(1) > 0)
    def _():
        o_ref[...] = sum(jnp.dot(x_ref[...], wb_ref[row0:row0 + x_ref.shape[1], :], preferred_element_type=F32)
                         for x_ref, row0 in parts)


def _out_proj_cast(y_ssd, y_nsa, y_gla, w, layer):
    m = y_ssd.shape[0]
    n = w.shape[2]
    tm = _pick(m, (1024,))
    tn = _pick(n, (512,))

    def act(width):
        return pl.BlockSpec((tm, width), lambda j, i: (i, 0))

    return pl.pallas_call(
        _out_proj_cast_kernel, grid=(n // tn, m // tm),
        in_specs=[act(D_SSD), act(D_NSA), act(D_GLA), pl.BlockSpec((1, D_MIX, tn), lambda j, i: (layer, 0, j))],
        out_specs=[pl.BlockSpec((tm, tn), lambda j, i: (i, j)), pl.BlockSpec((D_MIX, tn), lambda j, i: (0, j))],
        out_shape=[jax.ShapeDtypeStruct((m, n), F32), jax.ShapeDtypeStruct((D_MIX, n), BF16)],
        compiler_params=pltpu.CompilerParams(dimension_semantics=('parallel', 'arbitrary'),
                                             vmem_limit_bytes=VMEM_LIMIT_BYTES),
        name='out_proj',
    )(y_ssd, y_nsa, y_gla, w)


def _ln_kernel(x_ref, h_ref, g_ref, b_ref, o_ref, ob_ref):
    v = DN_ALPHA * x_ref[...] + h_ref[...]
    d = v - jnp.mean(v, -1, keepdims=True)
    y = d * lax.rsqrt(jnp.mean(d * d, -1, keepdims=True) + LN_EPS) * g_ref[...] + b_ref[...]
    o_ref[...] = y
    ob_ref[...] = y.astype(BF16)


def _add_layernorm(x, h, g, b):
    m, d = x.shape
    tm = _pick(m, (256, 64))
    row = pl.BlockSpec((tm, d), lambda i: (i, 0))
    vec = pl.BlockSpec((1, d), lambda i: (0, 0))
    return pl.pallas_call(
        _ln_kernel, grid=(m // tm,), in_specs=[row, row, vec, vec], out_specs=[row, row],
        out_shape=[jax.ShapeDtypeStruct((m, d), F32), jax.ShapeDtypeStruct((m, d), BF16)],
        compiler_params=pltpu.CompilerParams(dimension_semantics=('parallel',), vmem_limit_bytes=VMEM_LIMIT_BYTES),
        name='add_layernorm',
    )(x, h, g.reshape(1, d), b.reshape(1, d))


def _xattn_kernel(q_ref, k_ref, v_ref, o_ref):
    q = (q_ref[...] * (X_HEAD_DIM ** -0.5)).astype(BF16)
    s = _dot_nt(q, k_ref[0].astype(BF16))
    e = jnp.exp(s - jnp.max(s, -1, keepdims=True))
    p = e / jnp.sum(e, -1, keepdims=True)
    o_ref[...] = jnp.dot(p.astype(BF16), v_ref[0].astype(BF16), preferred_element_type=F32).astype(o_ref.dtype)


def _cross_attention(q, mem_kv, bsz):
    m = q.shape[0]
    l = m // bsz
    tq = _pick(l, (512, 8))
    nq = l // tq
    return pl.pallas_call(
        _xattn_kernel, grid=(bsz, X_HEADS, nq),
        in_specs=[pl.BlockSpec((tq, X_HEAD_DIM), lambda b, h, i: (b * nq + i, h)),
                  pl.BlockSpec((1, N_MEM, X_HEAD_DIM), lambda b, h, i: (b, 0, h)),
                  pl.BlockSpec((1, N_MEM, X_HEAD_DIM), lambda b, h, i: (b, 0, X_HEADS + h))],
        out_specs=pl.BlockSpec((tq, X_HEAD_DIM), lambda b, h, i: (b * nq + i, h)),
        out_shape=jax.ShapeDtypeStruct((m, D_MODEL), BF16),
        compiler_params=pltpu.CompilerParams(dimension_semantics=('parallel', 'parallel', 'parallel'),
                                             vmem_limit_bytes=VMEM_LIMIT_BYTES),
        name='cross_attention',
    )(q, mem_kv, mem_kv)


def _softplus(x):
    return jnp.maximum(x, 0.0) + jnp.log1p(jnp.exp(-jnp.abs(x)))


def _silu(x):
    return x * jax.nn.sigmoid(x)


def _pad_rows(x, rows):
    if x.shape[0] == rows:
        return x
    return jnp.concatenate([x, jnp.zeros((rows - x.shape[0],) + x.shape[1:], x.dtype)], axis=0)


def _cumsum_rows(x, seg):
    r = lax.broadcasted_iota(jnp.int32, x.shape, 0) % seg
    k = 1
    while k < seg:
        x = x + jnp.where(r >= k, pltpu.roll(x, k, 0), 0.0)
        k *= 2
    return x


def _segment_last(x, seg):
    n = x.shape[0]
    r = lax.broadcasted_iota(jnp.int32, x.shape, 0) % seg
    k = 1
    while k < seg:
        x = jnp.where(r < seg - k, pltpu.roll(x, n - k, 0), x)
        k *= 2
    return x


GLA_BLOCK = 128


def _gla_kernel(q_ref, k_ref, v_ref, g_ref, a_ref, w2_ref, gb_ref, nw_ref, s0_ref, o_ref, s_out_ref,
                s_scr, q_scr, k_scr, v_scr, bc_scr, o_scr, *, rows_in, ch):
    c = pl.program_id(1)
    nsub = -(-rows_in // ch)
    hk = GLA_DK

    @pl.when(c == 0)
    def _():
        s_scr[...] = s0_ref[0]

    valid = lax.broadcasted_iota(jnp.int32, (GLA_BLOCK, 1), 0) < rows_in
    a = _pad_rows(a_ref[...], GLA_BLOCK).astype(BF16)
    logf = -_softplus(-(jnp.dot(a, w2_ref[...], preferred_element_type=F32) + gb_ref[...])) / GLA_TAU
    logf = jnp.where(valid, logf, 0.0)
    bc = _cumsum_rows(logf, ch)
    b_last = _segment_last(bc, ch)
    q = _pad_rows(q_ref[...], GLA_BLOCK) * (GLA_DK ** -0.5)
    k = _pad_rows(k_ref[...], GLA_BLOCK)
    v = _pad_rows(v_ref[...], GLA_BLOCK)
    q_scr[...] = q
    k_scr[...] = k
    v_scr[...] = v
    bc_scr[...] = bc

    tt = lax.broadcasted_iota(jnp.int32, (ch, 1), 0)

    def intra(j, carry):
        r0 = pl.multiple_of(j * ch, ch)
        qj = q_scr[pl.ds(r0, ch), :]
        bj = bc_scr[pl.ds(r0, ch), :]
        acc = [jnp.zeros((ch, GLA_DV), F32) for _ in range(GLA_HEADS)]
        for s in range(ch):
            ks = k_scr[pl.ds(r0 + s, 1), :]
            bs = bc_scr[pl.ds(r0 + s, 1), :]
            vs = v_scr[pl.ds(r0 + s, 1), :]
            w = jnp.where(tt >= s, qj * ks * jnp.exp(jnp.minimum(bj - bs, 0.0)), 0.0)
            for h in range(GLA_HEADS):
                att = jnp.sum(w[:, h * hk:(h + 1) * hk], -1, keepdims=True)
                acc[h] = acc[h] + att * vs[:, h * GLA_DV:(h + 1) * GLA_DV]
        o_scr[pl.ds(r0, ch), :] = jnp.concatenate(acc, axis=1)
        return carry

    lax.fori_loop(0, nsub, intra, 0)

    qe = q * jnp.exp(bc)
    kst = k * jnp.exp(b_last - bc)
    dec = jnp.exp(b_last)
    o_intra = _pad_rows(o_scr[0:nsub * ch, :], GLA_BLOCK)
    chunk_of_col = lax.broadcasted_iota(jnp.int32, (hk, GLA_BLOCK), 1) // ch
    g = _pad_rows(g_ref[...], GLA_BLOCK)
    outs = []
    for h in range(GLA_HEADS):
        kst_t = kst[:, h * hk:(h + 1) * hk].T
        dec_t = dec[:, h * hk:(h + 1) * hk].T
        lhs = jnp.concatenate([jnp.where(chunk_of_col == j, kst_t, 0.0) for j in range(nsub)], axis=0).astype(BF16)
        kv = jnp.dot(lhs, v[:, h * GLA_DV:(h + 1) * GLA_DV].astype(BF16), preferred_element_type=F32)
        s_h = s_scr[h]
        s_before = []
        for j in range(nsub):
            s_before.append(s_h.astype(BF16))
            s_h = s_h * dec_t[:, j * ch:j * ch + 1] + kv[j * hk:(j + 1) * hk]
        s_scr[h] = s_h
        big = jnp.dot(qe[:, h * hk:(h + 1) * hk].astype(BF16), jnp.concatenate(s_before, axis=1), preferred_element_type=F32)
        o_inter = _pad_rows(jnp.concatenate([big[j * ch:(j + 1) * ch, j * GLA_DV:(j + 1) * GLA_DV] for j in range(nsub)], axis=0),
                            GLA_BLOCK)
        o_h = o_intra[:, h * GLA_DV:(h + 1) * GLA_DV] + o_inter
        o_h = o_h * lax.rsqrt(jnp.mean(o_h * o_h, -1, keepdims=True) + NORM_EPS) * nw_ref[...]
        outs.append(o_h)
    o = jnp.concatenate(outs, axis=1) * _silu(g)
    o_ref[...] = o[:rows_in].astype(o_ref.dtype)

    @pl.when(c == pl.num_programs(1) - 1)
    def _():
        s_out_ref[0] = s_scr[...]


def _gla(y, s0, gate_w2, gate_b, norm_w, bsz):
    m = y.shape[0]
    l = m // bsz
    rows_in = min(GLA_BLOCK, l)
    ch = GLA_CHUNK
    nblk = l // rows_in
    hdk = GLA_HEADS * GLA_DK
    w2 = jnp.zeros((LANE, hdk), BF16).at[ALR_OFF - SMALL_OFF:ALR_OFF - SMALL_OFF + GLA_RANK].set(gate_w2.astype(BF16))

    def rows(width, off):
        assert off % width == 0
        return pl.BlockSpec((rows_in, width), lambda b, c: (b * nblk + c, off // width))

    def const(shape):
        return pl.BlockSpec(shape, lambda b, c: (0,) * len(shape))

    state = pl.BlockSpec((1, GLA_HEADS, GLA_DK, GLA_DV), lambda b, c: (b, 0, 0, 0))
    return pl.pallas_call(
        functools.partial(_gla_kernel, rows_in=rows_in, ch=ch),
        grid=(bsz, nblk),
        in_specs=[rows(hdk, QG_OFF), rows(hdk, KG_OFF), rows(D_GLA, VG_OFF), rows(D_GLA, GG_OFF), rows(LANE, SMALL_OFF),
                  const((LANE, hdk)), const((1, hdk)), const((1, GLA_DV)), state],
        out_specs=[pl.BlockSpec((rows_in, D_GLA), lambda b, c: (b * nblk + c, 0)), state],
        out_shape=[jax.ShapeDtypeStruct((m, D_GLA), BF16), jax.ShapeDtypeStruct((bsz, GLA_HEADS, GLA_DK, GLA_DV), F32)],
        scratch_shapes=[pltpu.VMEM((GLA_HEADS, GLA_DK, GLA_DV), F32), pltpu.VMEM((GLA_BLOCK, hdk), F32),
                        pltpu.VMEM((GLA_BLOCK, hdk), F32), pltpu.VMEM((GLA_BLOCK, D_GLA), F32),
                        pltpu.VMEM((GLA_BLOCK, hdk), F32), pltpu.VMEM((GLA_BLOCK, D_GLA), F32)],
        compiler_params=pltpu.CompilerParams(dimension_semantics=('parallel', 'arbitrary'),
                                             vmem_limit_bytes=VMEM_LIMIT_BYTES),
        name='gla',
    )(y, y, y, y, y, w2, gate_b.reshape(1, hdk), norm_w.reshape(1, GLA_DV), s0)


SSD_R = SSD_HEADS // SSD_GROUPS
SSD_GW = SSD_R * SSD_HEAD_DIM
SSD_CW = SSD_GW + 2 * SSD_STATE
SSD_TAIL = 8


def _ssd_kernel(xs_ref, bm_ref, cm_ref, z_ref, dt_ref, conv0_ref, cw_ref, cb_ref, hp_ref, dskip_ref, nw_ref, h0_ref,
                o_ref, h_out_ref, xbuf, h_scr, *, rows_in):
    g = pl.program_id(1)
    c = pl.program_id(2)
    t = SSD_CHUNK

    @pl.when(c == 0)
    def _():
        xbuf[0:SSD_TAIL, :] = conv0_ref[0, 0]
        h_scr[...] = h0_ref[0, 0]

    xbuf[SSD_TAIL:SSD_TAIL + t, 0:SSD_GW] = _pad_rows(xs_ref[...], t)
    xbuf[SSD_TAIL:SSD_TAIL + t, SSD_GW:SSD_GW + SSD_STATE] = _pad_rows(bm_ref[...], t)
    xbuf[SSD_TAIL:SSD_TAIL + t, SSD_GW + SSD_STATE:SSD_CW] = _pad_rows(cm_ref[...], t)
    conv = cb_ref[0]
    for kk in range(SSD_CONV):
        conv = conv + cw_ref[0, kk:kk + 1, :] * xbuf[SSD_TAIL - (SSD_CONV - 1) + kk:SSD_TAIL - (SSD_CONV - 1) + kk + t, :]
    xbuf[0:SSD_TAIL, :] = xbuf[t:t + SSD_TAIL, :]
    xc = _silu(conv)
    xs = xc[:, 0:SSD_GW]
    bm = xc[:, SSD_GW:SSD_GW + SSD_STATE]
    cm = xc[:, SSD_GW + SSD_STATE:SSD_CW].astype(BF16)

    lane = lax.broadcasted_iota(jnp.int32, (t, LANE), 1)
    row = lax.broadcasted_iota(jnp.int32, (t, LANE), 0)
    dt = pltpu.roll(_pad_rows(dt_ref[...], t), (LANE - SSD_R * g) % LANE, 1)
    dt = jnp.where((lane < SSD_R) & (row < rows_in), _softplus(dt + hp_ref[0, 0:1, :]), 0.0)
    acs = _cumsum_rows(dt * -jnp.exp(hp_ref[0, 1:2, :]), t)
    acs_t = acs.T
    a_last = acs[t - 1:t, :]

    head_of_lane = lax.broadcasted_iota(jnp.int32, (1, SSD_GW), 1) // SSD_HEAD_DIM

    def spread(arr):
        return sum(jnp.where(head_of_lane == r, arr[:, r:r + 1], 0.0) for r in range(SSD_R))

    xdt = xs * spread(dt)
    cb = _dot_nt(cm, bm.astype(BF16))
    tri = lax.broadcasted_iota(jnp.int32, (t, t), 0) >= lax.broadcasted_iota(jnp.int32, (t, t), 1)
    y = jnp.zeros((t, SSD_GW), F32)
    for r in range(SSD_R):
        decay = jnp.where(tri, jnp.exp(jnp.minimum(acs[:, r:r + 1] - acs_t[r:r + 1, :], 0.0)), 0.0)
        x_r = jnp.where(head_of_lane == r, xdt, 0.0).astype(BF16)
        y = y + jnp.dot((cb * decay).astype(BF16), x_r, preferred_element_type=F32)
    h_prev = h_scr[...]
    y = y + jnp.dot(cm, h_prev.astype(BF16), preferred_element_type=F32) * jnp.exp(spread(acs))
    x_st = (xdt * jnp.exp(spread(a_last - acs))).astype(BF16)
    h_new = h_prev * jnp.exp(spread(a_last)) + jnp.dot(bm.T.astype(BF16), x_st, preferred_element_type=F32)
    h_scr[...] = h_new

    y = (y + xs * dskip_ref[0]) * _silu(_pad_rows(z_ref[...], t))
    y = y * lax.rsqrt(jnp.mean(y * y, -1, keepdims=True) + NORM_EPS) * nw_ref[0]
    o_ref[...] = y[:rows_in].astype(o_ref.dtype)

    @pl.when(c == pl.num_programs(2) - 1)
    def _():
        h_out_ref[0, 0] = h_new


def _ssd(y, conv0, h0, conv_w, conv_b, dt_bias, a_log, d_skip, norm_w, bsz):
    m = y.shape[0]
    l = m // bsz
    rows_in = min(SSD_CHUNK, l)
    nblk = l // rows_in
    ng = SSD_GROUPS

    def per_group(v):
        xs = v[..., :D_SSD].reshape(v.shape[:-1] + (ng, SSD_GW))
        bm = v[..., D_SSD:D_SSD + ng * SSD_STATE].reshape(v.shape[:-1] + (ng, SSD_STATE))
        cm = v[..., D_SSD + ng * SSD_STATE:].reshape(v.shape[:-1] + (ng, SSD_STATE))
        return jnp.moveaxis(jnp.concatenate([xs, bm, cm], axis=-1), -2, 0)

    conv0_g = jnp.moveaxis(per_group(jnp.pad(conv0, ((0, 0), (SSD_TAIL - (SSD_CONV - 1), 0), (0, 0)))), 0, 1)
    cw_g = per_group(conv_w)
    cb_g = per_group(conv_b[None])
    hp = jnp.zeros((ng, 8, LANE), F32)
    hp = hp.at[:, 0, :SSD_R].set(dt_bias.reshape(ng, SSD_R)).at[:, 1, :SSD_R].set(a_log.reshape(ng, SSD_R))
    dskip_g = jnp.repeat(d_skip, SSD_HEAD_DIM).reshape(ng, 1, SSD_GW)
    nw_g = norm_w.reshape(ng, 1, SSD_GW)
    h0_t = h0.reshape(bsz, ng, SSD_GW, SSD_STATE).transpose(0, 1, 3, 2)

    def rows(width, off):
        assert off % width == 0
        return pl.BlockSpec((rows_in, width), lambda b, g, c: (b * nblk + c, off // width + g))

    def grp(shape):
        return pl.BlockSpec((1,) + shape, lambda b, g, c: (g,) + (0,) * len(shape))

    state = pl.BlockSpec((1, 1, SSD_STATE, SSD_GW), lambda b, g, c: (b, g, 0, 0))
    out, h_t = pl.pallas_call(
        functools.partial(_ssd_kernel, rows_in=rows_in),
        grid=(bsz, ng, nblk),
        in_specs=[rows(SSD_GW, XBC_OFF), rows(SSD_STATE, XBC_OFF + D_SSD), rows(SSD_STATE, XBC_OFF + D_SSD + ng * SSD_STATE),
                  rows(SSD_GW, Z_OFF), pl.BlockSpec((rows_in, LANE), lambda b, g, c: (b * nblk + c, SMALL_OFF // LANE)),
                  pl.BlockSpec((1, 1, SSD_TAIL, SSD_CW), lambda b, g, c: (b, g, 0, 0)),
                  grp((SSD_CONV, SSD_CW)), grp((1, SSD_CW)), grp((8, LANE)), grp((1, SSD_GW)), grp((1, SSD_GW)), state],
        out_specs=[pl.BlockSpec((rows_in, SSD_GW), lambda b, g, c: (b * nblk + c, g)), state],
        out_shape=[jax.ShapeDtypeStruct((m, D_SSD), BF16), jax.ShapeDtypeStruct((bsz, ng, SSD_STATE, SSD_GW), F32)],
        scratch_shapes=[pltpu.VMEM((SSD_TAIL + SSD_CHUNK, SSD_CW), F32), pltpu.VMEM((SSD_STATE, SSD_GW), F32)],
        compiler_params=pltpu.CompilerParams(dimension_semantics=('parallel', 'parallel', 'arbitrary'),
                                             vmem_limit_bytes=VMEM_LIMIT_BYTES),
        name='ssd',
    )(y, y, y, y, y, conv0_g, cw_g, cb_g, hp, dskip_g, nw_g, h0_t)
    h_new = h_t.transpose(0, 1, 3, 2).reshape(bsz, SSD_HEADS, SSD_HEAD_DIM, SSD_STATE)
    return out, h_new


def _layer_flat(x, xb, bsz, lp, f32_weights, layer_idx, rel_bias, tables, mem_kv, conv0, ssd_h0, gla_s0, nsa_past, win_past):
    m = x.shape[0]
    l = m // bsz
    assert l >= SSD_CONV - 1
    y = _matmul(xb, lp['w_in'], name='in_proj')
    y3 = y.reshape(bsz, l, D_IN_PAD)
    y_ssd, h_new = _ssd(y, conv0, ssd_h0, lp['ssd_conv_w'], lp['ssd_conv_b'], lp['ssd_dt_bias'], lp['ssd_a_log'],
                        lp['ssd_d'], lp['ssd_norm_w'], bsz)
    conv_new = y3[:, l - (SSD_CONV - 1):, XBC_OFF:XBC_OFF + SSD_CONV_DIM]
    y_gla, s_new = _gla(y, gla_s0, lp['gla_gate_w2'], lp['gla_gate_b'], lp['gla_norm_w'], bsz)
    rows = y3[..., KV_OFF:KV_OFF + 4 * KVW].reshape(bsz, l, 4, NSA_KV_HEADS, NSA_HEAD_DIM)
    win_rows = y3[..., KV_OFF + 4 * KVW:KV_OFF + 6 * KVW].reshape(bsz, l, 2, NSA_KV_HEADS, NSA_HEAD_DIM)
    if nsa_past is None:
        win_new = win_rows[:, -min(WINDOW, l):]
        y_nsa = _nsa_prompt_t(y, _nsa_compress_prompt(y, lp['nsa_cmp_pe'], lp['nsa_cmp_w'], bsz), tables, bsz)
    else:
        cache_kv, layer, page_table = nsa_past
        win_new = jnp.concatenate([win_past, win_rows], axis=1)[:, -win_past.shape[1]:]
        y_nsa = _nsa_sample(y, cache_kv, layer, page_table, win_past, lp['nsa_cmp_pe'], lp['nsa_cmp_w'], rel_bias, bsz)
    def mm(key, a, **kw):
        if key not in lp:
            out, lp[key] = _matmul_cast(a, f32_weights[key], layer_idx, **kw)
            return out
        return _matmul(a, lp[key], **kw)

    if 'w_out' not in lp:
        h, lp['w_out'] = _out_proj_cast(y_ssd, y_nsa, y_gla, f32_weights['w_out'], layer_idx)
    else:
        h = _out_proj(y_ssd, y_nsa, y_gla, lp['w_out'])
    x, xb = _add_layernorm(x, h, lp['ln_g'][0], lp['ln_b'][0])
    o = _cross_attention(mm('x_wq', xb, name='xattn_q'), mem_kv, bsz)
    x, xb = _add_layernorm(x, mm('x_wo', o, name='xattn_o'), lp['ln_g'][1], lp['ln_b'][1])
    hidden = mm('ffn_w1', xb, out_dtype=BF16, act='sqrelu', name='ffn_up')
    x, xb = _add_layernorm(x, _matmul(hidden, lp['ffn_w2'], name='ffn_down'), lp['ln_g'][2], lp['ln_b'][2])
    return x, xb, (rows, win_new, conv_new, h_new, s_new)


def kernel(x_prompt, x_sample, cache_nsa_kv, cache_nsa_win, state_ssd, state_ssd_conv, state_gla, cache_mem_kv,
           page_table, mem_prompt, w_in, ssd_conv_w, ssd_conv_b, ssd_dt_bias, ssd_a_log, ssd_d, ssd_norm_w,
           nsa_cmp_pe, nsa_cmp_w, rel_bias, gla_gate_w2, gla_gate_b, gla_norm_w, w_out, x_wq, x_wkv, x_wo,
           ffn_w1, ffn_w2, ln_g, ln_b):
    bp = x_prompt.shape[0]
    bs = x_sample.shape[0]
    xp, xs = x_prompt.reshape(-1, D_MODEL), x_sample.reshape(-1, D_MODEL)
    xpb, xsb = xp.astype(BF16), xs.astype(BF16)
    st_p, st_s, mem_p = [], [], []
    tables = _nsa_prompt_bias_tables_t(rel_bias)
    f32_weights = dict(w_out=w_out, x_wq=x_wq, x_wo=x_wo, ffn_w1=ffn_w1)
    mem_b = mem_prompt.reshape(-1, D_MODEL).astype(BF16)
    for l in range(DEPTH):
        lp = dict(w_in=_reorder_w_in(w_in[l]), ssd_conv_w=ssd_conv_w[l], ssd_conv_b=ssd_conv_b[l],
                  ssd_dt_bias=ssd_dt_bias[l], ssd_a_log=ssd_a_log[l], ssd_d=ssd_d[l], ssd_norm_w=ssd_norm_w[l],
                  nsa_cmp_pe=nsa_cmp_pe[l], nsa_cmp_w=nsa_cmp_w[l], gla_gate_w2=gla_gate_w2[l], gla_gate_b=gla_gate_b[l],
                  gla_norm_w=gla_norm_w[l], ffn_w2=_to_bf16(ffn_w2, l), ln_g=ln_g[l], ln_b=ln_b[l])
        mem_kv_p, _ = _matmul_cast(mem_b, x_wkv, l, name='mem_kv')
        mem_kv_p = mem_kv_p.reshape(bp, N_MEM, 2 * D_MODEL)
        conv0 = jnp.zeros((bp, SSD_CONV - 1, SSD_CONV_DIM), F32)
        h0 = jnp.zeros((bp, SSD_HEADS, SSD_HEAD_DIM, SSD_STATE), F32)
        s0 = jnp.zeros((bp, GLA_HEADS, GLA_DK, GLA_DV), F32)
        xp, xpb, stp = _layer_flat(xp, xpb, bp, lp, f32_weights, l, rel_bias, tables, mem_kv_p, conv0, h0, s0, None, None)
        st_p.append(stp)
        mem_p.append(mem_kv_p.reshape(bp, N_MEM, 2, X_HEADS, X_HEAD_DIM))
        xs, xsb, sts = _layer_flat(xs, xsb, bs, lp, f32_weights, l, rel_bias, tables,
                                   cache_mem_kv[l].reshape(bs, N_MEM, 2 * D_MODEL), state_ssd_conv[l], state_ssd[l],
                                   state_gla[l], (cache_nsa_kv, l, page_table), cache_nsa_win[l])
        st_s.append(sts)
    p_rows, p_win, p_conv, p_ssd, p_gla = [jnp.stack(s) for s in zip(*st_p)]
    s_rows, s_win, s_conv, s_ssd, s_gla = [jnp.stack(s) for s in zip(*st_s)]
    p_mem = jnp.stack(mem_p)
    return (xp.reshape(x_prompt.shape), xs.reshape(x_sample.shape), p_rows, s_rows, p_win, s_win, p_ssd, s_ssd,
            p_conv, s_conv, p_gla, s_gla, p_mem)
```

```python
import functools
import math

import jax
import jax.numpy as jnp
import numpy as np
from jax import lax
from jax.experimental import pallas as pl
from jax.experimental.pallas import tpu as pltpu

F32 = jnp.float32
BF16 = jnp.bfloat16

D_MODEL = 4096
DEPTH = 2
PAGE_SIZE = 128
D_MIX = D_MODEL
D_SSD = D_MIX // 2
SSD_HEAD_DIM = 64
SSD_HEADS = D_SSD // SSD_HEAD_DIM
SSD_GROUPS = 8
SSD_STATE = 128
SSD_CONV = 4
SSD_CHUNK = 128
SSD_CONV_DIM = D_SSD + 2 * SSD_GROUPS * SSD_STATE
D_NSA = D_MIX // 4
NSA_HEAD_DIM = 128
NSA_HEADS = D_NSA // NSA_HEAD_DIM
NSA_KV_HEADS = 2
NSA_REP = NSA_HEADS // NSA_KV_HEADS
CMP_LEN = 32
CMP_STRIDE = 16
SEL_LEN = 64
TOP_N = 16
WINDOW = 512
Q_BLOCK = 128
FORCE_SCORE = 1e4
D_GLA = D_MIX - D_SSD - D_NSA
GLA_HEADS = 4
GLA_DV = D_GLA // GLA_HEADS
GLA_DK = GLA_DV // 2
GLA_RANK = 16
GLA_TAU = 16.0
GLA_CHUNK = 16
N_MEM = 256
X_HEADS = 4
X_HEAD_DIM = D_MODEL // X_HEADS
D_FF = 4 * D_MODEL
N_BUCKETS = 32
MAX_DISTANCE = 128
LN_EPS = 1e-5
NORM_EPS = 1e-6
DN_ALPHA = (2 * DEPTH) ** 0.25
IN_SPLITS = (D_SSD, SSD_CONV_DIM, SSD_HEADS, NSA_HEADS * NSA_HEAD_DIM) + (NSA_KV_HEADS * NSA_HEAD_DIM,) * 6 + (
    3 * NSA_HEADS, GLA_HEADS * GLA_DK, GLA_HEADS * GLA_DK, D_GLA, D_GLA, GLA_RANK)
D_IN = sum(IN_SPLITS)

LANE = 128
VMEM_LIMIT_BYTES = 56 * 1024 * 1024

KVW = NSA_KV_HEADS * NSA_HEAD_DIM
Z_OFF = 0
XBC_OFF = Z_OFF + D_SSD
QN_OFF = XBC_OFF + SSD_CONV_DIM
VG_OFF = QN_OFF + NSA_HEADS * NSA_HEAD_DIM
GG_OFF = VG_OFF + D_GLA
KV_OFF = GG_OFF + D_GLA
QG_OFF = KV_OFF + 6 * KVW
KG_OFF = QG_OFF + GLA_HEADS * GLA_DK
SMALL_OFF = KG_OFF + GLA_HEADS * GLA_DK
DT_OFF = SMALL_OFF
ALR_OFF = SMALL_OFF + SSD_HEADS
GATE_OFF = SMALL_OFF + LANE
D_IN_PAD = 12288
NEG = -1e30


def _in_proj_column_map():
    off = np.cumsum((0,) + IN_SPLITS)
    z, xbc, dt, qn = off[0], off[1], off[2], off[3]
    kv0, gn, qg, kg, vg, gg, alr = off[4], off[10], off[11], off[12], off[13], off[14], off[15]
    cmap = np.full((D_IN_PAD,), -1, np.int64)
    cmap[Z_OFF:Z_OFF + D_SSD] = z + np.arange(D_SSD)
    cmap[XBC_OFF:XBC_OFF + SSD_CONV_DIM] = xbc + np.arange(SSD_CONV_DIM)
    cmap[QN_OFF:QN_OFF + D_NSA] = qn + np.arange(D_NSA)
    cmap[KV_OFF:KV_OFF + 6 * KVW] = kv0 + np.arange(6 * KVW)
    cmap[QG_OFF:QG_OFF + 512] = qg + np.arange(512)
    cmap[KG_OFF:KG_OFF + 512] = kg + np.arange(512)
    cmap[VG_OFF:VG_OFF + D_GLA] = vg + np.arange(D_GLA)
    cmap[GG_OFF:GG_OFF + D_GLA] = gg + np.arange(D_GLA)
    cmap[DT_OFF:DT_OFF + SSD_HEADS] = dt + np.arange(SSD_HEADS)
    cmap[ALR_OFF:ALR_OFF + GLA_RANK] = alr + np.arange(GLA_RANK)
    for g in range(NSA_KV_HEADS):
        cmap[GATE_OFF + g * LANE:GATE_OFF + g * LANE + 3 * NSA_REP] = gn + g * 3 * NSA_REP + np.arange(3 * NSA_REP)
    return cmap


def _reorder_w_in(w):
    cmap = _in_proj_column_map()
    cuts = [0] + [i for i in range(1, D_IN_PAD) if cmap[i] != cmap[i - 1] + 1 and not (cmap[i] == -1 and cmap[i - 1] == -1)] + [D_IN_PAD]
    pieces = []
    for a, b in zip(cuts[:-1], cuts[1:]):
        pieces.append(jnp.zeros((w.shape[0], b - a), w.dtype) if cmap[a] < 0 else w[:, cmap[a]:cmap[a] + b - a])
    return jnp.concatenate(pieces, axis=1).astype(BF16)


def _dot_nt(a, b):
    return lax.dot_general(a, b, (((1,), (1,)), ((), ())), preferred_element_type=F32)


NSA_L = 2048
NSA_QB = Q_BLOCK
NSA_NQB = NSA_L // NSA_QB
NSA_NCP = NSA_L // CMP_STRIDE
NSA_NSEL = NSA_L // SEL_LEN
CMP_HALF = CMP_STRIDE * NSA_HEAD_DIM
NSA_KSTEP = 512


def _t5_bucket_np(dist):
    n = np.maximum(dist, 0)
    max_exact = N_BUCKETS // 2
    nf = np.maximum(n, 1).astype(np.float32)
    large = max_exact + (np.log(nf / max_exact) / np.float32(math.log(MAX_DISTANCE / max_exact)) * (N_BUCKETS - max_exact)).astype(np.int32)
    return np.where(n < max_exact, n, np.minimum(large, N_BUCKETS - 1)).astype(np.int32)


def _nsa_compress_kernel(a_ref, pe_ref, w_ref, o_ref):
    a = a_ref[0, 0, 0]
    lo = jnp.dot((a + pe_ref[0, 0]).astype(BF16), w_ref[0, 0], preferred_element_type=F32)
    hi = jnp.dot((a + pe_ref[0, 1]).astype(BF16), w_ref[0, 1], preferred_element_type=F32)
    o_ref[0, 0, 0] = lo + pltpu.roll(hi, NSA_NCP - 1, 0)


def _nsa_compress_prompt(y, cmp_pe, cmp_w, bsz):
    a = y[:, KV_OFF:KV_OFF + 2 * KVW].reshape(bsz, NSA_NCP, CMP_STRIDE, 2, NSA_KV_HEADS, NSA_HEAD_DIM)
    a = a.transpose(0, 3, 4, 1, 2, 5).reshape(bsz, 2, NSA_KV_HEADS, NSA_NCP, CMP_HALF)
    pe = cmp_pe.reshape(2, 2, 1, CMP_HALF)
    w = cmp_w.astype(BF16).reshape(2, 2, CMP_HALF, NSA_HEAD_DIM)
    return pl.pallas_call(
        _nsa_compress_kernel,
        grid=(bsz, 2, NSA_KV_HEADS),
        in_specs=[pl.BlockSpec((1, 1, 1, NSA_NCP, CMP_HALF), lambda b, c, g: (b, c, g, 0, 0)),
                  pl.BlockSpec((1, 2, 1, CMP_HALF), lambda b, c, g: (c, 0, 0, 0)),
                  pl.BlockSpec((1, 2, CMP_HALF, NSA_HEAD_DIM), lambda b, c, g: (c, 0, 0, 0))],
        out_specs=pl.BlockSpec((1, 1, 1, NSA_NCP, NSA_HEAD_DIM), lambda b, c, g: (b, c, g, 0, 0)),
        out_shape=jax.ShapeDtypeStruct((bsz, 2, NSA_KV_HEADS, NSA_NCP, NSA_HEAD_DIM), F32),
        compiler_params=pltpu.CompilerParams(dimension_semantics=('parallel', 'parallel', 'parallel')),
        name='nsa_compress',
    )(a, pe, w)


def _bias_lookup(rel_bias, idx):
    return rel_bias.T[:, jnp.asarray(idx)]


def _bucket_thresholds():
    buckets = _t5_bucket_np(np.arange(4 * MAX_DISTANCE))
    return [int(np.argmax(buckets >= k)) for k in range(N_BUCKETS)]


def _nsa_prompt_bias_tables_t(rel_bias):
    tq = np.arange(NSA_QB)[None, :]
    ts = np.arange(NSA_QB)[:, None]

    def table(idx):
        t = _bias_lookup(rel_bias, idx).reshape((NSA_KV_HEADS, NSA_REP) + idx.shape)
        t = jnp.moveaxis(t, 1, -2)
        return t.reshape(t.shape[:-2] + (NSA_REP * NSA_QB,))

    b_diag = table(_t5_bucket_np(tq - ts))
    b_prev = table(_t5_bucket_np(NSA_QB + tq - ts))
    by_bucket = jnp.repeat(rel_bias.reshape(N_BUCKETS, NSA_KV_HEADS, NSA_REP).transpose(1, 0, 2), NSA_QB, axis=2)
    return b_diag, b_prev, by_bucket


def _nsa_prompt_t_kernel(q_ref, gate_ref, cmp_ref, ks_ref, vs_ref, kw_ref, vw_ref, bdiag_ref, bprev_ref, bkt_ref,
                         covt_ref, expand_ref, o_ref, m_ref, l_ref, acc_ref, ksb, vst, kwb, vwt, bcmp_scr):
    i = pl.program_id(2)
    cols = NSA_REP * NSA_QB

    @pl.when(i == 0)
    def _():
        ksb[...] = ks_ref[...].astype(BF16)
        kwb[...] = kw_ref[...].astype(BF16)
        for c in range(NSA_L // NSA_KSTEP):
            vst[c] = vs_ref[c * NSA_KSTEP:(c + 1) * NSA_KSTEP, :].T.astype(BF16)
            vwt[c] = vw_ref[c * NSA_KSTEP:(c + 1) * NSA_KSTEP, :].T.astype(BF16)

    qblk = q_ref[...] * (NSA_HEAD_DIM ** -0.5)
    q_t = jnp.concatenate([qblk[:, r * LANE:(r + 1) * LANE].T for r in range(NSA_REP)], axis=1).astype(BF16)
    key = lax.broadcasted_iota(jnp.int32, (NSA_QB, cols), 0)
    tq = lax.broadcasted_iota(jnp.int32, (NSA_QB, cols), 1) % NSA_QB
    t_abs = i * NSA_QB + tq

    k_c = cmp_ref[0, 0, 0].astype(BF16)
    v_c_t = cmp_ref[0, 1, 0].T.astype(BF16)
    mask_c = key * CMP_STRIDE + (CMP_LEN - 1) <= t_abs
    bfar = bkt_ref[0, N_BUCKETS - 1:N_BUCKETS, :]
    per_qb = NSA_QB // CMP_STRIDE
    w0 = pl.multiple_of(jnp.maximum(i * per_qb - 2 * per_qb, 0), 8)
    tok = w0 + lax.broadcasted_iota(jnp.int32, (3 * per_qb, cols), 0)
    t_w = i * NSA_QB + lax.broadcasted_iota(jnp.int32, (3 * per_qb, cols), 1) % NSA_QB
    dist_c = t_w - (tok * CMP_STRIDE + (CMP_LEN - 1))
    bias_w = jnp.zeros((3 * per_qb, cols), F32) + bkt_ref[0, 0:1, :]
    for k, thr in enumerate(_bucket_thresholds()):
        if k > 0:
            bias_w = jnp.where(dist_c >= thr, bkt_ref[0, k:k + 1, :], bias_w)
    bcmp_scr[...] = jnp.zeros((NSA_NCP, cols), F32) + bfar
    bcmp_scr[pl.ds(w0, 3 * per_qb), :] = bias_w
    s_c = jnp.where(mask_c, jnp.dot(k_c, q_t, preferred_element_type=F32) + bcmp_scr[...], NEG)
    e_c = jnp.where(mask_c, jnp.exp(s_c - jnp.max(s_c, 0, keepdims=True)), 0.0)
    d_c = jnp.sum(e_c, 0, keepdims=True)
    p_c = (e_c / jnp.where(d_c > 0, d_c, 1.0)).astype(BF16)
    o_c = jnp.dot(v_c_t, p_c, preferred_element_type=F32)

    imp4 = jnp.dot(covt_ref[...], p_c, preferred_element_type=F32)
    imp = sum(imp4[:, r * NSA_QB:(r + 1) * NSA_QB] for r in range(NSA_REP))
    blk = lax.broadcasted_iota(jnp.int32, (LANE, NSA_QB), 0)
    t_row = i * NSA_QB + lax.broadcasted_iota(jnp.int32, (LANE, NSA_QB), 1)
    cur = t_row // SEL_LEN
    forced = (blk == 0) | (blk == cur) | (blk == cur - 1)
    score = jnp.where(forced, FORCE_SCORE, jnp.where(blk * SEL_LEN <= t_row, imp, -FORCE_SCORE))
    score = jnp.where(blk < NSA_NSEL, score, -3.0 * FORCE_SCORE)
    rank = jnp.zeros((LANE, NSA_QB), jnp.int32)
    for j in range(NSA_NSEL):
        row = score[j:j + 1, :]
        rank = rank + jnp.where((row > score) | ((row == score) & (blk > j)), 1, 0)
    sel_t = jnp.where((rank < TOP_N) & (blk < NSA_NSEL), 1.0, 0.0).astype(BF16)

    def init():
        m_ref[...] = jnp.full((1, cols), NEG, F32)
        l_ref[...] = jnp.zeros((1, cols), F32)
        acc_ref[...] = jnp.zeros((NSA_HEAD_DIM, cols), F32)

    dist0 = (lax.broadcasted_iota(jnp.int32, (NSA_KSTEP, cols), 1) % NSA_QB
             - lax.broadcasted_iota(jnp.int32, (NSA_KSTEP, cols), 0))
    per_step = NSA_KSTEP // NSA_QB

    def step(k_scr, v_scr, j, selected):
        start = pl.multiple_of(j * NSA_KSTEP, NSA_KSTEP)
        s = jnp.dot(k_scr[pl.ds(start, NSA_KSTEP), :], q_t, preferred_element_type=F32)
        bias = []
        for u in range(per_step):
            d = i - (j * per_step + u)
            bias.append(jnp.where(d == 0, bdiag_ref[0], jnp.where(d == 1, bprev_ref[0], bfar)))
        dist = dist0 + (i * NSA_QB - j * NSA_KSTEP)
        if selected:
            m1 = jnp.dot(expand_ref[j], sel_t, preferred_element_type=F32)
            off = jnp.where(dist >= 0, jnp.concatenate([(1.0 - m1) * NEG] * NSA_REP, axis=1), NEG)
        else:
            off = jnp.where((dist >= 0) & (dist < WINDOW), 0.0, NEG)
        s = s + jnp.concatenate(bias, axis=0) + off
        m_old = m_ref[...]
        m_new = jnp.maximum(m_old, jnp.max(s, 0, keepdims=True))
        p = jnp.exp(s - m_new)
        alpha = jnp.exp(m_old - m_new)
        l_ref[...] = alpha * l_ref[...] + jnp.sum(p, 0, keepdims=True)
        acc_ref[...] = alpha * acc_ref[...] + jnp.dot(v_scr[j], p.astype(BF16), preferred_element_type=F32)
        m_ref[...] = m_new

    def finish():
        l = l_ref[...]
        return acc_ref[...] / jnp.where(l > 0, l, 1.0)

    j_diag = i // per_step

    def far_step(j):
        start = pl.multiple_of(j * NSA_KSTEP, NSA_KSTEP)
        m1 = jnp.dot(expand_ref[j], sel_t, preferred_element_type=F32)
        off = (1.0 - m1) * NEG
        s = jnp.dot(ksb[pl.ds(start, NSA_KSTEP), :], q_t, preferred_element_type=F32) + bfar + jnp.concatenate([off] * NSA_REP, axis=1)
        m_old = m_ref[...]
        m_new = jnp.maximum(m_old, jnp.max(s, 0, keepdims=True))
        p = jnp.exp(s - m_new)
        alpha = jnp.exp(m_old - m_new)
        l_ref[...] = alpha * l_ref[...] + jnp.sum(p, 0, keepdims=True)
        acc_ref[...] = alpha * acc_ref[...] + jnp.dot(vst[j], p.astype(BF16), preferred_element_type=F32)
        m_ref[...] = m_new

    init()
    n_far = jnp.maximum(i - 1, 0) // per_step

    def far_body(j, carry):
        far_step(j)
        return carry

    def sel_body(j, carry):
        step(ksb, vst, j, True)
        return carry

    lax.fori_loop(0, n_far, far_body, 0)
    lax.fori_loop(n_far, j_diag + 1, sel_body, 0)
    o_s = finish()

    init()
    step(kwb, vwt, j_diag, False)

    @pl.when(j_diag >= 1)
    def _():
        step(kwb, vwt, j_diag - 1, False)

    o_w = finish()

    g_t = jax.nn.sigmoid(gate_ref[...]).T
    for r in range(NSA_REP):
        cs = slice(r * NSA_QB, (r + 1) * NSA_QB)
        o_r = g_t[3 * r:3 * r + 1, :] * o_c[:, cs] + g_t[3 * r + 1:3 * r + 2, :] * o_s[:, cs] + g_t[3 * r + 2:3 * r + 3, :] * o_w[:, cs]
        o_ref[:, r * LANE:(r + 1) * LANE] = o_r.T.astype(o_ref.dtype)


def _nsa_prompt_t(y, k_v_cmp, tables, bsz):
    b_diag, b_prev, by_bucket = tables
    cols = NSA_REP * NSA_QB
    nstep = NSA_L // NSA_KSTEP
    assert WINDOW <= NSA_KSTEP and NSA_KSTEP % NSA_QB == 0
    c_lo = np.arange(NSA_NCP) * CMP_STRIDE
    s_lo = np.arange(LANE) * SEL_LEN
    cov_t = ((c_lo[None] < s_lo[:, None] + SEL_LEN) & (c_lo[None] + CMP_LEN > s_lo[:, None])
             & (np.arange(NSA_NCP)[None] < NSA_NCP - 1) & (np.arange(LANE)[:, None] < NSA_NSEL))
    key_blk = (np.arange(NSA_L) // SEL_LEN).reshape(nstep, NSA_KSTEP, 1)
    expand = (np.arange(LANE)[None, None, :] == key_blk)
    kv_blk = KV_OFF // LANE
    grp = NSA_KV_HEADS

    def col(slot):
        return pl.BlockSpec((NSA_L, NSA_HEAD_DIM), lambda b, g, i: (b, kv_blk + slot * grp + g))

    tab = pl.BlockSpec((1, NSA_QB, cols), lambda b, g, i: (g, 0, 0))
    return pl.pallas_call(
        _nsa_prompt_t_kernel,
        grid=(bsz, NSA_KV_HEADS, NSA_NQB),
        in_specs=[pl.BlockSpec((NSA_QB, NSA_REP * LANE), lambda b, g, i: (b * NSA_NQB + i, QN_OFF // (NSA_REP * LANE) + g)),
                  pl.BlockSpec((NSA_QB, LANE), lambda b, g, i: (b * NSA_NQB + i, GATE_OFF // LANE + g)),
                  pl.BlockSpec((1, 2, 1, NSA_NCP, NSA_HEAD_DIM), lambda b, g, i: (b, 0, g, 0, 0)),
                  col(2), col(3), col(4), col(5), tab, tab,
                  pl.BlockSpec((1, N_BUCKETS, cols), lambda b, g, i: (g, 0, 0)),
                  pl.BlockSpec((LANE, NSA_NCP), lambda b, g, i: (0, 0)),
                  pl.BlockSpec((nstep, NSA_KSTEP, LANE), lambda b, g, i: (0, 0, 0))],
        out_specs=pl.BlockSpec((NSA_QB, NSA_REP * LANE), lambda b, g, i: (b * NSA_NQB + i, g)),
        out_shape=jax.ShapeDtypeStruct((bsz * NSA_L, D_NSA), BF16),
        scratch_shapes=[pltpu.VMEM((1, cols), F32), pltpu.VMEM((1, cols), F32), pltpu.VMEM((NSA_HEAD_DIM, cols), F32),
                        pltpu.VMEM((NSA_L, NSA_HEAD_DIM), BF16), pltpu.VMEM((nstep, NSA_HEAD_DIM, NSA_KSTEP), BF16),
                        pltpu.VMEM((NSA_L, NSA_HEAD_DIM), BF16), pltpu.VMEM((nstep, NSA_HEAD_DIM, NSA_KSTEP), BF16),
                        pltpu.VMEM((NSA_NCP, cols), F32)],
        compiler_params=pltpu.CompilerParams(dimension_semantics=('parallel', 'parallel', 'arbitrary'),
                                             vmem_limit_bytes=VMEM_LIMIT_BYTES),
        name='nsa_prompt',
    )(y, y, k_v_cmp, y, y, y, y, b_diag, b_prev, by_bucket,
      jnp.asarray(cov_t, BF16), jnp.asarray(expand, BF16))


NS_SELW = 3 * LANE
NS_CHUNK = 2048


def _nsa_sample_tables(rel_bias, past_len, lq, win_len):
    tq = (past_len + np.arange(lq))[:, None]

    def table(idx):
        t = _bias_lookup(rel_bias, idx)
        return t.reshape(NSA_KV_HEADS, NSA_REP * lq, idx.shape[-1])

    ncp = past_len // CMP_STRIDE
    b_cmp = table(_t5_bucket_np(tq - (np.arange(ncp)[None] * CMP_STRIDE + CMP_LEN - 1)))
    b_last = table(_t5_bucket_np(tq - (past_len - LANE + np.arange(LANE)[None])))
    b_new = table(_t5_bucket_np(tq - (past_len + np.arange(LANE)[None])))
    b_far = table(np.full((lq, LANE), N_BUCKETS - 1, np.int32))
    wpos = np.concatenate([past_len - win_len + np.arange(win_len), past_len + np.arange(LANE)])
    b_win = table(_t5_bucket_np(tq - wpos[None]))
    return b_cmp, b_last, b_new, b_far, b_win


def _nsa_sample_kernel(pt_ref, q_ref, kvn_ref, gate0_ref, gate1_ref, win_ref, cache_ref, w_ref, pe_ref,
                       bcmp_ref, blast_ref, bnew_ref, bfar_ref, bwin_ref, cov_ref, expand_ref, o_ref,
                       buf, cmp_scr, s_scr, sem, *, n_pages, lq, past_len, win_len, layer):
    b = pl.program_id(0)
    rows = NSA_REP * lq
    ncp = past_len // CMP_STRIDE
    n_cmp = (past_len + lq - CMP_LEN) // CMP_STRIDE + 1
    n_sel = -(-(past_len + lq) // SEL_LEN)
    n_chunks = past_len // NS_CHUNK
    ncol = 2 * NSA_KV_HEADS

    def page_copy(j, pair, col):
        slot, g = pair * 2 + col // NSA_KV_HEADS, col % NSA_KV_HEADS
        return pltpu.make_async_copy(cache_ref.at[layer, pt_ref[b, j], :, slot, g, :],
                                     buf.at[col, pl.ds(j * PAGE_SIZE, PAGE_SIZE), :], sem)

    def gather_start(pair):
        def body(j, carry):
            for col in range(ncol):
                page_copy(j, pair, col).start()
            return carry
        lax.fori_loop(0, n_pages, body, 0)

    def gather_wait(pair):
        def body(j, carry):
            for col in range(ncol):
                page_copy(j, pair, col).wait()
            return carry
        lax.fori_loop(0, n_pages, body, 0)

    gather_start(0)
    gather_wait(0)
    for c in range(2):
        for g in range(NSA_KV_HEADS):
            lo = jnp.zeros((ncp, NSA_HEAD_DIM), F32)
            hi = jnp.zeros((ncp, NSA_HEAD_DIM), F32)
            for l in range(CMP_STRIDE):
                x = buf[c * NSA_KV_HEADS + g, pl.ds(l, ncp, stride=CMP_STRIDE), :]
                lo = lo + jnp.dot((x + pe_ref[c, l:l + 1, :]).astype(BF16), w_ref[c, l], preferred_element_type=F32)
                hi = hi + jnp.dot((x + pe_ref[c, CMP_STRIDE + l:CMP_STRIDE + l + 1, :]).astype(BF16),
                                  w_ref[c, CMP_STRIDE + l], preferred_element_type=F32)
            cmp_scr[c, g] = lo + pltpu.roll(hi, ncp - 1, 0)
    gather_start(1)

    qall = q_ref[...] * (NSA_HEAD_DIM ** -0.5)
    tq = lax.broadcasted_iota(jnp.int32, (rows, 1), 0) % lq
    t_abs = past_len + tq
    o_c, sel, q_g = [], [], []
    for g in range(NSA_KV_HEADS):
        q = jnp.concatenate([qall[:, (g * NSA_REP + r) * LANE:(g * NSA_REP + r + 1) * LANE] for r in range(NSA_REP)],
                            axis=0).astype(BF16)
        q_g.append(q)
        n_idx = lax.broadcasted_iota(jnp.int32, (rows, ncp), 1)
        mask_c = (n_idx < n_cmp) & (n_idx * CMP_STRIDE + (CMP_LEN - 1) <= t_abs)
        s_c = jnp.where(mask_c, _dot_nt(q, cmp_scr[0, g].astype(BF16)) + bcmp_ref[g], NEG)
        e_c = jnp.where(mask_c, jnp.exp(s_c - jnp.max(s_c, -1, keepdims=True)), 0.0)
        d_c = jnp.sum(e_c, -1, keepdims=True)
        p_c = (e_c / jnp.where(d_c > 0, d_c, 1.0)).astype(BF16)
        o_c.append(jnp.dot(p_c, cmp_scr[1, g].astype(BF16), preferred_element_type=F32))
        imp4 = jnp.dot(p_c, cov_ref[...], preferred_element_type=F32)
        imp = sum(imp4[r * lq:(r + 1) * lq] for r in range(NSA_REP))
        blk = lax.broadcasted_iota(jnp.int32, (lq, NS_SELW), 1)
        t_q = past_len + lax.broadcasted_iota(jnp.int32, (lq, NS_SELW), 0)
        cur = t_q // SEL_LEN
        forced = (blk == 0) | (blk == cur) | (blk == cur - 1)
        score = jnp.where(forced, FORCE_SCORE, jnp.where(blk * SEL_LEN <= t_q, imp, -FORCE_SCORE))
        score = jnp.where(blk < n_sel, score, -3.0 * FORCE_SCORE)
        rank = jnp.zeros((lq, NS_SELW), jnp.int32)
        for j in range(n_sel):
            col = score[:, j:j + 1]
            rank = rank + jnp.where((col > score) | ((col == score) & (blk > j)), 1, 0)
        sel_q = jnp.where((rank < min(TOP_N, n_sel)) & (blk < n_sel), 1.0, 0.0)
        sel.append(jnp.concatenate([sel_q] * NSA_REP, axis=0))

    gather_wait(1)
    kvn = _pad_rows(kvn_ref[...], LANE)
    lane = lax.broadcasted_iota(jnp.int32, (rows, LANE), 1)
    blocks_per_chunk = NS_CHUNK // SEL_LEN
    gates = [jax.nn.sigmoid(gate0_ref[...]), jax.nn.sigmoid(gate1_ref[...])]
    for g in range(NSA_KV_HEADS):
        q = q_g[g]
        bfar = bfar_ref[g][:, 0:1]

        def new_rows(slot):
            return kvn[:, (slot * NSA_KV_HEADS + g) * NSA_HEAD_DIM:(slot * NSA_KV_HEADS + g + 1) * NSA_HEAD_DIM].astype(BF16)

        sel_b = sel[g].astype(BF16)
        for c in range(n_chunks):
            k = buf[g, c * NS_CHUNK:(c + 1) * NS_CHUNK, :].astype(BF16)
            s = _dot_nt(q, k) + bfar
            if c == n_chunks - 1:
                fix = blast_ref[g] - bfar_ref[g]
                s = jnp.concatenate([s[:, :NS_CHUNK - LANE], s[:, NS_CHUNK - LANE:] + fix], axis=1)
            m_c = jnp.dot(sel_b[:, c * blocks_per_chunk:(c + 1) * blocks_per_chunk], expand_ref[...],
                          preferred_element_type=F32)
            s_scr[:, c * NS_CHUNK:(c + 1) * NS_CHUNK] = jnp.where(m_c > 0.5, s, NEG)
        s_new = _dot_nt(q, new_rows(2)) + bnew_ref[g]
        mask_new = (lane <= tq) & (lane < lq) & (sel[g][:, n_sel - 1:n_sel] > 0.5)
        s_scr[:, past_len:past_len + LANE] = jnp.where(mask_new, s_new, NEG)
        m = jnp.max(s_scr[...], -1, keepdims=True)
        acc = jnp.zeros((rows, NSA_HEAD_DIM), F32)
        den = jnp.zeros((rows, 1), F32)
        for c in range(n_chunks):
            sc = s_scr[:, c * NS_CHUNK:(c + 1) * NS_CHUNK]
            e = jnp.where(sc > 0.5 * NEG, jnp.exp(sc - m), 0.0)
            den = den + jnp.sum(e, -1, keepdims=True)
            v = buf[NSA_KV_HEADS + g, c * NS_CHUNK:(c + 1) * NS_CHUNK, :]
            acc = acc + jnp.dot(e.astype(BF16), v.astype(BF16), preferred_element_type=F32)
        sc = s_scr[:, past_len:past_len + LANE]
        e = jnp.where(sc > 0.5 * NEG, jnp.exp(sc - m), 0.0)
        den = den + jnp.sum(e, -1, keepdims=True)
        acc = acc + jnp.dot(e.astype(BF16), new_rows(3), preferred_element_type=F32)
        o_s = acc / jnp.where(den > 0, den, 1.0)

        kw = jnp.concatenate([win_ref[0, :, g * NSA_HEAD_DIM:(g + 1) * NSA_HEAD_DIM].astype(BF16), new_rows(4)], axis=0)
        vw = jnp.concatenate([win_ref[0, :, (NSA_KV_HEADS + g) * NSA_HEAD_DIM:(NSA_KV_HEADS + g + 1) * NSA_HEAD_DIM].astype(BF16),
                              new_rows(5)], axis=0)
        wl = lax.broadcasted_iota(jnp.int32, (rows, win_len + LANE), 1)
        dist = jnp.where(wl < win_len, win_len + tq - wl, tq - (wl - win_len))
        mask_w = (dist >= 0) & (dist < WINDOW) & (wl < win_len + lq)
        s_w = jnp.where(mask_w, _dot_nt(q, kw) + bwin_ref[g], NEG)
        e_w = jnp.where(mask_w, jnp.exp(s_w - jnp.max(s_w, -1, keepdims=True)), 0.0)
        d_w = jnp.sum(e_w, -1, keepdims=True)
        o_w = jnp.dot((e_w / jnp.where(d_w > 0, d_w, 1.0)).astype(BF16), vw, preferred_element_type=F32)

        for r in range(NSA_REP):
            rs = slice(r * lq, (r + 1) * lq)
            gt = gates[g]
            h = g * NSA_REP + r
            o_ref[:, h * LANE:(h + 1) * LANE] = (gt[:, 3 * r:3 * r + 1] * o_c[g][rs] + gt[:, 3 * r + 1:3 * r + 2] * o_s[rs]
                                                 + gt[:, 3 * r + 2:3 * r + 3] * o_w[rs]).astype(o_ref.dtype)


def _nsa_sample(y, cache_kv, layer, page_table, cache_win, cmp_pe, cmp_w, rel_bias, bsz):
    lq = y.shape[0] // bsz
    n_pages = page_table.shape[1]
    past_len = n_pages * PAGE_SIZE
    win_len = cache_win.shape[1]
    rows = NSA_REP * lq
    ncp = past_len // CMP_STRIDE
    n_cmp = (past_len + lq - CMP_LEN) // CMP_STRIDE + 1
    n_sel = -(-(past_len + lq) // SEL_LEN)
    assert lq % 8 == 0 and lq <= CMP_STRIDE and n_cmp <= ncp - 1 + lq // CMP_STRIDE and past_len % NS_CHUNK == 0
    assert past_len % SEL_LEN == 0 and n_sel <= NS_SELW and lq % Q_BLOCK != 0 and win_len == WINDOW
    tables = _nsa_sample_tables(rel_bias, past_len, lq, win_len)
    c_lo = np.arange(ncp) * CMP_STRIDE
    s_lo = np.arange(NS_SELW) * SEL_LEN
    cover = ((c_lo[:, None] < s_lo[None] + SEL_LEN) & (c_lo[:, None] + CMP_LEN > s_lo[None])
             & (np.arange(ncp)[:, None] < n_cmp) & (np.arange(NS_SELW)[None] < n_sel))
    expand = np.arange(NS_CHUNK // SEL_LEN)[:, None] == (np.arange(NS_CHUNK) // SEL_LEN)[None]
    assert cache_kv.shape[2:] == (PAGE_SIZE, 4, NSA_KV_HEADS, NSA_HEAD_DIM)
    win2 =cache_win.reshape(bsz, win_len, 2 * KVW)

    def full(shape):
        return pl.BlockSpec(shape, lambda b, pt: (0,) * len(shape))

    grid_spec = pltpu.PrefetchScalarGridSpec(
        num_scalar_prefetch=1, grid=(bsz,),
        in_specs=[pl.BlockSpec((lq, D_NSA), lambda b, pt: (b, QN_OFF // D_NSA)),
                  pl.BlockSpec((lq, 6 * KVW), lambda b, pt: (b, KV_OFF // (6 * KVW))),
                  pl.BlockSpec((lq, LANE), lambda b, pt: (b, GATE_OFF // LANE)),
                  pl.BlockSpec((lq, LANE), lambda b, pt: (b, GATE_OFF // LANE + 1)),
                  pl.BlockSpec((1, win_len, 2 * KVW), lambda b, pt: (b, 0, 0)),
                  pl.BlockSpec(memory_space=pl.ANY),
                  full((2, CMP_LEN, NSA_HEAD_DIM, NSA_HEAD_DIM)), full((2, CMP_LEN, NSA_HEAD_DIM)),
                  full((NSA_KV_HEADS, rows, ncp)), full((NSA_KV_HEADS, rows, LANE)), full((NSA_KV_HEADS, rows, LANE)),
                  full((NSA_KV_HEADS, rows, LANE)), full((NSA_KV_HEADS, rows, win_len + LANE)),
                  full((ncp, NS_SELW)), full((NS_CHUNK // SEL_LEN, NS_CHUNK))],
        out_specs=pl.BlockSpec((lq, D_NSA), lambda b, pt: (b, 0)),
        scratch_shapes=[pltpu.VMEM((2 * NSA_KV_HEADS, past_len, NSA_HEAD_DIM), F32),
                        pltpu.VMEM((2, NSA_KV_HEADS, ncp, NSA_HEAD_DIM), F32),
                        pltpu.VMEM((rows, past_len + LANE), F32), pltpu.SemaphoreType.DMA(())])
    assert QN_OFF % D_NSA == 0 and KV_OFF % (6 * KVW) == 0
    return pl.pallas_call(
        functools.partial(_nsa_sample_kernel, n_pages=n_pages, lq=lq, past_len=past_len, win_len=win_len,
                          layer=layer),
        grid_spec=grid_spec,
        out_shape=jax.ShapeDtypeStruct((bsz * lq, D_NSA), BF16),
        compiler_params=pltpu.CompilerParams(dimension_semantics=('arbitrary',), vmem_limit_bytes=VMEM_LIMIT_BYTES),
        name='nsa_sample',
    )(page_table, y, y, y, y, win2, cache_kv, cmp_w.astype(BF16), cmp_pe, *tables,
      jnp.asarray(cover, BF16), jnp.asarray(expand, BF16))


def _mm_kernel(x_ref, w_ref, o_ref, *scratch, nk, act):
    def finish(acc):
        if act == 'sqrelu':
            acc = jnp.square(jnp.maximum(acc, 0.0))
        o_ref[...] = acc.astype(o_ref.dtype)

    if nk == 1:
        finish(jnp.dot(x_ref[...], w_ref[...], preferred_element_type=F32))
        return
    acc_ref, = scratch
    k = pl.program_id(2)
    part = jnp.dot(x_ref[...], w_ref[...], preferred_element_type=F32)

    @pl.when(k == 0)
    def _():
        acc_ref[...] = part

    @pl.when(k > 0)
    def _():
        acc_ref[...] += part

    @pl.when(k == nk - 1)
    def _():
        finish(acc_ref[...])


def _pick(n, pref):
    for t in pref:
        if n % t == 0:
            return t
    return n


def _matmul(x, w, out_dtype=F32, act=None, name='matmul'):
    m, k = x.shape
    n = w.shape[1]
    tm = _pick(m, (1024, 512, 256, 128, 64))
    tn = _pick(n, (2048, 512, 256, 128) if m <= 64 else (512, 256, 128))
    tk = _pick(k, (4096, 2048, 1024, 512))
    nk = k // tk
    scratch = [pltpu.VMEM((tm, tn), F32)] if nk > 1 else []
    return pl.pallas_call(
        functools.partial(_mm_kernel, nk=nk, act=act),
        grid=(m // tm, n // tn, nk),
        in_specs=[pl.BlockSpec((tm, tk), lambda i, j, kk: (i, kk)),
                  pl.BlockSpec((tk, tn), lambda i, j, kk: (kk, j))],
        out_specs=pl.BlockSpec((tm, tn), lambda i, j, kk: (i, j)),
        out_shape=jax.ShapeDtypeStruct((m, n), out_dtype),
        scratch_shapes=scratch,
        compiler_params=pltpu.CompilerParams(
            dimension_semantics=('parallel', 'parallel', 'arbitrary'),
            vmem_limit_bytes=VMEM_LIMIT_BYTES),
        name=name,
    )(x, w)


def _mm_ws_kernel(x_ref, w_ref, o_ref, wb_ref, *, act):
    @pl.when(pl.program_id(1) == 0)
    def _():
        wb_ref[...] = w_ref[0].astype(BF16)

    acc = jnp.dot(x_ref[...], wb_ref[...], preferred_element_type=F32)
    if act == 'sqrelu':
        acc = jnp.square(jnp.maximum(acc, 0.0))
    o_ref[...] = acc.astype(o_ref.dtype)


def _matmul_cast(x, w, layer, out_dtype=F32, act=None, name='matmul'):
    m, k = x.shape
    n = w.shape[2]
    tm = _pick(m, (1024,))
    tn = _pick(n, (512,))
    return pl.pallas_call(
        functools.partial(_mm_ws_kernel, act=act),
        grid=(n // tn, m // tm),
        in_specs=[pl.BlockSpec((tm, k), lambda j, i: (i, 0)),
                  pl.BlockSpec((1, k, tn), lambda j, i: (layer, 0, j))],
        out_specs=[pl.BlockSpec((tm, tn), lambda j, i: (i, j)), pl.BlockSpec((k, tn), lambda j, i: (0, j))],
        out_shape=[jax.ShapeDtypeStruct((m, n), out_dtype), jax.ShapeDtypeStruct((k, n), BF16)],
        compiler_params=pltpu.CompilerParams(dimension_semantics=('parallel', 'arbitrary'),
                                             vmem_limit_bytes=VMEM_LIMIT_BYTES),
        name=name,
    )(x, w)


CAST_BLOCK_BYTES = 8 * 1024 * 1024


def _cast_kernel(w_ref, o_ref):
    o_ref[...] = w_ref[0].astype(o_ref.dtype)


def _to_bf16(w, layer):
    _, k, n = w.shape
    tk = CAST_BLOCK_BYTES // (4 * n)
    assert tk % 16 == 0 and k % tk == 0
    return pl.pallas_call(
        _cast_kernel, grid=(k // tk,),
        in_specs=[pl.BlockSpec((1, tk, n), lambda i: (layer, i, 0))], out_specs=pl.BlockSpec((tk, n), lambda i: (i, 0)),
        out_shape=jax.ShapeDtypeStruct((k, n), BF16),
        compiler_params=pltpu.CompilerParams(dimension_semantics=('parallel',), vmem_limit_bytes=VMEM_LIMIT_BYTES),
        name='weight_cast',
    )(w)


def _out_proj_kernel(a_ref, b_ref, c_ref, wa_ref, wb_ref, wc_ref, o_ref):
    o_ref[...] = (jnp.dot(a_ref[...], wa_ref[...], preferred_element_type=F32)
                  + jnp.dot(b_ref[...], wb_ref[...], preferred_element_type=F32)
                  + jnp.dot(c_ref[...], wc_ref[...], preferred_element_type=F32))


def _out_proj(y_ssd, y_nsa, y_gla, w):
    m = y_ssd.shape[0]
    n = w.shape[1]
    tm = _pick(m, (1024, 64))
    tn = _pick(n, (512,))
    assert D_SSD % D_NSA == 0 and D_NSA == D_GLA

    def act(width):
        return pl.BlockSpec((tm, width), lambda i, j: (i, 0))

    return pl.pallas_call(
        _out_proj_kernel, grid=(m // tm, n // tn),
        in_specs=[act(D_SSD), act(D_NSA), act(D_GLA),
                  pl.BlockSpec((D_SSD, tn), lambda i, j: (0, j)),
                  pl.BlockSpec((D_NSA, tn), lambda i, j: (D_SSD // D_NSA, j)),
                  pl.BlockSpec((D_GLA, tn), lambda i, j: (D_SSD // D_NSA + 1, j))],
        out_specs=pl.BlockSpec((tm, tn), lambda i, j: (i, j)),
        out_shape=jax.ShapeDtypeStruct((m, n), F32),
        compiler_params=pltpu.CompilerParams(dimension_semantics=('parallel', 'parallel'),
                                             vmem_limit_bytes=VMEM_LIMIT_BYTES),
        name='out_proj',
    )(y_ssd, y_nsa, y_gla, w, w, w)


def _out_proj_cast_kernel(a_ref, b_ref, c_ref, w_ref, o_ref, wb_ref):
    @pl.when(pl.program_id(1) == 0)
    def _():
        wb_ref[...] = w_ref[0].astype(BF16)

    o_ref[...] = (jnp.dot(a_ref[...], wb_ref[0:D_SSD, :], preferred_element_type=F32)
                  + jnp.dot(b_ref[...], wb_ref[D_SSD:D_SSD + D_NSA, :], preferred_element_type=F32)
                  + jnp.dot(c_ref[...], wb_ref[D_SSD + D_NSA:D_MIX, :], preferred_element_type=F32))


def _out_proj_cast(y_ssd, y_nsa, y_gla, w, layer):
    m = y_ssd.shape[0]
    n = w.shape[2]
    tm = _pick(m, (1024,))
    tn = _pick(n, (512,))

    def act(width):
        return pl.BlockSpec((tm, width), lambda j, i: (i, 0))

    return pl.pallas_call(
        _out_proj_cast_kernel, grid=(n // tn, m // tm),
        in_specs=[act(D_SSD), act(D_NSA), act(D_GLA), pl.BlockSpec((1, D_MIX, tn), lambda j, i: (layer, 0, j))],
        out_specs=[pl.BlockSpec((tm, tn), lambda j, i: (i, j)), pl.BlockSpec((D_MIX, tn), lambda j, i: (0, j))],
        out_shape=[jax.ShapeDtypeStruct((m, n), F32), jax.ShapeDtypeStruct((D_MIX, n), BF16)],
        compiler_params=pltpu.CompilerParams(dimension_semantics=('parallel', 'arbitrary'),
                                             vmem_limit_bytes=VMEM_LIMIT_BYTES),
        name='out_proj',
    )(y_ssd, y_nsa, y_gla, w)


def _ln_kernel(x_ref, h_ref, g_ref, b_ref, o_ref, ob_ref):
    v = DN_ALPHA * x_ref[...] + h_ref[...]
    d = v - jnp.mean(v, -1, keepdims=True)
    y = d * lax.rsqrt(jnp.mean(d * d, -1, keepdims=True) + LN_EPS) * g_ref[...] + b_ref[...]
    o_ref[...] = y
    ob_ref[...] = y.astype(BF16)


def _add_layernorm(x, h, g, b):
    m, d = x.shape
    tm = _pick(m, (256, 64))
    row = pl.BlockSpec((tm, d), lambda i: (i, 0))
    vec = pl.BlockSpec((1, d), lambda i: (0, 0))
    return pl.pallas_call(
        _ln_kernel, grid=(m // tm,), in_specs=[row, row, vec, vec], out_specs=[row, row],
        out_shape=[jax.ShapeDtypeStruct((m, d), F32), jax.ShapeDtypeStruct((m, d), BF16)],
        compiler_params=pltpu.CompilerParams(dimension_semantics=('parallel',), vmem_limit_bytes=VMEM_LIMIT_BYTES),
        name='add_layernorm',
    )(x, h, g.reshape(1, d), b.reshape(1, d))


def _xattn_kernel(q_ref, k_ref, v_ref, o_ref):
    q = (q_ref[...] * (X_HEAD_DIM ** -0.5)).astype(BF16)
    s = _dot_nt(q, k_ref[0].astype(BF16))
    e = jnp.exp(s - jnp.max(s, -1, keepdims=True))
    p = e / jnp.sum(e, -1, keepdims=True)
    o_ref[...] = jnp.dot(p.astype(BF16), v_ref[0].astype(BF16), preferred_element_type=F32).astype(o_ref.dtype)


def _cross_attention(q, mem_kv, bsz):
    m = q.shape[0]
    l = m // bsz
    tq = _pick(l, (512, 8))
    nq = l // tq
    return pl.pallas_call(
        _xattn_kernel, grid=(bsz, X_HEADS, nq),
        in_specs=[pl.BlockSpec((tq, X_HEAD_DIM), lambda b, h, i: (b * nq + i, h)),
                  pl.BlockSpec((1, N_MEM, X_HEAD_DIM), lambda b, h, i: (b, 0, h)),
                  pl.BlockSpec((1, N_MEM, X_HEAD_DIM), lambda b, h, i: (b, 0, X_HEADS + h))],
        out_specs=pl.BlockSpec((tq, X_HEAD_DIM), lambda b, h, i: (b * nq + i, h)),
        out_shape=jax.ShapeDtypeStruct((m, D_MODEL), BF16),
        compiler_params=pltpu.CompilerParams(dimension_semantics=('parallel', 'parallel', 'parallel'),
                                             vmem_limit_bytes=VMEM_LIMIT_BYTES),
        name='cross_attention',
    )(q, mem_kv, mem_kv)


def _softplus(x):
    return jnp.maximum(x, 0.0) + jnp.log1p(jnp.exp(-jnp.abs(x)))


def _silu(x):
    return x * jax.nn.sigmoid(x)


def _pad_rows(x, rows):
    if x.shape[0] == rows:
        return x
    return jnp.concatenate([x, jnp.zeros((rows - x.shape[0],) + x.shape[1:], x.dtype)], axis=0)


def _cumsum_rows(x, seg):
    r = lax.broadcasted_iota(jnp.int32, x.shape, 0) % seg
    k = 1
    while k < seg:
        x = x + jnp.where(r >= k, pltpu.roll(x, k, 0), 0.0)
        k *= 2
    return x


def _segment_last(x, seg):
    n = x.shape[0]
    r = lax.broadcasted_iota(jnp.int32, x.shape, 0) % seg
    k = 1
    while k < seg:
        x = jnp.where(r < seg - k, pltpu.roll(x, n - k, 0), x)
        k *= 2
    return x


GLA_BLOCK = 128


def _gla_kernel(q_ref, k_ref, v_ref, g_ref, a_ref, w2_ref, gb_ref, nw_ref, s0_ref, o_ref, s_out_ref,
                s_scr, q_scr, k_scr, v_scr, bc_scr, o_scr, *, rows_in, ch):
    c = pl.program_id(1)
    nsub = -(-rows_in // ch)
    hk = GLA_DK

    @pl.when(c == 0)
    def _():
        s_scr[...] = s0_ref[0]

    valid = lax.broadcasted_iota(jnp.int32, (GLA_BLOCK, 1), 0) < rows_in
    a = _pad_rows(a_ref[...], GLA_BLOCK).astype(BF16)
    logf = -_softplus(-(jnp.dot(a, w2_ref[...], preferred_element_type=F32) + gb_ref[...])) / GLA_TAU
    logf = jnp.where(valid, logf, 0.0)
    bc = _cumsum_rows(logf, ch)
    b_last = _segment_last(bc, ch)
    q = _pad_rows(q_ref[...], GLA_BLOCK) * (GLA_DK ** -0.5)
    k = _pad_rows(k_ref[...], GLA_BLOCK)
    v = _pad_rows(v_ref[...], GLA_BLOCK)
    q_scr[...] = q
    k_scr[...] = k
    v_scr[...] = v
    bc_scr[...] = bc

    tt = lax.broadcasted_iota(jnp.int32, (ch, 1), 0)

    def intra(j, carry):
        r0 = pl.multiple_of(j * ch, ch)
        qj = q_scr[pl.ds(r0, ch), :]
        bj = bc_scr[pl.ds(r0, ch), :]
        acc = [jnp.zeros((ch, GLA_DV), F32) for _ in range(GLA_HEADS)]
        for s in range(ch):
            ks = k_scr[pl.ds(r0 + s, 1), :]
            bs = bc_scr[pl.ds(r0 + s, 1), :]
            vs = v_scr[pl.ds(r0 + s, 1), :]
            w = jnp.where(tt >= s, qj * ks * jnp.exp(jnp.minimum(bj - bs, 0.0)), 0.0)
            for h in range(GLA_HEADS):
                att = jnp.sum(w[:, h * hk:(h + 1) * hk], -1, keepdims=True)
                acc[h] = acc[h] + att * vs[:, h * GLA_DV:(h + 1) * GLA_DV]
        o_scr[pl.ds(r0, ch), :] = jnp.concatenate(acc, axis=1)
        return carry

    lax.fori_loop(0, nsub, intra, 0)

    qe = q * jnp.exp(bc)
    kst = k * jnp.exp(b_last - bc)
    dec = jnp.exp(b_last)
    o_intra = _pad_rows(o_scr[0:nsub * ch, :], GLA_BLOCK)
    chunk_of_col = lax.broadcasted_iota(jnp.int32, (hk, GLA_BLOCK), 1) // ch
    g = _pad_rows(g_ref[...], GLA_BLOCK)
    outs = []
    for h in range(GLA_HEADS):
        kst_t = kst[:, h * hk:(h + 1) * hk].T
        dec_t = dec[:, h * hk:(h + 1) * hk].T
        lhs = jnp.concatenate([jnp.where(chunk_of_col == j, kst_t, 0.0) for j in range(nsub)], axis=0).astype(BF16)
        kv = jnp.dot(lhs, v[:, h * GLA_DV:(h + 1) * GLA_DV].astype(BF16), preferred_element_type=F32)
        s_h = s_scr[h]
        s_before = []
        for j in range(nsub):
            s_before.append(s_h.astype(BF16))
            s_h = s_h * dec_t[:, j * ch:j * ch + 1] + kv[j * hk:(j + 1) * hk]
        s_scr[h] = s_h
        big = jnp.dot(qe[:, h * hk:(h + 1) * hk].astype(BF16), jnp.concatenate(s_before, axis=1), preferred_element_type=F32)
        o_inter = _pad_rows(jnp.concatenate([big[j * ch:(j + 1) * ch, j * GLA_DV:(j + 1) * GLA_DV] for j in range(nsub)], axis=0),
                            GLA_BLOCK)
        o_h = o_intra[:, h * GLA_DV:(h + 1) * GLA_DV] + o_inter
        o_h = o_h * lax.rsqrt(jnp.mean(o_h * o_h, -1, keepdims=True) + NORM_EPS) * nw_ref[...]
        outs.append(o_h)
    o = jnp.concatenate(outs, axis=1) * _silu(g)
    o_ref[...] = o[:rows_in].astype(o_ref.dtype)

    @pl.when(c == pl.num_programs(1) - 1)
    def _():
        s_out_ref[0] = s_scr[...]


def _gla(y, s0, gate_w2, gate_b, norm_w, bsz):
    m = y.shape[0]
    l = m // bsz
    rows_in = min(GLA_BLOCK, l)
    ch = GLA_CHUNK
    nblk = l // rows_in
    hdk = GLA_HEADS * GLA_DK
    w2 = jnp.zeros((LANE, hdk), BF16).at[ALR_OFF - SMALL_OFF:ALR_OFF - SMALL_OFF + GLA_RANK].set(gate_w2.astype(BF16))

    def rows(width, off):
        assert off % width == 0
        return pl.BlockSpec((rows_in, width), lambda b, c: (b * nblk + c, off // width))

    def const(shape):
        return pl.BlockSpec(shape, lambda b, c: (0,) * len(shape))

    state = pl.BlockSpec((1, GLA_HEADS, GLA_DK, GLA_DV), lambda b, c: (b, 0, 0, 0))
    return pl.pallas_call(
        functools.partial(_gla_kernel, rows_in=rows_in, ch=ch),
        grid=(bsz, nblk),
        in_specs=[rows(hdk, QG_OFF), rows(hdk, KG_OFF), rows(D_GLA, VG_OFF), rows(D_GLA, GG_OFF), rows(LANE, SMALL_OFF),
                  const((LANE, hdk)), const((1, hdk)), const((1, GLA_DV)), state],
        out_specs=[pl.BlockSpec((rows_in, D_GLA), lambda b, c: (b * nblk + c, 0)), state],
        out_shape=[jax.ShapeDtypeStruct((m, D_GLA), BF16), jax.ShapeDtypeStruct((bsz, GLA_HEADS, GLA_DK, GLA_DV), F32)],
        scratch_shapes=[pltpu.VMEM((GLA_HEADS, GLA_DK, GLA_DV), F32), pltpu.VMEM((GLA_BLOCK, hdk), F32),
                        pltpu.VMEM((GLA_BLOCK, hdk), F32), pltpu.VMEM((GLA_BLOCK, D_GLA), F32),
                        pltpu.VMEM((GLA_BLOCK, hdk), F32), pltpu.VMEM((GLA_BLOCK, D_GLA), F32)],
        compiler_params=pltpu.CompilerParams(dimension_semantics=('parallel', 'arbitrary'),
                                             vmem_limit_bytes=VMEM_LIMIT_BYTES),
        name='gla',
    )(y, y, y, y, y, w2, gate_b.reshape(1, hdk), norm_w.reshape(1, GLA_DV), s0)


SSD_R = SSD_HEADS // SSD_GROUPS
SSD_GW = SSD_R * SSD_HEAD_DIM
SSD_CW = SSD_GW + 2 * SSD_STATE
SSD_TAIL = 8


def _ssd_kernel(xs_ref, bm_ref, cm_ref, z_ref, dt_ref, conv0_ref, cw_ref, cb_ref, hp_ref, dskip_ref, nw_ref, h0_ref,
                o_ref, h_out_ref, xbuf, h_scr, *, rows_in):
    g = pl.program_id(1)
    c = pl.program_id(2)
    t = SSD_CHUNK

    @pl.when(c == 0)
    def _():
        xbuf[0:SSD_TAIL, :] = conv0_ref[0, 0]
        h_scr[...] = h0_ref[0, 0]

    xbuf[SSD_TAIL:SSD_TAIL + t, 0:SSD_GW] = _pad_rows(xs_ref[...], t)
    xbuf[SSD_TAIL:SSD_TAIL + t, SSD_GW:SSD_GW + SSD_STATE] = _pad_rows(bm_ref[...], t)
    xbuf[SSD_TAIL:SSD_TAIL + t, SSD_GW + SSD_STATE:SSD_CW] = _pad_rows(cm_ref[...], t)
    conv = cb_ref[0]
    for kk in range(SSD_CONV):
        conv = conv + cw_ref[0, kk:kk + 1, :] * xbuf[SSD_TAIL - (SSD_CONV - 1) + kk:SSD_TAIL - (SSD_CONV - 1) + kk + t, :]
    xbuf[0:SSD_TAIL, :] = xbuf[t:t + SSD_TAIL, :]
    xc = _silu(conv)
    xs = xc[:, 0:SSD_GW]
    bm = xc[:, SSD_GW:SSD_GW + SSD_STATE]
    cm = xc[:, SSD_GW + SSD_STATE:SSD_CW].astype(BF16)

    lane = lax.broadcasted_iota(jnp.int32, (t, LANE), 1)
    row = lax.broadcasted_iota(jnp.int32, (t, LANE), 0)
    dt = pltpu.roll(_pad_rows(dt_ref[...], t), (LANE - SSD_R * g) % LANE, 1)
    dt = jnp.where((lane < SSD_R) & (row < rows_in), _softplus(dt + hp_ref[0, 0:1, :]), 0.0)
    acs = _cumsum_rows(dt * -jnp.exp(hp_ref[0, 1:2, :]), t)
    acs_t = acs.T
    a_last = acs[t - 1:t, :]

    head_of_lane = lax.broadcasted_iota(jnp.int32, (1, SSD_GW), 1) // SSD_HEAD_DIM

    def spread(arr):
        return sum(jnp.where(head_of_lane == r, arr[:, r:r + 1], 0.0) for r in range(SSD_R))

    xdt = xs * spread(dt)
    cb = _dot_nt(cm, bm.astype(BF16))
    tri = lax.broadcasted_iota(jnp.int32, (t, t), 0) >= lax.broadcasted_iota(jnp.int32, (t, t), 1)
    y = jnp.zeros((t, SSD_GW), F32)
    for r in range(SSD_R):
        decay = jnp.where(tri, jnp.exp(jnp.minimum(acs[:, r:r + 1] - acs_t[r:r + 1, :], 0.0)), 0.0)
        x_r = jnp.where(head_of_lane == r, xdt, 0.0).astype(BF16)
        y = y + jnp.dot((cb * decay).astype(BF16), x_r, preferred_element_type=F32)
    h_prev = h_scr[...]
    y = y + jnp.dot(cm, h_prev.astype(BF16), preferred_element_type=F32) * jnp.exp(spread(acs))
    x_st = (xdt * jnp.exp(spread(a_last - acs))).astype(BF16)
    h_new = h_prev * jnp.exp(spread(a_last)) + jnp.dot(bm.T.astype(BF16), x_st, preferred_element_type=F32)
    h_scr[...] = h_new

    y = (y + xs * dskip_ref[0]) * _silu(_pad_rows(z_ref[...], t))
    y = y * lax.rsqrt(jnp.mean(y * y, -1, keepdims=True) + NORM_EPS) * nw_ref[0]
    o_ref[...] = y[:rows_in].astype(o_ref.dtype)

    @pl.when(c == pl.num_programs(2) - 1)
    def _():
        h_out_ref[0, 0] = h_new


def _ssd(y, conv0, h0, conv_w, conv_b, dt_bias, a_log, d_skip, norm_w, bsz):
    m = y.shape[0]
    l = m // bsz
    rows_in = min(SSD_CHUNK, l)
    nblk = l // rows_in
    ng = SSD_GROUPS

    def per_group(v):
        xs = v[..., :D_SSD].reshape(v.shape[:-1] + (ng, SSD_GW))
        bm = v[..., D_SSD:D_SSD + ng * SSD_STATE].reshape(v.shape[:-1] + (ng, SSD_STATE))
        cm = v[..., D_SSD + ng * SSD_STATE:].reshape(v.shape[:-1] + (ng, SSD_STATE))
        return jnp.moveaxis(jnp.concatenate([xs, bm, cm], axis=-1), -2, 0)

    conv0_g = jnp.moveaxis(per_group(jnp.pad(conv0, ((0, 0), (SSD_TAIL - (SSD_CONV - 1), 0), (0, 0)))), 0, 1)
    cw_g = per_group(conv_w)
    cb_g = per_group(conv_b[None])
    hp = jnp.zeros((ng, 8, LANE), F32)
    hp = hp.at[:, 0, :SSD_R].set(dt_bias.reshape(ng, SSD_R)).at[:, 1, :SSD_R].set(a_log.reshape(ng, SSD_R))
    dskip_g = jnp.repeat(d_skip, SSD_HEAD_DIM).reshape(ng, 1, SSD_GW)
    nw_g = norm_w.reshape(ng, 1, SSD_GW)
    h0_t = h0.reshape(bsz, ng, SSD_GW, SSD_STATE).transpose(0, 1, 3, 2)

    def rows(width, off):
        assert off % width == 0
        return pl.BlockSpec((rows_in, width), lambda b, g, c: (b * nblk + c, off // width + g))

    def grp(shape):
        return pl.BlockSpec((1,) + shape, lambda b, g, c: (g,) + (0,) * len(shape))

    state = pl.BlockSpec((1, 1, SSD_STATE, SSD_GW), lambda b, g, c: (b, g, 0, 0))
    out, h_t = pl.pallas_call(
        functools.partial(_ssd_kernel, rows_in=rows_in),
        grid=(bsz, ng, nblk),
        in_specs=[rows(SSD_GW, XBC_OFF), rows(SSD_STATE, XBC_OFF + D_SSD), rows(SSD_STATE, XBC_OFF + D_SSD + ng * SSD_STATE),
                  rows(SSD_GW, Z_OFF), pl.BlockSpec((rows_in, LANE), lambda b, g, c: (b * nblk + c, SMALL_OFF // LANE)),
                  pl.BlockSpec((1, 1, SSD_TAIL, SSD_CW), lambda b, g, c: (b, g, 0, 0)),
                  grp((SSD_CONV, SSD_CW)), grp((1, SSD_CW)), grp((8, LANE)), grp((1, SSD_GW)), grp((1, SSD_GW)), state],
        out_specs=[pl.BlockSpec((rows_in, SSD_GW), lambda b, g, c: (b * nblk + c, g)), state],
        out_shape=[jax.ShapeDtypeStruct((m, D_SSD), BF16), jax.ShapeDtypeStruct((bsz, ng, SSD_STATE, SSD_GW), F32)],
        scratch_shapes=[pltpu.VMEM((SSD_TAIL + SSD_CHUNK, SSD_CW), F32), pltpu.VMEM((SSD_STATE, SSD_GW), F32)],
        compiler_params=pltpu.CompilerParams(dimension_semantics=('parallel', 'parallel', 'arbitrary'),
                                             vmem_limit_bytes=VMEM_LIMIT_BYTES),
        name='ssd',
    )(y, y, y, y, y, conv0_g, cw_g, cb_g, hp, dskip_g, nw_g, h0_t)
    h_new = h_t.transpose(0, 1, 3, 2).reshape(bsz, SSD_HEADS, SSD_HEAD_DIM, SSD_STATE)
    return out, h_new


def _layer_flat(x, xb, bsz, lp, f32_weights, layer_idx, rel_bias, tables, mem_kv, conv0, ssd_h0, gla_s0, nsa_past, win_past):
    m = x.shape[0]
    l = m // bsz
    assert l >= SSD_CONV - 1
    y = _matmul(xb, lp['w_in'], name='in_proj')
    y3 = y.reshape(bsz, l, D_IN_PAD)
    y_ssd, h_new = _ssd(y, conv0, ssd_h0, lp['ssd_conv_w'], lp['ssd_conv_b'], lp['ssd_dt_bias'], lp['ssd_a_log'],
                        lp['ssd_d'], lp['ssd_norm_w'], bsz)
    conv_new = y3[:, l - (SSD_CONV - 1):, XBC_OFF:XBC_OFF + SSD_CONV_DIM]
    y_gla, s_new = _gla(y, gla_s0, lp['gla_gate_w2'], lp['gla_gate_b'], lp['gla_norm_w'], bsz)
    rows = y3[..., KV_OFF:KV_OFF + 4 * KVW].reshape(bsz, l, 4, NSA_KV_HEADS, NSA_HEAD_DIM)
    win_rows = y3[..., KV_OFF + 4 * KVW:KV_OFF + 6 * KVW].reshape(bsz, l, 2, NSA_KV_HEADS, NSA_HEAD_DIM)
    if nsa_past is None:
        win_new = win_rows[:, -min(WINDOW, l):]
        y_nsa = _nsa_prompt_t(y, _nsa_compress_prompt(y, lp['nsa_cmp_pe'], lp['nsa_cmp_w'], bsz), tables, bsz)
    else:
        cache_kv, layer, page_table = nsa_past
        win_new = jnp.concatenate([win_past, win_rows], axis=1)[:, -win_past.shape[1]:]
        y_nsa = _nsa_sample(y, cache_kv, layer, page_table, win_past, lp['nsa_cmp_pe'], lp['nsa_cmp_w'], rel_bias, bsz)
    def mm(key, a, **kw):
        if key not in lp:
            out, lp[key] = _matmul_cast(a, f32_weights[key], layer_idx, **kw)
            return out
        return _matmul(a, lp[key], **kw)

    if 'w_out' not in lp:
        h, lp['w_out'] = _out_proj_cast(y_ssd, y_nsa, y_gla, f32_weights['w_out'], layer_idx)
    else:
        h = _out_proj(y_ssd, y_nsa, y_gla, lp['w_out'])
    x, xb = _add_layernorm(x, h, lp['ln_g'][0], lp['ln_b'][0])
    o = _cross_attention(mm('x_wq', xb, name='xattn_q'), mem_kv, bsz)
    x, xb = _add_layernorm(x, mm('x_wo', o, name='xattn_o'), lp['ln_g'][1], lp['ln_b'][1])
    hidden = mm('ffn_w1', xb, out_dtype=BF16, act='sqrelu', name='ffn_up')
    x, xb = _add_layernorm(x, _matmul(hidden, lp['ffn_w2'], name='ffn_down'), lp['ln_g'][2], lp['ln_b'][2])
    return x, xb, (rows, win_new, conv_new, h_new, s_new)


def kernel(x_prompt, x_sample, cache_nsa_kv, cache_nsa_win, state_ssd, state_ssd_conv, state_gla, cache_mem_kv,
           page_table, mem_prompt, w_in, ssd_conv_w, ssd_conv_b, ssd_dt_bias, ssd_a_log, ssd_d, ssd_norm_w,
           nsa_cmp_pe, nsa_cmp_w, rel_bias, gla_gate_w2, gla_gate_b, gla_norm_w, w_out, x_wq, x_wkv, x_wo,
           ffn_w1, ffn_w2, ln_g, ln_b):
    bp = x_prompt.shape[0]
    bs = x_sample.shape[0]
    xp, xs = x_prompt.reshape(-1, D_MODEL), x_sample.reshape(-1, D_MODEL)
    xpb, xsb = xp.astype(BF16), xs.astype(BF16)
    st_p, st_s, mem_p = [], [], []
    tables = _nsa_prompt_bias_tables_t(rel_bias)
    f32_weights = dict(w_out=w_out, x_wq=x_wq, x_wo=x_wo, ffn_w1=ffn_w1)
    mem_b = mem_prompt.reshape(-1, D_MODEL).astype(BF16)
    for l in range(DEPTH):
        lp = dict(w_in=_reorder_w_in(w_in[l]), ssd_conv_w=ssd_conv_w[l], ssd_conv_b=ssd_conv_b[l],
                  ssd_dt_bias=ssd_dt_bias[l], ssd_a_log=ssd_a_log[l], ssd_d=ssd_d[l], ssd_norm_w=ssd_norm_w[l],
                  nsa_cmp_pe=nsa_cmp_pe[l], nsa_cmp_w=nsa_cmp_w[l], gla_gate_w2=gla_gate_w2[l], gla_gate_b=gla_gate_b[l],
                  gla_norm_w=gla_norm_w[l], ffn_w2=_to_bf16(ffn_w2, l), ln_g=ln_g[l], ln_b=ln_b[l])
        mem_kv_p, _ = _matmul_cast(mem_b, x_wkv, l, name='mem_kv')
        mem_kv_p = mem_kv_p.reshape(bp, N_MEM, 2 * D_MODEL)
        conv0 = jnp.zeros((bp, SSD_CONV - 1, SSD_CONV_DIM), F32)
        h0 = jnp.zeros((bp, SSD_HEADS, SSD_HEAD_DIM, SSD_STATE), F32)
        s0 = jnp.zeros((bp, GLA_HEADS, GLA_DK, GLA_DV), F32)
        xp, xpb, stp = _layer_flat(xp, xpb, bp, lp, f32_weights, l, rel_bias, tables, mem_kv_p, conv0, h0, s0, None, None)
        st_p.append(stp)
        mem_p.append(mem_kv_p.reshape(bp, N_MEM, 2, X_HEADS, X_HEAD_DIM))
        xs, xsb, sts = _layer_flat(xs, xsb, bs, lp, f32_weights, l, rel_bias, tables,
                                   cache_mem_kv[l].reshape(bs, N_MEM, 2 * D_MODEL), state_ssd_conv[l], state_ssd[l],
                                   state_gla[l], (cache_nsa_kv, l, page_table), cache_nsa_win[l])
        st_s.append(sts)
    p_rows, p_win, p_conv, p_ssd, p_gla = [jnp.stack(s) for s in zip(*st_p)]
    s_rows, s_win, s_conv, s_ssd, s_gla = [jnp.stack(s) for s in zip(*st_s)]
    p_mem = jnp.stack(mem_p)
    return (xp.reshape(x_prompt.shape), xs.reshape(x_sample.shape), p_rows, s_rows, p_win, s_win, p_ssd, s_ssd,
            p_conv, s_conv, p_gla, s_gla, p_mem)
```
